```python
import jax, jax.numpy as jnp
from jax import lax
import numpy as np

D_MODEL = 1024
BATCH = 8
SEQ = 2048
DEPTH = 1
DEC_BATCH = 128
DEC_SEQ = 1
PAST_LEN = 2048
PAGE_SIZE = 128

H_ATT = 8
DH_ATT = 64
ATT_WIDTH = H_ATT * DH_ATT
MOBA_BLOCK = 256
MOBA_TOPK = 3
Q_CHUNK = 32
H_MLSTM = 4
DK_MLSTM = 128
DV_MLSTM = 128
MLSTM_WIDTH = H_MLSTM * DV_MLSTM
MLSTM_CHUNK = 64
MIX_WIDTH = ATT_WIDTH + MLSTM_WIDTH
D_FF = ((8 * D_MODEL + 3 * 256 - 1) // (3 * 256)) * 256
ALPHA = (2.0 * DEPTH) ** 0.25
BETA = (8.0 * DEPTH) ** -0.25
LN_EPS = 1e-5
NEG = -1e30
IN_SIZES = (ATT_WIDTH, ATT_WIDTH, ATT_WIDTH, H_MLSTM * DK_MLSTM, H_MLSTM * DK_MLSTM,
            MLSTM_WIDTH, MLSTM_WIDTH, H_MLSTM, H_MLSTM)
IN_SPLITS = tuple(int(s) for s in np.cumsum(IN_SIZES)[:-1])
IN_COLS = sum(IN_SIZES)

kernel_name = 'hymba_moba_mlstm_deepnorm_adaln_step'


def _layernorm(x, g, b):
    xf = x.astype(jnp.float32)
    mu = xf.mean(-1, keepdims=True)
    var = jnp.square(xf - mu).mean(-1, keepdims=True)
    return ((xf - mu) * lax.rsqrt(var + LN_EPS) * g + b).astype(x.dtype)


def _adaln(c, w_ada, b_ada):
    return jnp.split(jax.nn.silu(c) @ w_ada + b_ada, 6, axis=-1)


def _in_proj(x, shift, scale, w_in, b_if):
    B, S, _ = x.shape
    h = x * (1.0 + scale[:, None, :]) + shift[:, None, :]
    aq, ak, av, mq, mk, mv, mo, mi, mf = jnp.split(h @ w_in, IN_SPLITS, axis=-1)
    att = lambda t: t.reshape(B, S, H_ATT, DH_ATT)
    mq = mq.reshape(B, S, H_MLSTM, DK_MLSTM).astype(jnp.float32)
    mk = (mk.reshape(B, S, H_MLSTM, DK_MLSTM) * DK_MLSTM ** -0.5).astype(jnp.float32)
    mv = mv.reshape(B, S, H_MLSTM, DV_MLSTM).astype(jnp.float32)
    ig = (mi + b_if[:H_MLSTM]).astype(jnp.float32)
    lf = jax.nn.log_sigmoid((mf + b_if[H_MLSTM:]).astype(jnp.float32))
    return att(aq), att(ak), att(av), mq, mk, mv, mo, ig, lf


def _moba_core(q, qblk, kmean, kb, vb, s_own, own_mask):
    B, Q, H, Dh = q.shape
    nb = kb.shape[1]
    scale = DH_ATT ** -0.5
    blk_score = jnp.einsum('bqhd,bnhd->bqhn', q.astype(jnp.float32), kmean)
    past = jnp.arange(nb)[None, :] < qblk[:, None]
    blk_score = jnp.where(past[None, :, None, :], blk_score, NEG)
    _, sel = lax.top_k(blk_score, min(MOBA_TOPK, nb))
    valid = sel < qblk[None, :, None, None]
    bi = jnp.arange(B)[:, None, None, None]
    hi = jnp.arange(H)[None, None, :, None]
    k_sel = kb[bi, sel, :, hi]
    v_sel = vb[bi, sel, :, hi]
    n_sel = sel.shape[-1] * MOBA_BLOCK
    s_sel = jnp.einsum('bqhd,bqhjsd->bqhjs', q, k_sel,
                       preferred_element_type=jnp.float32).reshape(B, Q, H, n_sel)
    mask_sel = jnp.broadcast_to(valid[..., None], k_sel.shape[:-1]).reshape(B, Q, H, n_sel)
    s = jnp.concatenate([jnp.where(own_mask, s_own * scale, NEG),
                         jnp.where(mask_sel, s_sel * scale, NEG)], axis=-1)
    p = jax.nn.softmax(s, axis=-1).astype(vb.dtype)
    p_own = p[..., :MOBA_BLOCK]
    p_sel = p[..., MOBA_BLOCK:].reshape(k_sel.shape[:-1])
    o_sel = jnp.einsum('bqhjs,bqhjsd->bqhd', p_sel, v_sel)
    return p_own, o_sel


def _moba_prompt(q, k, v):
    B, S, H, Dh = q.shape
    nb = -(-S // MOBA_BLOCK)
    pad = ((0, 0), (0, nb * MOBA_BLOCK - S), (0, 0), (0, 0))
    kb = jnp.pad(k, pad).reshape(B, nb, MOBA_BLOCK, H, Dh)
    vb = jnp.pad(v, pad).reshape(B, nb, MOBA_BLOCK, H, Dh)
    kmean = kb.astype(jnp.float32).mean(axis=2)

    def q_block(ci):
        start = ci * Q_CHUNK
        qc = lax.dynamic_slice_in_dim(q, start, Q_CHUNK, axis=1)
        pos = start + jnp.arange(Q_CHUNK)
        blk = start // MOBA_BLOCK
        k_own = lax.dynamic_index_in_dim(kb, blk, axis=1, keepdims=False)
        v_own = lax.dynamic_index_in_dim(vb, blk, axis=1, keepdims=False)
        s_own = jnp.einsum('bqhd,bshd->bqhs', qc, k_own, preferred_element_type=jnp.float32)
        own_mask = (blk * MOBA_BLOCK + jnp.arange(MOBA_BLOCK))[None, :] <= pos[:, None]
        p_own, o_sel = _moba_core(qc, pos // MOBA_BLOCK, kmean, kb, vb, s_own,
                                  own_mask[None, :, None, :])
        return (o_sel + jnp.einsum('bqhs,bshd->bqhd', p_own, v_own)).astype(q.dtype)

    out = lax.map(q_block, jnp.arange(S // Q_CHUNK))
    return jnp.moveaxis(out, 0, 1).reshape(B, S, H * Dh)


def _moba_sample(q, k, v, cache_k, cache_v, page_table):
    B, T, H, Dh = q.shape
    past_len = page_table.shape[1] * PAGE_SIZE
    total = past_len + T
    nb = -(-total // MOBA_BLOCK)
    fill = jnp.zeros((B, nb * MOBA_BLOCK - total, H, Dh), k.dtype)
    kb = jnp.concatenate([cache_k[page_table].reshape(B, past_len, H, Dh).astype(k.dtype), k, fill],
                         axis=1).reshape(B, nb, MOBA_BLOCK, H, Dh)
    vb = jnp.concatenate([cache_v[page_table].reshape(B, past_len, H, Dh).astype(v.dtype), v, fill],
                         axis=1).reshape(B, nb, MOBA_BLOCK, H, Dh)
    kmean = kb.astype(jnp.float32).mean(axis=2)
    pos = past_len + jnp.arange(T)
    qblk = pos // MOBA_BLOCK
    k_own = kb[:, qblk]
    v_own = vb[:, qblk]
    s_own = jnp.einsum('bqhd,bqshd->bqhs', q, k_own, preferred_element_type=jnp.float32)
    own_mask = (qblk[:, None] * MOBA_BLOCK + jnp.arange(MOBA_BLOCK)[None, :]) <= pos[:, None]
    p_own, o_sel = _moba_core(q, qblk, kmean, kb, vb, s_own, own_mask[None, :, None, :])
    out = o_sel + jnp.einsum('bqhs,bqshd->bqhd', p_own, v_own)
    return out.astype(q.dtype).reshape(B, T, H * Dh)


def _mlstm_chunk(carry, inp):
    C, n, m = carry
    q, k, v, ig, lf = inp
    L = q.shape[1]
    b = jnp.cumsum(lf, axis=1)
    causal = jnp.tril(jnp.ones((L, L), bool))
    dmat = jnp.where(causal[None, :, :, None],
                     b[:, :, None, :] - b[:, None, :, :] + ig[:, None, :, :], NEG)
    m_inter = b + m[:, None, :]
    m_t = jnp.maximum(m_inter, dmat.max(axis=2))
    w_inter = jnp.exp(m_inter - m_t)
    a = jnp.exp(dmat - m_t[:, :, None, :]) * jnp.einsum('bthd,bjhd->btjh', q, k)
    num = w_inter[..., None] * jnp.einsum('bhvd,bthd->bthv', C, q) + jnp.einsum('btjh,bjhv->bthv', a, v)
    den = w_inter * jnp.einsum('bhd,bthd->bth', n, q) + a.sum(axis=2)
    h = num / jnp.maximum(jnp.abs(den), jnp.exp(-m_t))[..., None]
    m_new = m_t[:, -1]
    g_inter = jnp.exp(b[:, -1] + m - m_new)
    g_in = jnp.exp(b[:, -1:] - b + ig - m_new[:, None])
    C_new = g_inter[..., None, None] * C + jnp.einsum('bjh,bjhv,bjhd->bhvd', g_in, v, k)
    n_new = g_inter[..., None] * n + jnp.einsum('bjh,bjhd->bhd', g_in, k)
    return (C_new, n_new, m_new), h


def _mlstm_prompt(q, k, v, ig, lf):
    B, S = q.shape[:2]
    nc = S // MLSTM_CHUNK
    chunks = lambda a: jnp.moveaxis(a.reshape(B, nc, MLSTM_CHUNK, *a.shape[2:]), 1, 0)
    init = (jnp.zeros((B, H_MLSTM, DV_MLSTM, DK_MLSTM), jnp.float32),
            jnp.zeros((B, H_MLSTM, DK_MLSTM), jnp.float32),
            jnp.zeros((B, H_MLSTM), jnp.float32))
    state, h = lax.scan(_mlstm_chunk, init, (chunks(q), chunks(k), chunks(v), chunks(ig), chunks(lf)))
    return jnp.moveaxis(h, 0, 1).reshape(B, S, H_MLSTM, DV_MLSTM), state


def _mlstm_out(h, o, g):
    mu = h.mean(-1, keepdims=True)
    var = jnp.square(h - mu).mean(-1, keepdims=True)
    hn = ((h - mu) * lax.rsqrt(var + LN_EPS)).reshape(*h.shape[:2], MLSTM_WIDTH)
    return (hn * g * jax.nn.sigmoid(o.astype(jnp.float32))).astype(o.dtype)


def _finish(x, att, mem, g1, sh2, sc2, g2, w_out, ln1_g, ln1_b, w_gate, w_up, w_down, ln2_g, ln2_b):
    mix = jnp.concatenate([att, mem], axis=-1) @ w_out
    x = _layernorm(ALPHA * x + (1.0 + g1[:, None, :]) * mix, ln1_g, ln1_b)
    h = x * (1.0 + sc2[:, None, :]) + sh2[:, None, :]
    f = (jax.nn.silu(h @ w_gate) * (h @ w_up)) @ w_down
    return _layernorm(ALPHA * x + (1.0 + g2[:, None, :]) * f, ln2_g, ln2_b)


def _layer_prompt(x, c, weights):
    (w_ada, b_ada, w_in, b_if, mlstm_norm_g, w_out, ln1_g, ln1_b,
     w_gate, w_up, w_down, ln2_g, ln2_b) = weights
    sh1, sc1, g1, sh2, sc2, g2 = _adaln(c, w_ada, b_ada)
    aq, ak, av, mq, mk, mv, mo, ig, lf = _in_proj(x, sh1, sc1, w_in, b_if)
    att = _moba_prompt(aq, ak, av)
    h, (C, n, m) = _mlstm_prompt(mq, mk, mv, ig, lf)
    y = _finish(x, att, _mlstm_out(h, mo, mlstm_norm_g), g1, sh2, sc2, g2,
                w_out, ln1_g, ln1_b, w_gate, w_up, w_down, ln2_g, ln2_b)
    return y, ak, av, C.astype(x.dtype), n.astype(x.dtype), m.astype(x.dtype)


def _layer_sample(x, c, cache_k, cache_v, page_table, C0, n0, m0, weights):
    (w_ada, b_ada, w_in, b_if, mlstm_norm_g, w_out, ln1_g, ln1_b,
     w_gate, w_up, w_down, ln2_g, ln2_b) = weights
    sh1, sc1, g1, sh2, sc2, g2 = _adaln(c, w_ada, b_ada)
    aq, ak, av, mq, mk, mv, mo, ig, lf = _in_proj(x, sh1, sc1, w_in, b_if)
    att = _moba_sample(aq, ak, av, cache_k, cache_v, page_table)
    carry = (C0.astype(jnp.float32), n0.astype(jnp.float32), m0.astype(jnp.float32))
    (C, n, m), h = _mlstm_chunk(carry, (mq, mk, mv, ig, lf))
    y = _finish(x, att, _mlstm_out(h, mo, mlstm_norm_g), g1, sh2, sc2, g2,
                w_out, ln1_g, ln1_b, w_gate, w_up, w_down, ln2_g, ln2_b)
    return y, ak, av, C.astype(x.dtype), n.astype(x.dtype), m.astype(x.dtype)


def setup_inputs(seed: int = 0) -> dict:
    key = jax.random.key(seed)
    ks = jax.random.split(key, 26)
    n_pages = PAST_LEN // PAGE_SIZE
    n_used = DEC_BATCH * n_pages
    n_phys = n_used + max(1, n_used // 4)
    nrm = lambda k, shape, s: jax.random.normal(k, shape, jnp.float32) * s
    page_table = jax.random.permutation(ks[7], n_phys)[:n_used].reshape(DEC_BATCH, n_pages).astype(jnp.int32)
    f_bias = jnp.broadcast_to(jnp.linspace(3.0, 6.0, H_MLSTM), (DEPTH, H_MLSTM))
    b_if = jnp.concatenate([jnp.full((DEPTH, H_MLSTM), -2.0), f_bias], axis=-1) + nrm(ks[13], (DEPTH, 2 * H_MLSTM), 0.1)
    return {
        'x_prompt': nrm(ks[0], (BATCH, SEQ, D_MODEL), 1.0),
        'x_sample': nrm(ks[1], (DEC_BATCH, DEC_SEQ, D_MODEL), 1.0),
        'cache_k': nrm(ks[2], (DEPTH, n_phys, PAGE_SIZE, H_ATT, DH_ATT), 1.0),
        'cache_v': nrm(ks[3], (DEPTH, n_phys, PAGE_SIZE, H_ATT, DH_ATT), 1.0),
        'state_C': nrm(ks[4], (DEPTH, DEC_BATCH, H_MLSTM, DV_MLSTM, DK_MLSTM), 0.5),
        'state_n': nrm(ks[5], (DEPTH, DEC_BATCH, H_MLSTM, DK_MLSTM), 0.5),
        'state_m': nrm(ks[6], (DEPTH, DEC_BATCH, H_MLSTM), 1.0),
        'page_table': page_table,
        'c_prompt': nrm(ks[8], (BATCH, D_MODEL), 1.0),
        'c_sample': nrm(ks[9], (DEC_BATCH, D_MODEL), 1.0),
        'w_ada': nrm(ks[10], (DEPTH, D_MODEL, 6 * D_MODEL), 0.1 * D_MODEL ** -0.5),
        'b_ada': nrm(ks[11], (DEPTH, 6 * D_MODEL), 0.01),
        'w_in': nrm(ks[12], (DEPTH, D_MODEL, IN_COLS), D_MODEL ** -0.5),
        'b_if': b_if,
        'mlstm_norm_g': 1.0 + nrm(ks[14], (DEPTH, MLSTM_WIDTH), 0.01),
        'w_out': nrm(ks[15], (DEPTH, MIX_WIDTH, D_MODEL), BETA * MIX_WIDTH ** -0.5),
        'ln1_g': 1.0 + nrm(ks[16], (DEPTH, D_MODEL), 0.01),
        'ln1_b': nrm(ks[17], (DEPTH, D_MODEL), 0.01),
        'w_gate': nrm(ks[18], (DEPTH, D_MODEL, D_FF), D_MODEL ** -0.5),
        'w_up': nrm(ks[19], (DEPTH, D_MODEL, D_FF), D_MODEL ** -0.5),
        'w_down': nrm(ks[20], (DEPTH, D_FF, D_MODEL), BETA * D_FF ** -0.5),
        'ln2_g': 1.0 + nrm(ks[21], (DEPTH, D_MODEL), 0.01),
        'ln2_b': nrm(ks[22], (DEPTH, D_MODEL), 0.01),
    }


def reference(x_prompt, x_sample, cache_k, cache_v, state_C, state_n, state_m, page_table,
              c_prompt, c_sample, w_ada, b_ada, w_in, b_if, mlstm_norm_g, w_out,
              ln1_g, ln1_b, w_gate, w_up, w_down, ln2_g, ln2_b):
    y_prompt, y_sample = x_prompt, x_sample
    new_p, new_s = [], []
    for l in range(DEPTH):
        wl = (w_ada[l], b_ada[l], w_in[l], b_if[l], mlstm_norm_g[l], w_out[l], ln1_g[l], ln1_b[l],
              w_gate[l], w_up[l], w_down[l], ln2_g[l], ln2_b[l])
        y_prompt, *sp = _layer_prompt(y_prompt, c_prompt, wl)
        y_sample, *ss = _layer_sample(y_sample, c_sample, cache_k[l], cache_v[l], page_table,
                                      state_C[l], state_n[l], state_m[l], wl)
        new_p.append(sp)
        new_s.append(ss)
    k_prompt, v_prompt, C_prompt, n_prompt, m_prompt = (jnp.stack(a) for a in zip(*new_p))
    k_sample, v_sample, C_sample, n_sample, m_sample = (jnp.stack(a) for a in zip(*new_s))
    return (y_prompt, y_sample, k_prompt, v_prompt, C_prompt, n_prompt, m_prompt,
            k_sample, v_sample, C_sample, n_sample, m_sample)
```

```python
import functools

import jax
import jax.numpy as jnp
from jax import lax
from jax.experimental import pallas as pl
from jax.experimental.pallas import tpu as pltpu

F32 = jnp.float32
BF16 = jnp.bfloat16
HIGHEST = lax.Precision.HIGHEST

LANE = 128
SUBLANE = 8
VMEM_LIMIT_BYTES = 56 * 1024 * 1024

H_ATT = 8
DH_ATT = 64
ATT_WIDTH = H_ATT * DH_ATT
MOBA_BLOCK = 256
MOBA_TOPK = 3
H_MLSTM = 4
DK_MLSTM = 128
DV_MLSTM = 128
MLSTM_WIDTH = H_MLSTM * DV_MLSTM
MLSTM_CHUNK = 128
PAGE_SIZE = 128
PAGES_PER_BLOCK = MOBA_BLOCK // PAGE_SIZE
LN_EPS = 1e-5
NEG = -1e30
N_PROJ_GROUPS = 7
PROJ_GROUP = 512
MK_GROUP = 4

_NT = (((1,), (1,)), ((), ()))
_TN = (((0,), (0,)), ((), ()))


def _params(*sem):
    return pltpu.CompilerParams(dimension_semantics=sem, vmem_limit_bytes=VMEM_LIMIT_BYTES)


def _const_spec(shape):
    return pl.BlockSpec(shape, lambda *_: (0,) * len(shape), pipeline_mode=pl.Buffered(1))


def _layernorm(x, g, b):
    mu = jnp.mean(x, axis=-1, keepdims=True)
    d = x - mu
    var = jnp.mean(d * d, axis=-1, keepdims=True)
    return d * lax.rsqrt(var + LN_EPS) * g + b


def _top_blocks(val, nidx):
    cnt = jnp.zeros(val.shape, jnp.int32)
    for r in range(1, SUBLANE):
        other = pltpu.roll(val, r, 0)
        oidx = pltpu.roll(nidx, r, 0)
        beats = (other > val) | ((other == val) & (oidx < nidx))
        cnt = cnt + jnp.where(beats, 1, 0)
    return cnt < MOBA_TOPK


def _adaln_kernel(c_ref, w_ref, b_ref, o_ref):
    c = c_ref[...]
    s = c * jax.nn.sigmoid(c)
    o_ref[...] = jnp.dot(s, w_ref[...], preferred_element_type=F32) + b_ref[...]


def _adaln(c, w_ada, b_ada):
    rows, d = c.shape
    n = w_ada.shape[1]
    tn = d
    return pl.pallas_call(
        _adaln_kernel,
        grid=(n // tn,),
        in_specs=[pl.BlockSpec((rows, d), lambda j: (0, 0)),
                  pl.BlockSpec((d, tn), lambda j: (0, j)),
                  pl.BlockSpec((1, tn), lambda j: (0, j))],
        out_specs=pl.BlockSpec((rows, tn), lambda j: (0, j)),
        out_shape=jax.ShapeDtypeStruct((rows, n), F32),
        compiler_params=_params("arbitrary"),
    )(c, w_ada, b_ada.reshape(1, n))


def _in_proj_kernel(x_ref, sc_ref, sh_ref, w_ref, wg_ref, bg_ref, *out_refs, n_transposed):
    proj_refs, g_ref = out_refs[:N_PROJ_GROUPS], out_refs[N_PROJ_GROUPS]
    h = (x_ref[...] * (1.0 + sc_ref[...]) + sh_ref[...]).astype(BF16)
    for gi, o_ref in enumerate(proj_refs):
        y = jnp.dot(h, w_ref[:, gi * PROJ_GROUP:(gi + 1) * PROJ_GROUP], preferred_element_type=F32)
        if gi == MK_GROUP:
            y = y * (DK_MLSTM ** -0.5)
        if gi < n_transposed:
            y = y.T
        o_ref[...] = y.astype(o_ref.dtype)
    g = jnp.dot(h, wg_ref[...], preferred_element_type=F32) + bg_ref[...]
    lane = lax.broadcasted_iota(jnp.int32, g.shape, 1)
    logsig = jnp.minimum(g, 0.0) - jnp.log1p(jnp.exp(-jnp.abs(g)))
    g_ref[...] = jnp.where(lane >= H_MLSTM, logsig, g)


def _in_proj(x, sc, sh, w_main, w_gate, b_gate, tm, rows_per_mod, out_dtypes, n_transposed=0):
    t, d = x.shape
    if rows_per_mod is None:
        mod_spec = pl.BlockSpec((tm, d), lambda i: (i, 0))
    else:
        per = rows_per_mod // tm
        mod_spec = pl.BlockSpec((None, 1, d), lambda i: (i // per, 0, 0))
    out_shape, out_specs = [], []
    for gi, dt in enumerate(out_dtypes):
        if gi < n_transposed:
            out_shape.append(jax.ShapeDtypeStruct((PROJ_GROUP, t), dt))
            out_specs.append(pl.BlockSpec((PROJ_GROUP, tm), lambda i: (0, i)))
        else:
            out_shape.append(jax.ShapeDtypeStruct((t, PROJ_GROUP), dt))
            out_specs.append(pl.BlockSpec((tm, PROJ_GROUP), lambda i: (i, 0)))
    out_shape.append(jax.ShapeDtypeStruct((t, LANE), F32))
    out_specs.append(pl.BlockSpec((tm, LANE), lambda i: (i, 0)))
    return pl.pallas_call(
        functools.partial(_in_proj_kernel, n_transposed=n_transposed),
        grid=(t // tm,),
        in_specs=[pl.BlockSpec((tm, d), lambda i: (i, 0)), mod_spec, mod_spec,
                  _const_spec(w_main.shape), _const_spec(w_gate.shape), _const_spec(b_gate.shape)],
        out_specs=out_specs,
        out_shape=out_shape,
        compiler_params=_params("arbitrary"),
    )(x, sc, sh, w_main, w_gate, b_gate)


def _moba_prompt_kernel(q_ref, k_ref, v_ref, o_ref, kaug_ref, vb_ref, km_ref, *, nb):
    i = pl.program_id(1)
    blk = MOBA_BLOCK
    half = LANE // 2
    lane = lax.broadcasted_iota(jnp.int32, (blk, LANE), 1)

    @pl.when(i == 0)
    def _prepare_batch():
        vb_ref[...] = v_ref[...].astype(BF16)
        rows = []
        for j in range(nb):
            kj = k_ref[j * blk:(j + 1) * blk, :]
            rows.append(jnp.mean(kj, axis=0, keepdims=True))
            onehot_even = jnp.where(lane == half + j, 1.0, 0.0)
            onehot_odd = jnp.where(lane == j, 1.0, 0.0)
            for p in range(H_ATT // 2):
                kp = kj[:, p * LANE:(p + 1) * LANE]
                kaug_ref[2 * p, j * blk:(j + 1) * blk, :] = jnp.where(lane < half, kp, onehot_even).astype(BF16)
                kaug_ref[2 * p + 1, j * blk:(j + 1) * blk, :] = jnp.where(lane >= half, kp, onehot_odd).astype(BF16)
        rows += [jnp.zeros_like(rows[0])] * (SUBLANE - nb)
        kmean = jnp.concatenate(rows, axis=0)
        head_of_lane = lax.broadcasted_iota(jnp.int32, kmean.shape, 1) // DH_ATT
        for h in range(H_ATT):
            km_ref[h * SUBLANE:(h + 1) * SUBLANE, :] = jnp.where(head_of_lane == h, kmean, 0.0)

    q32 = q_ref[...]
    sc_t = lax.dot_general(km_ref[...], q32, _NT, precision=HIGHEST, preferred_element_type=F32)
    nidx = lax.broadcasted_iota(jnp.int32, (SUBLANE, blk), 0)
    past = nidx < i
    biases = []
    for h in range(H_ATT):
        val = jnp.where(past, sc_t[h * SUBLANE:(h + 1) * SUBLANE, :], NEG)
        keep = (_top_blocks(val, nidx) & past) | (nidx == i)
        biases.append(jnp.where(keep, 0.0, NEG))

    row = lax.broadcasted_iota(jnp.int32, (blk, blk), 0)
    col = lax.broadcasted_iota(jnp.int32, (blk, blk), 1)
    own_start = pl.multiple_of(i * blk, blk)

    def attend(h, lhs, p):
        lanes = slice(p * LANE, (p + 1) * LANE)
        s = lax.dot_general(lhs, kaug_ref[h, pl.ds(own_start, blk), :], _NT, preferred_element_type=F32)
        s = jnp.where(col <= row, s, NEG)
        m = jnp.max(s, axis=1, keepdims=True)
        pe = jnp.exp(s - m)
        l = jnp.sum(pe, axis=1, keepdims=True)
        acc = jnp.dot(pe.astype(BF16), vb_ref[pl.ds(own_start, blk), lanes], preferred_element_type=F32)

        def past_block(j, carry):
            m, l, acc = carry
            start = pl.multiple_of(j * blk, blk)
            s = lax.dot_general(lhs, kaug_ref[h, pl.ds(start, blk), :], _NT, preferred_element_type=F32)
            m_new = jnp.maximum(m, jnp.max(s, axis=1, keepdims=True))
            alpha = jnp.exp(m - m_new)
            pe = jnp.exp(s - m_new)
            l = alpha * l + jnp.sum(pe, axis=1, keepdims=True)
            acc = alpha * acc + jnp.dot(pe.astype(BF16), vb_ref[pl.ds(start, blk), lanes],
                                        preferred_element_type=F32)
            return m_new, l, acc

        m, l, acc = lax.fori_loop(0, i, past_block, (m, l, acc))
        return acc / l

    zpad = jnp.zeros((half - SUBLANE, blk), F32)
    for p in range(H_ATT // 2):
        bias_p = jnp.concatenate([biases[2 * p + 1], zpad, biases[2 * p], zpad], axis=0).T
        qp = q32[:, p * LANE:(p + 1) * LANE] * (DH_ATT ** -0.5)
        out_even = attend(2 * p, jnp.where(lane < half, qp, bias_p).astype(BF16), p)
        out_odd = attend(2 * p + 1, jnp.where(lane >= half, qp, bias_p).astype(BF16), p)
        o_ref[:, p * LANE:(p + 1) * LANE] = jnp.where(lane < half, out_even, out_odd).astype(o_ref.dtype)


def _moba_prompt(q, k, v):
    b, s, w = q.shape
    nb = s // MOBA_BLOCK
    assert s % MOBA_BLOCK == 0 and nb <= SUBLANE and w == ATT_WIDTH
    return pl.pallas_call(
        functools.partial(_moba_prompt_kernel, nb=nb),
        grid=(b, nb),
        in_specs=[pl.BlockSpec((None, MOBA_BLOCK, w), lambda bi, i: (bi, i, 0)),
                  pl.BlockSpec((None, s, w), lambda bi, i: (bi, 0, 0)),
                  pl.BlockSpec((None, s, w), lambda bi, i: (bi, 0, 0))],
        out_specs=pl.BlockSpec((None, MOBA_BLOCK, w), lambda bi, i: (bi, i, 0)),
        out_shape=jax.ShapeDtypeStruct((b, s, w), BF16),
        scratch_shapes=[pltpu.VMEM((H_ATT, s, LANE), BF16),
                        pltpu.VMEM((s, w), BF16),
                        pltpu.VMEM((H_ATT * SUBLANE, w), F32)],
        compiler_params=_params("arbitrary", "arbitrary"),
    )(q, k, v)


def _mlstm_head_out(hh, gain, ogate):
    mu = jnp.mean(hh, axis=-1, keepdims=True)
    d = hh - mu
    var = jnp.mean(d * d, axis=-1, keepdims=True)
    return d * lax.rsqrt(var + LN_EPS) * gain * jax.nn.sigmoid(ogate)


def _mlstm_prompt_kernel(q_ref, k_ref, v_ref, o_ref, g_ref, gain_ref,
                         mem_ref, c_out, n_out, m_out, c_sc, n_sc, m_sc):
    c = pl.program_id(1)
    L = q_ref.shape[0]

    @pl.when(c == 0)
    def _reset_state():
        c_sc[...] = jnp.zeros_like(c_sc)
        n_sc[...] = jnp.zeros_like(n_sc)
        m_sc[...] = jnp.zeros_like(m_sc)

    g = g_ref[...]
    g_t = g.T
    row = lax.broadcasted_iota(jnp.int32, (L, L), 0)
    col = lax.broadcasted_iota(jnp.int32, (L, L), 1)
    causal = col <= row
    lower = jnp.where(causal, 1.0, 0.0)
    upper = jnp.where(row <= col, 1.0, 0.0)
    b_col_all = jnp.dot(lower, g, precision=HIGHEST, preferred_element_type=F32)
    b_row_all = jnp.dot(g_t[0:SUBLANE, :], upper, precision=HIGHEST, preferred_element_type=F32)

    for h in range(H_MLSTM):
        lanes = slice(h * DK_MLSTM, (h + 1) * DK_MLSTM)
        ig_row, ig_col = g_t[h:h + 1, :], g[:, h:h + 1]
        b_row = b_row_all[H_MLSTM + h:H_MLSTM + h + 1, :]
        b_col = b_col_all[:, H_MLSTM + h:H_MLSTM + h + 1]
        m_prev = m_sc[h:h + 1, 0:1]
        dmat = jnp.where(causal, b_col - b_row + ig_row, NEG)
        m_inter = b_col + m_prev
        m_t = jnp.maximum(m_inter, jnp.max(dmat, axis=1, keepdims=True))
        w_inter = jnp.exp(m_inter - m_t)
        qh, kh, vh = q_ref[:, lanes], k_ref[:, lanes], v_ref[:, lanes]
        a = jnp.exp(dmat - m_t) * lax.dot_general(qh, kh, _NT, preferred_element_type=F32)
        c_prev = c_sc[h]
        n_prev = n_sc[h:h + 1, :]
        num = (w_inter * lax.dot_general(qh, c_prev.astype(BF16), _NT, preferred_element_type=F32)
               + jnp.dot(a.astype(BF16), vh, preferred_element_type=F32))
        den = (w_inter * jnp.sum(qh.astype(F32) * n_prev, axis=1, keepdims=True)
               + jnp.sum(a, axis=1, keepdims=True))
        hh = num / jnp.maximum(jnp.abs(den), jnp.exp(-m_t))
        mem_ref[:, lanes] = _mlstm_head_out(hh, gain_ref[:, lanes], o_ref[:, lanes].astype(F32)).astype(mem_ref.dtype)

        m_new = m_t[L - 1:L, :]
        b_last = b_col[L - 1:L, :]
        g_inter = jnp.exp(b_last + m_prev - m_new)
        g_in = jnp.exp(b_last - b_col + ig_col - m_new)
        v_scaled = (vh.astype(F32) * g_in).astype(BF16)
        c_sc[h] = g_inter * c_prev + lax.dot_general(v_scaled, kh, _TN, preferred_element_type=F32)
        n_sc[h:h + 1, :] = g_inter * n_prev + jnp.sum(kh.astype(F32) * g_in, axis=0, keepdims=True)
        m_sc[h:h + 1, :] = jnp.broadcast_to(m_new, (1, LANE))

    @pl.when(c == pl.num_programs(1) - 1)
    def _emit_state():
        c_out[...] = c_sc[...]
        n_out[...] = n_sc[...]
        m_out[...] = m_sc[...]


def _mlstm_prompt(mq, mk, mv, mo, gates, gain):
    b, s, w = mq.shape
    L = MLSTM_CHUNK
    assert s % L == 0 and w == MLSTM_WIDTH
    tok = lambda width: pl.BlockSpec((None, L, width), lambda bi, c: (bi, c, 0))
    state = lambda *dims: pl.BlockSpec((None,) + dims, lambda bi, c: (bi,) + (0,) * len(dims))
    return pl.pallas_call(
        _mlstm_prompt_kernel,
        grid=(b, s // L),
        in_specs=[tok(w), tok(w), tok(w), tok(w), tok(LANE), pl.BlockSpec((1, w), lambda bi, c: (0, 0))],
        out_specs=[tok(w), state(H_MLSTM, DV_MLSTM, DK_MLSTM), state(SUBLANE, LANE), state(SUBLANE, LANE)],
        out_shape=[jax.ShapeDtypeStruct((b, s, w), BF16),
                   jax.ShapeDtypeStruct((b, H_MLSTM, DV_MLSTM, DK_MLSTM), F32),
                   jax.ShapeDtypeStruct((b, SUBLANE, LANE), F32),
                   jax.ShapeDtypeStruct((b, SUBLANE, LANE), F32)],
        scratch_shapes=[pltpu.VMEM((H_MLSTM, DV_MLSTM, DK_MLSTM), F32),
                        pltpu.VMEM((SUBLANE, LANE), F32),
                        pltpu.VMEM((SUBLANE, LANE), F32)],
        compiler_params=_params("arbitrary", "arbitrary"),
    )(mq, mk, mv, mo, gates, gain)


def _moba_sample_kernel(pt_ref, qt_ref, knt_ref, vnt_ref, *rest, n_pages):
    del pt_ref
    kp_refs, vp_refs, o_ref = rest[:n_pages], rest[n_pages:2 * n_pages], rest[2 * n_pages]
    b = pl.program_id(0)
    n_blocks = n_pages // PAGES_PER_BLOCK
    w = qt_ref.shape[0]
    on_b = lax.broadcasted_iota(jnp.int32, (w, LANE), 1) == b

    @pl.when(b == 0)
    def _init_out():
        o_ref[...] = jnp.zeros_like(o_ref)

    def column(ref):
        return jnp.sum(jnp.where(on_b, ref[...], 0.0), axis=1, keepdims=True)

    q_col = column(qt_ref) * (DH_ATT ** -0.5)
    kn_col, vn_col = column(knt_ref), column(vnt_ref)
    q_wide = jnp.broadcast_to(q_col, (w, LANE))

    out_cols = []
    for h in range(H_ATT):
        rows = slice(h * DH_ATT, (h + 1) * DH_ATT)
        s_pages = [jnp.sum(kp_refs[p][rows, :] * q_wide[rows, :], axis=0, keepdims=True)
                   for p in range(n_pages)]
        blk = []
        for n in range(n_blocks):
            tot = sum(s_pages[n * PAGES_PER_BLOCK:(n + 1) * PAGES_PER_BLOCK])
            blk.append(jnp.sum(tot, axis=1, keepdims=True))
        sel = []
        for n in range(n_blocks):
            cnt = jnp.zeros((1, 1), jnp.int32)
            for o in range(n_blocks):
                if o != n:
                    beats = (blk[o] >= blk[n]) if o < n else (blk[o] > blk[n])
                    cnt = cnt + jnp.where(beats, 1, 0)
            sel.append(cnt < MOBA_TOPK)
        s_own = jnp.sum(q_col[rows, :] * kn_col[rows, :], axis=0, keepdims=True)
        m = s_own
        for p in range(n_pages):
            page_max = jnp.max(s_pages[p], axis=1, keepdims=True)
            m = jnp.maximum(m, jnp.where(sel[p // PAGES_PER_BLOCK], page_max, NEG))
        p_own = jnp.exp(s_own - m)
        l = p_own
        acc = jnp.zeros((DH_ATT, LANE), F32)
        for p in range(n_pages):
            pe = jnp.where(sel[p // PAGES_PER_BLOCK], jnp.exp(s_pages[p] - m), 0.0)
            l = l + jnp.sum(pe, axis=1, keepdims=True)
            acc = acc + pe * vp_refs[p][rows, :]
        out_cols.append((jnp.sum(acc, axis=1, keepdims=True) + p_own * vn_col[rows, :]) / l)
    o_col = jnp.concatenate(out_cols, axis=0)
    o_ref[...] = jnp.where(on_b, o_col, o_ref[...])


def _moba_sample(qt, knt, vnt, cache_k, cache_v, page_table):
    w, b = qt.shape
    n_pages = page_table.shape[1]
    assert n_pages % PAGES_PER_BLOCK == 0 and b == LANE
    n_phys = cache_k.shape[0]
    to_pages = lambda c: jnp.transpose(c, (0, 2, 3, 1)).reshape(n_phys, w, PAGE_SIZE)
    ck, cv = to_pages(cache_k), to_pages(cache_v)
    whole = pl.BlockSpec((w, b), lambda bi, pt: (0, 0))
    page_spec = lambda p: pl.BlockSpec((None, w, PAGE_SIZE), lambda bi, pt, p=p: (pt[bi, p], 0, 0))
    grid_spec = pltpu.PrefetchScalarGridSpec(
        num_scalar_prefetch=1,
        grid=(b,),
        in_specs=[whole, whole, whole] + [page_spec(p) for p in range(n_pages)] * 2,
        out_specs=whole,
    )
    return pl.pallas_call(
        functools.partial(_moba_sample_kernel, n_pages=n_pages),
        grid_spec=grid_spec,
        out_shape=jax.ShapeDtypeStruct((w, b), F32),
        compiler_params=_params("arbitrary"),
    )(page_table, qt, knt, vnt, *([ck] * n_pages), *([cv] * n_pages))


def _mlstm_sample_kernel(q_ref, k_ref, v_ref, o_ref, g_ref, gain_ref, c0_ref, n0_ref, m0_ref,
                         mem_ref, c_ref, n_ref, m_ref):
    tb = q_ref.shape[0]
    g = g_ref[...]
    lane = lax.broadcasted_iota(jnp.int32, (DV_MLSTM, LANE), 1)
    zpad = jnp.zeros((LANE - tb, LANE), F32)
    for h in range(H_MLSTM):
        lanes = slice(h * DK_MLSTM, (h + 1) * DK_MLSTM)
        ig, lf, m0 = g[:, h:h + 1], g[:, H_MLSTM + h:H_MLSTM + h + 1], m0_ref[:, h:h + 1]
        q, k, v = q_ref[:, lanes], k_ref[:, lanes], v_ref[:, lanes]
        n0 = n0_ref[:, lanes]
        m_t = jnp.maximum(lf + m0, ig)
        w_inter = jnp.exp(lf + m0 - m_t)
        g_in = jnp.exp(ig - m_t)
        a = g_in * jnp.sum(q * k, axis=1, keepdims=True)
        den = w_inter * jnp.sum(n0 * q, axis=1, keepdims=True) + a
        v_t = jnp.concatenate([g_in * v, zpad], axis=0).T
        w_t = jnp.concatenate([jnp.broadcast_to(w_inter, (tb, LANE)), zpad], axis=0).T
        cq_t = jnp.zeros((DV_MLSTM, LANE), F32)
        for r in range(tb):
            c_prev = c0_ref[r, h]
            cq_t = jnp.where(lane == r, jnp.sum(c_prev * q[r:r + 1, :], axis=1, keepdims=True), cq_t)
            c_ref[r, h] = w_t[:, r:r + 1] * c_prev + v_t[:, r:r + 1] * k[r:r + 1, :]
        cq = cq_t.T[0:tb, :]
        hh = (w_inter * cq + a * v) / jnp.maximum(jnp.abs(den), jnp.exp(-m_t))
        mem_ref[:, lanes] = _mlstm_head_out(hh, gain_ref[:, lanes], o_ref[:, lanes]).astype(mem_ref.dtype)
        n_ref[:, lanes] = w_inter * n0 + g_in * k
        m_ref[:, h:h + 1] = m_t


def _mlstm_sample(mq, mk, mv, mo, gates, gain, c0, n0, m0):
    b, w = mq.shape
    tb = SUBLANE
    rows = lambda width: pl.BlockSpec((tb, width), lambda i: (i, 0))
    c_spec = pl.BlockSpec((tb, H_MLSTM, DV_MLSTM, DK_MLSTM), lambda i: (i, 0, 0, 0))
    return pl.pallas_call(
        _mlstm_sample_kernel,
        grid=(b // tb,),
        in_specs=[rows(w), rows(w), rows(w), rows(w), rows(LANE), pl.BlockSpec((1, w), lambda i: (0, 0)),
                  c_spec, rows(w), rows(H_MLSTM)],
        out_specs=[rows(w), c_spec, rows(w), rows(H_MLSTM)],
        out_shape=[jax.ShapeDtypeStruct((b, w), F32),
                   jax.ShapeDtypeStruct(c0.shape, F32),
                   jax.ShapeDtypeStruct((b, w), F32),
                   jax.ShapeDtypeStruct((b, H_MLSTM), F32)],
        compiler_params=_params("arbitrary"),
    )(mq, mk, mv, mo, gates, gain, c0, n0.reshape(b, w), m0)


def _finish_kernel(x_ref, att_ref, mem_ref, g1_ref, sh2_ref, sc2_ref, g2_ref, wo_ref, ln1g_ref, ln1b_ref,
                   wg_ref, wu_ref, wd_ref, ln2g_ref, ln2b_ref, y_ref, *, alpha, ff_chunk, att_transposed):
    att = att_ref[...].T if att_transposed else att_ref[...]
    aw = att.shape[1]
    mix = (jnp.dot(att.astype(BF16), wo_ref[0:aw, :], preferred_element_type=F32)
           + jnp.dot(mem_ref[...].astype(BF16), wo_ref[aw:, :], preferred_element_type=F32))
    x1 = _layernorm(alpha * x_ref[...] + (1.0 + g1_ref[...]) * mix, ln1g_ref[...], ln1b_ref[...])
    h2 = (x1 * (1.0 + sc2_ref[...]) + sh2_ref[...]).astype(BF16)
    f = jnp.zeros(x1.shape, F32)
    for c in range(wg_ref.shape[1] // ff_chunk):
        cols = slice(c * ff_chunk, (c + 1) * ff_chunk)
        gate = jnp.dot(h2, wg_ref[:, cols], preferred_element_type=F32)
        up = jnp.dot(h2, wu_ref[:, cols], preferred_element_type=F32)
        act = (gate * jax.nn.sigmoid(gate) * up).astype(BF16)
        f = f + jnp.dot(act, wd_ref[cols, :], preferred_element_type=F32)
    y_ref[...] = _layernorm(alpha * x1 + (1.0 + g2_ref[...]) * f, ln2g_ref[...], ln2b_ref[...])


def _finish(x, att, mem, mods, weights, tm, rows_per_mod, alpha, att_transposed=False):
    t, d = x.shape
    assert not att_transposed or tm == t
    w_out, ln1_g, ln1_b, w_gate, w_up, w_down, ln2_g, ln2_b = weights
    if rows_per_mod is None:
        mod_spec = pl.BlockSpec((tm, d), lambda i: (i, 0))
    else:
        per = rows_per_mod // tm
        mod_spec = pl.BlockSpec((None, 1, d), lambda i: (i // per, 0, 0))
    tok = lambda width: pl.BlockSpec((tm, width), lambda i: (i, 0))
    d_ff = w_gate.shape[1]
    ff_chunk = 256
    assert d_ff % ff_chunk == 0
    return pl.pallas_call(
        functools.partial(_finish_kernel, alpha=alpha, ff_chunk=ff_chunk, att_transposed=att_transposed),
        grid=(t // tm,),
        in_specs=[tok(d), pl.BlockSpec(att.shape, lambda i: (0, 0)) if att_transposed else tok(att.shape[1]),
                  tok(mem.shape[1]), mod_spec, mod_spec, mod_spec, mod_spec,
                  _const_spec(w_out.shape), _const_spec(ln1_g.shape), _const_spec(ln1_b.shape),
                  _const_spec(w_gate.shape), _const_spec(w_up.shape), _const_spec(w_down.shape),
                  _const_spec(ln2_g.shape), _const_spec(ln2_b.shape)],
        out_specs=tok(d),
        out_shape=jax.ShapeDtypeStruct((t, d), F32),
        compiler_params=_params("arbitrary"),
    )(x, att, mem, *mods, w_out, ln1_g, ln1_b, w_gate, w_up, w_down, ln2_g, ln2_b)


def kernel(x_prompt, x_sample, cache_k, cache_v, state_C, state_n, state_m, page_table, c_prompt, c_sample,
           w_ada, b_ada, w_in, b_if, mlstm_norm_g, w_out, ln1_g, ln1_b, w_gate, w_up, w_down, ln2_g, ln2_b):
    depth = w_in.shape[0]
    assert depth == 1, "single-layer step"
    alpha = (2.0 * depth) ** 0.25
    bp, s, d = x_prompt.shape
    bs = x_sample.shape[0]
    assert x_sample.shape[1] == 1, "single-token decode step"
    n_main = N_PROJ_GROUPS * PROJ_GROUP

    w_main = w_in[0, :, :n_main].astype(BF16)
    w_gates = jnp.pad(w_in[0, :, n_main:], ((0, 0), (0, LANE - 2 * H_MLSTM))).astype(BF16)
    b_gates = jnp.pad(b_if[0], (0, LANE - 2 * H_MLSTM)).reshape(1, LANE)
    gain = mlstm_norm_g[0].reshape(1, MLSTM_WIDTH)
    row = lambda a: a[0].reshape(1, -1)
    fin_w = (w_out[0].astype(BF16), row(ln1_g), row(ln1_b), w_gate[0].astype(BF16), w_up[0].astype(BF16),
             w_down[0].astype(BF16), row(ln2_g), row(ln2_b))

    c_all = jnp.concatenate([c_prompt, c_sample], axis=0)
    mod = _adaln(c_all, w_ada[0], b_ada[0])
    sh1, sc1, g1, sh2, sc2, g2 = (mod[:, i * d:(i + 1) * d] for i in range(6))
    pm = lambda a: a[:bp].reshape(bp, 1, d)
    sm = lambda a: a[bp:]

    xp = x_prompt.reshape(bp * s, d)
    tm = 512
    aq, ak, av, mq, mk, mv, mo, gates = _in_proj(
        xp, pm(sc1), pm(sh1), w_main, w_gates, b_gates, tm, s, (F32, F32, F32, BF16, BF16, BF16, F32))
    seq = lambda a: a.reshape(bp, s, a.shape[-1])
    att = _moba_prompt(seq(aq), seq(ak), seq(av))
    mem, c_p, n_p, m_p = _mlstm_prompt(seq(mq), seq(mk), seq(mv), seq(mo), seq(gates), gain)
    y_p = _finish(xp, att.reshape(bp * s, -1), mem.reshape(bp * s, -1), (pm(g1), pm(sh2), pm(sc2), pm(g2)),
                  fin_w, tm, s, alpha)

    xs = x_sample.reshape(bs, d)
    aq_t, ak_t, av_t, mq_s, mk_s, mv_s, mo_s, gates_s = _in_proj(
        xs, sm(sc1), sm(sh1), w_main, w_gates, b_gates, bs, None, (F32,) * N_PROJ_GROUPS, n_transposed=3)
    att_t = _moba_sample(aq_t, ak_t, av_t, cache_k[0], cache_v[0], page_table)
    mem_s, c_s, n_s, m_s = _mlstm_sample(mq_s, mk_s, mv_s, mo_s, gates_s, gain,
                                         state_C[0], state_n[0], state_m[0])
    y_s = _finish(xs, att_t, mem_s, (sm(g1), sm(sh2), sm(sc2), sm(g2)), fin_w, bs, None, alpha,
                  att_transposed=True)

    heads = lambda a, lead: a.reshape(1, *lead, H_ATT, DH_ATT)
    heads_t = lambda a: jnp.transpose(a.reshape(H_ATT, DH_ATT, bs), (2, 0, 1)).reshape(1, bs, 1, H_ATT, DH_ATT)
    return (y_p.reshape(bp, s, d), y_s.reshape(bs, 1, d),
            heads(ak, (bp, s)), heads(av, (bp, s)),
            c_p[None], n_p[None, :, :H_MLSTM, :], m_p[None, :, :H_MLSTM, 0],
            heads_t(ak_t), heads_t(av_t),
            c_s[None], n_s.reshape(1, bs, H_MLSTM, DK_MLSTM), m_s[None])
```

```python
import functools
import math

import jax
import jax.numpy as jnp
from jax import lax
from jax.experimental import pallas as pl
from jax.experimental.pallas import tpu as pltpu

F32 = jnp.float32
BF16 = jnp.bfloat16
HIGHEST = lax.Precision.HIGHEST

LANE = 128
SUBLANE = 8
VMEM_LIMIT_BYTES = 56 * 1024 * 1024

H_ATT = 8
DH_ATT = 64
ATT_WIDTH = H_ATT * DH_ATT
MOBA_BLOCK = 256
MOBA_TOPK = 3
H_MLSTM = 4
DK_MLSTM = 128
DV_MLSTM = 128
MLSTM_WIDTH = H_MLSTM * DV_MLSTM
MLSTM_CHUNK = LANE
PAGE_SIZE = 128
PAGES_PER_BLOCK = MOBA_BLOCK // PAGE_SIZE
LN_EPS = 1e-5
NEG = -1e30
LOG2E = math.log2(math.e)
N_PROJ_GROUPS = 7
PROJ_GROUP = 512
MK_GROUP = 4

_NT = (((1,), (1,)), ((), ()))
_TN = (((0,), (0,)), ((), ()))


def _params(*sem):
    return pltpu.CompilerParams(dimension_semantics=sem, vmem_limit_bytes=VMEM_LIMIT_BYTES)


def _const_spec(shape):
    return pl.BlockSpec(shape, lambda *_: (0,) * len(shape), pipeline_mode=pl.Buffered(1))


def _layernorm(x, g, b):
    mu = jnp.mean(x, axis=-1, keepdims=True)
    d = x - mu
    var = jnp.mean(d * d, axis=-1, keepdims=True)
    return d * lax.rsqrt(var + LN_EPS) * g + b


def _top_blocks(val, nidx):
    cnt = jnp.zeros(val.shape, jnp.int32)
    for r in range(1, SUBLANE):
        other = pltpu.roll(val, r, 0)
        oidx = pltpu.roll(nidx, r, 0)
        beats = (other > val) | ((other == val) & (oidx < nidx))
        cnt = cnt + jnp.where(beats, 1, 0)
    return cnt < MOBA_TOPK


def _adaln_kernel(c_ref, w_ref, b_ref, o_ref):
    c = c_ref[...]
    s = c * jax.nn.sigmoid(c)
    o_ref[...] = jnp.dot(s, w_ref[...], preferred_element_type=F32) + b_ref[...]


def _adaln(c, w_ada, b_ada):
    rows, d = c.shape
    n = w_ada.shape[1]
    tn = d
    return pl.pallas_call(
        _adaln_kernel,
        grid=(n // tn,),
        in_specs=[pl.BlockSpec((rows, d), lambda j: (0, 0)),
                  pl.BlockSpec((d, tn), lambda j: (0, j)),
                  pl.BlockSpec((1, tn), lambda j: (0, j))],
        out_specs=pl.BlockSpec((rows, tn), lambda j: (0, j)),
        out_shape=jax.ShapeDtypeStruct((rows, n), F32),
        compiler_params=_params("arbitrary"),
    )(c, w_ada, b_ada.reshape(1, n))


def _in_proj_kernel(x_ref, sc_ref, sh_ref, w_ref, wt_ref, wg_ref, bg_ref, *out_refs, transposed):
    proj_refs, g_ref = out_refs[:N_PROJ_GROUPS], out_refs[N_PROJ_GROUPS]
    h = (x_ref[...] * (1.0 + sc_ref[...]) + sh_ref[...]).astype(BF16)
    for gi, o_ref in enumerate(proj_refs):
        cols = slice(gi * PROJ_GROUP, (gi + 1) * PROJ_GROUP)
        if gi in transposed:
            y = lax.dot_general(wt_ref[cols, :], h, _NT, preferred_element_type=F32)
        else:
            y = jnp.dot(h, w_ref[:, cols], preferred_element_type=F32)
        if gi == MK_GROUP:
            y = y * (DK_MLSTM ** -0.5)
        o_ref[...] = y.astype(o_ref.dtype)
    g = jnp.dot(h, wg_ref[...], preferred_element_type=F32) + bg_ref[...]
    lane = lax.broadcasted_iota(jnp.int32, g.shape, 1)
    logsig = jnp.minimum(g, 0.0) - jnp.log1p(jnp.exp(-jnp.abs(g)))
    g_ref[...] = jnp.where(lane >= H_MLSTM, logsig, g)


def _in_proj(x, sc, sh, w_main, w_att_t, w_gate, b_gate, tm, rows_per_mod, out_dtypes, transposed=()):
    t, d = x.shape
    assert MK_GROUP not in transposed
    if rows_per_mod is None:
        mod_spec = pl.BlockSpec((tm, d), lambda i: (i, 0))
        t_shape, t_spec = (PROJ_GROUP, t), pl.BlockSpec((PROJ_GROUP, tm), lambda i: (0, i))
    else:
        per = rows_per_mod // tm
        mod_spec = pl.BlockSpec((None, 1, d), lambda i: (i // per, 0, 0))
        t_shape = (t // rows_per_mod, PROJ_GROUP, rows_per_mod)
        t_spec = pl.BlockSpec((None, PROJ_GROUP, tm), lambda i: (i // per, 0, i % per))
    out_shape, out_specs = [], []
    for gi, dt in enumerate(out_dtypes):
        if gi in transposed:
            out_shape.append(jax.ShapeDtypeStruct(t_shape, dt))
            out_specs.append(t_spec)
        else:
            out_shape.append(jax.ShapeDtypeStruct((t, PROJ_GROUP), dt))
            out_specs.append(pl.BlockSpec((tm, PROJ_GROUP), lambda i: (i, 0)))
    out_shape.append(jax.ShapeDtypeStruct((t, LANE), F32))
    out_specs.append(pl.BlockSpec((tm, LANE), lambda i: (i, 0)))
    return pl.pallas_call(
        functools.partial(_in_proj_kernel, transposed=tuple(transposed)),
        grid=(t // tm,),
        in_specs=[pl.BlockSpec((tm, d), lambda i: (i, 0)), mod_spec, mod_spec,
                  _const_spec(w_main.shape), _const_spec(w_att_t.shape),
                  _const_spec(w_gate.shape), _const_spec(b_gate.shape)],
        out_specs=out_specs,
        out_shape=out_shape,
        compiler_params=_params("arbitrary"),
    )(x, sc, sh, w_main, w_att_t, w_gate, b_gate)


def _moba_prompt_kernel(q_ref, kt_ref, vt_ref, o_ref, kaug_ref, vaug_ref, kmt_ref, lhs_sc, m_sc, acc_sc, *, nb):
    i = pl.program_id(1)
    blk = MOBA_BLOCK
    half = LANE // 2
    w = q_ref.shape[1]

    @pl.when(i == 0)
    def _prepare_batch():
        srow = lax.broadcasted_iota(jnp.int32, (LANE, blk), 0)
        in_lo = srow < half
        head_of_row = lax.broadcasted_iota(jnp.int32, (w, LANE), 0) // DH_ATT
        lane_w = lax.broadcasted_iota(jnp.int32, (w, LANE), 1)
        kmt = jnp.zeros((w, LANE), F32)
        for j in range(nb):
            ktj = kt_ref[:, j * blk:(j + 1) * blk]
            vtj = vt_ref[:, j * blk:(j + 1) * blk]
            col = jnp.mean(ktj, axis=1, keepdims=True)
            kmt = jnp.where((lane_w % SUBLANE == j) & (lane_w // SUBLANE == head_of_row), col, kmt)
            for p in range(H_ATT // 2):
                kp, vp = ktj[p * LANE:(p + 1) * LANE, :], vtj[p * LANE:(p + 1) * LANE, :]
                kaug_ref[2 * p, j] = jnp.where(in_lo, kp, jnp.where(srow == half + j, 1.0, 0.0)).astype(BF16)
                kaug_ref[2 * p + 1, j] = jnp.where(in_lo, jnp.where(srow == j, 1.0, 0.0), kp).astype(BF16)
                vaug_ref[2 * p, j] = jnp.where(in_lo, vp, 1.0).astype(BF16)
                vaug_ref[2 * p + 1, j] = jnp.where(in_lo, 1.0, vp).astype(BF16)
        km_hi = kmt.astype(BF16)
        kmt_ref[0] = km_hi
        kmt_ref[1] = (kmt - km_hi.astype(F32)).astype(BF16)

    q32 = q_ref[...]
    q_hi = q32.astype(BF16)
    q_lo = (q32 - q_hi.astype(F32)).astype(BF16)
    sc = (jnp.dot(q_hi, kmt_ref[0], preferred_element_type=F32)
          + (jnp.dot(q_hi, kmt_ref[1], preferred_element_type=F32)
             + jnp.dot(q_lo, kmt_ref[0], preferred_element_type=F32)))
    sc_t = sc.T
    nidx = lax.broadcasted_iota(jnp.int32, (SUBLANE, blk), 0)
    past = nidx < i
    biases = []
    for h in range(H_ATT):
        val = jnp.where(past, sc_t[h * SUBLANE:(h + 1) * SUBLANE, :], NEG)
        keep = (_top_blocks(val, nidx) & past) | (nidx == i)
        biases.append(jnp.where(keep, 0.0, NEG))

    lane = lax.broadcasted_iota(jnp.int32, (blk, LANE), 1)
    lo_lanes = lane < half
    zpad = jnp.zeros((half - SUBLANE, blk), F32)
    for p in range(H_ATT // 2):
        bias_p = jnp.concatenate([biases[2 * p + 1], zpad, biases[2 * p], zpad], axis=0).T
        qp = q32[:, p * LANE:(p + 1) * LANE] * (DH_ATT ** -0.5 * LOG2E)
        lhs_sc[2 * p] = jnp.where(lo_lanes, qp, bias_p).astype(BF16)
        lhs_sc[2 * p + 1] = jnp.where(lo_lanes, bias_p, qp).astype(BF16)

    def scores(h, j):
        return jnp.dot(lhs_sc[h], kaug_ref[h, j], preferred_element_type=F32)

    def row_max(s):
        return jnp.broadcast_to(jnp.max(s, axis=1, keepdims=True), (blk, LANE))

    def weights(s, m):
        return jnp.exp2(s - jnp.concatenate([m, m], axis=1)).astype(BF16)

    row = lax.broadcasted_iota(jnp.int32, (blk, blk), 0)
    col = lax.broadcasted_iota(jnp.int32, (blk, blk), 1)
    causal = col <= row
    for h in range(H_ATT):
        s = jnp.where(causal, scores(h, i), NEG)
        m = row_max(s)
        acc_sc[h] = lax.dot_general(weights(s, m), vaug_ref[h, i], _NT, preferred_element_type=F32)
        m_sc[h] = m

    def past_block(j, carry):
        for h in range(H_ATT):
            s = scores(h, j)
            m_old = m_sc[h]
            m_new = jnp.maximum(m_old, row_max(s))
            acc_sc[h] = (jnp.exp2(m_old - m_new) * acc_sc[h]
                         + lax.dot_general(weights(s, m_new), vaug_ref[h, j], _NT, preferred_element_type=F32))
            m_sc[h] = m_new
        return carry

    lax.fori_loop(0, i, past_block, 0)

    for p in range(H_ATT // 2):
        acc_e, acc_o = acc_sc[2 * p], acc_sc[2 * p + 1]
        num = jnp.where(lo_lanes, acc_e, acc_o)
        den = pltpu.roll(jnp.where(lo_lanes, acc_o, acc_e), half, 1)
        o_ref[:, p * LANE:(p + 1) * LANE] = (num / den).astype(o_ref.dtype)


def _moba_prompt(q, kt, vt):
    b, s, w = q.shape
    nb = s // MOBA_BLOCK
    assert s % MOBA_BLOCK == 0 and nb <= SUBLANE and w == ATT_WIDTH
    blk_state = lambda dt: pltpu.VMEM((H_ATT, MOBA_BLOCK, LANE), dt)
    return pl.pallas_call(
        functools.partial(_moba_prompt_kernel, nb=nb),
        grid=(b, nb),
        in_specs=[pl.BlockSpec((None, MOBA_BLOCK, w), lambda bi, i: (bi, i, 0)),
                  pl.BlockSpec((None, w, s), lambda bi, i: (bi, 0, 0)),
                  pl.BlockSpec((None, w, s), lambda bi, i: (bi, 0, 0))],
        out_specs=pl.BlockSpec((None, MOBA_BLOCK, w), lambda bi, i: (bi, i, 0)),
        out_shape=jax.ShapeDtypeStruct((b, s, w), BF16),
        scratch_shapes=[pltpu.VMEM((H_ATT, nb, LANE, MOBA_BLOCK), BF16),
                        pltpu.VMEM((H_ATT, nb, LANE, MOBA_BLOCK), BF16),
                        pltpu.VMEM((2, w, LANE), BF16),
                        blk_state(BF16), blk_state(F32), blk_state(F32)],
        compiler_params=_params("arbitrary", "arbitrary"),
    )(q, kt, vt)


def _mlstm_head_out(hh, gain, ogate):
    mu = jnp.mean(hh, axis=-1, keepdims=True)
    d = hh - mu
    var = jnp.mean(d * d, axis=-1, keepdims=True)
    return d * lax.rsqrt(var + LN_EPS) * gain * jax.nn.sigmoid(ogate)


def _mlstm_prompt_kernel(q_ref, k_ref, v_ref, o_ref, g_ref, gain_ref,
                         mem_ref, c_out, n_out, m_out, c_sc, n_sc, m_sc):
    c = pl.program_id(1)
    L = q_ref.shape[0]

    @pl.when(c == 0)
    def _reset_state():
        c_sc[...] = jnp.zeros_like(c_sc)
        n_sc[...] = jnp.zeros_like(n_sc)
        m_sc[...] = jnp.zeros_like(m_sc)

    g = g_ref[...]
    g_t = g.T
    row = lax.broadcasted_iota(jnp.int32, (L, L), 0)
    col = lax.broadcasted_iota(jnp.int32, (L, L), 1)
    causal = col <= row
    lower = jnp.where(causal, 1.0, 0.0)
    upper = jnp.where(row <= col, 1.0, 0.0)
    b_col_all = jnp.dot(lower, g, precision=HIGHEST, preferred_element_type=F32)
    b_row_all = jnp.dot(g_t[0:SUBLANE, :], upper, precision=HIGHEST, preferred_element_type=F32)
    ones = jnp.ones((L, DV_MLSTM), BF16)

    for h in range(H_MLSTM):
        lanes = slice(h * DK_MLSTM, (h + 1) * DK_MLSTM)
        ig_row = g_t[h:h + 1, :]
        b_row = b_row_all[H_MLSTM + h:H_MLSTM + h + 1, :]
        ig = jnp.broadcast_to(g[:, h:h + 1], (L, LANE))
        b = jnp.broadcast_to(b_col_all[:, H_MLSTM + h:H_MLSTM + h + 1], (L, LANE))
        m_prev = m_sc[h:h + 1, :]
        dmat = jnp.where(causal, b - b_row + ig_row, NEG)
        m_inter = b + m_prev
        m_t = jnp.maximum(m_inter, jnp.broadcast_to(jnp.max(dmat, axis=1, keepdims=True), (L, LANE)))
        w_inter = jnp.exp(m_inter - m_t)
        qh, kh, vh = q_ref[:, lanes], k_ref[:, lanes], v_ref[:, lanes]
        a = jnp.exp(dmat - m_t) * lax.dot_general(qh, kh, _NT, preferred_element_type=F32)
        c_prev = c_sc[h]
        n_prev = n_sc[h:h + 1, :]
        state = jnp.concatenate([c_prev, jnp.broadcast_to(n_prev, (DV_MLSTM, DK_MLSTM))], axis=0).astype(BF16)
        num_den = (jnp.concatenate([w_inter, w_inter], axis=1)
                   * lax.dot_general(qh, state, _NT, preferred_element_type=F32)
                   + jnp.dot(a.astype(BF16), jnp.concatenate([vh, ones], axis=1), preferred_element_type=F32))
        hh = num_den[:, :DV_MLSTM] / jnp.maximum(jnp.abs(num_den[:, DV_MLSTM:]), jnp.exp(-m_t))
        mem_ref[:, lanes] = _mlstm_head_out(hh, gain_ref[:, lanes], o_ref[:, lanes].astype(F32)).astype(mem_ref.dtype)

        m_new = m_t[L - 1:L, :]
        b_last = b[L - 1:L, :]
        g_inter = jnp.exp(b_last + m_prev - m_new)
        g_in = jnp.exp(b_last - b + ig - m_new)
        v_scaled = (vh.astype(F32) * g_in).astype(BF16)
        c_sc[h] = g_inter * c_prev + lax.dot_general(v_scaled, kh, _TN, preferred_element_type=F32)
        n_sc[h:h + 1, :] = g_inter * n_prev + jnp.sum(kh.astype(F32) * g_in, axis=0, keepdims=True)
        m_sc[h:h + 1, :] = m_new

    @pl.when(c == pl.num_programs(1) - 1)
    def _emit_state():
        c_out[...] = c_sc[...]
        n_out[...] = n_sc[...]
        m_out[...] = m_sc[...]


def _mlstm_prompt(mq, mk, mv, mo, gates, gain):
    b, s, w = mq.shape
    L = MLSTM_CHUNK
    assert s % L == 0 and w == MLSTM_WIDTH and L == LANE == DK_MLSTM == DV_MLSTM
    tok = lambda width: pl.BlockSpec((None, L, width), lambda bi, c: (bi, c, 0))
    state = lambda *dims: pl.BlockSpec((None,) + dims, lambda bi, c: (bi,) + (0,) * len(dims))
    return pl.pallas_call(
        _mlstm_prompt_kernel,
        grid=(b, s // L),
        in_specs=[tok(w), tok(w), tok(w), tok(w), tok(LANE), pl.BlockSpec((1, w), lambda bi, c: (0, 0))],
        out_specs=[tok(w), state(H_MLSTM, DV_MLSTM, DK_MLSTM), state(SUBLANE, LANE), state(SUBLANE, LANE)],
        out_shape=[jax.ShapeDtypeStruct((b, s, w), BF16),
                   jax.ShapeDtypeStruct((b, H_MLSTM, DV_MLSTM, DK_MLSTM), F32),
                   jax.ShapeDtypeStruct((b, SUBLANE, LANE), F32),
                   jax.ShapeDtypeStruct((b, SUBLANE, LANE), F32)],
        scratch_shapes=[pltpu.VMEM((H_MLSTM, DV_MLSTM, DK_MLSTM), F32),
                        pltpu.VMEM((SUBLANE, LANE), F32),
                        pltpu.VMEM((SUBLANE, LANE), F32)],
        compiler_params=_params("arbitrary", "arbitrary"),
    )(mq, mk, mv, mo, gates, gain)


def _moba_sample_kernel(pt_ref, qt_ref, knt_ref, vnt_ref, *rest, n_pages):
    del pt_ref
    kp_refs, vp_refs, o_ref = rest[:n_pages], rest[n_pages:2 * n_pages], rest[2 * n_pages]
    b = pl.program_id(0)
    n_blocks = n_pages // PAGES_PER_BLOCK
    w = qt_ref.shape[0]
    on_b = lax.broadcasted_iota(jnp.int32, (w, LANE), 1) == b

    @pl.when(b == 0)
    def _init_out():
        o_ref[...] = jnp.zeros_like(o_ref)

    def column(ref):
        return jnp.sum(jnp.where(on_b, ref[...], 0.0), axis=1, keepdims=True)

    q_col = column(qt_ref) * (DH_ATT ** -0.5)
    kn_col, vn_col = column(knt_ref), column(vnt_ref)
    q_wide = jnp.broadcast_to(q_col, (w, LANE))

    out_cols = []
    for h in range(H_ATT):
        rows = slice(h * DH_ATT, (h + 1) * DH_ATT)
        s_pages = [jnp.sum(kp_refs[p][rows, :] * q_wide[rows, :], axis=0, keepdims=True)
                   for p in range(n_pages)]
        blk = []
        for n in range(n_blocks):
            tot = sum(s_pages[n * PAGES_PER_BLOCK:(n + 1) * PAGES_PER_BLOCK])
            blk.append(jnp.sum(tot, axis=1, keepdims=True))
        sel = []
        for n in range(n_blocks):
            cnt = jnp.zeros((1, 1), jnp.int32)
            for o in range(n_blocks):
                if o != n:
                    beats = (blk[o] >= blk[n]) if o < n else (blk[o] > blk[n])
                    cnt = cnt + jnp.where(beats, 1, 0)
            sel.append(cnt < MOBA_TOPK)
        s_own = jnp.sum(q_col[rows, :] * kn_col[rows, :], axis=0, keepdims=True)
        m = s_own
        for p in range(n_pages):
            page_max = jnp.max(s_pages[p], axis=1, keepdims=True)
            m = jnp.maximum(m, jnp.where(sel[p // PAGES_PER_BLOCK], page_max, NEG))
        p_own = jnp.exp(s_own - m)
        l = p_own
        acc = jnp.zeros((DH_ATT, LANE), F32)
        for p in range(n_pages):
            pe = jnp.where(sel[p // PAGES_PER_BLOCK], jnp.exp(s_pages[p] - m), 0.0)
            l = l + jnp.sum(pe, axis=1, keepdims=True)
            acc = acc + pe * vp_refs[p][rows, :]
        out_cols.append((jnp.sum(acc, axis=1, keepdims=True) + p_own * vn_col[rows, :]) / l)
    o_col = jnp.concatenate(out_cols, axis=0)
    o_ref[...] = jnp.where(on_b, o_col, o_ref[...])


def _moba_sample(qt, knt, vnt, cache_k, cache_v, page_table):
    w, b = qt.shape
    n_pages = page_table.shape[1]
    assert n_pages % PAGES_PER_BLOCK == 0 and b == LANE
    n_phys = cache_k.shape[0]
    to_pages = lambda c: jnp.transpose(c, (0, 2, 3, 1)).reshape(n_phys, w, PAGE_SIZE)
    ck, cv = to_pages(cache_k), to_pages(cache_v)
    whole = pl.BlockSpec((w, b), lambda bi, pt: (0, 0))
    page_spec = lambda p: pl.BlockSpec((None, w, PAGE_SIZE), lambda bi, pt, p=p: (pt[bi, p], 0, 0))
    grid_spec = pltpu.PrefetchScalarGridSpec(
        num_scalar_prefetch=1,
        grid=(b,),
        in_specs=[whole, whole, whole] + [page_spec(p) for p in range(n_pages)] * 2,
        out_specs=whole,
    )
    return pl.pallas_call(
        functools.partial(_moba_sample_kernel, n_pages=n_pages),
        grid_spec=grid_spec,
        out_shape=jax.ShapeDtypeStruct((w, b), F32),
        compiler_params=_params("arbitrary"),
    )(page_table, qt, knt, vnt, *([ck] * n_pages), *([cv] * n_pages))


def _mlstm_sample_kernel(q_ref, k_ref, v_ref, o_ref, g_ref, gain_ref, c0_ref, n0_ref, m0_ref,
                         mem_ref, c_ref, n_ref, m_ref):
    tb = q_ref.shape[0]
    g = g_ref[...]
    lane = lax.broadcasted_iota(jnp.int32, (DV_MLSTM, LANE), 1)
    zpad = jnp.zeros((LANE - tb, LANE), F32)
    for h in range(H_MLSTM):
        lanes = slice(h * DK_MLSTM, (h + 1) * DK_MLSTM)
        ig, lf, m0 = g[:, h:h + 1], g[:, H_MLSTM + h:H_MLSTM + h + 1], m0_ref[:, h:h + 1]
        q, k, v = q_ref[:, lanes], k_ref[:, lanes], v_ref[:, lanes]
        n0 = n0_ref[:, lanes]
        m_t = jnp.maximum(lf + m0, ig)
        w_inter = jnp.exp(lf + m0 - m_t)
        g_in = jnp.exp(ig - m_t)
        a = g_in * jnp.sum(q * k, axis=1, keepdims=True)
        den = w_inter * jnp.sum(n0 * q, axis=1, keepdims=True) + a
        v_t = jnp.concatenate([g_in * v, zpad], axis=0).T
        w_t = jnp.concatenate([jnp.broadcast_to(w_inter, (tb, LANE)), zpad], axis=0).T
        cq_t = jnp.zeros((DV_MLSTM, LANE), F32)
        for r in range(tb):
            c_prev = c0_ref[r, h]
            cq_t = jnp.where(lane == r, jnp.sum(c_prev * q[r:r + 1, :], axis=1, keepdims=True), cq_t)
            c_ref[r, h] = w_t[:, r:r + 1] * c_prev + v_t[:, r:r + 1] * k[r:r + 1, :]
        cq = cq_t.T[0:tb, :]
        hh = (w_inter * cq + a * v) / jnp.maximum(jnp.abs(den), jnp.exp(-m_t))
        mem_ref[:, lanes] = _mlstm_head_out(hh, gain_ref[:, lanes], o_ref[:, lanes]).astype(mem_ref.dtype)
        n_ref[:, lanes] = w_inter * n0 + g_in * k
        m_ref[:, h:h + 1] = m_t


def _mlstm_sample(mq, mk, mv, mo, gates, gain, c0, n0, m0):
    b, w = mq.shape
    tb = SUBLANE
    rows = lambda width: pl.BlockSpec((tb, width), lambda i: (i, 0))
    c_spec = pl.BlockSpec((tb, H_MLSTM, DV_MLSTM, DK_MLSTM), lambda i: (i, 0, 0, 0))
    return pl.pallas_call(
        _mlstm_sample_kernel,
        grid=(b // tb,),
        in_specs=[rows(w), rows(w), rows(w), rows(w), rows(LANE), pl.BlockSpec((1, w), lambda i: (0, 0)),
                  c_spec, rows(w), rows(H_MLSTM)],
        out_specs=[rows(w), c_spec, rows(w), rows(H_MLSTM)],
        out_shape=[jax.ShapeDtypeStruct((b, w), F32),
                   jax.ShapeDtypeStruct(c0.shape, F32),
                   jax.ShapeDtypeStruct((b, w), F32),
                   jax.ShapeDtypeStruct((b, H_MLSTM), F32)],
        compiler_params=_params("arbitrary"),
    )(mq, mk, mv, mo, gates, gain, c0, n0.reshape(b, w), m0)


def _finish_kernel(x_ref, att_ref, mem_ref, g1_ref, sh2_ref, sc2_ref, g2_ref, wo_ref, ln1g_ref, ln1b_ref,
                   wg_ref, wu_ref, wd_ref, ln2g_ref, ln2b_ref, y_ref, *, alpha, ff_chunk, att_transposed):
    att = att_ref[...].T if att_transposed else att_ref[...]
    aw = att.shape[1]
    mix = (jnp.dot(att.astype(BF16), wo_ref[0:aw, :], preferred_element_type=F32)
           + jnp.dot(mem_ref[...].astype(BF16), wo_ref[aw:, :], preferred_element_type=F32))
    x1 = _layernorm(alpha * x_ref[...] + (1.0 + g1_ref[...]) * mix, ln1g_ref[...], ln1b_ref[...])
    h2 = (x1 * (1.0 + sc2_ref[...]) + sh2_ref[...]).astype(BF16)
    f = jnp.zeros(x1.shape, F32)
    for c in range(wg_ref.shape[1] // ff_chunk):
        cols = slice(c * ff_chunk, (c + 1) * ff_chunk)
        gate = jnp.dot(h2, wg_ref[:, cols], preferred_element_type=F32)
        up = jnp.dot(h2, wu_ref[:, cols], preferred_element_type=F32)
        act = (gate * jax.nn.sigmoid(gate) * up).astype(BF16)
        f = f + jnp.dot(act, wd_ref[cols, :], preferred_element_type=F32)
    y_ref[...] = _layernorm(alpha * x1 + (1.0 + g2_ref[...]) * f, ln2g_ref[...], ln2b_ref[...])


def _finish(x, att, mem, mods, weights, tm, rows_per_mod, alpha, att_transposed=False):
    t, d = x.shape
    assert not att_transposed or tm == t
    w_out, ln1_g, ln1_b, w_gate, w_up, w_down, ln2_g, ln2_b = weights
    if rows_per_mod is None:
        mod_spec = pl.BlockSpec((tm, d), lambda i: (i, 0))
    else:
        per = rows_per_mod // tm
        mod_spec = pl.BlockSpec((None, 1, d), lambda i: (i // per, 0, 0))
    tok = lambda width: pl.BlockSpec((tm, width), lambda i: (i, 0))
    d_ff = w_gate.shape[1]
    ff_chunk = 256
    assert d_ff % ff_chunk == 0
    return pl.pallas_call(
        functools.partial(_finish_kernel, alpha=alpha, ff_chunk=ff_chunk, att_transposed=att_transposed),
        grid=(t // tm,),
        in_specs=[tok(d), pl.BlockSpec(att.shape, lambda i: (0, 0)) if att_transposed else tok(att.shape[1]),
                  tok(mem.shape[1]), mod_spec, mod_spec, mod_spec, mod_spec,
                  _const_spec(w_out.shape), _const_spec(ln1_g.shape), _const_spec(ln1_b.shape),
                  _const_spec(w_gate.shape), _const_spec(w_up.shape), _const_spec(w_down.shape),
                  _const_spec(ln2_g.shape), _const_spec(ln2_b.shape)],
        out_specs=tok(d),
        out_shape=jax.ShapeDtypeStruct((t, d), F32),
        compiler_params=_params("arbitrary"),
    )(x, att, mem, *mods, w_out, ln1_g, ln1_b, w_gate, w_up, w_down, ln2_g, ln2_b)


def kernel(x_prompt, x_sample, cache_k, cache_v, state_C, state_n, state_m, page_table, c_prompt, c_sample,
           w_ada, b_ada, w_in, b_if, mlstm_norm_g, w_out, ln1_g, ln1_b, w_gate, w_up, w_down, ln2_g, ln2_b):
    depth = w_in.shape[0]
    assert depth == 1, "single-layer step"
    alpha = (2.0 * depth) ** 0.25
    bp, s, d = x_prompt.shape
    bs = x_sample.shape[0]
    assert x_sample.shape[1] == 1, "single-token decode step"
    n_main = N_PROJ_GROUPS * PROJ_GROUP

    w_main = w_in[0, :, :n_main].astype(BF16)
    w_att_t = w_in[0, :, :3 * PROJ_GROUP].T.astype(BF16)
    w_gates = jnp.pad(w_in[0, :, n_main:], ((0, 0), (0, LANE - 2 * H_MLSTM))).astype(BF16)
    b_gates = jnp.pad(b_if[0], (0, LANE - 2 * H_MLSTM)).reshape(1, LANE)
    gain = mlstm_norm_g[0].reshape(1, MLSTM_WIDTH)
    row = lambda a: a[0].reshape(1, -1)
    fin_w = (w_out[0].astype(BF16), row(ln1_g), row(ln1_b), w_gate[0].astype(BF16), w_up[0].astype(BF16),
             w_down[0].astype(BF16), row(ln2_g), row(ln2_b))

    c_all = jnp.concatenate([c_prompt, c_sample], axis=0)
    mod = _adaln(c_all, w_ada[0], b_ada[0])
    sh1, sc1, g1, sh2, sc2, g2 = (mod[:, i * d:(i + 1) * d] for i in range(6))
    pm = lambda a: a[:bp].reshape(bp, 1, d)
    sm = lambda a: a[bp:]

    xp = x_prompt.reshape(bp * s, d)
    tm = 512
    aq, ak_t, av_t, mq, mk, mv, mo, gates = _in_proj(
        xp, pm(sc1), pm(sh1), w_main, w_att_t, w_gates, b_gates, tm, s,
        (F32, F32, F32, BF16, BF16, BF16, F32), transposed=(1, 2))
    seq = lambda a: a.reshape(bp, s, a.shape[-1])
    att = _moba_prompt(seq(aq), ak_t, av_t)
    mem, c_p, n_p, m_p = _mlstm_prompt(seq(mq), seq(mk), seq(mv), seq(mo), seq(gates), gain)
    y_p = _finish(xp, att.reshape(bp * s, -1), mem.reshape(bp * s, -1), (pm(g1), pm(sh2), pm(sc2), pm(g2)),
                  fin_w, tm, s, alpha)

    xs = x_sample.reshape(bs, d)
    aq_s, ak_s, av_s, mq_s, mk_s, mv_s, mo_s, gates_s = _in_proj(
        xs, sm(sc1), sm(sh1), w_main, w_att_t, w_gates, b_gates, bs, None, (F32,) * N_PROJ_GROUPS,
        transposed=(0, 1, 2))
    att_s = _moba_sample(aq_s, ak_s, av_s, cache_k[0], cache_v[0], page_table)
    mem_s, c_s, n_s, m_s = _mlstm_sample(mq_s, mk_s, mv_s, mo_s, gates_s, gain,
                                         state_C[0], state_n[0], state_m[0])
    y_s = _finish(xs, att_s, mem_s, (sm(g1), sm(sh2), sm(sc2), sm(g2)), fin_w, bs, None, alpha,
                  att_transposed=True)

    rows_p = lambda a: jnp.transpose(a.reshape(bp, H_ATT, DH_ATT, s), (0, 3, 1, 2))[None]
    rows_s = lambda a: jnp.transpose(a.reshape(H_ATT, DH_ATT, bs), (2, 0, 1)).reshape(1, bs, 1, H_ATT, DH_ATT)
    return (y_p.reshape(bp, s, d), y_s.reshape(bs, 1, d),
            rows_p(ak_t), rows_p(av_t),
            c_p[None], n_p[None, :, :H_MLSTM, :], m_p[None, :, :H_MLSTM, 0],
            rows_s(ak_s), rows_s(av_s),
            c_s[None], n_s.reshape(1, bs, H_MLSTM, DK_MLSTM), m_s[None])
```

```python
import functools
import math

import jax
import jax.numpy as jnp
from jax import lax
from jax.experimental import pallas as pl
from jax.experimental.pallas import tpu as pltpu

F32 = jnp.float32
BF16 = jnp.bfloat16
HIGHEST = lax.Precision.HIGHEST

LANE = 128
SUBLANE = 8
VMEM_LIMIT_BYTES = 56 * 1024 * 1024

H_ATT = 8
DH_ATT = 64
ATT_WIDTH = H_ATT * DH_ATT
MOBA_BLOCK = 256
MOBA_TOPK = 3
H_MLSTM = 4
DK_MLSTM = 128
DV_MLSTM = 128
MLSTM_WIDTH = H_MLSTM * DV_MLSTM
MLSTM_CHUNK = LANE
PAGE_SIZE = 128
PAGES_PER_BLOCK = MOBA_BLOCK // PAGE_SIZE
LN_EPS = 1e-5
NEG = -1e30
LOG2E = math.log2(math.e)
N_PROJ_GROUPS = 7
PROJ_GROUP = 512
MK_GROUP = 4

_NT = (((1,), (1,)), ((), ()))
_TN = (((0,), (0,)), ((), ()))


def _params(*sem):
    return pltpu.CompilerParams(dimension_semantics=sem, vmem_limit_bytes=VMEM_LIMIT_BYTES)


def _const_spec(shape):
    return pl.BlockSpec(shape, lambda *_: (0,) * len(shape), pipeline_mode=pl.Buffered(1))


def _layernorm(x, g, b):
    mu = jnp.mean(x, axis=-1, keepdims=True)
    d = x - mu
    var = jnp.mean(d * d, axis=-1, keepdims=True)
    return d * lax.rsqrt(var + LN_EPS) * g + b


def _top_blocks(val, nidx):
    cnt = jnp.zeros(val.shape, jnp.int32)
    for r in range(1, SUBLANE):
        other = pltpu.roll(val, r, 0)
        oidx = pltpu.roll(nidx, r, 0)
        beats = (other > val) | ((other == val) & (oidx < nidx))
        cnt = cnt + jnp.where(beats, 1, 0)
    return cnt < MOBA_TOPK


def _adaln_kernel(c_ref, w_ref, b_ref, o_ref):
    c = c_ref[...]
    s = c * jax.nn.sigmoid(c)
    o_ref[...] = jnp.dot(s, w_ref[...], preferred_element_type=F32) + b_ref[...]


def _adaln(c, w_ada, b_ada):
    rows, d = c.shape
    n = w_ada.shape[1]
    tn = d
    return pl.pallas_call(
        _adaln_kernel,
        grid=(n // tn,),
        in_specs=[pl.BlockSpec((rows, d), lambda j: (0, 0)),
                  pl.BlockSpec((d, tn), lambda j: (0, j)),
                  pl.BlockSpec((1, tn), lambda j: (0, j))],
        out_specs=pl.BlockSpec((rows, tn), lambda j: (0, j)),
        out_shape=jax.ShapeDtypeStruct((rows, n), F32),
        compiler_params=_params("arbitrary"),
    )(c, w_ada, b_ada.reshape(1, n))


def _in_proj_kernel(x_ref, sc_ref, sh_ref, w_ref, wt_ref, wg_ref, bg_ref, *out_refs, transposed):
    proj_refs, g_ref = out_refs[:N_PROJ_GROUPS], out_refs[N_PROJ_GROUPS]
    h = (x_ref[...] * (1.0 + sc_ref[...]) + sh_ref[...]).astype(BF16)
    for gi, o_ref in enumerate(proj_refs):
        cols = slice(gi * PROJ_GROUP, (gi + 1) * PROJ_GROUP)
        if gi in transposed:
            y = lax.dot_general(wt_ref[cols, :], h, _NT, preferred_element_type=F32)
        else:
            y = jnp.dot(h, w_ref[:, cols], preferred_element_type=F32)
        if gi == MK_GROUP:
            y = y * (DK_MLSTM ** -0.5)
        o_ref[...] = y.astype(o_ref.dtype)
    g = jnp.dot(h, wg_ref[...], preferred_element_type=F32) + bg_ref[...]
    lane = lax.broadcasted_iota(jnp.int32, g.shape, 1)
    logsig = jnp.minimum(g, 0.0) - jnp.log1p(jnp.exp(-jnp.abs(g)))
    g_ref[...] = jnp.where(lane >= H_MLSTM, logsig, g)


def _in_proj(x, sc, sh, w_main, w_att_t, w_gate, b_gate, tm, rows_per_mod, out_dtypes, transposed=()):
    t, d = x.shape
    assert MK_GROUP not in transposed
    if rows_per_mod is None:
        mod_spec = pl.BlockSpec((tm, d), lambda i: (i, 0))
        t_shape, t_spec = (PROJ_GROUP, t), pl.BlockSpec((PROJ_GROUP, tm), lambda i: (0, i))
    else:
        per = rows_per_mod // tm
        mod_spec = pl.BlockSpec((None, 1, d), lambda i: (i // per, 0, 0))
        t_shape = (t // rows_per_mod, PROJ_GROUP, rows_per_mod)
        t_spec = pl.BlockSpec((None, PROJ_GROUP, tm), lambda i: (i // per, 0, i % per))
    out_shape, out_specs = [], []
    for gi, dt in enumerate(out_dtypes):
        if gi in transposed:
            out_shape.append(jax.ShapeDtypeStruct(t_shape, dt))
            out_specs.append(t_spec)
        else:
            out_shape.append(jax.ShapeDtypeStruct((t, PROJ_GROUP), dt))
            out_specs.append(pl.BlockSpec((tm, PROJ_GROUP), lambda i: (i, 0)))
    out_shape.append(jax.ShapeDtypeStruct((t, LANE), F32))
    out_specs.append(pl.BlockSpec((tm, LANE), lambda i: (i, 0)))
    return pl.pallas_call(
        functools.partial(_in_proj_kernel, transposed=tuple(transposed)),
        grid=(t // tm,),
        in_specs=[pl.BlockSpec((tm, d), lambda i: (i, 0)), mod_spec, mod_spec,
                  _const_spec(w_main.shape), _const_spec(w_att_t.shape),
                  _const_spec(w_gate.shape), _const_spec(b_gate.shape)],
        out_specs=out_specs,
        out_shape=out_shape,
        compiler_params=_params("arbitrary"),
    )(x, sc, sh, w_main, w_att_t, w_gate, b_gate)


def _moba_prompt_kernel(q_ref, kt_ref, vt_ref, o_ref, kaug_ref, vaug_ref, kmt_ref, lhs_sc, m_sc, acc_sc, *, nb):
    i = pl.program_id(1)
    blk = MOBA_BLOCK
    half = LANE // 2
    w = q_ref.shape[1]

    @pl.when(i == 0)
    def _prepare_batch():
        srow = lax.broadcasted_iota(jnp.int32, (LANE, blk), 0)
        in_lo = srow < half
        head_of_row = lax.broadcasted_iota(jnp.int32, (w, LANE), 0) // DH_ATT
        lane_w = lax.broadcasted_iota(jnp.int32, (w, LANE), 1)
        kmt = jnp.zeros((w, LANE), F32)
        for j in range(nb):
            ktj = kt_ref[:, j * blk:(j + 1) * blk]
            vtj = vt_ref[:, j * blk:(j + 1) * blk]
            col = jnp.mean(ktj, axis=1, keepdims=True)
            kmt = jnp.where((lane_w % SUBLANE == j) & (lane_w // SUBLANE == head_of_row), col, kmt)
            for p in range(H_ATT // 2):
                kp, vp = ktj[p * LANE:(p + 1) * LANE, :], vtj[p * LANE:(p + 1) * LANE, :]
                kaug_ref[2 * p, j] = jnp.where(in_lo, kp, jnp.where(srow == half + j, 1.0, 0.0)).astype(BF16)
                kaug_ref[2 * p + 1, j] = jnp.where(in_lo, jnp.where(srow == j, 1.0, 0.0), kp).astype(BF16)
                vaug_ref[2 * p, j] = jnp.where(in_lo, vp, 1.0).astype(BF16)
                vaug_ref[2 * p + 1, j] = jnp.where(in_lo, 1.0, vp).astype(BF16)
        km_hi = kmt.astype(BF16)
        kmt_ref[0] = km_hi
        kmt_ref[1] = (kmt - km_hi.astype(F32)).astype(BF16)

    q32 = q_ref[...]
    q_hi = q32.astype(BF16)
    q_lo = (q32 - q_hi.astype(F32)).astype(BF16)
    sc = (jnp.dot(q_hi, kmt_ref[0], preferred_element_type=F32)
          + (jnp.dot(q_hi, kmt_ref[1], preferred_element_type=F32)
             + jnp.dot(q_lo, kmt_ref[0], preferred_element_type=F32)))
    sc_t = sc.T
    nidx = lax.broadcasted_iota(jnp.int32, (SUBLANE, blk), 0)
    past = nidx < i
    biases = []
    for h in range(H_ATT):
        val = jnp.where(past, sc_t[h * SUBLANE:(h + 1) * SUBLANE, :], NEG)
        keep = (_top_blocks(val, nidx) & past) | (nidx == i)
        biases.append(jnp.where(keep, 0.0, NEG))

    lane = lax.broadcasted_iota(jnp.int32, (blk, LANE), 1)
    lo_lanes = lane < half
    zpad = jnp.zeros((half - SUBLANE, blk), F32)
    for p in range(H_ATT // 2):
        bias_p = jnp.concatenate([biases[2 * p + 1], zpad, biases[2 * p], zpad], axis=0).T
        qp = q32[:, p * LANE:(p + 1) * LANE] * (DH_ATT ** -0.5 * LOG2E)
        lhs_sc[2 * p] = jnp.where(lo_lanes, qp, bias_p).astype(BF16)
        lhs_sc[2 * p + 1] = jnp.where(lo_lanes, bias_p, qp).astype(BF16)

    def scores(h, j):
        return jnp.dot(lhs_sc[h], kaug_ref[h, j], preferred_element_type=F32)

    def row_max(s):
        return jnp.broadcast_to(jnp.max(s, axis=1, keepdims=True), (blk, LANE))

    def weights(s, m):
        return jnp.exp2(s - jnp.concatenate([m, m], axis=1)).astype(BF16)

    row = lax.broadcasted_iota(jnp.int32, (blk, blk), 0)
    col = lax.broadcasted_iota(jnp.int32, (blk, blk), 1)
    causal = col <= row
    for h in range(H_ATT):
        s = jnp.where(causal, scores(h, i), NEG)
        m = row_max(s)
        acc_sc[h] = lax.dot_general(weights(s, m), vaug_ref[h, i], _NT, preferred_element_type=F32)
        m_sc[h] = m

    def past_block(j, carry):
        for h in range(H_ATT):
            s = scores(h, j)
            m_old = m_sc[h]
            m_new = jnp.maximum(m_old, row_max(s))
            acc_sc[h] = (jnp.exp2(m_old - m_new) * acc_sc[h]
                         + lax.dot_general(weights(s, m_new), vaug_ref[h, j], _NT, preferred_element_type=F32))
            m_sc[h] = m_new
        return carry

    lax.fori_loop(0, i, past_block, 0)

    for p in range(H_ATT // 2):
        acc_e, acc_o = acc_sc[2 * p], acc_sc[2 * p + 1]
        num = jnp.where(lo_lanes, acc_e, acc_o)
        den = pltpu.roll(jnp.where(lo_lanes, acc_o, acc_e), half, 1)
        o_ref[:, p * LANE:(p + 1) * LANE] = (num / den).astype(o_ref.dtype)


def _moba_prompt(q, kt, vt):
    b, s, w = q.shape
    nb = s // MOBA_BLOCK
    assert s % MOBA_BLOCK == 0 and nb <= SUBLANE and w == ATT_WIDTH
    blk_state = lambda dt: pltpu.VMEM((H_ATT, MOBA_BLOCK, LANE), dt)
    return pl.pallas_call(
        functools.partial(_moba_prompt_kernel, nb=nb),
        grid=(b, nb),
        in_specs=[pl.BlockSpec((None, MOBA_BLOCK, w), lambda bi, i: (bi, i, 0)),
                  pl.BlockSpec((None, w, s), lambda bi, i: (bi, 0, 0)),
                  pl.BlockSpec((None, w, s), lambda bi, i: (bi, 0, 0))],
        out_specs=pl.BlockSpec((None, MOBA_BLOCK, w), lambda bi, i: (bi, i, 0)),
        out_shape=jax.ShapeDtypeStruct((b, s, w), BF16),
        scratch_shapes=[pltpu.VMEM((H_ATT, nb, LANE, MOBA_BLOCK), BF16),
                        pltpu.VMEM((H_ATT, nb, LANE, MOBA_BLOCK), BF16),
                        pltpu.VMEM((2, w, LANE), BF16),
                        blk_state(BF16), blk_state(F32), blk_state(F32)],
        compiler_params=_params("arbitrary", "arbitrary"),
    )(q, kt, vt)


def _mlstm_head_out(hh, gain, ogate):
    mu = jnp.mean(hh, axis=-1, keepdims=True)
    d = hh - mu
    var = jnp.mean(d * d, axis=-1, keepdims=True)
    return d * lax.rsqrt(var + LN_EPS) * gain * jax.nn.sigmoid(ogate)


def _mlstm_prompt_kernel(q_ref, k_ref, v_ref, o_ref, g_ref, gain_ref,
                         mem_ref, c_out, n_out, m_out, c_sc, n_sc, m_sc):
    c = pl.program_id(1)
    nbat, L = q_ref.shape[0], q_ref.shape[1]

    @pl.when(c == 0)
    def _reset_state():
        c_sc[...] = jnp.zeros_like(c_sc)
        n_sc[...] = jnp.zeros_like(n_sc)
        m_sc[...] = jnp.zeros_like(m_sc)

    row = lax.broadcasted_iota(jnp.int32, (L, L), 0)
    col = lax.broadcasted_iota(jnp.int32, (L, L), 1)
    causal = col <= row
    lower = jnp.where(causal, 1.0, 0.0)
    upper = jnp.where(row <= col, 1.0, 0.0)
    ones = jnp.ones((L, DV_MLSTM), BF16)

    for bi in range(nbat):
        g = g_ref[bi]
        g_t = g.T
        b_col_all = jnp.dot(lower, g, precision=HIGHEST, preferred_element_type=F32)
        b_row_all = jnp.dot(g_t[0:SUBLANE, :], upper, precision=HIGHEST, preferred_element_type=F32)
        for h in range(H_MLSTM):
            lanes = slice(h * DK_MLSTM, (h + 1) * DK_MLSTM)
            ig_row = g_t[h:h + 1, :]
            b_row = b_row_all[H_MLSTM + h:H_MLSTM + h + 1, :]
            ig = jnp.broadcast_to(g[:, h:h + 1], (L, LANE))
            b = jnp.broadcast_to(b_col_all[:, H_MLSTM + h:H_MLSTM + h + 1], (L, LANE))
            m_prev = m_sc[bi, h:h + 1, :]
            dmat = jnp.where(causal, b - b_row + ig_row, NEG)
            m_inter = b + m_prev
            m_t = jnp.maximum(m_inter, jnp.broadcast_to(jnp.max(dmat, axis=1, keepdims=True), (L, LANE)))
            w_inter = jnp.exp(m_inter - m_t)
            qh, kh, vh = q_ref[bi, :, lanes], k_ref[bi, :, lanes], v_ref[bi, :, lanes]
            a = jnp.exp(dmat - m_t) * lax.dot_general(qh, kh, _NT, preferred_element_type=F32)
            c_prev = c_sc[bi, h]
            n_prev = n_sc[bi, h:h + 1, :]
            state = jnp.concatenate([c_prev, jnp.broadcast_to(n_prev, (DV_MLSTM, DK_MLSTM))], axis=0).astype(BF16)
            num_den = (jnp.concatenate([w_inter, w_inter], axis=1)
                       * lax.dot_general(qh, state, _NT, preferred_element_type=F32)
                       + jnp.dot(a.astype(BF16), jnp.concatenate([vh, ones], axis=1), preferred_element_type=F32))
            hh = num_den[:, :DV_MLSTM] / jnp.maximum(jnp.abs(num_den[:, DV_MLSTM:]), jnp.exp(-m_t))
            mem_ref[bi, :, lanes] = _mlstm_head_out(
                hh, gain_ref[:, lanes], o_ref[bi, :, lanes].astype(F32)).astype(mem_ref.dtype)

            m_new = m_t[L - 1:L, :]
            b_last = b[L - 1:L, :]
            g_inter = jnp.exp(b_last + m_prev - m_new)
            g_in = jnp.exp(b_last - b + ig - m_new)
            v_scaled = (vh.astype(F32) * g_in).astype(BF16)
            c_sc[bi, h] = g_inter * c_prev + lax.dot_general(v_scaled, kh, _TN, preferred_element_type=F32)
            n_sc[bi, h:h + 1, :] = g_inter * n_prev + jnp.sum(kh.astype(F32) * g_in, axis=0, keepdims=True)
            m_sc[bi, h:h + 1, :] = m_new

    @pl.when(c == pl.num_programs(1) - 1)
    def _emit_state():
        c_out[...] = c_sc[...]
        n_out[...] = n_sc[...]
        m_out[...] = m_sc[...]


def _mlstm_prompt(mq, mk, mv, mo, gates, gain):
    b, s, w = mq.shape
    L = MLSTM_CHUNK
    nbat = 2 if b % 2 == 0 else 1
    assert s % L == 0 and w == MLSTM_WIDTH and L == LANE == DK_MLSTM == DV_MLSTM
    tok = lambda width: pl.BlockSpec((nbat, L, width), lambda bi, c: (bi, c, 0))
    state = lambda *dims: pl.BlockSpec((nbat,) + dims, lambda bi, c: (bi,) + (0,) * len(dims))
    return pl.pallas_call(
        _mlstm_prompt_kernel,
        grid=(b // nbat, s // L),
        in_specs=[tok(w), tok(w), tok(w), tok(w), tok(LANE), pl.BlockSpec((1, w), lambda bi, c: (0, 0))],
        out_specs=[tok(w), state(H_MLSTM, DV_MLSTM, DK_MLSTM), state(SUBLANE, LANE), state(SUBLANE, LANE)],
        out_shape=[jax.ShapeDtypeStruct((b, s, w), BF16),
                   jax.ShapeDtypeStruct((b, H_MLSTM, DV_MLSTM, DK_MLSTM), F32),
                   jax.ShapeDtypeStruct((b, SUBLANE, LANE), F32),
                   jax.ShapeDtypeStruct((b, SUBLANE, LANE), F32)],
        scratch_shapes=[pltpu.VMEM((nbat, H_MLSTM, DV_MLSTM, DK_MLSTM), F32),
                        pltpu.VMEM((nbat, SUBLANE, LANE), F32),
                        pltpu.VMEM((nbat, SUBLANE, LANE), F32)],
        compiler_params=_params("arbitrary", "arbitrary"),
    )(mq, mk, mv, mo, gates, gain)


def _head_sublane(h):
    return (H_ATT // 2 - 1 - h) if h < H_ATT // 2 else (H_ATT + H_ATT // 2 - 1 - h)


def _head_rows(x):
    parts = []
    for h in range(H_ATT):
        tiles = [x[h * DH_ATT + SUBLANE * t:h * DH_ATT + SUBLANE * (t + 1), :] for t in range(DH_ATT // SUBLANE)]
        parts.append(sum(tiles[1:], tiles[0]))
    sub = lax.broadcasted_iota(jnp.int32, parts[0].shape, 0)
    folded = [p + pltpu.roll(p, 4, 0) for p in parts]
    quads = [jnp.where(sub < 4, folded[i], folded[i + 4]) for i in range(4)]
    take_up = (sub & 2) != 0
    pairs = [jnp.where(take_up, quads[i] + pltpu.roll(quads[i], 2, 0),
                       quads[i + 2] + pltpu.roll(quads[i + 2], 6, 0)) for i in range(2)]
    return jnp.where((sub & 1) != 0, pairs[0] + pltpu.roll(pairs[0], 1, 0), pairs[1] + pltpu.roll(pairs[1], 7, 0))


def _moba_sample_score_kernel(pt_ref, qt_ref, knt_ref, *rest, n_pages):
    del pt_ref
    kp_refs = rest[:n_pages]
    pe_ref, stats_ref, idx_ref = rest[n_pages:]
    b = pl.program_id(0)
    n_blocks = n_pages // PAGES_PER_BLOCK
    w = qt_ref.shape[0]
    on_b = lax.broadcasted_iota(jnp.int32, (w, LANE), 1) == b

    def column(ref):
        return jnp.sum(jnp.where(on_b, ref[...], 0.0), axis=1, keepdims=True)

    q_col = column(qt_ref) * (DH_ATT ** -0.5)
    q_wide = jnp.broadcast_to(q_col, (w, LANE))
    s_own = _head_rows(jnp.broadcast_to(q_col * column(knt_ref), (w, LANE)))[:, 0:1]
    s_pages = [_head_rows(kp_refs[p][...] * q_wide) for p in range(n_pages)]

    blk = [jnp.sum(sum(s_pages[n * PAGES_PER_BLOCK + 1:(n + 1) * PAGES_PER_BLOCK], s_pages[n * PAGES_PER_BLOCK]),
                   axis=1, keepdims=True) for n in range(n_blocks)]
    lane = lax.broadcasted_iota(jnp.int32, (H_ATT, LANE), 1)
    sel, ranked = [], jnp.zeros((H_ATT, LANE), jnp.int32)
    for n in range(n_blocks):
        rank = jnp.zeros((H_ATT, 1), jnp.int32)
        for o in range(n_blocks):
            if o != n:
                beats = (blk[o] >= blk[n]) if o < n else (blk[o] > blk[n])
                rank = rank + jnp.where(beats, 1, 0)
        sel.append(rank < MOBA_TOPK)
        ranked = jnp.where(rank == lane, n, ranked)
    m = s_own
    for p in range(n_pages):
        page_max = jnp.max(s_pages[p], axis=1, keepdims=True)
        m = jnp.maximum(m, jnp.where(sel[p // PAGES_PER_BLOCK], page_max, NEG))
    p_own = jnp.exp(s_own - m)
    total = jnp.zeros((H_ATT, LANE), F32)
    for p in range(n_pages):
        pe = jnp.where(sel[p // PAGES_PER_BLOCK], jnp.exp(s_pages[p] - m), 0.0)
        pe_ref[p] = pe
        total = total + pe
    row_sum = p_own + jnp.sum(total, axis=1, keepdims=True)
    stats_ref[...] = jnp.where(lane == 0, p_own, row_sum)
    idx_ref[...] = ranked


def _moba_sample_mix_kernel(pt_ref, sel_ref, pe_ref, stats_ref, vnt_ref, cv_ref, o_ref, vbuf, sem):
    n_chunks = MOBA_TOPK * PAGES_PER_BLOCK
    b = pl.program_id(0)
    slot = b % 2
    w = vnt_ref.shape[0]
    on_b = lax.broadcasted_iota(jnp.int32, (w, LANE), 1) == b

    def local_page(row, h, c):
        return sel_ref[row, h * MOBA_TOPK + c // PAGES_PER_BLOCK] * PAGES_PER_BLOCK + c % PAGES_PER_BLOCK

    def chunk_copy(row, buf, h, c):
        return pltpu.make_async_copy(cv_ref.at[pt_ref[row, local_page(row, h, c)], h],
                                     vbuf.at[buf, h * n_chunks + c], sem.at[buf])

    def for_all_chunks(row, buf, act):
        for h in range(H_ATT):
            for c in range(n_chunks):
                act(chunk_copy(row, buf, h, c))

    @pl.when(b == 0)
    def _first_row():
        o_ref[...] = jnp.zeros_like(o_ref)
        for_all_chunks(0, 0, lambda cp: cp.start())

    @pl.when(b + 1 < pl.num_programs(0))
    def _prefetch_next_row():
        for_all_chunks(b + 1, 1 - slot, lambda cp: cp.start())

    for_all_chunks(b, slot, lambda cp: cp.wait())

    vn_col = jnp.sum(jnp.where(on_b, vnt_ref[...], 0.0), axis=1, keepdims=True)
    stats = stats_ref[...]
    out_cols = []
    for h in range(H_ATT):
        r = _head_sublane(h)
        acc = jnp.zeros((DH_ATT, LANE), F32)
        for c in range(n_chunks):
            acc = acc + pe_ref[local_page(b, h, c), r:r + 1, :] * vbuf[slot, h * n_chunks + c]
        p_own, row_sum = stats[r:r + 1, 0:1], stats[r:r + 1, 1:2]
        rows = slice(h * DH_ATT, (h + 1) * DH_ATT)
        out_cols.append((jnp.sum(acc, axis=1, keepdims=True) + p_own * vn_col[rows, :]) / row_sum)
    o_ref[...] = jnp.where(on_b, jnp.concatenate(out_cols, axis=0), o_ref[...])


def _moba_sample(qt, knt, vnt, cache_k, cache_v, page_table):
    w, b = qt.shape
    n_pages = page_table.shape[1]
    n_blocks = n_pages // PAGES_PER_BLOCK
    assert n_pages % PAGES_PER_BLOCK == 0 and b == LANE and MOBA_TOPK <= n_blocks <= LANE
    n_phys = cache_k.shape[0]
    ck = jnp.transpose(cache_k, (0, 2, 3, 1)).reshape(n_phys, w, PAGE_SIZE)
    cv = jnp.transpose(cache_v, (0, 2, 3, 1))
    whole = lambda *_: (0, 0)
    per_b = lambda *dims: pl.BlockSpec((None,) + dims, lambda bi, *_: (bi,) + (0,) * len(dims))

    k_page = lambda p: pl.BlockSpec((None, w, PAGE_SIZE), lambda bi, pt, p=p: (pt[bi, p], 0, 0))
    pe, stats, ranked = pl.pallas_call(
        functools.partial(_moba_sample_score_kernel, n_pages=n_pages),
        grid_spec=pltpu.PrefetchScalarGridSpec(
            num_scalar_prefetch=1,
            grid=(b,),
            in_specs=[pl.BlockSpec((w, b), whole), pl.BlockSpec((w, b), whole)] + [k_page(p) for p in range(n_pages)],
            out_specs=[per_b(n_pages, H_ATT, LANE), per_b(H_ATT, LANE), per_b(H_ATT, LANE)],
        ),
        out_shape=[jax.ShapeDtypeStruct((b, n_pages, H_ATT, LANE), F32),
                   jax.ShapeDtypeStruct((b, H_ATT, LANE), F32),
                   jax.ShapeDtypeStruct((b, H_ATT, LANE), jnp.int32)],
        compiler_params=_params("arbitrary"),
    )(page_table, qt, knt, *([ck] * n_pages))

    sel = jnp.stack([ranked[:, _head_sublane(h), :MOBA_TOPK] for h in range(H_ATT)], axis=1).reshape(b, -1)

    n_buffered = H_ATT * MOBA_TOPK * PAGES_PER_BLOCK
    return pl.pallas_call(
        _moba_sample_mix_kernel,
        grid_spec=pltpu.PrefetchScalarGridSpec(
            num_scalar_prefetch=2,
            grid=(b,),
            in_specs=[per_b(n_pages, H_ATT, LANE), per_b(H_ATT, LANE), pl.BlockSpec((w, b), whole),
                      pl.BlockSpec(memory_space=pl.ANY)],
            out_specs=pl.BlockSpec((w, b), whole),
            scratch_shapes=[pltpu.VMEM((2, n_buffered, DH_ATT, PAGE_SIZE), F32),
                            pltpu.SemaphoreType.DMA((2,))],
        ),
        out_shape=jax.ShapeDtypeStruct((w, b), F32),
        compiler_params=_params("arbitrary"),
    )(page_table, sel, pe, stats, vnt, cv)


def _mlstm_sample_kernel(q_ref, k_ref, v_ref, o_ref, g_ref, gain_ref, c0_ref, n0_ref, m0_ref,
                         mem_ref, c_ref, n_ref, m_ref):
    tb = q_ref.shape[0]
    g = g_ref[...]
    sub = lax.broadcasted_iota(jnp.int32, (2 * tb, LANE), 0)
    zrows = jnp.zeros((tb, LANE), F32)
    for h in range(H_MLSTM):
        lanes = slice(h * DK_MLSTM, (h + 1) * DK_MLSTM)
        ig, lf, m0 = g[:, h:h + 1], g[:, H_MLSTM + h:H_MLSTM + h + 1], m0_ref[:, h:h + 1]
        q, k, v = q_ref[:, lanes], k_ref[:, lanes], v_ref[:, lanes]
        n0 = n0_ref[:, lanes]
        m_t = jnp.maximum(lf + m0, ig)
        w_inter = jnp.exp(lf + m0 - m_t)
        g_in = jnp.exp(ig - m_t)
        a = g_in * jnp.sum(q * k, axis=1, keepdims=True)
        den = w_inter * jnp.sum(n0 * q, axis=1, keepdims=True) + a
        q_b = q.astype(BF16)
        gv = jnp.concatenate([g_in * v, zrows], axis=0)
        k_b = jnp.concatenate([k, zrows], axis=0).astype(BF16)
        cq_rows = []
        for r in range(tb):
            c_prev = c0_ref[r, h]
            cq_rows.append(lax.dot_general(q_b, c_prev.astype(BF16), _NT, preferred_element_type=F32)[r:r + 1, :])
            outer = lax.dot_general(jnp.where(sub == r, gv, 0.0).astype(BF16), k_b, _TN,
                                    preferred_element_type=F32)
            c_ref[r, h] = w_inter[r:r + 1, :] * c_prev + outer
        cq = jnp.concatenate(cq_rows, axis=0)
        hh = (w_inter * cq + a * v) / jnp.maximum(jnp.abs(den), jnp.exp(-m_t))
        mem_ref[:, lanes] = _mlstm_head_out(hh, gain_ref[:, lanes], o_ref[:, lanes]).astype(mem_ref.dtype)
        n_ref[:, lanes] = w_inter * n0 + g_in * k
        m_ref[:, h:h + 1] = m_t


def _mlstm_sample(mq, mk, mv, mo, gates, gain, c0, n0, m0):
    b, w = mq.shape
    tb = SUBLANE
    rows = lambda width: pl.BlockSpec((tb, width), lambda i: (i, 0))
    c_spec = pl.BlockSpec((tb, H_MLSTM, DV_MLSTM, DK_MLSTM), lambda i: (i, 0, 0, 0))
    return pl.pallas_call(
        _mlstm_sample_kernel,
        grid=(b // tb,),
        in_specs=[rows(w), rows(w), rows(w), rows(w), rows(LANE), pl.BlockSpec((1, w), lambda i: (0, 0)),
                  c_spec, rows(w), rows(H_MLSTM)],
        out_specs=[rows(w), c_spec, rows(w), rows(H_MLSTM)],
        out_shape=[jax.ShapeDtypeStruct((b, w), F32),
                   jax.ShapeDtypeStruct(c0.shape, F32),
                   jax.ShapeDtypeStruct((b, w), F32),
                   jax.ShapeDtypeStruct((b, H_MLSTM), F32)],
        compiler_params=_params("arbitrary"),
    )(mq, mk, mv, mo, gates, gain, c0, n0.reshape(b, w), m0)


def _finish_kernel(x_ref, att_ref, mem_ref, g1_ref, sh2_ref, sc2_ref, g2_ref, wo_ref, ln1g_ref, ln1b_ref,
                   wg_ref, wu_ref, wd_ref, ln2g_ref, ln2b_ref, y_ref, *, alpha, ff_chunk, att_transposed):
    att = att_ref[...].T if att_transposed else att_ref[...]
    aw = att.shape[1]
    mix = (jnp.dot(att.astype(BF16), wo_ref[0:aw, :], preferred_element_type=F32)
           + jnp.dot(mem_ref[...].astype(BF16), wo_ref[aw:, :], preferred_element_type=F32))
    x1 = _layernorm(alpha * x_ref[...] + (1.0 + g1_ref[...]) * mix, ln1g_ref[...], ln1b_ref[...])
    h2 = (x1 * (1.0 + sc2_ref[...]) + sh2_ref[...]).astype(BF16)
    f = jnp.zeros(x1.shape, F32)
    for c in range(wg_ref.shape[1] // ff_chunk):
        cols = slice(c * ff_chunk, (c + 1) * ff_chunk)
        gate = jnp.dot(h2, wg_ref[:, cols], preferred_element_type=F32)
        up = jnp.dot(h2, wu_ref[:, cols], preferred_element_type=F32)
        act = (gate * jax.nn.sigmoid(gate) * up).astype(BF16)
        f = f + jnp.dot(act, wd_ref[cols, :], preferred_element_type=F32)
    y_ref[...] = _layernorm(alpha * x1 + (1.0 + g2_ref[...]) * f, ln2g_ref[...], ln2b_ref[...])


def _finish(x, att, mem, mods, weights, tm, rows_per_mod, alpha, att_transposed=False):
    t, d = x.shape
    assert not att_transposed or tm == t
    w_out, ln1_g, ln1_b, w_gate, w_up, w_down, ln2_g, ln2_b = weights
    if rows_per_mod is None:
        mod_spec = pl.BlockSpec((tm, d), lambda i: (i, 0))
    else:
        per = rows_per_mod // tm
        mod_spec = pl.BlockSpec((None, 1, d), lambda i: (i // per, 0, 0))
    tok = lambda width: pl.BlockSpec((tm, width), lambda i: (i, 0))
    d_ff = w_gate.shape[1]
    ff_chunk = 256
    assert d_ff % ff_chunk == 0
    return pl.pallas_call(
        functools.partial(_finish_kernel, alpha=alpha, ff_chunk=ff_chunk, att_transposed=att_transposed),
        grid=(t // tm,),
        in_specs=[tok(d), pl.BlockSpec(att.shape, lambda i: (0, 0)) if att_transposed else tok(att.shape[1]),
                  tok(mem.shape[1]), mod_spec, mod_spec, mod_spec, mod_spec,
                  _const_spec(w_out.shape), _const_spec(ln1_g.shape), _const_spec(ln1_b.shape),
                  _const_spec(w_gate.shape), _const_spec(w_up.shape), _const_spec(w_down.shape),
                  _const_spec(ln2_g.shape), _const_spec(ln2_b.shape)],
        out_specs=tok(d),
        out_shape=jax.ShapeDtypeStruct((t, d), F32),
        compiler_params=_params("arbitrary"),
    )(x, att, mem, *mods, w_out, ln1_g, ln1_b, w_gate, w_up, w_down, ln2_g, ln2_b)


def kernel(x_prompt, x_sample, cache_k, cache_v, state_C, state_n, state_m, page_table, c_prompt, c_sample,
           w_ada, b_ada, w_in, b_if, mlstm_norm_g, w_out, ln1_g, ln1_b, w_gate, w_up, w_down, ln2_g, ln2_b):
    depth = w_in.shape[0]
    assert depth == 1, "single-layer step"
    alpha = (2.0 * depth) ** 0.25
    bp, s, d = x_prompt.shape
    bs = x_sample.shape[0]
    assert x_sample.shape[1] == 1, "single-token decode step"
    n_main = N_PROJ_GROUPS * PROJ_GROUP

    w_main = w_in[0, :, :n_main].astype(BF16)
    w_att_t = w_in[0, :, :3 * PROJ_GROUP].T.astype(BF16)
    w_gates = jnp.pad(w_in[0, :, n_main:], ((0, 0), (0, LANE - 2 * H_MLSTM))).astype(BF16)
    b_gates = jnp.pad(b_if[0], (0, LANE - 2 * H_MLSTM)).reshape(1, LANE)
    gain = mlstm_norm_g[0].reshape(1, MLSTM_WIDTH)
    row = lambda a: a[0].reshape(1, -1)
    fin_w = (w_out[0].astype(BF16), row(ln1_g), row(ln1_b), w_gate[0].astype(BF16), w_up[0].astype(BF16),
             w_down[0].astype(BF16), row(ln2_g), row(ln2_b))

    c_all = jnp.concatenate([c_prompt, c_sample], axis=0)
    mod = _adaln(c_all, w_ada[0], b_ada[0])
    sh1, sc1, g1, sh2, sc2, g2 = (mod[:, i * d:(i + 1) * d] for i in range(6))
    pm = lambda a: a[:bp].reshape(bp, 1, d)
    sm = lambda a: a[bp:]

    xp = x_prompt.reshape(bp * s, d)
    tm = 512
    aq, ak_t, av_t, mq, mk, mv, mo, gates = _in_proj(
        xp, pm(sc1), pm(sh1), w_main, w_att_t, w_gates, b_gates, tm, s,
        (F32, F32, F32, BF16, BF16, BF16, F32), transposed=(1, 2))
    seq = lambda a: a.reshape(bp, s, a.shape[-1])
    att = _moba_prompt(seq(aq), ak_t, av_t)
    mem, c_p, n_p, m_p = _mlstm_prompt(seq(mq), seq(mk), seq(mv), seq(mo), seq(gates), gain)
    y_p = _finish(xp, att.reshape(bp * s, -1), mem.reshape(bp * s, -1), (pm(g1), pm(sh2), pm(sc2), pm(g2)),
                  fin_w, tm, s, alpha)

    xs = x_sample.reshape(bs, d)
    aq_s, ak_s, av_s, mq_s, mk_s, mv_s, mo_s, gates_s = _in_proj(
        xs, sm(sc1), sm(sh1), w_main, w_att_t, w_gates, b_gates, bs, None, (F32,) * N_PROJ_GROUPS,
        transposed=(0, 1, 2))
    att_s = _moba_sample(aq_s, ak_s, av_s, cache_k[0], cache_v[0], page_table)
    mem_s, c_s, n_s, m_s = _mlstm_sample(mq_s, mk_s, mv_s, mo_s, gates_s, gain,
                                         state_C[0], state_n[0], state_m[0])
    y_s = _finish(xs, att_s, mem_s, (sm(g1), sm(sh2), sm(sc2), sm(g2)), fin_w, bs, None, alpha,
                  att_transposed=True)

    rows_p = lambda a: jnp.transpose(a.reshape(bp, H_ATT, DH_ATT, s), (0, 3, 1, 2))[None]
    rows_s = lambda a: jnp.transpose(a.reshape(H_ATT, DH_ATT, bs), (2, 0, 1)).reshape(1, bs, 1, H_ATT, DH_ATT)
    return (y_p.reshape(bp, s, d), y_s.reshape(bs, 1, d),
            rows_p(ak_t), rows_p(av_t),
            c_p[None], n_p[None, :, :H_MLSTM, :], m_p[None, :, :H_MLSTM, 0],
            rows_s(ak_s), rows_s(av_s),
            c_s[None], n_s.reshape(1, bs, H_MLSTM, DK_MLSTM), m_s[None])
```

```python
import functools
import math

import jax
import jax.numpy as jnp
from jax import lax
from jax.experimental import pallas as pl
from jax.experimental.pallas import tpu as pltpu

F32 = jnp.float32
BF16 = jnp.bfloat16
HIGHEST = lax.Precision.HIGHEST

LANE = 128
SUBLANE = 8
VMEM_LIMIT_BYTES = 56 * 1024 * 1024

H_ATT = 8
DH_ATT = 64
ATT_WIDTH = H_ATT * DH_ATT
MOBA_BLOCK = 256
MOBA_TOPK = 3
H_MLSTM = 4
DK_MLSTM = 128
DV_MLSTM = 128
MLSTM_WIDTH = H_MLSTM * DV_MLSTM
MLSTM_CHUNK = LANE
PAGE_SIZE = 128
PAGES_PER_BLOCK = MOBA_BLOCK // PAGE_SIZE
LN_EPS = 1e-5
NEG = -1e30
LOG2E = math.log2(math.e)
N_PROJ_GROUPS = 7
PROJ_GROUP = 512
MK_GROUP = 4

_NT = (((1,), (1,)), ((), ()))
_TN = (((0,), (0,)), ((), ()))


def _params(*sem):
    return pltpu.CompilerParams(dimension_semantics=sem, vmem_limit_bytes=VMEM_LIMIT_BYTES)


def _const_spec(shape):
    return pl.BlockSpec(shape, lambda *_: (0,) * len(shape), pipeline_mode=pl.Buffered(1))


def _layernorm(x, g, b):
    mu = jnp.mean(x, axis=-1, keepdims=True)
    d = x - mu
    var = jnp.mean(d * d, axis=-1, keepdims=True)
    return d * lax.rsqrt(var + LN_EPS) * g + b


def _top_blocks(val, nidx):
    cnt = jnp.zeros(val.shape, jnp.int32)
    for r in range(1, SUBLANE):
        other = pltpu.roll(val, r, 0)
        oidx = pltpu.roll(nidx, r, 0)
        beats = (other > val) | ((other == val) & (oidx < nidx))
        cnt = cnt + jnp.where(beats, 1, 0)
    return cnt < MOBA_TOPK


def _adaln_kernel(c_ref, w_ref, b_ref, o_ref):
    c = c_ref[...]
    s = c * jax.nn.sigmoid(c)
    o_ref[...] = jnp.dot(s, w_ref[...], preferred_element_type=F32) + b_ref[...]


def _adaln(c, w_ada, b_ada):
    rows, d = c.shape
    n = w_ada.shape[1]
    tn = d
    return pl.pallas_call(
        _adaln_kernel,
        grid=(n // tn,),
        in_specs=[pl.BlockSpec((rows, d), lambda j: (0, 0)),
                  pl.BlockSpec((d, tn), lambda j: (0, j)),
                  pl.BlockSpec((1, tn), lambda j: (0, j))],
        out_specs=pl.BlockSpec((rows, tn), lambda j: (0, j)),
        out_shape=jax.ShapeDtypeStruct((rows, n), F32),
        compiler_params=_params("arbitrary"),
    )(c, w_ada, b_ada.reshape(1, n))


def _in_proj_kernel(x_ref, sc_ref, sh_ref, w_ref, wt_ref, wg_ref, bg_ref, *out_refs, transposed):
    proj_refs, g_ref = out_refs[:N_PROJ_GROUPS], out_refs[N_PROJ_GROUPS]
    h = (x_ref[...] * (1.0 + sc_ref[...]) + sh_ref[...]).astype(BF16)
    for gi, o_ref in enumerate(proj_refs):
        cols = slice(gi * PROJ_GROUP, (gi + 1) * PROJ_GROUP)
        if gi in transposed:
            y = lax.dot_general(wt_ref[cols, :], h, _NT, preferred_element_type=F32)
        else:
            y = jnp.dot(h, w_ref[:, cols], preferred_element_type=F32)
        if gi == MK_GROUP:
            y = y * (DK_MLSTM ** -0.5)
        o_ref[...] = y.astype(o_ref.dtype)
    g = jnp.dot(h, wg_ref[...], preferred_element_type=F32) + bg_ref[...]
    lane = lax.broadcasted_iota(jnp.int32, g.shape, 1)
    logsig = jnp.minimum(g, 0.0) - jnp.log1p(jnp.exp(-jnp.abs(g)))
    g_ref[...] = jnp.where(lane >= H_MLSTM, logsig, g)


def _in_proj(x, sc, sh, w_main, w_att_t, w_gate, b_gate, tm, rows_per_mod, out_dtypes, transposed=()):
    t, d = x.shape
    assert MK_GROUP not in transposed
    if rows_per_mod is None:
        mod_spec = pl.BlockSpec((tm, d), lambda i: (i, 0))
        t_shape, t_spec = (PROJ_GROUP, t), pl.BlockSpec((PROJ_GROUP, tm), lambda i: (0, i))
    else:
        per = rows_per_mod // tm
        mod_spec = pl.BlockSpec((None, 1, d), lambda i: (i // per, 0, 0))
        t_shape = (t // rows_per_mod, PROJ_GROUP, rows_per_mod)
        t_spec = pl.BlockSpec((None, PROJ_GROUP, tm), lambda i: (i // per, 0, i % per))
    out_shape, out_specs = [], []
    for gi, dt in enumerate(out_dtypes):
        if gi in transposed:
            out_shape.append(jax.ShapeDtypeStruct(t_shape, dt))
            out_specs.append(t_spec)
        else:
            out_shape.append(jax.ShapeDtypeStruct((t, PROJ_GROUP), dt))
            out_specs.append(pl.BlockSpec((tm, PROJ_GROUP), lambda i: (i, 0)))
    out_shape.append(jax.ShapeDtypeStruct((t, LANE), F32))
    out_specs.append(pl.BlockSpec((tm, LANE), lambda i: (i, 0)))
    return pl.pallas_call(
        functools.partial(_in_proj_kernel, transposed=tuple(transposed)),
        grid=(t // tm,),
        in_specs=[pl.BlockSpec((tm, d), lambda i: (i, 0)), mod_spec, mod_spec,
                  _const_spec(w_main.shape), _const_spec(w_att_t.shape),
                  _const_spec(w_gate.shape), _const_spec(b_gate.shape)],
        out_specs=out_specs,
        out_shape=out_shape,
        compiler_params=_params("arbitrary"),
    )(x, sc, sh, w_main, w_att_t, w_gate, b_gate)


def _moba_prompt_kernel(q_ref, kt_ref, vt_ref, o_ref, kaug_ref, vaug_ref, kmt_ref, lhs_sc, m_sc, acc_sc, *, nb):
    i = pl.program_id(1)
    blk = MOBA_BLOCK
    half = LANE // 2
    w = q_ref.shape[1]

    @pl.when(i == 0)
    def _prepare_batch():
        srow = lax.broadcasted_iota(jnp.int32, (LANE, blk), 0)
        in_lo = srow < half
        head_of_row = lax.broadcasted_iota(jnp.int32, (w, LANE), 0) // DH_ATT
        lane_w = lax.broadcasted_iota(jnp.int32, (w, LANE), 1)
        kmt = jnp.zeros((w, LANE), F32)
        for j in range(nb):
            ktj = kt_ref[:, j * blk:(j + 1) * blk]
            vtj = vt_ref[:, j * blk:(j + 1) * blk]
            col = jnp.mean(ktj, axis=1, keepdims=True)
            kmt = jnp.where((lane_w % SUBLANE == j) & (lane_w // SUBLANE == head_of_row), col, kmt)
            for p in range(H_ATT // 2):
                kp, vp = ktj[p * LANE:(p + 1) * LANE, :], vtj[p * LANE:(p + 1) * LANE, :]
                kaug_ref[2 * p, j] = jnp.where(in_lo, kp, jnp.where(srow == half + j, 1.0, 0.0)).astype(BF16)
                kaug_ref[2 * p + 1, j] = jnp.where(in_lo, jnp.where(srow == j, 1.0, 0.0), kp).astype(BF16)
                vaug_ref[2 * p, j] = jnp.where(in_lo, vp, 1.0).astype(BF16)
                vaug_ref[2 * p + 1, j] = jnp.where(in_lo, 1.0, vp).astype(BF16)
        km_hi = kmt.astype(BF16)
        kmt_ref[0] = km_hi
        kmt_ref[1] = (kmt - km_hi.astype(F32)).astype(BF16)

    q32 = q_ref[...]
    q_hi = q32.astype(BF16)
    q_lo = (q32 - q_hi.astype(F32)).astype(BF16)
    sc = (jnp.dot(q_hi, kmt_ref[0], preferred_element_type=F32)
          + (jnp.dot(q_hi, kmt_ref[1], preferred_element_type=F32)
             + jnp.dot(q_lo, kmt_ref[0], preferred_element_type=F32)))
    sc_t = sc.T
    nidx = lax.broadcasted_iota(jnp.int32, (SUBLANE, blk), 0)
    past = nidx < i
    biases = []
    for h in range(H_ATT):
        val = jnp.where(past, sc_t[h * SUBLANE:(h + 1) * SUBLANE, :], NEG)
        keep = (_top_blocks(val, nidx) & past) | (nidx == i)
        biases.append(jnp.where(keep, 0.0, NEG))

    lane = lax.broadcasted_iota(jnp.int32, (blk, LANE), 1)
    lo_lanes = lane < half
    zpad = jnp.zeros((half - SUBLANE, blk), F32)
    for p in range(H_ATT // 2):
        bias_p = jnp.concatenate([biases[2 * p + 1], zpad, biases[2 * p], zpad], axis=0).T
        qp = q32[:, p * LANE:(p + 1) * LANE] * (DH_ATT ** -0.5 * LOG2E)
        lhs_sc[2 * p] = jnp.where(lo_lanes, qp, bias_p).astype(BF16)
        lhs_sc[2 * p + 1] = jnp.where(lo_lanes, bias_p, qp).astype(BF16)

    def scores(h, j):
        return jnp.dot(lhs_sc[h], kaug_ref[h, j], preferred_element_type=F32)

    def row_max(s):
        return jnp.broadcast_to(jnp.max(s, axis=1, keepdims=True), (blk, LANE))

    def weights(s, m):
        return jnp.exp2(s - jnp.concatenate([m, m], axis=1)).astype(BF16)

    row = lax.broadcasted_iota(jnp.int32, (blk, blk), 0)
    col = lax.broadcasted_iota(jnp.int32, (blk, blk), 1)
    causal = col <= row
    for h in range(H_ATT):
        s = jnp.where(causal, scores(h, i), NEG)
        m = row_max(s)
        acc_sc[h] = lax.dot_general(weights(s, m), vaug_ref[h, i], _NT, preferred_element_type=F32)
        m_sc[h] = m

    def past_block(j, carry):
        for h in range(H_ATT):
            s = scores(h, j)
            m_old = m_sc[h]
            m_new = jnp.maximum(m_old, row_max(s))
            acc_sc[h] = (jnp.exp2(m_old - m_new) * acc_sc[h]
                         + lax.dot_general(weights(s, m_new), vaug_ref[h, j], _NT, preferred_element_type=F32))
            m_sc[h] = m_new
        return carry

    lax.fori_loop(0, i, past_block, 0)

    for p in range(H_ATT // 2):
        acc_e, acc_o = acc_sc[2 * p], acc_sc[2 * p + 1]
        num = jnp.where(lo_lanes, acc_e, acc_o)
        den = pltpu.roll(jnp.where(lo_lanes, acc_o, acc_e), half, 1)
        o_ref[:, p * LANE:(p + 1) * LANE] = (num / den).astype(o_ref.dtype)


def _moba_prompt(q, kt, vt):
    b, s, w = q.shape
    nb = s // MOBA_BLOCK
    assert s % MOBA_BLOCK == 0 and nb <= SUBLANE and w == ATT_WIDTH
    blk_state = lambda dt: pltpu.VMEM((H_ATT, MOBA_BLOCK, LANE), dt)
    return pl.pallas_call(
        functools.partial(_moba_prompt_kernel, nb=nb),
        grid=(b, nb),
        in_specs=[pl.BlockSpec((None, MOBA_BLOCK, w), lambda bi, i: (bi, i, 0)),
                  pl.BlockSpec((None, w, s), lambda bi, i: (bi, 0, 0)),
                  pl.BlockSpec((None, w, s), lambda bi, i: (bi, 0, 0))],
        out_specs=pl.BlockSpec((None, MOBA_BLOCK, w), lambda bi, i: (bi, i, 0)),
        out_shape=jax.ShapeDtypeStruct((b, s, w), BF16),
        scratch_shapes=[pltpu.VMEM((H_ATT, nb, LANE, MOBA_BLOCK), BF16),
                        pltpu.VMEM((H_ATT, nb, LANE, MOBA_BLOCK), BF16),
                        pltpu.VMEM((2, w, LANE), BF16),
                        blk_state(BF16), blk_state(F32), blk_state(F32)],
        compiler_params=_params("arbitrary", "arbitrary"),
    )(q, kt, vt)


def _mlstm_head_out(hh, gain, ogate):
    mu = jnp.mean(hh, axis=-1, keepdims=True)
    d = hh - mu
    var = jnp.mean(d * d, axis=-1, keepdims=True)
    return d * lax.rsqrt(var + LN_EPS) * gain * jax.nn.sigmoid(ogate)


def _mlstm_prompt_kernel(q_ref, k_ref, v_ref, o_ref, g_ref, gain_ref,
                         mem_ref, c_out, n_out, m_out, c_sc, n_sc, m_sc):
    c = pl.program_id(1)
    nbat, L = q_ref.shape[0], q_ref.shape[1]

    @pl.when(c == 0)
    def _reset_state():
        c_sc[...] = jnp.zeros_like(c_sc)
        n_sc[...] = jnp.zeros_like(n_sc)
        m_sc[...] = jnp.zeros_like(m_sc)

    row = lax.broadcasted_iota(jnp.int32, (L, L), 0)
    col = lax.broadcasted_iota(jnp.int32, (L, L), 1)
    causal = col <= row
    lower = jnp.where(causal, 1.0, 0.0)
    upper = jnp.where(row <= col, 1.0, 0.0)
    ones = jnp.ones((L, DV_MLSTM), BF16)

    for bi in range(nbat):
        g = g_ref[bi]
        g_t = g.T
        b_col_all = jnp.dot(lower, g, precision=HIGHEST, preferred_element_type=F32)
        b_row_all = jnp.dot(g_t[0:SUBLANE, :], upper, precision=HIGHEST, preferred_element_type=F32)
        for h in range(H_MLSTM):
            lanes = slice(h * DK_MLSTM, (h + 1) * DK_MLSTM)
            ig_row = g_t[h:h + 1, :]
            b_row = b_row_all[H_MLSTM + h:H_MLSTM + h + 1, :]
            ig = jnp.broadcast_to(g[:, h:h + 1], (L, LANE))
            b = jnp.broadcast_to(b_col_all[:, H_MLSTM + h:H_MLSTM + h + 1], (L, LANE))
            m_prev = m_sc[bi, h:h + 1, :]
            dmat = jnp.where(causal, b - b_row + ig_row, NEG)
            m_inter = b + m_prev
            m_t = jnp.maximum(m_inter, jnp.broadcast_to(jnp.max(dmat, axis=1, keepdims=True), (L, LANE)))
            w_inter = jnp.exp(m_inter - m_t)
            qh, kh, vh = q_ref[bi, :, lanes], k_ref[bi, :, lanes], v_ref[bi, :, lanes]
            a = jnp.exp(dmat - m_t) * lax.dot_general(qh, kh, _NT, preferred_element_type=F32)
            c_prev = c_sc[bi, h]
            n_prev = n_sc[bi, h:h + 1, :]
            state = jnp.concatenate([c_prev, jnp.broadcast_to(n_prev, (DV_MLSTM, DK_MLSTM))], axis=0).astype(BF16)
            num_den = (jnp.concatenate([w_inter, w_inter], axis=1)
                       * lax.dot_general(qh, state, _NT, preferred_element_type=F32)
                       + jnp.dot(a.astype(BF16), jnp.concatenate([vh, ones], axis=1), preferred_element_type=F32))
            hh = num_den[:, :DV_MLSTM] / jnp.maximum(jnp.abs(num_den[:, DV_MLSTM:]), jnp.exp(-m_t))
            mem_ref[bi, :, lanes] = _mlstm_head_out(
                hh, gain_ref[:, lanes], o_ref[bi, :, lanes].astype(F32)).astype(mem_ref.dtype)

            m_new = m_t[L - 1:L, :]
            b_last = b[L - 1:L, :]
            g_inter = jnp.exp(b_last + m_prev - m_new)
            g_in = jnp.exp(b_last - b + ig - m_new)
            v_scaled = (vh.astype(F32) * g_in).astype(BF16)
            c_sc[bi, h] = g_inter * c_prev + lax.dot_general(v_scaled, kh, _TN, preferred_element_type=F32)
            n_sc[bi, h:h + 1, :] = g_inter * n_prev + jnp.sum(kh.astype(F32) * g_in, axis=0, keepdims=True)
            m_sc[bi, h:h + 1, :] = m_new

    @pl.when(c == pl.num_programs(1) - 1)
    def _emit_state():
        c_out[...] = c_sc[...]
        n_out[...] = n_sc[...]
        m_out[...] = m_sc[...]


def _mlstm_prompt(mq, mk, mv, mo, gates, gain):
    b, s, w = mq.shape
    L = MLSTM_CHUNK
    nbat = 2 if b % 2 == 0 else 1
    assert s % L == 0 and w == MLSTM_WIDTH and L == LANE == DK_MLSTM == DV_MLSTM
    tok = lambda width: pl.BlockSpec((nbat, L, width), lambda bi, c: (bi, c, 0))
    state = lambda *dims: pl.BlockSpec((nbat,) + dims, lambda bi, c: (bi,) + (0,) * len(dims))
    return pl.pallas_call(
        _mlstm_prompt_kernel,
        grid=(b // nbat, s // L),
        in_specs=[tok(w), tok(w), tok(w), tok(w), tok(LANE), pl.BlockSpec((1, w), lambda bi, c: (0, 0))],
        out_specs=[tok(w), state(H_MLSTM, DV_MLSTM, DK_MLSTM), state(SUBLANE, LANE), state(SUBLANE, LANE)],
        out_shape=[jax.ShapeDtypeStruct((b, s, w), BF16),
                   jax.ShapeDtypeStruct((b, H_MLSTM, DV_MLSTM, DK_MLSTM), F32),
                   jax.ShapeDtypeStruct((b, SUBLANE, LANE), F32),
                   jax.ShapeDtypeStruct((b, SUBLANE, LANE), F32)],
        scratch_shapes=[pltpu.VMEM((nbat, H_MLSTM, DV_MLSTM, DK_MLSTM), F32),
                        pltpu.VMEM((nbat, SUBLANE, LANE), F32),
                        pltpu.VMEM((nbat, SUBLANE, LANE), F32)],
        compiler_params=_params("arbitrary", "arbitrary"),
    )(mq, mk, mv, mo, gates, gain)


def _head_sublane(h):
    return (H_ATT // 2 - 1 - h) if h < H_ATT // 2 else (H_ATT + H_ATT // 2 - 1 - h)


def _head_rows(x):
    parts = []
    for h in range(H_ATT):
        tiles = [x[h * DH_ATT + SUBLANE * t:h * DH_ATT + SUBLANE * (t + 1), :] for t in range(DH_ATT // SUBLANE)]
        parts.append(sum(tiles[1:], tiles[0]))
    sub = lax.broadcasted_iota(jnp.int32, parts[0].shape, 0)
    folded = [p + pltpu.roll(p, 4, 0) for p in parts]
    quads = [jnp.where(sub < 4, folded[i], folded[i + 4]) for i in range(4)]
    take_up = (sub & 2) != 0
    pairs = [jnp.where(take_up, quads[i] + pltpu.roll(quads[i], 2, 0),
                       quads[i + 2] + pltpu.roll(quads[i + 2], 6, 0)) for i in range(2)]
    return jnp.where((sub & 1) != 0, pairs[0] + pltpu.roll(pairs[0], 1, 0), pairs[1] + pltpu.roll(pairs[1], 7, 0))


def _moba_sample_scores(b, qt_ref, knt_ref, kp_refs, pe_ref, stats_ref, idx_ref):
    n_pages = len(kp_refs)
    n_blocks = n_pages // PAGES_PER_BLOCK
    w = qt_ref.shape[0]
    on_b = lax.broadcasted_iota(jnp.int32, (w, LANE), 1) == b

    def column(ref):
        return jnp.sum(jnp.where(on_b, ref[...], 0.0), axis=1, keepdims=True)

    q_col = column(qt_ref) * (DH_ATT ** -0.5)
    q_wide = jnp.broadcast_to(q_col, (w, LANE))
    s_own = _head_rows(jnp.broadcast_to(q_col * column(knt_ref), (w, LANE)))[:, 0:1]
    s_pages = [_head_rows(kp_refs[p][...] * q_wide) for p in range(n_pages)]

    blk = [jnp.sum(sum(s_pages[n * PAGES_PER_BLOCK + 1:(n + 1) * PAGES_PER_BLOCK], s_pages[n * PAGES_PER_BLOCK]),
                   axis=1, keepdims=True) for n in range(n_blocks)]
    lane = lax.broadcasted_iota(jnp.int32, (H_ATT, LANE), 1)
    sel, ranked = [], jnp.zeros((H_ATT, LANE), jnp.int32)
    for n in range(n_blocks):
        rank = jnp.zeros((H_ATT, 1), jnp.int32)
        for o in range(n_blocks):
            if o != n:
                beats = (blk[o] >= blk[n]) if o < n else (blk[o] > blk[n])
                rank = rank + jnp.where(beats, 1, 0)
        sel.append(rank < MOBA_TOPK)
        ranked = jnp.where(rank == lane, n, ranked)
    m = s_own
    for p in range(n_pages):
        page_max = jnp.max(s_pages[p], axis=1, keepdims=True)
        m = jnp.maximum(m, jnp.where(sel[p // PAGES_PER_BLOCK], page_max, NEG))
    p_own = jnp.exp(s_own - m)
    total = jnp.zeros((H_ATT, LANE), F32)
    for p in range(n_pages):
        pe = jnp.where(sel[p // PAGES_PER_BLOCK], jnp.exp(s_pages[p] - m), 0.0)
        pe_ref[p] = pe
        total = total + pe
    row_sum = p_own + jnp.sum(total, axis=1, keepdims=True)
    stats_ref[...] = jnp.where(lane == 0, p_own, row_sum)
    idx_ref[...] = ranked


def _moba_sample_mix_kernel(pt_ref, sel_ref, pe_ref, stats_ref, vnt_ref, cv_ref, o_ref, vbuf, sem):
    n_chunks = MOBA_TOPK * PAGES_PER_BLOCK
    b = pl.program_id(0)
    slot = b % 2
    w = vnt_ref.shape[0]
    on_b = lax.broadcasted_iota(jnp.int32, (w, LANE), 1) == b

    def local_page(row, h, c):
        return sel_ref[row, h * MOBA_TOPK + c // PAGES_PER_BLOCK] * PAGES_PER_BLOCK + c % PAGES_PER_BLOCK

    def chunk_copy(row, buf, h, c):
        return pltpu.make_async_copy(cv_ref.at[pt_ref[row, local_page(row, h, c)], h],
                                     vbuf.at[buf, h * n_chunks + c], sem.at[buf])

    def for_all_chunks(row, buf, act):
        for h in range(H_ATT):
            for c in range(n_chunks):
                act(chunk_copy(row, buf, h, c))

    @pl.when(b == 0)
    def _first_row():
        o_ref[...] = jnp.zeros_like(o_ref)
        for_all_chunks(0, 0, lambda cp: cp.start())

    @pl.when(b + 1 < pl.num_programs(0))
    def _prefetch_next_row():
        for_all_chunks(b + 1, 1 - slot, lambda cp: cp.start())

    for_all_chunks(b, slot, lambda cp: cp.wait())

    vn_col = jnp.sum(jnp.where(on_b, vnt_ref[...], 0.0), axis=1, keepdims=True)
    stats = stats_ref[...]
    out_cols = []
    for h in range(H_ATT):
        r = _head_sublane(h)
        acc = jnp.zeros((DH_ATT, LANE), F32)
        for c in range(n_chunks):
            acc = acc + pe_ref[local_page(b, h, c), r:r + 1, :] * vbuf[slot, h * n_chunks + c]
        p_own, row_sum = stats[r:r + 1, 0:1], stats[r:r + 1, 1:2]
        rows = slice(h * DH_ATT, (h + 1) * DH_ATT)
        out_cols.append((jnp.sum(acc, axis=1, keepdims=True) + p_own * vn_col[rows, :]) / row_sum)
    o_ref[...] = jnp.where(on_b, jnp.concatenate(out_cols, axis=0), o_ref[...])


def _cache_pages(cache):
    return jnp.transpose(cache, (0, 2, 3, 1))


def _moba_sample_mix(pe, stats, ranked, vnt, cache_v, page_table):
    w, b = vnt.shape
    n_pages = page_table.shape[1]
    assert b == LANE
    cv = _cache_pages(cache_v)
    whole = lambda *_: (0, 0)
    per_b = lambda *dims: pl.BlockSpec((None,) + dims, lambda bi, *_: (bi,) + (0,) * len(dims))
    sel = jnp.stack([ranked[:, _head_sublane(h), :MOBA_TOPK] for h in range(H_ATT)], axis=1).reshape(b, -1)

    n_buffered = H_ATT * MOBA_TOPK * PAGES_PER_BLOCK
    return pl.pallas_call(
        _moba_sample_mix_kernel,
        grid_spec=pltpu.PrefetchScalarGridSpec(
            num_scalar_prefetch=2,
            grid=(b,),
            in_specs=[per_b(n_pages, H_ATT, LANE), per_b(H_ATT, LANE), pl.BlockSpec((w, b), whole),
                      pl.BlockSpec(memory_space=pl.ANY)],
            out_specs=pl.BlockSpec((w, b), whole),
            scratch_shapes=[pltpu.VMEM((2, n_buffered, DH_ATT, PAGE_SIZE), F32),
                            pltpu.SemaphoreType.DMA((2,))],
        ),
        out_shape=jax.ShapeDtypeStruct((w, b), F32),
        compiler_params=_params("arbitrary"),
    )(page_table, sel, pe, stats, vnt, cv)


def _mlstm_sample_kernel(q_ref, k_ref, v_ref, o_ref, g_ref, gain_ref, c0_ref, n0_ref, m0_ref,
                         mem_ref, c_ref, n_ref, m_ref):
    tb = q_ref.shape[0]
    g = g_ref[...]
    sub = lax.broadcasted_iota(jnp.int32, (2 * tb, LANE), 0)
    zrows = jnp.zeros((tb, LANE), F32)
    for h in range(H_MLSTM):
        lanes = slice(h * DK_MLSTM, (h + 1) * DK_MLSTM)
        ig, lf, m0 = g[:, h:h + 1], g[:, H_MLSTM + h:H_MLSTM + h + 1], m0_ref[:, h:h + 1]
        q, k, v = q_ref[:, lanes], k_ref[:, lanes], v_ref[:, lanes]
        n0 = n0_ref[:, lanes]
        m_t = jnp.maximum(lf + m0, ig)
        w_inter = jnp.exp(lf + m0 - m_t)
        g_in = jnp.exp(ig - m_t)
        a = g_in * jnp.sum(q * k, axis=1, keepdims=True)
        den = w_inter * jnp.sum(n0 * q, axis=1, keepdims=True) + a
        q_b = q.astype(BF16)
        gv = jnp.concatenate([g_in * v, zrows], axis=0)
        k_b = jnp.concatenate([k, zrows], axis=0).astype(BF16)
        cq_rows = []
        for r in range(tb):
            c_prev = c0_ref[r, h]
            cq_rows.append(lax.dot_general(q_b, c_prev.astype(BF16), _NT, preferred_element_type=F32)[r:r + 1, :])
            outer = lax.dot_general(jnp.where(sub == r, gv, 0.0).astype(BF16), k_b, _TN,
                                    preferred_element_type=F32)
            c_ref[r, h] = w_inter[r:r + 1, :] * c_prev + outer
        cq = jnp.concatenate(cq_rows, axis=0)
        hh = (w_inter * cq + a * v) / jnp.maximum(jnp.abs(den), jnp.exp(-m_t))
        mem_ref[:, lanes] = _mlstm_head_out(hh, gain_ref[:, lanes], o_ref[:, lanes]).astype(mem_ref.dtype)
        n_ref[:, lanes] = w_inter * n0 + g_in * k
        m_ref[:, h:h + 1] = m_t


def _mlstm_sample(mq, mk, mv, mo, gates, gain, c0, n0, m0):
    b, w = mq.shape
    tb = SUBLANE
    rows = lambda width: pl.BlockSpec((tb, width), lambda i: (i, 0))
    c_spec = pl.BlockSpec((tb, H_MLSTM, DV_MLSTM, DK_MLSTM), lambda i: (i, 0, 0, 0))
    return pl.pallas_call(
        _mlstm_sample_kernel,
        grid=(b // tb,),
        in_specs=[rows(w), rows(w), rows(w), rows(w), rows(LANE), pl.BlockSpec((1, w), lambda i: (0, 0)),
                  c_spec, rows(w), rows(H_MLSTM)],
        out_specs=[rows(w), c_spec, rows(w), rows(H_MLSTM)],
        out_shape=[jax.ShapeDtypeStruct((b, w), F32),
                   jax.ShapeDtypeStruct(c0.shape, F32),
                   jax.ShapeDtypeStruct((b, w), F32),
                   jax.ShapeDtypeStruct((b, H_MLSTM), F32)],
        compiler_params=_params("arbitrary"),
    )(mq, mk, mv, mo, gates, gain, c0, n0.reshape(b, w), m0)


N_FINISH_INPUTS = 15


def _finish_kernel(*refs, alpha, ff_chunk, att_transposed, guest_rows):
    if guest_rows:
        pt_ref, refs = refs[0], refs[1:]
    (x_ref, att_ref, mem_ref, g1_ref, sh2_ref, sc2_ref, g2_ref, wo_ref, ln1g_ref, ln1b_ref,
     wg_ref, wu_ref, wd_ref, ln2g_ref, ln2b_ref) = refs[:N_FINISH_INPUTS]
    if guest_rows:
        qt_ref, knt_ref, ck_ref, y_ref, pe_ref, stats_ref, idx_ref, kbuf, sem = refs[N_FINISH_INPUTS:]
    else:
        (y_ref,) = refs[N_FINISH_INPUTS:]

    att = att_ref[...].T if att_transposed else att_ref[...]
    aw = att.shape[1]
    mix = (jnp.dot(att.astype(BF16), wo_ref[0:aw, :], preferred_element_type=F32)
           + jnp.dot(mem_ref[...].astype(BF16), wo_ref[aw:, :], preferred_element_type=F32))
    x1 = _layernorm(alpha * x_ref[...] + (1.0 + g1_ref[...]) * mix, ln1g_ref[...], ln1b_ref[...])
    h2 = (x1 * (1.0 + sc2_ref[...]) + sh2_ref[...]).astype(BF16)
    n_ff = wg_ref.shape[1] // ff_chunk

    def ffn_chunk(c):
        cols = slice(c * ff_chunk, (c + 1) * ff_chunk)
        gate = jnp.dot(h2, wg_ref[:, cols], preferred_element_type=F32)
        up = jnp.dot(h2, wu_ref[:, cols], preferred_element_type=F32)
        act = (gate * jax.nn.sigmoid(gate) * up).astype(BF16)
        return jnp.dot(act, wd_ref[cols, :], preferred_element_type=F32)

    f = jnp.zeros(x1.shape, F32)
    if not guest_rows:
        for c in range(n_ff):
            f = f + ffn_chunk(c)
    else:
        i = pl.program_id(0)
        n_pages = kbuf.shape[1]

        def page_copy(row, slot, p):
            return pltpu.make_async_copy(ck_ref.at[pt_ref[row, p]], kbuf.at[slot, p], sem.at[slot])

        def start_row(row, slot):
            for p in range(n_pages):
                page_copy(row, slot, p).start()

        @pl.when(i == 0)
        def _first_row():
            start_row(0, 0)

        per_row = -(-n_ff // guest_rows)
        for r in range(guest_rows):
            row, slot = i * guest_rows + r, r % 2
            if r + 1 < guest_rows:
                start_row(row + 1, 1 - slot)
            else:
                @pl.when(i + 1 < pl.num_programs(0))
                def _next_step_row():
                    start_row(row + 1, 1 - slot)
            for p in range(n_pages):
                page_copy(row, slot, p).wait()
            _moba_sample_scores(row, qt_ref, knt_ref, [kbuf.at[slot, p] for p in range(n_pages)],
                                pe_ref.at[r], stats_ref.at[r], idx_ref.at[r])
            for c in range(r * per_row, min((r + 1) * per_row, n_ff)):
                f = f + ffn_chunk(c)
    y_ref[...] = _layernorm(alpha * x1 + (1.0 + g2_ref[...]) * f, ln2g_ref[...], ln2b_ref[...])


def _finish(x, att, mem, mods, weights, tm, rows_per_mod, alpha, att_transposed=False, guest=None):
    t, d = x.shape
    assert not att_transposed or tm == t
    steps = t // tm
    w_out, ln1_g, ln1_b, w_gate, w_up, w_down, ln2_g, ln2_b = weights
    if rows_per_mod is None:
        mod_spec = pl.BlockSpec((tm, d), lambda i, *_: (i, 0))
    else:
        per = rows_per_mod // tm
        mod_spec = pl.BlockSpec((None, 1, d), lambda i, *_: (i // per, 0, 0))
    tok = lambda width: pl.BlockSpec((tm, width), lambda i, *_: (i, 0))
    ff_chunk = 256
    assert w_gate.shape[1] % ff_chunk == 0
    in_specs = [tok(d), pl.BlockSpec(att.shape, lambda *_: (0, 0)) if att_transposed else tok(att.shape[1]),
                tok(mem.shape[1]), mod_spec, mod_spec, mod_spec, mod_spec,
                _const_spec(w_out.shape), _const_spec(ln1_g.shape), _const_spec(ln1_b.shape),
                _const_spec(w_gate.shape), _const_spec(w_up.shape), _const_spec(w_down.shape),
                _const_spec(ln2_g.shape), _const_spec(ln2_b.shape)]
    args = (x, att, mem, *mods, w_out, ln1_g, ln1_b, w_gate, w_up, w_down, ln2_g, ln2_b)
    assert len(in_specs) == N_FINISH_INPUTS
    body = functools.partial(_finish_kernel, alpha=alpha, ff_chunk=ff_chunk, att_transposed=att_transposed,
                             guest_rows=0)
    y_shape = jax.ShapeDtypeStruct((t, d), F32)
    if guest is None:
        return pl.pallas_call(body, grid=(steps,), in_specs=in_specs, out_specs=tok(d), out_shape=y_shape,
                              compiler_params=_params("arbitrary"))(*args)

    qt, knt, cache_k, page_table = guest
    w, bs = qt.shape
    n_pages = page_table.shape[1]
    guest_rows = bs // steps
    assert bs % steps == 0 and guest_rows % 2 == 0 and n_pages % PAGES_PER_BLOCK == 0
    assert MOBA_TOPK <= n_pages // PAGES_PER_BLOCK <= LANE and bs == LANE
    ck = _cache_pages(cache_k).reshape(cache_k.shape[0], w, PAGE_SIZE)
    rows = lambda *dims: pl.BlockSpec((guest_rows,) + dims, lambda i, *_: (i,) + (0,) * len(dims))
    return pl.pallas_call(
        functools.partial(body, guest_rows=guest_rows),
        grid_spec=pltpu.PrefetchScalarGridSpec(
            num_scalar_prefetch=1,
            grid=(steps,),
            in_specs=in_specs + [pl.BlockSpec((w, bs), lambda *_: (0, 0)), pl.BlockSpec((w, bs), lambda *_: (0, 0)),
                                 pl.BlockSpec(memory_space=pl.ANY)],
            out_specs=[tok(d), rows(n_pages, H_ATT, LANE), rows(H_ATT, LANE), rows(H_ATT, LANE)],
            scratch_shapes=[pltpu.VMEM((2, n_pages, w, PAGE_SIZE), F32), pltpu.SemaphoreType.DMA((2,))],
        ),
        out_shape=[y_shape,
                   jax.ShapeDtypeStruct((bs, n_pages, H_ATT, LANE), F32),
                   jax.ShapeDtypeStruct((bs, H_ATT, LANE), F32),
                   jax.ShapeDtypeStruct((bs, H_ATT, LANE), jnp.int32)],
        compiler_params=_params("arbitrary"),
    )(page_table, *args, qt, knt, ck)


def kernel(x_prompt, x_sample, cache_k, cache_v, state_C, state_n, state_m, page_table, c_prompt, c_sample,
           w_ada, b_ada, w_in, b_if, mlstm_norm_g, w_out, ln1_g, ln1_b, w_gate, w_up, w_down, ln2_g, ln2_b):
    depth = w_in.shape[0]
    assert depth == 1, "single-layer step"
    alpha = (2.0 * depth) ** 0.25
    bp, s, d = x_prompt.shape
    bs = x_sample.shape[0]
    assert x_sample.shape[1] == 1, "single-token decode step"
    n_main = N_PROJ_GROUPS * PROJ_GROUP

    w_main = w_in[0, :, :n_main].astype(BF16)
    w_att_t = w_in[0, :, :3 * PROJ_GROUP].T.astype(BF16)
    w_gates = jnp.pad(w_in[0, :, n_main:], ((0, 0), (0, LANE - 2 * H_MLSTM))).astype(BF16)
    b_gates = jnp.pad(b_if[0], (0, LANE - 2 * H_MLSTM)).reshape(1, LANE)
    gain = mlstm_norm_g[0].reshape(1, MLSTM_WIDTH)
    row = lambda a: a[0].reshape(1, -1)
    fin_w = (w_out[0].astype(BF16), row(ln1_g), row(ln1_b), w_gate[0].astype(BF16), w_up[0].astype(BF16),
             w_down[0].astype(BF16), row(ln2_g), row(ln2_b))

    c_all = jnp.concatenate([c_prompt, c_sample], axis=0)
    mod = _adaln(c_all, w_ada[0], b_ada[0])
    sh1, sc1, g1, sh2, sc2, g2 = (mod[:, i * d:(i + 1) * d] for i in range(6))
    pm = lambda a: a[:bp].reshape(bp, 1, d)
    sm = lambda a: a[bp:]

    xs = x_sample.reshape(bs, d)
    aq_s, ak_s, av_s, mq_s, mk_s, mv_s, mo_s, gates_s = _in_proj(
        xs, sm(sc1), sm(sh1), w_main, w_att_t, w_gates, b_gates, bs, None, (F32,) * N_PROJ_GROUPS,
        transposed=(0, 1, 2))

    xp = x_prompt.reshape(bp * s, d)
    tm = 512
    aq, ak_t, av_t, mq, mk, mv, mo, gates = _in_proj(
        xp, pm(sc1), pm(sh1), w_main, w_att_t, w_gates, b_gates, tm, s,
        (F32, F32, F32, BF16, BF16, BF16, F32), transposed=(1, 2))
    seq = lambda a: a.reshape(bp, s, a.shape[-1])
    att = _moba_prompt(seq(aq), ak_t, av_t)
    mem, c_p, n_p, m_p = _mlstm_prompt(seq(mq), seq(mk), seq(mv), seq(mo), seq(gates), gain)
    y_p, pe, stats, ranked = _finish(
        xp, att.reshape(bp * s, -1), mem.reshape(bp * s, -1), (pm(g1), pm(sh2), pm(sc2), pm(g2)),
        fin_w, tm, s, alpha, guest=(aq_s, ak_s, cache_k[0], page_table))

    att_s = _moba_sample_mix(pe, stats, ranked, av_s, cache_v[0], page_table)
    mem_s, c_s, n_s, m_s = _mlstm_sample(mq_s, mk_s, mv_s, mo_s, gates_s, gain,
                                         state_C[0], state_n[0], state_m[0])
    y_s = _finish(xs, att_s, mem_s, (sm(g1), sm(sh2), sm(sc2), sm(g2)), fin_w, bs, None, alpha,
                  att_transposed=True)

    rows_p = lambda a: jnp.transpose(a.reshape(bp, H_ATT, DH_ATT, s), (0, 3, 1, 2))[None]
    rows_s = lambda a: jnp.transpose(a.reshape(H_ATT, DH_ATT, bs), (2, 0, 1)).reshape(1, bs, 1, H_ATT, DH_ATT)
    return (y_p.reshape(bp, s, d), y_s.reshape(bs, 1, d),
            rows_p(ak_t), rows_p(av_t),
            c_p[None], n_p[None, :, :H_MLSTM, :], m_p[None, :, :H_MLSTM, 0],
            rows_s(ak_s), rows_s(av_s),
            c_s[None], n_s.reshape(1, bs, H_MLSTM, DK_MLSTM), m_s[None])
```

```python
import functools
import math

import jax
import jax.numpy as jnp
from jax import lax
from jax.experimental import pallas as pl
from jax.experimental.pallas import tpu as pltpu

F32 = jnp.float32
BF16 = jnp.bfloat16
HIGHEST = lax.Precision.HIGHEST

LANE = 128
SUBLANE = 8
VMEM_LIMIT_BYTES = 56 * 1024 * 1024

H_ATT = 8
DH_ATT = 64
ATT_WIDTH = H_ATT * DH_ATT
MOBA_BLOCK = 256
MOBA_TOPK = 3
H_MLSTM = 4
DK_MLSTM = 128
DV_MLSTM = 128
MLSTM_WIDTH = H_MLSTM * DV_MLSTM
MLSTM_CHUNK = LANE
PAGE_SIZE = 128
PAGES_PER_BLOCK = MOBA_BLOCK // PAGE_SIZE
LN_EPS = 1e-5
NEG = -1e30
LOG2E = math.log2(math.e)
N_PROJ_GROUPS = 7
PROJ_GROUP = 512
MK_GROUP = 4

_NT = (((1,), (1,)), ((), ()))
_TN = (((0,), (0,)), ((), ()))


def _params(*sem):
    return pltpu.CompilerParams(dimension_semantics=sem, vmem_limit_bytes=VMEM_LIMIT_BYTES)


def _const_spec(shape):
    return pl.BlockSpec(shape, lambda *_: (0,) * len(shape), pipeline_mode=pl.Buffered(1))


def _layernorm(x, g, b):
    mu = jnp.mean(x, axis=-1, keepdims=True)
    d = x - mu
    var = jnp.mean(d * d, axis=-1, keepdims=True)
    return d * lax.rsqrt(var + LN_EPS) * g + b


def _top_blocks(val, nidx):
    cnt = jnp.zeros(val.shape, jnp.int32)
    for r in range(1, SUBLANE):
        other = pltpu.roll(val, r, 0)
        oidx = pltpu.roll(nidx, r, 0)
        beats = (other > val) | ((other == val) & (oidx < nidx))
        cnt = cnt + jnp.where(beats, 1, 0)
    return cnt < MOBA_TOPK


def _adaln_kernel(c_ref, w_ref, b_ref, o_ref):
    c = c_ref[...]
    s = c * jax.nn.sigmoid(c)
    o_ref[...] = jnp.dot(s, w_ref[...], preferred_element_type=F32) + b_ref[...]


def _adaln(c, w_ada, b_ada):
    rows, d = c.shape
    n = w_ada.shape[1]
    tn = d
    return pl.pallas_call(
        _adaln_kernel,
        grid=(n // tn,),
        in_specs=[pl.BlockSpec((rows, d), lambda j: (0, 0)),
                  pl.BlockSpec((d, tn), lambda j: (0, j)),
                  pl.BlockSpec((1, tn), lambda j: (0, j))],
        out_specs=pl.BlockSpec((rows, tn), lambda j: (0, j)),
        out_shape=jax.ShapeDtypeStruct((rows, n), F32),
        compiler_params=_params("arbitrary"),
    )(c, w_ada, b_ada.reshape(1, n))


def _guest_rows_ring(i, guest_rows, start_row, wait_row, work):
    @pl.when(i == 0)
    def _first_row():
        start_row(0, 0)

    for r in range(guest_rows):
        row, slot = i * guest_rows + r, r % 2
        if r + 1 < guest_rows:
            start_row(row + 1, 1 - slot)
        else:
            @pl.when(i + 1 < pl.num_programs(0))
            def _next_step_row():
                start_row(row + 1, 1 - slot)
        wait_row(row, slot)
        work(r, row, slot)


N_PROJ_INPUTS = 7


def _in_proj_kernel(*refs, transposed, guest_rows):
    if guest_rows:
        pt_ref, refs = refs[0], refs[1:]
    x_ref, sc_ref, sh_ref, w_ref, wt_ref, wg_ref, bg_ref = refs[:N_PROJ_INPUTS]
    refs = refs[N_PROJ_INPUTS:]
    if guest_rows:
        (qt_ref, knt_ref, ck_ref), refs = refs[:3], refs[3:]
    proj_refs, g_ref = refs[:N_PROJ_GROUPS], refs[N_PROJ_GROUPS]
    h = (x_ref[...] * (1.0 + sc_ref[...]) + sh_ref[...]).astype(BF16)

    def project(gi):
        cols = slice(gi * PROJ_GROUP, (gi + 1) * PROJ_GROUP)
        if gi in transposed:
            y = lax.dot_general(wt_ref[cols, :], h, _NT, preferred_element_type=F32)
        else:
            y = jnp.dot(h, w_ref[:, cols], preferred_element_type=F32)
        if gi == MK_GROUP:
            y = y * (DK_MLSTM ** -0.5)
        proj_refs[gi][...] = y.astype(proj_refs[gi].dtype)

    def gates():
        g = jnp.dot(h, wg_ref[...], preferred_element_type=F32) + bg_ref[...]
        lane = lax.broadcasted_iota(jnp.int32, g.shape, 1)
        logsig = jnp.minimum(g, 0.0) - jnp.log1p(jnp.exp(-jnp.abs(g)))
        g_ref[...] = jnp.where(lane >= H_MLSTM, logsig, g)

    jobs = [functools.partial(project, gi) for gi in range(N_PROJ_GROUPS)] + [gates]
    if not guest_rows:
        for job in jobs:
            job()
        return

    pe_ref, stats_ref, idx_ref, kbuf, sem = refs[N_PROJ_GROUPS + 1:]
    n_pages = kbuf.shape[1]
    per_row = -(-len(jobs) // guest_rows)

    def page_copy(row, slot, p):
        return pltpu.make_async_copy(ck_ref.at[pt_ref[row, p]], kbuf.at[slot, p], sem.at[slot])

    def start_row(row, slot):
        for p in range(n_pages):
            page_copy(row, slot, p).start()

    def wait_row(row, slot):
        for p in range(n_pages):
            page_copy(row, slot, p).wait()

    def work(r, row, slot):
        _moba_sample_scores(row, qt_ref, knt_ref, [kbuf.at[slot, p] for p in range(n_pages)],
                            pe_ref.at[r], stats_ref.at[r], idx_ref.at[r])
        for job in jobs[r * per_row:(r + 1) * per_row]:
            job()

    _guest_rows_ring(pl.program_id(0), guest_rows, start_row, wait_row, work)


def _in_proj(x, sc, sh, w_main, w_att_t, w_gate, b_gate, tm, rows_per_mod, out_dtypes, transposed=(), guest=None):
    t, d = x.shape
    steps = t // tm
    assert MK_GROUP not in transposed
    if rows_per_mod is None:
        mod_spec = pl.BlockSpec((tm, d), lambda i, *_: (i, 0))
        t_shape, t_spec = (PROJ_GROUP, t), pl.BlockSpec((PROJ_GROUP, tm), lambda i, *_: (0, i))
    else:
        per = rows_per_mod // tm
        mod_spec = pl.BlockSpec((None, 1, d), lambda i, *_: (i // per, 0, 0))
        t_shape = (t // rows_per_mod, PROJ_GROUP, rows_per_mod)
        t_spec = pl.BlockSpec((None, PROJ_GROUP, tm), lambda i, *_: (i // per, 0, i % per))
    out_shape, out_specs = [], []
    for gi, dt in enumerate(out_dtypes):
        if gi in transposed:
            out_shape.append(jax.ShapeDtypeStruct(t_shape, dt))
            out_specs.append(t_spec)
        else:
            out_shape.append(jax.ShapeDtypeStruct((t, PROJ_GROUP), dt))
            out_specs.append(pl.BlockSpec((tm, PROJ_GROUP), lambda i, *_: (i, 0)))
    out_shape.append(jax.ShapeDtypeStruct((t, LANE), F32))
    out_specs.append(pl.BlockSpec((tm, LANE), lambda i, *_: (i, 0)))
    in_specs = [pl.BlockSpec((tm, d), lambda i, *_: (i, 0)), mod_spec, mod_spec,
                _const_spec(w_main.shape), _const_spec(w_att_t.shape),
                _const_spec(w_gate.shape), _const_spec(b_gate.shape)]
    args = (x, sc, sh, w_main, w_att_t, w_gate, b_gate)
    assert len(in_specs) == N_PROJ_INPUTS
    body = functools.partial(_in_proj_kernel, transposed=tuple(transposed), guest_rows=0)
    if guest is None:
        return pl.pallas_call(body, grid=(steps,), in_specs=in_specs, out_specs=out_specs, out_shape=out_shape,
                              compiler_params=_params("arbitrary"))(*args)

    qt, knt, cache_k, page_table = guest
    w, bs = qt.shape
    n_pages = page_table.shape[1]
    guest_rows = bs // steps
    assert bs % steps == 0 and guest_rows % 2 == 0 and n_pages % PAGES_PER_BLOCK == 0
    assert MOBA_TOPK <= n_pages // PAGES_PER_BLOCK <= LANE and bs == LANE
    ck = _cache_pages(cache_k).reshape(cache_k.shape[0], w, PAGE_SIZE)
    rows = lambda *dims: pl.BlockSpec((guest_rows,) + dims, lambda i, *_: (i,) + (0,) * len(dims))
    whole = pl.BlockSpec((w, bs), lambda *_: (0, 0))
    return pl.pallas_call(
        functools.partial(body, guest_rows=guest_rows),
        grid_spec=pltpu.PrefetchScalarGridSpec(
            num_scalar_prefetch=1,
            grid=(steps,),
            in_specs=in_specs + [whole, whole, pl.BlockSpec(memory_space=pl.ANY)],
            out_specs=out_specs + [rows(n_pages, H_ATT, LANE), rows(H_ATT, LANE), rows(H_ATT, LANE)],
            scratch_shapes=[pltpu.VMEM((2, n_pages, w, PAGE_SIZE), F32), pltpu.SemaphoreType.DMA((2,))],
        ),
        out_shape=out_shape + [jax.ShapeDtypeStruct((bs, n_pages, H_ATT, LANE), F32),
                               jax.ShapeDtypeStruct((bs, H_ATT, LANE), F32),
                               jax.ShapeDtypeStruct((bs, H_ATT, LANE), jnp.int32)],
        compiler_params=_params("arbitrary"),
    )(page_table, *args, qt, knt, ck)


def _moba_prompt_kernel(q_ref, kt_ref, vt_ref, o_ref, kaug_ref, vaug_ref, kmt_ref, lhs_sc, m_sc, acc_sc, *, nb):
    i = pl.program_id(1)
    blk = MOBA_BLOCK
    half = LANE // 2
    w = q_ref.shape[1]

    @pl.when(i == 0)
    def _prepare_batch():
        srow = lax.broadcasted_iota(jnp.int32, (LANE, blk), 0)
        in_lo = srow < half
        head_of_row = lax.broadcasted_iota(jnp.int32, (w, LANE), 0) // DH_ATT
        lane_w = lax.broadcasted_iota(jnp.int32, (w, LANE), 1)
        kmt = jnp.zeros((w, LANE), F32)
        for j in range(nb):
            ktj = kt_ref[:, j * blk:(j + 1) * blk]
            vtj = vt_ref[:, j * blk:(j + 1) * blk]
            col = jnp.mean(ktj, axis=1, keepdims=True)
            kmt = jnp.where((lane_w % SUBLANE == j) & (lane_w // SUBLANE == head_of_row), col, kmt)
            for p in range(H_ATT // 2):
                kp, vp = ktj[p * LANE:(p + 1) * LANE, :], vtj[p * LANE:(p + 1) * LANE, :]
                kaug_ref[2 * p, j] = jnp.where(in_lo, kp, jnp.where(srow == half + j, 1.0, 0.0)).astype(BF16)
                kaug_ref[2 * p + 1, j] = jnp.where(in_lo, jnp.where(srow == j, 1.0, 0.0), kp).astype(BF16)
                vaug_ref[2 * p, j] = jnp.where(in_lo, vp, 1.0).astype(BF16)
                vaug_ref[2 * p + 1, j] = jnp.where(in_lo, 1.0, vp).astype(BF16)
        km_hi = kmt.astype(BF16)
        kmt_ref[0] = km_hi
        kmt_ref[1] = (kmt - km_hi.astype(F32)).astype(BF16)

    q32 = q_ref[...]
    q_hi = q32.astype(BF16)
    q_lo = (q32 - q_hi.astype(F32)).astype(BF16)
    sc = (jnp.dot(q_hi, kmt_ref[0], preferred_element_type=F32)
          + (jnp.dot(q_hi, kmt_ref[1], preferred_element_type=F32)
             + jnp.dot(q_lo, kmt_ref[0], preferred_element_type=F32)))
    sc_t = sc.T
    nidx = lax.broadcasted_iota(jnp.int32, (SUBLANE, blk), 0)
    past = nidx < i
    biases = []
    for h in range(H_ATT):
        val = jnp.where(past, sc_t[h * SUBLANE:(h + 1) * SUBLANE, :], NEG)
        keep = (_top_blocks(val, nidx) & past) | (nidx == i)
        biases.append(jnp.where(keep, 0.0, NEG))

    lane = lax.broadcasted_iota(jnp.int32, (blk, LANE), 1)
    lo_lanes = lane < half
    zpad = jnp.zeros((half - SUBLANE, blk), F32)
    for p in range(H_ATT // 2):
        bias_p = jnp.concatenate([biases[2 * p + 1], zpad, biases[2 * p], zpad], axis=0).T
        qp = q32[:, p * LANE:(p + 1) * LANE] * (DH_ATT ** -0.5 * LOG2E)
        lhs_sc[2 * p] = jnp.where(lo_lanes, qp, bias_p).astype(BF16)
        lhs_sc[2 * p + 1] = jnp.where(lo_lanes, bias_p, qp).astype(BF16)

    def scores(h, j):
        return jnp.dot(lhs_sc[h], kaug_ref[h, j], preferred_element_type=F32)

    def row_max(s):
        return jnp.broadcast_to(jnp.max(s, axis=1, keepdims=True), (blk, LANE))

    def weights(s, m):
        return jnp.exp2(s - jnp.concatenate([m, m], axis=1)).astype(BF16)

    row = lax.broadcasted_iota(jnp.int32, (blk, blk), 0)
    col = lax.broadcasted_iota(jnp.int32, (blk, blk), 1)
    causal = col <= row
    for h in range(H_ATT):
        s = jnp.where(causal, scores(h, i), NEG)
        m = row_max(s)
        acc_sc[h] = lax.dot_general(weights(s, m), vaug_ref[h, i], _NT, preferred_element_type=F32)
        m_sc[h] = m

    def past_block(j, carry):
        for h in range(H_ATT):
            s = scores(h, j)
            m_old = m_sc[h]
            m_new = jnp.maximum(m_old, row_max(s))
            acc_sc[h] = (jnp.exp2(m_old - m_new) * acc_sc[h]
                         + lax.dot_general(weights(s, m_new), vaug_ref[h, j], _NT, preferred_element_type=F32))
            m_sc[h] = m_new
        return carry

    lax.fori_loop(0, i, past_block, 0)

    for p in range(H_ATT // 2):
        acc_e, acc_o = acc_sc[2 * p], acc_sc[2 * p + 1]
        num = jnp.where(lo_lanes, acc_e, acc_o)
        den = pltpu.roll(jnp.where(lo_lanes, acc_o, acc_e), half, 1)
        o_ref[:, p * LANE:(p + 1) * LANE] = (num / den).astype(o_ref.dtype)


def _moba_prompt(q, kt, vt):
    b, s, w = q.shape
    nb = s // MOBA_BLOCK
    assert s % MOBA_BLOCK == 0 and nb <= SUBLANE and w == ATT_WIDTH
    blk_state = lambda dt: pltpu.VMEM((H_ATT, MOBA_BLOCK, LANE), dt)
    return pl.pallas_call(
        functools.partial(_moba_prompt_kernel, nb=nb),
        grid=(b, nb),
        in_specs=[pl.BlockSpec((None, MOBA_BLOCK, w), lambda bi, i: (bi, i, 0)),
                  pl.BlockSpec((None, w, s), lambda bi, i: (bi, 0, 0)),
                  pl.BlockSpec((None, w, s), lambda bi, i: (bi, 0, 0))],
        out_specs=pl.BlockSpec((None, MOBA_BLOCK, w), lambda bi, i: (bi, i, 0)),
        out_shape=jax.ShapeDtypeStruct((b, s, w), BF16),
        scratch_shapes=[pltpu.VMEM((H_ATT, nb, LANE, MOBA_BLOCK), BF16),
                        pltpu.VMEM((H_ATT, nb, LANE, MOBA_BLOCK), BF16),
                        pltpu.VMEM((2, w, LANE), BF16),
                        blk_state(BF16), blk_state(F32), blk_state(F32)],
        compiler_params=_params("arbitrary", "arbitrary"),
    )(q, kt, vt)


def _mlstm_head_out(hh, gain, ogate):
    mu = jnp.mean(hh, axis=-1, keepdims=True)
    d = hh - mu
    var = jnp.mean(d * d, axis=-1, keepdims=True)
    return d * lax.rsqrt(var + LN_EPS) * gain * jax.nn.sigmoid(ogate)


def _mlstm_prompt_kernel(q_ref, k_ref, v_ref, o_ref, g_ref, gain_ref,
                         mem_ref, c_out, n_out, m_out, c_sc, n_sc, m_sc):
    c = pl.program_id(1)
    nbat, L = q_ref.shape[0], q_ref.shape[1]

    @pl.when(c == 0)
    def _reset_state():
        c_sc[...] = jnp.zeros_like(c_sc)
        n_sc[...] = jnp.zeros_like(n_sc)
        m_sc[...] = jnp.zeros_like(m_sc)

    row = lax.broadcasted_iota(jnp.int32, (L, L), 0)
    col = lax.broadcasted_iota(jnp.int32, (L, L), 1)
    causal = col <= row
    lower = jnp.where(causal, 1.0, 0.0)
    upper = jnp.where(row <= col, 1.0, 0.0)
    ones = jnp.ones((L, DV_MLSTM), BF16)

    for bi in range(nbat):
        g = g_ref[bi]
        g_t = g.T
        b_col_all = jnp.dot(lower, g, precision=HIGHEST, preferred_element_type=F32)
        b_row_all = jnp.dot(g_t[0:SUBLANE, :], upper, precision=HIGHEST, preferred_element_type=F32)
        for h in range(H_MLSTM):
            lanes = slice(h * DK_MLSTM, (h + 1) * DK_MLSTM)
            ig_row = g_t[h:h + 1, :]
            b_row = b_row_all[H_MLSTM + h:H_MLSTM + h + 1, :]
            ig = jnp.broadcast_to(g[:, h:h + 1], (L, LANE))
            b = jnp.broadcast_to(b_col_all[:, H_MLSTM + h:H_MLSTM + h + 1], (L, LANE))
            m_prev = m_sc[bi, h:h + 1, :]
            dmat = jnp.where(causal, b - b_row + ig_row, NEG)
            m_inter = b + m_prev
            m_t = jnp.maximum(m_inter, jnp.broadcast_to(jnp.max(dmat, axis=1, keepdims=True), (L, LANE)))
            w_inter = jnp.exp(m_inter - m_t)
            qh, kh, vh = q_ref[bi, :, lanes], k_ref[bi, :, lanes], v_ref[bi, :, lanes]
            a = jnp.exp(dmat - m_t) * lax.dot_general(qh, kh, _NT, preferred_element_type=F32)
            c_prev = c_sc[bi, h]
            n_prev = n_sc[bi, h:h + 1, :]
            state = jnp.concatenate([c_prev, jnp.broadcast_to(n_prev, (DV_MLSTM, DK_MLSTM))], axis=0).astype(BF16)
            num_den = (jnp.concatenate([w_inter, w_inter], axis=1)
                       * lax.dot_general(qh, state, _NT, preferred_element_type=F32)
                       + jnp.dot(a.astype(BF16), jnp.concatenate([vh, ones], axis=1), preferred_element_type=F32))
            hh = num_den[:, :DV_MLSTM] / jnp.maximum(jnp.abs(num_den[:, DV_MLSTM:]), jnp.exp(-m_t))
            mem_ref[bi, :, lanes] = _mlstm_head_out(
                hh, gain_ref[:, lanes], o_ref[bi, :, lanes].astype(F32)).astype(mem_ref.dtype)

            m_new = m_t[L - 1:L, :]
            b_last = b[L - 1:L, :]
            g_inter = jnp.exp(b_last + m_prev - m_new)
            g_in = jnp.exp(b_last - b + ig - m_new)
            v_scaled = (vh.astype(F32) * g_in).astype(BF16)
            c_sc[bi, h] = g_inter * c_prev + lax.dot_general(v_scaled, kh, _TN, preferred_element_type=F32)
            n_sc[bi, h:h + 1, :] = g_inter * n_prev + jnp.sum(kh.astype(F32) * g_in, axis=0, keepdims=True)
            m_sc[bi, h:h + 1, :] = m_new

    @pl.when(c == pl.num_programs(1) - 1)
    def _emit_state():
        c_out[...] = c_sc[...]
        n_out[...] = n_sc[...]
        m_out[...] = m_sc[...]


def _mlstm_prompt(mq, mk, mv, mo, gates, gain):
    b, s, w = mq.shape
    L = MLSTM_CHUNK
    nbat = 2 if b % 2 == 0 else 1
    assert s % L == 0 and w == MLSTM_WIDTH and L == LANE == DK_MLSTM == DV_MLSTM
    tok = lambda width: pl.BlockSpec((nbat, L, width), lambda bi, c: (bi, c, 0))
    state = lambda *dims: pl.BlockSpec((nbat,) + dims, lambda bi, c: (bi,) + (0,) * len(dims))
    return pl.pallas_call(
        _mlstm_prompt_kernel,
        grid=(b // nbat, s // L),
        in_specs=[tok(w), tok(w), tok(w), tok(w), tok(LANE), pl.BlockSpec((1, w), lambda bi, c: (0, 0))],
        out_specs=[tok(w), state(H_MLSTM, DV_MLSTM, DK_MLSTM), state(SUBLANE, LANE), state(SUBLANE, LANE)],
        out_shape=[jax.ShapeDtypeStruct((b, s, w), BF16),
                   jax.ShapeDtypeStruct((b, H_MLSTM, DV_MLSTM, DK_MLSTM), F32),
                   jax.ShapeDtypeStruct((b, SUBLANE, LANE), F32),
                   jax.ShapeDtypeStruct((b, SUBLANE, LANE), F32)],
        scratch_shapes=[pltpu.VMEM((nbat, H_MLSTM, DV_MLSTM, DK_MLSTM), F32),
                        pltpu.VMEM((nbat, SUBLANE, LANE), F32),
                        pltpu.VMEM((nbat, SUBLANE, LANE), F32)],
        compiler_params=_params("arbitrary", "arbitrary"),
    )(mq, mk, mv, mo, gates, gain)


def _head_sublane(h):
    return (H_ATT // 2 - 1 - h) if h < H_ATT // 2 else (H_ATT + H_ATT // 2 - 1 - h)


def _head_rows(x):
    parts = []
    for h in range(H_ATT):
        tiles = [x[h * DH_ATT + SUBLANE * t:h * DH_ATT + SUBLANE * (t + 1), :] for t in range(DH_ATT // SUBLANE)]
        parts.append(sum(tiles[1:], tiles[0]))
    sub = lax.broadcasted_iota(jnp.int32, parts[0].shape, 0)
    folded = [p + pltpu.roll(p, 4, 0) for p in parts]
    quads = [jnp.where(sub < 4, folded[i], folded[i + 4]) for i in range(4)]
    take_up = (sub & 2) != 0
    pairs = [jnp.where(take_up, quads[i] + pltpu.roll(quads[i], 2, 0),
                       quads[i + 2] + pltpu.roll(quads[i + 2], 6, 0)) for i in range(2)]
    return jnp.where((sub & 1) != 0, pairs[0] + pltpu.roll(pairs[0], 1, 0), pairs[1] + pltpu.roll(pairs[1], 7, 0))


def _moba_sample_scores(b, qt_ref, knt_ref, kp_refs, pe_ref, stats_ref, idx_ref):
    n_pages = len(kp_refs)
    n_blocks = n_pages // PAGES_PER_BLOCK
    w = qt_ref.shape[0]
    on_b = lax.broadcasted_iota(jnp.int32, (w, LANE), 1) == b

    def column(ref):
        return jnp.sum(jnp.where(on_b, ref[...], 0.0), axis=1, keepdims=True)

    q_col = column(qt_ref) * (DH_ATT ** -0.5)
    q_wide = jnp.broadcast_to(q_col, (w, LANE))
    s_own = _head_rows(jnp.broadcast_to(q_col * column(knt_ref), (w, LANE)))[:, 0:1]
    s_pages = [_head_rows(kp_refs[p][...] * q_wide) for p in range(n_pages)]

    blk = [jnp.sum(sum(s_pages[n * PAGES_PER_BLOCK + 1:(n + 1) * PAGES_PER_BLOCK], s_pages[n * PAGES_PER_BLOCK]),
                   axis=1, keepdims=True) for n in range(n_blocks)]
    lane = lax.broadcasted_iota(jnp.int32, (H_ATT, LANE), 1)
    sel, ranked = [], jnp.zeros((H_ATT, LANE), jnp.int32)
    for n in range(n_blocks):
        rank = jnp.zeros((H_ATT, 1), jnp.int32)
        for o in range(n_blocks):
            if o != n:
                beats = (blk[o] >= blk[n]) if o < n else (blk[o] > blk[n])
                rank = rank + jnp.where(beats, 1, 0)
        sel.append(rank < MOBA_TOPK)
        ranked = jnp.where(rank == lane, n, ranked)
    m = s_own
    for p in range(n_pages):
        page_max = jnp.max(s_pages[p], axis=1, keepdims=True)
        m = jnp.maximum(m, jnp.where(sel[p // PAGES_PER_BLOCK], page_max, NEG))
    p_own = jnp.exp(s_own - m)
    total = jnp.zeros((H_ATT, LANE), F32)
    for p in range(n_pages):
        pe = jnp.where(sel[p // PAGES_PER_BLOCK], jnp.exp(s_pages[p] - m), 0.0)
        pe_ref[p] = pe
        total = total + pe
    row_sum = p_own + jnp.sum(total, axis=1, keepdims=True)
    stats_ref[...] = jnp.where(lane == 0, p_own, row_sum)
    idx_ref[...] = ranked


N_VALUE_CHUNKS = MOBA_TOPK * PAGES_PER_BLOCK


def _value_chunk_page(sel_ref, row, h, c):
    return sel_ref[row, h * MOBA_TOPK + c // PAGES_PER_BLOCK] * PAGES_PER_BLOCK + c % PAGES_PER_BLOCK


def _moba_sample_mix_row(row, sel_ref, pe_ref, stats_ref, vnt_ref, chunks_ref, o_ref):
    w = vnt_ref.shape[0]
    on_row = lax.broadcasted_iota(jnp.int32, (w, LANE), 1) == row
    vn_col = jnp.sum(jnp.where(on_row, vnt_ref[...], 0.0), axis=1, keepdims=True)
    stats = stats_ref[...]
    out_cols = []
    for h in range(H_ATT):
        r = _head_sublane(h)
        acc = jnp.zeros((DH_ATT, LANE), F32)
        for c in range(N_VALUE_CHUNKS):
            acc = acc + (pe_ref[_value_chunk_page(sel_ref, row, h, c), r:r + 1, :]
                         * chunks_ref[h * N_VALUE_CHUNKS + c])
        p_own, row_sum = stats[r:r + 1, 0:1], stats[r:r + 1, 1:2]
        rows = slice(h * DH_ATT, (h + 1) * DH_ATT)
        out_cols.append((jnp.sum(acc, axis=1, keepdims=True) + p_own * vn_col[rows, :]) / row_sum)
    o_ref[...] = jnp.where(on_row, jnp.concatenate(out_cols, axis=0), o_ref[...])


def _cache_pages(cache):
    return jnp.transpose(cache, (0, 2, 3, 1))


def _selected_blocks(ranked):
    b = ranked.shape[0]
    return jnp.stack([ranked[:, _head_sublane(h), :MOBA_TOPK] for h in range(H_ATT)], axis=1).reshape(b, -1)


def _mlstm_sample_kernel(q_ref, k_ref, v_ref, o_ref, g_ref, gain_ref, c0_ref, n0_ref, m0_ref,
                         mem_ref, c_ref, n_ref, m_ref):
    tb = q_ref.shape[0]
    g = g_ref[...]
    sub = lax.broadcasted_iota(jnp.int32, (2 * tb, LANE), 0)
    zrows = jnp.zeros((tb, LANE), F32)
    for h in range(H_MLSTM):
        lanes = slice(h * DK_MLSTM, (h + 1) * DK_MLSTM)
        ig, lf, m0 = g[:, h:h + 1], g[:, H_MLSTM + h:H_MLSTM + h + 1], m0_ref[:, h:h + 1]
        q, k, v = q_ref[:, lanes], k_ref[:, lanes], v_ref[:, lanes]
        n0 = n0_ref[:, lanes]
        m_t = jnp.maximum(lf + m0, ig)
        w_inter = jnp.exp(lf + m0 - m_t)
        g_in = jnp.exp(ig - m_t)
        a = g_in * jnp.sum(q * k, axis=1, keepdims=True)
        den = w_inter * jnp.sum(n0 * q, axis=1, keepdims=True) + a
        q_b = q.astype(BF16)
        gv = jnp.concatenate([g_in * v, zrows], axis=0)
        k_b = jnp.concatenate([k, zrows], axis=0).astype(BF16)
        cq_rows = []
        for r in range(tb):
            c_prev = c0_ref[r, h]
            cq_rows.append(lax.dot_general(q_b, c_prev.astype(BF16), _NT, preferred_element_type=F32)[r:r + 1, :])
            outer = lax.dot_general(jnp.where(sub == r, gv, 0.0).astype(BF16), k_b, _TN,
                                    preferred_element_type=F32)
            c_ref[r, h] = w_inter[r:r + 1, :] * c_prev + outer
        cq = jnp.concatenate(cq_rows, axis=0)
        hh = (w_inter * cq + a * v) / jnp.maximum(jnp.abs(den), jnp.exp(-m_t))
        mem_ref[:, lanes] = _mlstm_head_out(hh, gain_ref[:, lanes], o_ref[:, lanes]).astype(mem_ref.dtype)
        n_ref[:, lanes] = w_inter * n0 + g_in * k
        m_ref[:, h:h + 1] = m_t


def _mlstm_sample(mq, mk, mv, mo, gates, gain, c0, n0, m0):
    b, w = mq.shape
    tb = SUBLANE
    rows = lambda width: pl.BlockSpec((tb, width), lambda i: (i, 0))
    c_spec = pl.BlockSpec((tb, H_MLSTM, DV_MLSTM, DK_MLSTM), lambda i: (i, 0, 0, 0))
    return pl.pallas_call(
        _mlstm_sample_kernel,
        grid=(b // tb,),
        in_specs=[rows(w), rows(w), rows(w), rows(w), rows(LANE), pl.BlockSpec((1, w), lambda i: (0, 0)),
                  c_spec, rows(w), rows(H_MLSTM)],
        out_specs=[rows(w), c_spec, rows(w), rows(H_MLSTM)],
        out_shape=[jax.ShapeDtypeStruct((b, w), F32),
                   jax.ShapeDtypeStruct(c0.shape, F32),
                   jax.ShapeDtypeStruct((b, w), F32),
                   jax.ShapeDtypeStruct((b, H_MLSTM), F32)],
        compiler_params=_params("arbitrary"),
    )(mq, mk, mv, mo, gates, gain, c0, n0.reshape(b, w), m0)


N_FINISH_INPUTS = 15


def _finish_kernel(*refs, alpha, ff_chunk, att_transposed, guest_rows):
    if guest_rows:
        pt_ref, sel_ref, refs = refs[0], refs[1], refs[2:]
    (x_ref, att_ref, mem_ref, g1_ref, sh2_ref, sc2_ref, g2_ref, wo_ref, ln1g_ref, ln1b_ref,
     wg_ref, wu_ref, wd_ref, ln2g_ref, ln2b_ref) = refs[:N_FINISH_INPUTS]
    if guest_rows:
        pe_ref, stats_ref, vnt_ref, cv_ref, y_ref, o_ref, vbuf, sem = refs[N_FINISH_INPUTS:]
    else:
        (y_ref,) = refs[N_FINISH_INPUTS:]

    att = att_ref[...].T if att_transposed else att_ref[...]
    aw = att.shape[1]
    mix = (jnp.dot(att.astype(BF16), wo_ref[0:aw, :], preferred_element_type=F32)
           + jnp.dot(mem_ref[...].astype(BF16), wo_ref[aw:, :], preferred_element_type=F32))
    x1 = _layernorm(alpha * x_ref[...] + (1.0 + g1_ref[...]) * mix, ln1g_ref[...], ln1b_ref[...])
    h2 = (x1 * (1.0 + sc2_ref[...]) + sh2_ref[...]).astype(BF16)
    n_ff = wg_ref.shape[1] // ff_chunk

    def ffn_chunk(c):
        cols = slice(c * ff_chunk, (c + 1) * ff_chunk)
        gate = jnp.dot(h2, wg_ref[:, cols], preferred_element_type=F32)
        up = jnp.dot(h2, wu_ref[:, cols], preferred_element_type=F32)
        act = (gate * jax.nn.sigmoid(gate) * up).astype(BF16)
        return jnp.dot(act, wd_ref[cols, :], preferred_element_type=F32)

    ffn = [jnp.zeros(x1.shape, F32)]
    if not guest_rows:
        for c in range(n_ff):
            ffn[0] = ffn[0] + ffn_chunk(c)
    else:
        i = pl.program_id(0)
        per_row = -(-n_ff // guest_rows)

        @pl.when(i == 0)
        def _init_out():
            o_ref[...] = jnp.zeros_like(o_ref)

        def chunk_copy(row, slot, h, c):
            page = pt_ref[row, _value_chunk_page(sel_ref, row, h, c)]
            return pltpu.make_async_copy(cv_ref.at[page, h], vbuf.at[slot, h * N_VALUE_CHUNKS + c], sem.at[slot])

        def start_row(row, slot):
            for h in range(H_ATT):
                for c in range(N_VALUE_CHUNKS):
                    chunk_copy(row, slot, h, c).start()

        def wait_row(row, slot):
            for h in range(H_ATT):
                for c in range(N_VALUE_CHUNKS):
                    chunk_copy(row, slot, h, c).wait()

        def work(r, row, slot):
            _moba_sample_mix_row(row, sel_ref, pe_ref.at[r], stats_ref.at[r], vnt_ref, vbuf.at[slot], o_ref)
            for c in range(r * per_row, min((r + 1) * per_row, n_ff)):
                ffn[0] = ffn[0] + ffn_chunk(c)

        _guest_rows_ring(i, guest_rows, start_row, wait_row, work)
    y_ref[...] = _layernorm(alpha * x1 + (1.0 + g2_ref[...]) * ffn[0], ln2g_ref[...], ln2b_ref[...])


def _finish(x, att, mem, mods, weights, tm, rows_per_mod, alpha, att_transposed=False, guest=None):
    t, d = x.shape
    assert not att_transposed or tm == t
    steps = t // tm
    w_out, ln1_g, ln1_b, w_gate, w_up, w_down, ln2_g, ln2_b = weights
    if rows_per_mod is None:
        mod_spec = pl.BlockSpec((tm, d), lambda i, *_: (i, 0))
    else:
        per = rows_per_mod // tm
        mod_spec = pl.BlockSpec((None, 1, d), lambda i, *_: (i // per, 0, 0))
    tok = lambda width: pl.BlockSpec((tm, width), lambda i, *_: (i, 0))
    ff_chunk = 256
    assert w_gate.shape[1] % ff_chunk == 0
    in_specs = [tok(d), pl.BlockSpec(att.shape, lambda *_: (0, 0)) if att_transposed else tok(att.shape[1]),
                tok(mem.shape[1]), mod_spec, mod_spec, mod_spec, mod_spec,
                _const_spec(w_out.shape), _const_spec(ln1_g.shape), _const_spec(ln1_b.shape),
                _const_spec(w_gate.shape), _const_spec(w_up.shape), _const_spec(w_down.shape),
                _const_spec(ln2_g.shape), _const_spec(ln2_b.shape)]
    args = (x, att, mem, *mods, w_out, ln1_g, ln1_b, w_gate, w_up, w_down, ln2_g, ln2_b)
    assert len(in_specs) == N_FINISH_INPUTS
    body = functools.partial(_finish_kernel, alpha=alpha, ff_chunk=ff_chunk, att_transposed=att_transposed,
                             guest_rows=0)
    y_shape = jax.ShapeDtypeStruct((t, d), F32)
    if guest is None:
        return pl.pallas_call(body, grid=(steps,), in_specs=in_specs, out_specs=tok(d), out_shape=y_shape,
                              compiler_params=_params("arbitrary"))(*args)

    pe, stats, sel, vnt, cache_v, page_table = guest
    w, bs = vnt.shape
    n_pages = page_table.shape[1]
    guest_rows = bs // steps
    assert bs % steps == 0 and guest_rows % 2 == 0 and bs == LANE
    rows = lambda *dims: pl.BlockSpec((guest_rows,) + dims, lambda i, *_: (i,) + (0,) * len(dims))
    whole = pl.BlockSpec((w, bs), lambda *_: (0, 0))
    return pl.pallas_call(
        functools.partial(body, guest_rows=guest_rows),
        grid_spec=pltpu.PrefetchScalarGridSpec(
            num_scalar_prefetch=2,
            grid=(steps,),
            in_specs=in_specs + [rows(n_pages, H_ATT, LANE), rows(H_ATT, LANE), whole,
                                 pl.BlockSpec(memory_space=pl.ANY)],
            out_specs=[tok(d), whole],
            scratch_shapes=[pltpu.VMEM((2, H_ATT * N_VALUE_CHUNKS, DH_ATT, PAGE_SIZE), F32),
                            pltpu.SemaphoreType.DMA((2,))],
        ),
        out_shape=[y_shape, jax.ShapeDtypeStruct((w, bs), F32)],
        compiler_params=_params("arbitrary"),
    )(page_table, sel, *args, pe, stats, vnt, _cache_pages(cache_v))


def kernel(x_prompt, x_sample, cache_k, cache_v, state_C, state_n, state_m, page_table, c_prompt, c_sample,
           w_ada, b_ada, w_in, b_if, mlstm_norm_g, w_out, ln1_g, ln1_b, w_gate, w_up, w_down, ln2_g, ln2_b):
    depth = w_in.shape[0]
    assert depth == 1, "single-layer step"
    alpha = (2.0 * depth) ** 0.25
    bp, s, d = x_prompt.shape
    bs = x_sample.shape[0]
    assert x_sample.shape[1] == 1, "single-token decode step"
    n_main = N_PROJ_GROUPS * PROJ_GROUP

    w_main = w_in[0, :, :n_main].astype(BF16)
    w_att_t = w_in[0, :, :3 * PROJ_GROUP].T.astype(BF16)
    w_gates = jnp.pad(w_in[0, :, n_main:], ((0, 0), (0, LANE - 2 * H_MLSTM))).astype(BF16)
    b_gates = jnp.pad(b_if[0], (0, LANE - 2 * H_MLSTM)).reshape(1, LANE)
    gain = mlstm_norm_g[0].reshape(1, MLSTM_WIDTH)
    row = lambda a: a[0].reshape(1, -1)
    fin_w = (w_out[0].astype(BF16), row(ln1_g), row(ln1_b), w_gate[0].astype(BF16), w_up[0].astype(BF16),
             w_down[0].astype(BF16), row(ln2_g), row(ln2_b))

    c_all = jnp.concatenate([c_prompt, c_sample], axis=0)
    mod = _adaln(c_all, w_ada[0], b_ada[0])
    sh1, sc1, g1, sh2, sc2, g2 = (mod[:, i * d:(i + 1) * d] for i in range(6))
    pm = lambda a: a[:bp].reshape(bp, 1, d)
    sm = lambda a: a[bp:]

    xs = x_sample.reshape(bs, d)
    aq_s, ak_s, av_s, mq_s, mk_s, mv_s, mo_s, gates_s = _in_proj(
        xs, sm(sc1), sm(sh1), w_main, w_att_t, w_gates, b_gates, bs, None, (F32,) * N_PROJ_GROUPS,
        transposed=(0, 1, 2))

    xp = x_prompt.reshape(bp * s, d)
    tm = 512
    aq, ak_t, av_t, mq, mk, mv, mo, gates, pe, stats, ranked = _in_proj(
        xp, pm(sc1), pm(sh1), w_main, w_att_t, w_gates, b_gates, tm, s,
        (F32, F32, F32, BF16, BF16, BF16, F32), transposed=(1, 2), guest=(aq_s, ak_s, cache_k[0], page_table))
    seq = lambda a: a.reshape(bp, s, a.shape[-1])
    att = _moba_prompt(seq(aq), ak_t, av_t)
    mem, c_p, n_p, m_p = _mlstm_prompt(seq(mq), seq(mk), seq(mv), seq(mo), seq(gates), gain)
    y_p, att_s = _finish(
        xp, att.reshape(bp * s, -1), mem.reshape(bp * s, -1), (pm(g1), pm(sh2), pm(sc2), pm(g2)),
        fin_w, tm, s, alpha, guest=(pe, stats, _selected_blocks(ranked), av_s, cache_v[0], page_table))

    mem_s, c_s, n_s, m_s = _mlstm_sample(mq_s, mk_s, mv_s, mo_s, gates_s, gain,
                                         state_C[0], state_n[0], state_m[0])
    y_s = _finish(xs, att_s, mem_s, (sm(g1), sm(sh2), sm(sc2), sm(g2)), fin_w, bs, None, alpha,
                  att_transposed=True)

    rows_p = lambda a: jnp.transpose(a.reshape(bp, H_ATT, DH_ATT, s), (0, 3, 1, 2))[None]
    rows_s = lambda a: jnp.transpose(a.reshape(H_ATT, DH_ATT, bs), (2, 0, 1)).reshape(1, bs, 1, H_ATT, DH_ATT)
    return (y_p.reshape(bp, s, d), y_s.reshape(bs, 1, d),
            rows_p(ak_t), rows_p(av_t),
            c_p[None], n_p[None, :, :H_MLSTM, :], m_p[None, :, :H_MLSTM, 0],
            rows_s(ak_s), rows_s(av_s),
            c_s[None], n_s.reshape(1, bs, H_MLSTM, DK_MLSTM), m_s[None])
```

```python
import functools
import math

import jax
import jax.numpy as jnp
from jax import lax
from jax.experimental import pallas as pl
from jax.experimental.pallas import tpu as pltpu

F32 = jnp.float32
BF16 = jnp.bfloat16
HIGHEST = lax.Precision.HIGHEST

LANE = 128
SUBLANE = 8
VMEM_LIMIT_BYTES = 56 * 1024 * 1024

H_ATT = 8
DH_ATT = 64
ATT_WIDTH = H_ATT * DH_ATT
MOBA_BLOCK = 256
MOBA_TOPK = 3
H_MLSTM = 4
DK_MLSTM = 128
DV_MLSTM = 128
MLSTM_WIDTH = H_MLSTM * DV_MLSTM
MLSTM_CHUNK = LANE
PAGE_SIZE = 128
PAGES_PER_BLOCK = MOBA_BLOCK // PAGE_SIZE
LN_EPS = 1e-5
NEG = -1e30
LOG2E = math.log2(math.e)
N_PROJ_GROUPS = 7
PROJ_GROUP = 512
MK_GROUP = 4

_NT = (((1,), (1,)), ((), ()))
_TN = (((0,), (0,)), ((), ()))


def _params(*sem):
    return pltpu.CompilerParams(dimension_semantics=sem, vmem_limit_bytes=VMEM_LIMIT_BYTES)


def _const_spec(shape):
    return pl.BlockSpec(shape, lambda *_: (0,) * len(shape), pipeline_mode=pl.Buffered(1))


def _layernorm(x, g, b):
    mu = jnp.mean(x, axis=-1, keepdims=True)
    d = x - mu
    var = jnp.mean(d * d, axis=-1, keepdims=True)
    return d * lax.rsqrt(var + LN_EPS) * g + b


def _top_blocks(val, nidx):
    cnt = jnp.zeros(val.shape, jnp.int32)
    for r in range(1, SUBLANE):
        other = pltpu.roll(val, r, 0)
        oidx = pltpu.roll(nidx, r, 0)
        beats = (other > val) | ((other == val) & (oidx < nidx))
        cnt = cnt + jnp.where(beats, 1, 0)
    return cnt < MOBA_TOPK


def _adaln_kernel(c_ref, w_ref, b_ref, o_ref):
    c = c_ref[...]
    s = c * jax.nn.sigmoid(c)
    o_ref[...] = jnp.dot(s, w_ref[...], preferred_element_type=F32) + b_ref[...]


def _adaln(c, w_ada, b_ada):
    rows, d = c.shape
    n = w_ada.shape[1]
    tn = d
    return pl.pallas_call(
        _adaln_kernel,
        grid=(n // tn,),
        in_specs=[pl.BlockSpec((rows, d), lambda j: (0, 0)),
                  pl.BlockSpec((d, tn), lambda j: (0, j)),
                  pl.BlockSpec((1, tn), lambda j: (0, j))],
        out_specs=pl.BlockSpec((rows, tn), lambda j: (0, j)),
        out_shape=jax.ShapeDtypeStruct((rows, n), F32),
        compiler_params=_params("arbitrary"),
    )(c, w_ada, b_ada.reshape(1, n))


def _guest_rows_ring(i, n_steps, guest_rows, start_row, wait_row, work):
    @pl.when(i == 0)
    def _first_row():
        start_row(0, 0)

    for r in range(guest_rows):
        row, slot = i * guest_rows + r, r % 2
        if r + 1 < guest_rows:
            start_row(row + 1, 1 - slot)
        else:
            @pl.when(i + 1 < n_steps)
            def _next_step_row():
                start_row(row + 1, 1 - slot)
        wait_row(row, slot)
        work(r, row, slot)


def _key_page_ring(pt_ref, ck_ref, kbuf, sem):
    n_pages = kbuf.shape[1]

    def page_copy(row, slot, p):
        return pltpu.make_async_copy(ck_ref.at[pt_ref[row, p]], kbuf.at[slot, p], sem.at[slot])

    def start_row(row, slot):
        for p in range(n_pages):
            page_copy(row, slot, p).start()

    def wait_row(row, slot):
        for p in range(n_pages):
            page_copy(row, slot, p).wait()

    return start_row, wait_row


N_PROJ_INPUTS = 7


def _in_proj_kernel(*refs, transposed, guest_rows):
    if guest_rows:
        pt_ref, refs = refs[0], refs[1:]
    x_ref, sc_ref, sh_ref, w_ref, wt_ref, wg_ref, bg_ref = refs[:N_PROJ_INPUTS]
    refs = refs[N_PROJ_INPUTS:]
    if guest_rows:
        (qt_ref, knt_ref, ck_ref), refs = refs[:3], refs[3:]
    proj_refs, g_ref = refs[:N_PROJ_GROUPS], refs[N_PROJ_GROUPS]
    h = (x_ref[...] * (1.0 + sc_ref[...]) + sh_ref[...]).astype(BF16)

    def project(gi):
        cols = slice(gi * PROJ_GROUP, (gi + 1) * PROJ_GROUP)
        if gi in transposed:
            y = lax.dot_general(wt_ref[cols, :], h, _NT, preferred_element_type=F32)
        else:
            y = jnp.dot(h, w_ref[:, cols], preferred_element_type=F32)
        if gi == MK_GROUP:
            y = y * (DK_MLSTM ** -0.5)
        proj_refs[gi][...] = y.astype(proj_refs[gi].dtype)

    def gates():
        g = jnp.dot(h, wg_ref[...], preferred_element_type=F32) + bg_ref[...]
        lane = lax.broadcasted_iota(jnp.int32, g.shape, 1)
        logsig = jnp.minimum(g, 0.0) - jnp.log1p(jnp.exp(-jnp.abs(g)))
        g_ref[...] = jnp.where(lane >= H_MLSTM, logsig, g)

    jobs = [functools.partial(project, gi) for gi in range(N_PROJ_GROUPS)] + [gates]
    if not guest_rows:
        for job in jobs:
            job()
        return

    pe_ref, stats_ref, idx_ref, kbuf, sem = refs[N_PROJ_GROUPS + 1:]
    per_row = -(-len(jobs) // guest_rows)

    def work(r, row, slot):
        _moba_sample_scores(row, qt_ref, knt_ref, [kbuf.at[slot, p] for p in range(kbuf.shape[1])],
                            pe_ref.at[r], stats_ref.at[r], idx_ref.at[r])
        for job in jobs[r * per_row:(r + 1) * per_row]:
            job()

    _guest_rows_ring(pl.program_id(0), pl.num_programs(0), guest_rows,
                     *_key_page_ring(pt_ref, ck_ref, kbuf, sem), work)


def _in_proj(x, sc, sh, w_main, w_att_t, w_gate, b_gate, tm, rows_per_mod, out_dtypes, transposed=(), guest=None):
    t, d = x.shape
    steps = t // tm
    assert MK_GROUP not in transposed
    if rows_per_mod is None:
        mod_spec = pl.BlockSpec((tm, d), lambda i, *_: (i, 0))
        t_shape, t_spec = (PROJ_GROUP, t), pl.BlockSpec((PROJ_GROUP, tm), lambda i, *_: (0, i))
    else:
        per = rows_per_mod // tm
        mod_spec = pl.BlockSpec((None, 1, d), lambda i, *_: (i // per, 0, 0))
        t_shape = (t // rows_per_mod, PROJ_GROUP, rows_per_mod)
        t_spec = pl.BlockSpec((None, PROJ_GROUP, tm), lambda i, *_: (i // per, 0, i % per))
    out_shape, out_specs = [], []
    for gi, dt in enumerate(out_dtypes):
        if gi in transposed:
            out_shape.append(jax.ShapeDtypeStruct(t_shape, dt))
            out_specs.append(t_spec)
        else:
            out_shape.append(jax.ShapeDtypeStruct((t, PROJ_GROUP), dt))
            out_specs.append(pl.BlockSpec((tm, PROJ_GROUP), lambda i, *_: (i, 0)))
    out_shape.append(jax.ShapeDtypeStruct((t, LANE), F32))
    out_specs.append(pl.BlockSpec((tm, LANE), lambda i, *_: (i, 0)))
    in_specs = [pl.BlockSpec((tm, d), lambda i, *_: (i, 0)), mod_spec, mod_spec,
                _const_spec(w_main.shape), _const_spec(w_att_t.shape),
                _const_spec(w_gate.shape), _const_spec(b_gate.shape)]
    args = (x, sc, sh, w_main, w_att_t, w_gate, b_gate)
    assert len(in_specs) == N_PROJ_INPUTS
    body = functools.partial(_in_proj_kernel, transposed=tuple(transposed), guest_rows=0)
    if guest is None:
        return pl.pallas_call(body, grid=(steps,), in_specs=in_specs, out_specs=out_specs, out_shape=out_shape,
                              compiler_params=_params("arbitrary"))(*args)

    qt, knt, cache_k, page_table = guest
    w, bs = qt.shape
    n_pages = page_table.shape[1]
    guest_rows = bs // steps
    assert bs % steps == 0 and guest_rows % 2 == 0 and n_pages % PAGES_PER_BLOCK == 0
    assert MOBA_TOPK <= n_pages // PAGES_PER_BLOCK <= LANE and bs == LANE
    ck = _cache_pages(cache_k).reshape(cache_k.shape[0], w, PAGE_SIZE)
    rows = lambda *dims: pl.BlockSpec((guest_rows,) + dims, lambda i, *_: (i,) + (0,) * len(dims))
    whole = pl.BlockSpec((w, bs), lambda *_: (0, 0))
    return pl.pallas_call(
        functools.partial(body, guest_rows=guest_rows),
        grid_spec=pltpu.PrefetchScalarGridSpec(
            num_scalar_prefetch=1,
            grid=(steps,),
            in_specs=in_specs + [whole, whole, pl.BlockSpec(memory_space=pl.ANY)],
            out_specs=out_specs + [rows(n_pages, H_ATT, LANE), rows(H_ATT, LANE), rows(H_ATT, LANE)],
            scratch_shapes=[pltpu.VMEM((2, n_pages, w, PAGE_SIZE), F32), pltpu.SemaphoreType.DMA((2,))],
        ),
        out_shape=out_shape + [jax.ShapeDtypeStruct((bs, n_pages, H_ATT, LANE), F32),
                               jax.ShapeDtypeStruct((bs, H_ATT, LANE), F32),
                               jax.ShapeDtypeStruct((bs, H_ATT, LANE), jnp.int32)],
        compiler_params=_params("arbitrary"),
    )(page_table, *args, qt, knt, ck)


def _moba_prompt_kernel(q_ref, kt_ref, vt_ref, o_ref, kaug_ref, vaug_ref, kmt_ref, lhs_sc, m_sc, acc_sc, *, nb):
    i = pl.program_id(1)
    blk = MOBA_BLOCK
    half = LANE // 2
    w = q_ref.shape[1]

    @pl.when(i == 0)
    def _prepare_batch():
        srow = lax.broadcasted_iota(jnp.int32, (LANE, blk), 0)
        in_lo = srow < half
        head_of_row = lax.broadcasted_iota(jnp.int32, (w, LANE), 0) // DH_ATT
        lane_w = lax.broadcasted_iota(jnp.int32, (w, LANE), 1)
        kmt = jnp.zeros((w, LANE), F32)
        for j in range(nb):
            ktj = kt_ref[:, j * blk:(j + 1) * blk]
            vtj = vt_ref[:, j * blk:(j + 1) * blk]
            col = jnp.mean(ktj, axis=1, keepdims=True)
            kmt = jnp.where((lane_w % SUBLANE == j) & (lane_w // SUBLANE == head_of_row), col, kmt)
            for p in range(H_ATT // 2):
                kp, vp = ktj[p * LANE:(p + 1) * LANE, :], vtj[p * LANE:(p + 1) * LANE, :]
                kaug_ref[2 * p, j] = jnp.where(in_lo, kp, jnp.where(srow == half + j, 1.0, 0.0)).astype(BF16)
                kaug_ref[2 * p + 1, j] = jnp.where(in_lo, jnp.where(srow == j, 1.0, 0.0), kp).astype(BF16)
                vaug_ref[2 * p, j] = jnp.where(in_lo, vp, 1.0).astype(BF16)
                vaug_ref[2 * p + 1, j] = jnp.where(in_lo, 1.0, vp).astype(BF16)
        km_hi = kmt.astype(BF16)
        kmt_ref[0] = km_hi
        kmt_ref[1] = (kmt - km_hi.astype(F32)).astype(BF16)

    q32 = q_ref[...]
    q_hi = q32.astype(BF16)
    q_lo = (q32 - q_hi.astype(F32)).astype(BF16)
    sc = (jnp.dot(q_hi, kmt_ref[0], preferred_element_type=F32)
          + (jnp.dot(q_hi, kmt_ref[1], preferred_element_type=F32)
             + jnp.dot(q_lo, kmt_ref[0], preferred_element_type=F32)))
    sc_t = sc.T
    nidx = lax.broadcasted_iota(jnp.int32, (SUBLANE, blk), 0)
    past = nidx < i
    biases = []
    for h in range(H_ATT):
        val = jnp.where(past, sc_t[h * SUBLANE:(h + 1) * SUBLANE, :], NEG)
        keep = (_top_blocks(val, nidx) & past) | (nidx == i)
        biases.append(jnp.where(keep, 0.0, NEG))

    lane = lax.broadcasted_iota(jnp.int32, (blk, LANE), 1)
    lo_lanes = lane < half
    zpad = jnp.zeros((half - SUBLANE, blk), F32)
    for p in range(H_ATT // 2):
        bias_p = jnp.concatenate([biases[2 * p + 1], zpad, biases[2 * p], zpad], axis=0).T
        qp = q32[:, p * LANE:(p + 1) * LANE] * (DH_ATT ** -0.5 * LOG2E)
        lhs_sc[2 * p] = jnp.where(lo_lanes, qp, bias_p).astype(BF16)
        lhs_sc[2 * p + 1] = jnp.where(lo_lanes, bias_p, qp).astype(BF16)

    def scores(h, j):
        return jnp.dot(lhs_sc[h], kaug_ref[h, j], preferred_element_type=F32)

    def row_max(s):
        return jnp.broadcast_to(jnp.max(s, axis=1, keepdims=True), (blk, LANE))

    def weights(s, m):
        return jnp.exp2(s - jnp.concatenate([m, m], axis=1)).astype(BF16)

    row = lax.broadcasted_iota(jnp.int32, (blk, blk), 0)
    col = lax.broadcasted_iota(jnp.int32, (blk, blk), 1)
    causal = col <= row
    for h in range(H_ATT):
        s = jnp.where(causal, scores(h, i), NEG)
        m = row_max(s)
        acc_sc[h] = lax.dot_general(weights(s, m), vaug_ref[h, i], _NT, preferred_element_type=F32)
        m_sc[h] = m

    def past_block(j, carry):
        for h in range(H_ATT):
            s = scores(h, j)
            m_old = m_sc[h]
            m_new = jnp.maximum(m_old, row_max(s))
            acc_sc[h] = (jnp.exp2(m_old - m_new) * acc_sc[h]
                         + lax.dot_general(weights(s, m_new), vaug_ref[h, j], _NT, preferred_element_type=F32))
            m_sc[h] = m_new
        return carry

    lax.fori_loop(0, i, past_block, 0)

    for p in range(H_ATT // 2):
        acc_e, acc_o = acc_sc[2 * p], acc_sc[2 * p + 1]
        num = jnp.where(lo_lanes, acc_e, acc_o)
        den = pltpu.roll(jnp.where(lo_lanes, acc_o, acc_e), half, 1)
        o_ref[:, p * LANE:(p + 1) * LANE] = (num / den).astype(o_ref.dtype)


def _moba_prompt(q, kt, vt):
    b, s, w = q.shape
    nb = s // MOBA_BLOCK
    assert s % MOBA_BLOCK == 0 and nb <= SUBLANE and w == ATT_WIDTH
    blk_state = lambda dt: pltpu.VMEM((H_ATT, MOBA_BLOCK, LANE), dt)
    return pl.pallas_call(
        functools.partial(_moba_prompt_kernel, nb=nb),
        grid=(b, nb),
        in_specs=[pl.BlockSpec((None, MOBA_BLOCK, w), lambda bi, i: (bi, i, 0)),
                  pl.BlockSpec((None, w, s), lambda bi, i: (bi, 0, 0)),
                  pl.BlockSpec((None, w, s), lambda bi, i: (bi, 0, 0))],
        out_specs=pl.BlockSpec((None, MOBA_BLOCK, w), lambda bi, i: (bi, i, 0)),
        out_shape=jax.ShapeDtypeStruct((b, s, w), BF16),
        scratch_shapes=[pltpu.VMEM((H_ATT, nb, LANE, MOBA_BLOCK), BF16),
                        pltpu.VMEM((H_ATT, nb, LANE, MOBA_BLOCK), BF16),
                        pltpu.VMEM((2, w, LANE), BF16),
                        blk_state(BF16), blk_state(F32), blk_state(F32)],
        compiler_params=_params("arbitrary", "arbitrary"),
    )(q, kt, vt)


def _mlstm_head_out(hh, gain, ogate):
    mu = jnp.mean(hh, axis=-1, keepdims=True)
    d = hh - mu
    var = jnp.mean(d * d, axis=-1, keepdims=True)
    return d * lax.rsqrt(var + LN_EPS) * gain * jax.nn.sigmoid(ogate)


N_MLSTM_INPUTS = 6


def _mlstm_prompt_kernel(*refs, guest):
    if guest:
        pt_ref, refs = refs[0], refs[1:]
    q_ref, k_ref, v_ref, o_ref, g_ref, gain_ref = refs[:N_MLSTM_INPUTS]
    refs = refs[N_MLSTM_INPUTS:]
    if guest:
        (qt_ref, knt_ref, ck_ref), refs = refs[:3], refs[3:]
        mem_ref, c_out, n_out, m_out, pe_ref, stats_ref, idx_ref, c_sc, n_sc, m_sc, kbuf, sem = refs
    else:
        mem_ref, c_out, n_out, m_out, c_sc, n_sc, m_sc = refs
    c = pl.program_id(1)
    nbat, L = q_ref.shape[0], q_ref.shape[1]

    @pl.when(c == 0)
    def _reset_state():
        c_sc[...] = jnp.zeros_like(c_sc)
        n_sc[...] = jnp.zeros_like(n_sc)
        m_sc[...] = jnp.zeros_like(m_sc)

    row = lax.broadcasted_iota(jnp.int32, (L, L), 0)
    col = lax.broadcasted_iota(jnp.int32, (L, L), 1)
    causal = col <= row
    lower = jnp.where(causal, 1.0, 0.0)
    upper = jnp.where(row <= col, 1.0, 0.0)
    ones = jnp.ones((L, DV_MLSTM), BF16)

    def batch_row(bi):
        g = g_ref[bi]
        g_t = g.T
        b_col_all = jnp.dot(lower, g, precision=HIGHEST, preferred_element_type=F32)
        b_row_all = jnp.dot(g_t[0:SUBLANE, :], upper, precision=HIGHEST, preferred_element_type=F32)
        for h in range(H_MLSTM):
            lanes = slice(h * DK_MLSTM, (h + 1) * DK_MLSTM)
            ig_row = g_t[h:h + 1, :]
            b_row = b_row_all[H_MLSTM + h:H_MLSTM + h + 1, :]
            ig = jnp.broadcast_to(g[:, h:h + 1], (L, LANE))
            b = jnp.broadcast_to(b_col_all[:, H_MLSTM + h:H_MLSTM + h + 1], (L, LANE))
            m_prev = m_sc[bi, h:h + 1, :]
            dmat = jnp.where(causal, b - b_row + ig_row, NEG)
            m_inter = b + m_prev
            m_t = jnp.maximum(m_inter, jnp.broadcast_to(jnp.max(dmat, axis=1, keepdims=True), (L, LANE)))
            w_inter = jnp.exp(m_inter - m_t)
            qh, kh, vh = q_ref[bi, :, lanes], k_ref[bi, :, lanes], v_ref[bi, :, lanes]
            a = jnp.exp(dmat - m_t) * lax.dot_general(qh, kh, _NT, preferred_element_type=F32)
            c_prev = c_sc[bi, h]
            n_prev = n_sc[bi, h:h + 1, :]
            state = jnp.concatenate([c_prev, jnp.broadcast_to(n_prev, (DV_MLSTM, DK_MLSTM))], axis=0).astype(BF16)
            num_den = (jnp.concatenate([w_inter, w_inter], axis=1)
                       * lax.dot_general(qh, state, _NT, preferred_element_type=F32)
                       + jnp.dot(a.astype(BF16), jnp.concatenate([vh, ones], axis=1), preferred_element_type=F32))
            hh = num_den[:, :DV_MLSTM] / jnp.maximum(jnp.abs(num_den[:, DV_MLSTM:]), jnp.exp(-m_t))
            mem_ref[bi, :, lanes] = _mlstm_head_out(
                hh, gain_ref[:, lanes], o_ref[bi, :, lanes].astype(F32)).astype(mem_ref.dtype)

            m_new = m_t[L - 1:L, :]
            b_last = b[L - 1:L, :]
            g_inter = jnp.exp(b_last + m_prev - m_new)
            g_in = jnp.exp(b_last - b + ig - m_new)
            v_scaled = (vh.astype(F32) * g_in).astype(BF16)
            c_sc[bi, h] = g_inter * c_prev + lax.dot_general(v_scaled, kh, _TN, preferred_element_type=F32)
            n_sc[bi, h:h + 1, :] = g_inter * n_prev + jnp.sum(kh.astype(F32) * g_in, axis=0, keepdims=True)
            m_sc[bi, h:h + 1, :] = m_new

    if not guest:
        for bi in range(nbat):
            batch_row(bi)
    else:
        def work(r, row, slot):
            _moba_sample_scores(row, qt_ref, knt_ref, [kbuf.at[slot, p] for p in range(kbuf.shape[1])],
                                pe_ref.at[r], stats_ref.at[r], idx_ref.at[r])
            batch_row(r)

        _guest_rows_ring(pl.program_id(0) * pl.num_programs(1) + c, pl.num_programs(0) * pl.num_programs(1),
                         nbat, *_key_page_ring(pt_ref, ck_ref, kbuf, sem), work)

    @pl.when(c == pl.num_programs(1) - 1)
    def _emit_state():
        c_out[...] = c_sc[...]
        n_out[...] = n_sc[...]
        m_out[...] = m_sc[...]


def _mlstm_prompt(mq, mk, mv, mo, gates, gain, guest=None):
    b, s, w = mq.shape
    L = MLSTM_CHUNK
    nbat = 2 if b % 2 == 0 else 1
    nc = s // L
    assert s % L == 0 and w == MLSTM_WIDTH and L == LANE == DK_MLSTM == DV_MLSTM
    tok = lambda width: pl.BlockSpec((nbat, L, width), lambda bi, c, *_: (bi, c, 0))
    state = lambda *dims: pl.BlockSpec((nbat,) + dims, lambda bi, c, *_: (bi,) + (0,) * len(dims))
    in_specs = [tok(w), tok(w), tok(w), tok(w), tok(LANE), pl.BlockSpec((1, w), lambda *_: (0, 0))]
    out_specs = [tok(w), state(H_MLSTM, DV_MLSTM, DK_MLSTM), state(SUBLANE, LANE), state(SUBLANE, LANE)]
    out_shape = [jax.ShapeDtypeStruct((b, s, w), BF16),
                 jax.ShapeDtypeStruct((b, H_MLSTM, DV_MLSTM, DK_MLSTM), F32),
                 jax.ShapeDtypeStruct((b, SUBLANE, LANE), F32),
                 jax.ShapeDtypeStruct((b, SUBLANE, LANE), F32)]
    scratch = [pltpu.VMEM((nbat, H_MLSTM, DV_MLSTM, DK_MLSTM), F32),
               pltpu.VMEM((nbat, SUBLANE, LANE), F32),
               pltpu.VMEM((nbat, SUBLANE, LANE), F32)]
    args = (mq, mk, mv, mo, gates, gain)
    assert len(in_specs) == N_MLSTM_INPUTS
    grid = (b // nbat, nc)
    if guest is None:
        return pl.pallas_call(
            functools.partial(_mlstm_prompt_kernel, guest=False), grid=grid, in_specs=in_specs,
            out_specs=out_specs, out_shape=out_shape, scratch_shapes=scratch,
            compiler_params=_params("arbitrary", "arbitrary"))(*args)

    qt, knt, cache_k, page_table = guest
    wa, bs = qt.shape
    n_pages = page_table.shape[1]
    assert bs == grid[0] * nc * nbat == LANE and nbat % 2 == 0 and n_pages % PAGES_PER_BLOCK == 0
    assert MOBA_TOPK <= n_pages // PAGES_PER_BLOCK <= LANE
    ck = _cache_pages(cache_k).reshape(cache_k.shape[0], wa, PAGE_SIZE)
    rows = lambda *dims: pl.BlockSpec((nbat,) + dims, lambda bi, c, *_: (bi * nc + c,) + (0,) * len(dims))
    whole = pl.BlockSpec((wa, bs), lambda *_: (0, 0))
    return pl.pallas_call(
        functools.partial(_mlstm_prompt_kernel, guest=True),
        grid_spec=pltpu.PrefetchScalarGridSpec(
            num_scalar_prefetch=1,
            grid=grid,
            in_specs=in_specs + [whole, whole, pl.BlockSpec(memory_space=pl.ANY)],
            out_specs=out_specs + [rows(n_pages, H_ATT, LANE), rows(H_ATT, LANE), rows(H_ATT, LANE)],
            scratch_shapes=scratch + [pltpu.VMEM((2, n_pages, wa, PAGE_SIZE), F32), pltpu.SemaphoreType.DMA((2,))],
        ),
        out_shape=out_shape + [jax.ShapeDtypeStruct((bs, n_pages, H_ATT, LANE), F32),
                               jax.ShapeDtypeStruct((bs, H_ATT, LANE), F32),
                               jax.ShapeDtypeStruct((bs, H_ATT, LANE), jnp.int32)],
        compiler_params=_params("arbitrary", "arbitrary"),
    )(page_table, *args, qt, knt, ck)


def _head_sublane(h):
    return (H_ATT // 2 - 1 - h) if h < H_ATT // 2 else (H_ATT + H_ATT // 2 - 1 - h)


def _head_rows(x):
    parts = []
    for h in range(H_ATT):
        tiles = [x[h * DH_ATT + SUBLANE * t:h * DH_ATT + SUBLANE * (t + 1), :] for t in range(DH_ATT // SUBLANE)]
        parts.append(sum(tiles[1:], tiles[0]))
    sub = lax.broadcasted_iota(jnp.int32, parts[0].shape, 0)
    folded = [p + pltpu.roll(p, 4, 0) for p in parts]
    quads = [jnp.where(sub < 4, folded[i], folded[i + 4]) for i in range(4)]
    take_up = (sub & 2) != 0
    pairs = [jnp.where(take_up, quads[i] + pltpu.roll(quads[i], 2, 0),
                       quads[i + 2] + pltpu.roll(quads[i + 2], 6, 0)) for i in range(2)]
    return jnp.where((sub & 1) != 0, pairs[0] + pltpu.roll(pairs[0], 1, 0), pairs[1] + pltpu.roll(pairs[1], 7, 0))


def _moba_sample_scores(b, qt_ref, knt_ref, kp_refs, pe_ref, stats_ref, idx_ref):
    n_pages = len(kp_refs)
    n_blocks = n_pages // PAGES_PER_BLOCK
    w = qt_ref.shape[0]
    on_b = lax.broadcasted_iota(jnp.int32, (w, LANE), 1) == b

    def column(ref):
        return jnp.sum(jnp.where(on_b, ref[...], 0.0), axis=1, keepdims=True)

    q_col = column(qt_ref) * (DH_ATT ** -0.5)
    q_wide = jnp.broadcast_to(q_col, (w, LANE))
    s_own = _head_rows(jnp.broadcast_to(q_col * column(knt_ref), (w, LANE)))[:, 0:1]
    s_pages = [_head_rows(kp_refs[p][...] * q_wide) for p in range(n_pages)]

    blk = [jnp.sum(sum(s_pages[n * PAGES_PER_BLOCK + 1:(n + 1) * PAGES_PER_BLOCK], s_pages[n * PAGES_PER_BLOCK]),
                   axis=1, keepdims=True) for n in range(n_blocks)]
    lane = lax.broadcasted_iota(jnp.int32, (H_ATT, LANE), 1)
    sel, ranked = [], jnp.zeros((H_ATT, LANE), jnp.int32)
    for n in range(n_blocks):
        rank = jnp.zeros((H_ATT, 1), jnp.int32)
        for o in range(n_blocks):
            if o != n:
                beats = (blk[o] >= blk[n]) if o < n else (blk[o] > blk[n])
                rank = rank + jnp.where(beats, 1, 0)
        sel.append(rank < MOBA_TOPK)
        ranked = jnp.where(rank == lane, n, ranked)
    m = s_own
    for p in range(n_pages):
        page_max = jnp.max(s_pages[p], axis=1, keepdims=True)
        m = jnp.maximum(m, jnp.where(sel[p // PAGES_PER_BLOCK], page_max, NEG))
    p_own = jnp.exp(s_own - m)
    total = jnp.zeros((H_ATT, LANE), F32)
    for p in range(n_pages):
        pe = jnp.where(sel[p // PAGES_PER_BLOCK], jnp.exp(s_pages[p] - m), 0.0)
        pe_ref[p] = pe
        total = total + pe
    row_sum = p_own + jnp.sum(total, axis=1, keepdims=True)
    stats_ref[...] = jnp.where(lane == 0, p_own, row_sum)
    idx_ref[...] = ranked


N_VALUE_CHUNKS = MOBA_TOPK * PAGES_PER_BLOCK


def _value_chunk_page(sel_ref, row, h, c):
    return sel_ref[row, h * MOBA_TOPK + c // PAGES_PER_BLOCK] * PAGES_PER_BLOCK + c % PAGES_PER_BLOCK


def _moba_sample_mix_row(row, sel_ref, pe_ref, stats_ref, vnt_ref, chunks_ref, o_ref):
    w = vnt_ref.shape[0]
    on_row = lax.broadcasted_iota(jnp.int32, (w, LANE), 1) == row
    vn_col = jnp.sum(jnp.where(on_row, vnt_ref[...], 0.0), axis=1, keepdims=True)
    stats = stats_ref[...]
    out_cols = []
    for h in range(H_ATT):
        r = _head_sublane(h)
        acc = jnp.zeros((DH_ATT, LANE), F32)
        for c in range(N_VALUE_CHUNKS):
            acc = acc + (pe_ref[_value_chunk_page(sel_ref, row, h, c), r:r + 1, :]
                         * chunks_ref[h * N_VALUE_CHUNKS + c])
        p_own, row_sum = stats[r:r + 1, 0:1], stats[r:r + 1, 1:2]
        rows = slice(h * DH_ATT, (h + 1) * DH_ATT)
        out_cols.append((jnp.sum(acc, axis=1, keepdims=True) + p_own * vn_col[rows, :]) / row_sum)
    o_ref[...] = jnp.where(on_row, jnp.concatenate(out_cols, axis=0), o_ref[...])


def _cache_pages(cache):
    return jnp.transpose(cache, (0, 2, 3, 1))


def _selected_blocks(ranked):
    b = ranked.shape[0]
    return jnp.stack([ranked[:, _head_sublane(h), :MOBA_TOPK] for h in range(H_ATT)], axis=1).reshape(b, -1)


def _mlstm_sample_kernel(q_ref, k_ref, v_ref, o_ref, g_ref, gain_ref, c0_ref, n0_ref, m0_ref,
                         mem_ref, c_ref, n_ref, m_ref):
    tb = q_ref.shape[0]
    g = g_ref[...]
    sub = lax.broadcasted_iota(jnp.int32, (2 * tb, LANE), 0)
    zrows = jnp.zeros((tb, LANE), F32)
    for h in range(H_MLSTM):
        lanes = slice(h * DK_MLSTM, (h + 1) * DK_MLSTM)
        ig, lf, m0 = g[:, h:h + 1], g[:, H_MLSTM + h:H_MLSTM + h + 1], m0_ref[:, h:h + 1]
        q, k, v = q_ref[:, lanes], k_ref[:, lanes], v_ref[:, lanes]
        n0 = n0_ref[:, lanes]
        m_t = jnp.maximum(lf + m0, ig)
        w_inter = jnp.exp(lf + m0 - m_t)
        g_in = jnp.exp(ig - m_t)
        a = g_in * jnp.sum(q * k, axis=1, keepdims=True)
        den = w_inter * jnp.sum(n0 * q, axis=1, keepdims=True) + a
        q_b = q.astype(BF16)
        gv = jnp.concatenate([g_in * v, zrows], axis=0)
        k_b = jnp.concatenate([k, zrows], axis=0).astype(BF16)
        cq_rows = []
        for r in range(tb):
            c_prev = c0_ref[r, h]
            cq_rows.append(lax.dot_general(q_b, c_prev.astype(BF16), _NT, preferred_element_type=F32)[r:r + 1, :])
            outer = lax.dot_general(jnp.where(sub == r, gv, 0.0).astype(BF16), k_b, _TN,
                                    preferred_element_type=F32)
            c_ref[r, h] = w_inter[r:r + 1, :] * c_prev + outer
        cq = jnp.concatenate(cq_rows, axis=0)
        hh = (w_inter * cq + a * v) / jnp.maximum(jnp.abs(den), jnp.exp(-m_t))
        mem_ref[:, lanes] = _mlstm_head_out(hh, gain_ref[:, lanes], o_ref[:, lanes]).astype(mem_ref.dtype)
        n_ref[:, lanes] = w_inter * n0 + g_in * k
        m_ref[:, h:h + 1] = m_t


def _mlstm_sample(mq, mk, mv, mo, gates, gain, c0, n0, m0):
    b, w = mq.shape
    tb = SUBLANE
    rows = lambda width: pl.BlockSpec((tb, width), lambda i: (i, 0))
    c_spec = pl.BlockSpec((tb, H_MLSTM, DV_MLSTM, DK_MLSTM), lambda i: (i, 0, 0, 0))
    return pl.pallas_call(
        _mlstm_sample_kernel,
        grid=(b // tb,),
        in_specs=[rows(w), rows(w), rows(w), rows(w), rows(LANE), pl.BlockSpec((1, w), lambda i: (0, 0)),
                  c_spec, rows(w), rows(H_MLSTM)],
        out_specs=[rows(w), c_spec, rows(w), rows(H_MLSTM)],
        out_shape=[jax.ShapeDtypeStruct((b, w), F32),
                   jax.ShapeDtypeStruct(c0.shape, F32),
                   jax.ShapeDtypeStruct((b, w), F32),
                   jax.ShapeDtypeStruct((b, H_MLSTM), F32)],
        compiler_params=_params("arbitrary"),
    )(mq, mk, mv, mo, gates, gain, c0, n0.reshape(b, w), m0)


N_FINISH_INPUTS = 15


def _finish_kernel(*refs, alpha, ff_chunk, att_transposed, guest_rows):
    if guest_rows:
        pt_ref, sel_ref, refs = refs[0], refs[1], refs[2:]
    (x_ref, att_ref, mem_ref, g1_ref, sh2_ref, sc2_ref, g2_ref, wo_ref, ln1g_ref, ln1b_ref,
     wg_ref, wu_ref, wd_ref, ln2g_ref, ln2b_ref) = refs[:N_FINISH_INPUTS]
    if guest_rows:
        pe_ref, stats_ref, vnt_ref, cv_ref, y_ref, o_ref, vbuf, sem = refs[N_FINISH_INPUTS:]
    else:
        (y_ref,) = refs[N_FINISH_INPUTS:]

    att = att_ref[...].T if att_transposed else att_ref[...]
    aw = att.shape[1]
    mix = (jnp.dot(att.astype(BF16), wo_ref[0:aw, :], preferred_element_type=F32)
           + jnp.dot(mem_ref[...].astype(BF16), wo_ref[aw:, :], preferred_element_type=F32))
    x1 = _layernorm(alpha * x_ref[...] + (1.0 + g1_ref[...]) * mix, ln1g_ref[...], ln1b_ref[...])
    h2 = (x1 * (1.0 + sc2_ref[...]) + sh2_ref[...]).astype(BF16)
    n_ff = wg_ref.shape[1] // ff_chunk

    def ffn_chunk(c):
        cols = slice(c * ff_chunk, (c + 1) * ff_chunk)
        gate = jnp.dot(h2, wg_ref[:, cols], preferred_element_type=F32)
        up = jnp.dot(h2, wu_ref[:, cols], preferred_element_type=F32)
        act = (gate * jax.nn.sigmoid(gate) * up).astype(BF16)
        return jnp.dot(act, wd_ref[cols, :], preferred_element_type=F32)

    ffn = [jnp.zeros(x1.shape, F32)]
    if not guest_rows:
        for c in range(n_ff):
            ffn[0] = ffn[0] + ffn_chunk(c)
    else:
        i = pl.program_id(0)
        per_row = -(-n_ff // guest_rows)

        @pl.when(i == 0)
        def _init_out():
            o_ref[...] = jnp.zeros_like(o_ref)

        def chunk_copy(row, slot, h, c):
            page = pt_ref[row, _value_chunk_page(sel_ref, row, h, c)]
            return pltpu.make_async_copy(cv_ref.at[page, h], vbuf.at[slot, h * N_VALUE_CHUNKS + c], sem.at[slot])

        def start_row(row, slot):
            for h in range(H_ATT):
                for c in range(N_VALUE_CHUNKS):
                    chunk_copy(row, slot, h, c).start()

        def wait_row(row, slot):
            for h in range(H_ATT):
                for c in range(N_VALUE_CHUNKS):
                    chunk_copy(row, slot, h, c).wait()

        def work(r, row, slot):
            _moba_sample_mix_row(row, sel_ref, pe_ref.at[r], stats_ref.at[r], vnt_ref, vbuf.at[slot], o_ref)
            for c in range(r * per_row, min((r + 1) * per_row, n_ff)):
                ffn[0] = ffn[0] + ffn_chunk(c)

        _guest_rows_ring(i, pl.num_programs(0), guest_rows, start_row, wait_row, work)
    y_ref[...] = _layernorm(alpha * x1 + (1.0 + g2_ref[...]) * ffn[0], ln2g_ref[...], ln2b_ref[...])


def _finish(x, att, mem, mods, weights, tm, rows_per_mod, alpha, att_transposed=False, guest=None):
    t, d = x.shape
    assert not att_transposed or tm == t
    steps = t // tm
    w_out, ln1_g, ln1_b, w_gate, w_up, w_down, ln2_g, ln2_b = weights
    if rows_per_mod is None:
        mod_spec = pl.BlockSpec((tm, d), lambda i, *_: (i, 0))
    else:
        per = rows_per_mod // tm
        mod_spec = pl.BlockSpec((None, 1, d), lambda i, *_: (i // per, 0, 0))
    tok = lambda width: pl.BlockSpec((tm, width), lambda i, *_: (i, 0))
    ff_chunk = 256
    assert w_gate.shape[1] % ff_chunk == 0
    in_specs = [tok(d), pl.BlockSpec(att.shape, lambda *_: (0, 0)) if att_transposed else tok(att.shape[1]),
                tok(mem.shape[1]), mod_spec, mod_spec, mod_spec, mod_spec,
                _const_spec(w_out.shape), _const_spec(ln1_g.shape), _const_spec(ln1_b.shape),
                _const_spec(w_gate.shape), _const_spec(w_up.shape), _const_spec(w_down.shape),
                _const_spec(ln2_g.shape), _const_spec(ln2_b.shape)]
    args = (x, att, mem, *mods, w_out, ln1_g, ln1_b, w_gate, w_up, w_down, ln2_g, ln2_b)
    assert len(in_specs) == N_FINISH_INPUTS
    body = functools.partial(_finish_kernel, alpha=alpha, ff_chunk=ff_chunk, att_transposed=att_transposed,
                             guest_rows=0)
    y_shape = jax.ShapeDtypeStruct((t, d), F32)
    if guest is None:
        return pl.pallas_call(body, grid=(steps,), in_specs=in_specs, out_specs=tok(d), out_shape=y_shape,
                              compiler_params=_params("arbitrary"))(*args)

    pe, stats, sel, vnt, cache_v, page_table = guest
    w, bs = vnt.shape
    n_pages = page_table.shape[1]
    guest_rows = bs // steps
    assert bs % steps == 0 and guest_rows % 2 == 0 and bs == LANE
    rows = lambda *dims: pl.BlockSpec((guest_rows,) + dims, lambda i, *_: (i,) + (0,) * len(dims))
    whole = pl.BlockSpec((w, bs), lambda *_: (0, 0))
    return pl.pallas_call(
        functools.partial(body, guest_rows=guest_rows),
        grid_spec=pltpu.PrefetchScalarGridSpec(
            num_scalar_prefetch=2,
            grid=(steps,),
            in_specs=in_specs + [rows(n_pages, H_ATT, LANE), rows(H_ATT, LANE), whole,
                                 pl.BlockSpec(memory_space=pl.ANY)],
            out_specs=[tok(d), whole],
            scratch_shapes=[pltpu.VMEM((2, H_ATT * N_VALUE_CHUNKS, DH_ATT, PAGE_SIZE), F32),
                            pltpu.SemaphoreType.DMA((2,))],
        ),
        out_shape=[y_shape, jax.ShapeDtypeStruct((w, bs), F32)],
        compiler_params=_params("arbitrary"),
    )(page_table, sel, *args, pe, stats, vnt, _cache_pages(cache_v))


def kernel(x_prompt, x_sample, cache_k, cache_v, state_C, state_n, state_m, page_table, c_prompt, c_sample,
           w_ada, b_ada, w_in, b_if, mlstm_norm_g, w_out, ln1_g, ln1_b, w_gate, w_up, w_down, ln2_g, ln2_b):
    depth = w_in.shape[0]
    assert depth == 1, "single-layer step"
    alpha = (2.0 * depth) ** 0.25
    bp, s, d = x_prompt.shape
    bs = x_sample.shape[0]
    assert x_sample.shape[1] == 1, "single-token decode step"
    n_main = N_PROJ_GROUPS * PROJ_GROUP

    w_main = w_in[0, :, :n_main].astype(BF16)
    w_att_t = w_in[0, :, :3 * PROJ_GROUP].T.astype(BF16)
    w_gates = jnp.pad(w_in[0, :, n_main:], ((0, 0), (0, LANE - 2 * H_MLSTM))).astype(BF16)
    b_gates = jnp.pad(b_if[0], (0, LANE - 2 * H_MLSTM)).reshape(1, LANE)
    gain = mlstm_norm_g[0].reshape(1, MLSTM_WIDTH)
    row = lambda a: a[0].reshape(1, -1)
    fin_w = (w_out[0].astype(BF16), row(ln1_g), row(ln1_b), w_gate[0].astype(BF16), w_up[0].astype(BF16),
             w_down[0].astype(BF16), row(ln2_g), row(ln2_b))

    c_all = jnp.concatenate([c_prompt, c_sample], axis=0)
    mod = _adaln(c_all, w_ada[0], b_ada[0])
    sh1, sc1, g1, sh2, sc2, g2 = (mod[:, i * d:(i + 1) * d] for i in range(6))
    pm = lambda a: a[:bp].reshape(bp, 1, d)
    sm = lambda a: a[bp:]

    xs = x_sample.reshape(bs, d)
    aq_s, ak_s, av_s, mq_s, mk_s, mv_s, mo_s, gates_s = _in_proj(
        xs, sm(sc1), sm(sh1), w_main, w_att_t, w_gates, b_gates, bs, None, (F32,) * N_PROJ_GROUPS,
        transposed=(0, 1, 2))

    xp = x_prompt.reshape(bp * s, d)
    tm = 512
    aq, ak_t, av_t, mq, mk, mv, mo, gates = _in_proj(
        xp, pm(sc1), pm(sh1), w_main, w_att_t, w_gates, b_gates, tm, s,
        (F32, F32, F32, BF16, BF16, BF16, F32), transposed=(1, 2))
    seq = lambda a: a.reshape(bp, s, a.shape[-1])
    att = _moba_prompt(seq(aq), ak_t, av_t)
    mem, c_p, n_p, m_p, pe, stats, ranked = _mlstm_prompt(
        seq(mq), seq(mk), seq(mv), seq(mo), seq(gates), gain, guest=(aq_s, ak_s, cache_k[0], page_table))
    y_p, att_s = _finish(
        xp, att.reshape(bp * s, -1), mem.reshape(bp * s, -1), (pm(g1), pm(sh2), pm(sc2), pm(g2)),
        fin_w, tm, s, alpha, guest=(pe, stats, _selected_blocks(ranked), av_s, cache_v[0], page_table))

    mem_s, c_s, n_s, m_s = _mlstm_sample(mq_s, mk_s, mv_s, mo_s, gates_s, gain,
                                         state_C[0], state_n[0], state_m[0])
    y_s = _finish(xs, att_s, mem_s, (sm(g1), sm(sh2), sm(sc2), sm(g2)), fin_w, bs, None, alpha,
                  att_transposed=True)

    rows_p = lambda a: jnp.transpose(a.reshape(bp, H_ATT, DH_ATT, s), (0, 3, 1, 2))[None]
    rows_s = lambda a: jnp.transpose(a.reshape(H_ATT, DH_ATT, bs), (2, 0, 1)).reshape(1, bs, 1, H_ATT, DH_ATT)
    return (y_p.reshape(bp, s, d), y_s.reshape(bs, 1, d),
            rows_p(ak_t), rows_p(av_t),
            c_p[None], n_p[None, :, :H_MLSTM, :], m_p[None, :, :H_MLSTM, 0],
            rows_s(ak_s), rows_s(av_s),
            c_s[None], n_s.reshape(1, bs, H_MLSTM, DK_MLSTM), m_s[None])
```

```python
import functools
import math

import jax
import jax.numpy as jnp
from jax import lax
from jax.experimental import pallas as pl
from jax.experimental.pallas import tpu as pltpu

F32 = jnp.float32
BF16 = jnp.bfloat16
HIGHEST = lax.Precision.HIGHEST

LANE = 128
SUBLANE = 8
VMEM_LIMIT_BYTES = 56 * 1024 * 1024

H_ATT = 8
DH_ATT = 64
ATT_WIDTH = H_ATT * DH_ATT
MOBA_BLOCK = 256
MOBA_TOPK = 3
H_MLSTM = 4
DK_MLSTM = 128
DV_MLSTM = 128
MLSTM_WIDTH = H_MLSTM * DV_MLSTM
MLSTM_CHUNK = LANE
PAGE_SIZE = 128
PAGES_PER_BLOCK = MOBA_BLOCK // PAGE_SIZE
LN_EPS = 1e-5
NEG = -1e30
LOG2E = math.log2(math.e)
N_PROJ_GROUPS = 7
PROJ_GROUP = 512
MK_GROUP = 4

_NT = (((1,), (1,)), ((), ()))
_TN = (((0,), (0,)), ((), ()))


def _params(*sem):
    return pltpu.CompilerParams(dimension_semantics=sem, vmem_limit_bytes=VMEM_LIMIT_BYTES)


def _const_spec(shape):
    return pl.BlockSpec(shape, lambda *_: (0,) * len(shape), pipeline_mode=pl.Buffered(1))


def _layernorm(x, g, b):
    mu = jnp.mean(x, axis=-1, keepdims=True)
    d = x - mu
    var = jnp.mean(d * d, axis=-1, keepdims=True)
    return d * lax.rsqrt(var + LN_EPS) * g + b


def _top_blocks(val, nidx):
    cnt = jnp.zeros(val.shape, jnp.int32)
    for r in range(1, SUBLANE):
        other = pltpu.roll(val, r, 0)
        oidx = pltpu.roll(nidx, r, 0)
        beats = (other > val) | ((other == val) & (oidx < nidx))
        cnt = cnt + jnp.where(beats, 1, 0)
    return cnt < MOBA_TOPK


def _adaln_kernel(c_ref, w_ref, b_ref, o_ref):
    c = c_ref[...]
    s = c * jax.nn.sigmoid(c)
    o_ref[...] = jnp.dot(s, w_ref[...], preferred_element_type=F32) + b_ref[...]


def _adaln(c, w_ada, b_ada):
    rows, d = c.shape
    n = w_ada.shape[1]
    tn = d
    return pl.pallas_call(
        _adaln_kernel,
        grid=(n // tn,),
        in_specs=[pl.BlockSpec((rows, d), lambda j: (0, 0)),
                  pl.BlockSpec((d, tn), lambda j: (0, j)),
                  pl.BlockSpec((1, tn), lambda j: (0, j))],
        out_specs=pl.BlockSpec((rows, tn), lambda j: (0, j)),
        out_shape=jax.ShapeDtypeStruct((rows, n), F32),
        compiler_params=_params("arbitrary"),
    )(c, w_ada, b_ada.reshape(1, n))


def _guest_rows_ring(i, n_steps, guest_rows, start_row, wait_row, work):
    half = (i % 2) * guest_rows

    @pl.when(i == 0)
    def _first_step_rows():
        for r in range(guest_rows):
            start_row(r, r)

    @pl.when(i + 1 < n_steps)
    def _next_step_rows():
        for r in range(guest_rows):
            start_row((i + 1) * guest_rows + r, guest_rows - half + r)

    for r in range(guest_rows):
        row, slot = i * guest_rows + r, half + r
        wait_row(row, slot)
        work(r, row, slot)


def _key_page_ring(pt_ref, ck_ref, kbuf, sem):
    n_pages = kbuf.shape[1]

    def page_copy(row, slot, p):
        return pltpu.make_async_copy(ck_ref.at[pt_ref[row, p]], kbuf.at[slot, p], sem.at[slot])

    def start_row(row, slot):
        for p in range(n_pages):
            page_copy(row, slot, p).start()

    def wait_row(row, slot):
        for p in range(n_pages):
            page_copy(row, slot, p).wait()

    return start_row, wait_row


def _in_proj_kernel(x_ref, sc_ref, sh_ref, w_ref, wt_ref, wg_ref, bg_ref, *out_refs, transposed):
    proj_refs, g_ref = out_refs[:N_PROJ_GROUPS], out_refs[N_PROJ_GROUPS]
    h = (x_ref[...] * (1.0 + sc_ref[...]) + sh_ref[...]).astype(BF16)
    for gi, o_ref in enumerate(proj_refs):
        cols = slice(gi * PROJ_GROUP, (gi + 1) * PROJ_GROUP)
        if gi in transposed:
            y = lax.dot_general(wt_ref[cols, :], h, _NT, preferred_element_type=F32)
        else:
            y = jnp.dot(h, w_ref[:, cols], preferred_element_type=F32)
        if gi == MK_GROUP:
            y = y * (DK_MLSTM ** -0.5)
        o_ref[...] = y.astype(o_ref.dtype)
    g = jnp.dot(h, wg_ref[...], preferred_element_type=F32) + bg_ref[...]
    lane = lax.broadcasted_iota(jnp.int32, g.shape, 1)
    logsig = jnp.minimum(g, 0.0) - jnp.log1p(jnp.exp(-jnp.abs(g)))
    g_ref[...] = jnp.where(lane >= H_MLSTM, logsig, g)


def _in_proj(x, sc, sh, w_main, w_att_t, w_gate, b_gate, tm, rows_per_mod, out_dtypes, transposed=()):
    t, d = x.shape
    steps = t // tm
    assert MK_GROUP not in transposed
    if rows_per_mod is None:
        mod_spec = pl.BlockSpec((tm, d), lambda i, *_: (i, 0))
        t_shape, t_spec = (PROJ_GROUP, t), pl.BlockSpec((PROJ_GROUP, tm), lambda i, *_: (0, i))
    else:
        per = rows_per_mod // tm
        mod_spec = pl.BlockSpec((None, 1, d), lambda i, *_: (i // per, 0, 0))
        t_shape = (t // rows_per_mod, PROJ_GROUP, rows_per_mod)
        t_spec = pl.BlockSpec((None, PROJ_GROUP, tm), lambda i, *_: (i // per, 0, i % per))
    out_shape, out_specs = [], []
    for gi, dt in enumerate(out_dtypes):
        if gi in transposed:
            out_shape.append(jax.ShapeDtypeStruct(t_shape, dt))
            out_specs.append(t_spec)
        else:
            out_shape.append(jax.ShapeDtypeStruct((t, PROJ_GROUP), dt))
            out_specs.append(pl.BlockSpec((tm, PROJ_GROUP), lambda i, *_: (i, 0)))
    out_shape.append(jax.ShapeDtypeStruct((t, LANE), F32))
    out_specs.append(pl.BlockSpec((tm, LANE), lambda i, *_: (i, 0)))
    in_specs = [pl.BlockSpec((tm, d), lambda i, *_: (i, 0)), mod_spec, mod_spec,
                _const_spec(w_main.shape), _const_spec(w_att_t.shape),
                _const_spec(w_gate.shape), _const_spec(b_gate.shape)]
    return pl.pallas_call(
        functools.partial(_in_proj_kernel, transposed=tuple(transposed)),
        grid=(steps,), in_specs=in_specs, out_specs=out_specs, out_shape=out_shape,
        compiler_params=_params("arbitrary"),
    )(x, sc, sh, w_main, w_att_t, w_gate, b_gate)


def _moba_prompt_kernel(q_ref, kt_ref, vt_ref, o_ref, kaug_ref, vaug_ref, kmt_ref, lhs_sc, m_sc, acc_sc, *, nb):
    i = pl.program_id(1)
    blk = MOBA_BLOCK
    half = LANE // 2
    w = q_ref.shape[1]

    @pl.when(i == 0)
    def _prepare_batch():
        srow = lax.broadcasted_iota(jnp.int32, (LANE, blk), 0)
        in_lo = srow < half
        head_of_row = lax.broadcasted_iota(jnp.int32, (w, LANE), 0) // DH_ATT
        lane_w = lax.broadcasted_iota(jnp.int32, (w, LANE), 1)
        kmt = jnp.zeros((w, LANE), F32)
        for j in range(nb):
            ktj = kt_ref[:, j * blk:(j + 1) * blk]
            vtj = vt_ref[:, j * blk:(j + 1) * blk]
            col = jnp.mean(ktj, axis=1, keepdims=True)
            kmt = jnp.where((lane_w % SUBLANE == j) & (lane_w // SUBLANE == head_of_row), col, kmt)
            for p in range(H_ATT // 2):
                kp, vp = ktj[p * LANE:(p + 1) * LANE, :], vtj[p * LANE:(p + 1) * LANE, :]
                kaug_ref[2 * p, j] = jnp.where(in_lo, kp, jnp.where(srow == half + j, 1.0, 0.0)).astype(BF16)
                kaug_ref[2 * p + 1, j] = jnp.where(in_lo, jnp.where(srow == j, 1.0, 0.0), kp).astype(BF16)
                vaug_ref[2 * p, j] = jnp.where(in_lo, vp, 1.0).astype(BF16)
                vaug_ref[2 * p + 1, j] = jnp.where(in_lo, 1.0, vp).astype(BF16)
        km_hi = kmt.astype(BF16)
        kmt_ref[0] = km_hi
        kmt_ref[1] = (kmt - km_hi.astype(F32)).astype(BF16)

    q32 = q_ref[...]
    q_hi = q32.astype(BF16)
    q_lo = (q32 - q_hi.astype(F32)).astype(BF16)
    sc = (jnp.dot(q_hi, kmt_ref[0], preferred_element_type=F32)
          + (jnp.dot(q_hi, kmt_ref[1], preferred_element_type=F32)
             + jnp.dot(q_lo, kmt_ref[0], preferred_element_type=F32)))
    sc_t = sc.T
    nidx = lax.broadcasted_iota(jnp.int32, (SUBLANE, blk), 0)
    past = nidx < i
    biases = []
    for h in range(H_ATT):
        val = jnp.where(past, sc_t[h * SUBLANE:(h + 1) * SUBLANE, :], NEG)
        keep = (_top_blocks(val, nidx) & past) | (nidx == i)
        biases.append(jnp.where(keep, 0.0, NEG))

    lane = lax.broadcasted_iota(jnp.int32, (blk, LANE), 1)
    lo_lanes = lane < half
    zpad = jnp.zeros((half - SUBLANE, blk), F32)
    for p in range(H_ATT // 2):
        bias_p = jnp.concatenate([biases[2 * p + 1], zpad, biases[2 * p], zpad], axis=0).T
        qp = q32[:, p * LANE:(p + 1) * LANE] * (DH_ATT ** -0.5 * LOG2E)
        lhs_sc[2 * p] = jnp.where(lo_lanes, qp, bias_p).astype(BF16)
        lhs_sc[2 * p + 1] = jnp.where(lo_lanes, bias_p, qp).astype(BF16)

    def scores(h, j):
        return jnp.dot(lhs_sc[h], kaug_ref[h, j], preferred_element_type=F32)

    def row_max(s):
        return jnp.broadcast_to(jnp.max(s, axis=1, keepdims=True), (blk, LANE))

    def weights(s, m):
        return jnp.exp2(s - jnp.concatenate([m, m], axis=1)).astype(BF16)

    row = lax.broadcasted_iota(jnp.int32, (blk, blk), 0)
    col = lax.broadcasted_iota(jnp.int32, (blk, blk), 1)
    causal = col <= row
    for h in range(H_ATT):
        s = jnp.where(causal, scores(h, i), NEG)
        m = row_max(s)
        acc_sc[h] = lax.dot_general(weights(s, m), vaug_ref[h, i], _NT, preferred_element_type=F32)
        m_sc[h] = m

    def past_block(j, carry):
        for h in range(H_ATT):
            s = scores(h, j)
            m_old = m_sc[h]
            m_new = jnp.maximum(m_old, row_max(s))
            acc_sc[h] = (jnp.exp2(m_old - m_new) * acc_sc[h]
                         + lax.dot_general(weights(s, m_new), vaug_ref[h, j], _NT, preferred_element_type=F32))
            m_sc[h] = m_new
        return carry

    lax.fori_loop(0, i, past_block, 0)

    for p in range(H_ATT // 2):
        acc_e, acc_o = acc_sc[2 * p], acc_sc[2 * p + 1]
        num = jnp.where(lo_lanes, acc_e, acc_o)
        den = pltpu.roll(jnp.where(lo_lanes, acc_o, acc_e), half, 1)
        o_ref[:, p * LANE:(p + 1) * LANE] = (num / den).astype(o_ref.dtype)


def _moba_prompt(q, kt, vt):
    b, s, w = q.shape
    nb = s // MOBA_BLOCK
    assert s % MOBA_BLOCK == 0 and nb <= SUBLANE and w == ATT_WIDTH
    blk_state = lambda dt: pltpu.VMEM((H_ATT, MOBA_BLOCK, LANE), dt)
    return pl.pallas_call(
        functools.partial(_moba_prompt_kernel, nb=nb),
        grid=(b, nb),
        in_specs=[pl.BlockSpec((None, MOBA_BLOCK, w), lambda bi, i: (bi, i, 0)),
                  pl.BlockSpec((None, w, s), lambda bi, i: (bi, 0, 0)),
                  pl.BlockSpec((None, w, s), lambda bi, i: (bi, 0, 0))],
        out_specs=pl.BlockSpec((None, MOBA_BLOCK, w), lambda bi, i: (bi, i, 0)),
        out_shape=jax.ShapeDtypeStruct((b, s, w), BF16),
        scratch_shapes=[pltpu.VMEM((H_ATT, nb, LANE, MOBA_BLOCK), BF16),
                        pltpu.VMEM((H_ATT, nb, LANE, MOBA_BLOCK), BF16),
                        pltpu.VMEM((2, w, LANE), BF16),
                        blk_state(BF16), blk_state(F32), blk_state(F32)],
        compiler_params=_params("arbitrary", "arbitrary"),
    )(q, kt, vt)


def _mlstm_head_out(hh, gain, ogate):
    mu = jnp.mean(hh, axis=-1, keepdims=True)
    d = hh - mu
    var = jnp.mean(d * d, axis=-1, keepdims=True)
    return d * lax.rsqrt(var + LN_EPS) * gain * jax.nn.sigmoid(ogate)


N_MLSTM_INPUTS = 6


def _mlstm_prompt_kernel(*refs, guest):
    if guest:
        pt_ref, refs = refs[0], refs[1:]
    q_ref, k_ref, v_ref, o_ref, g_ref, gain_ref = refs[:N_MLSTM_INPUTS]
    refs = refs[N_MLSTM_INPUTS:]
    if guest:
        (qt_ref, knt_ref, ck_ref), refs = refs[:3], refs[3:]
        mem_ref, c_out, n_out, m_out, pe_ref, stats_ref, idx_ref, c_sc, n_sc, m_sc, kbuf, sem = refs
    else:
        mem_ref, c_out, n_out, m_out, c_sc, n_sc, m_sc = refs
    c = pl.program_id(1)
    nbat, L = q_ref.shape[0], q_ref.shape[1]

    @pl.when(c == 0)
    def _reset_state():
        c_sc[...] = jnp.zeros_like(c_sc)
        n_sc[...] = jnp.zeros_like(n_sc)
        m_sc[...] = jnp.zeros_like(m_sc)

    row = lax.broadcasted_iota(jnp.int32, (L, L), 0)
    col = lax.broadcasted_iota(jnp.int32, (L, L), 1)
    causal = col <= row
    lower = jnp.where(causal, 1.0, 0.0)
    upper = jnp.where(row <= col, 1.0, 0.0)
    ones = jnp.ones((L, DV_MLSTM), BF16)

    def batch_row(bi):
        g = g_ref[bi]
        g_t = g.T
        b_col_all = jnp.dot(lower, g, precision=HIGHEST, preferred_element_type=F32)
        b_row_all = jnp.dot(g_t[0:SUBLANE, :], upper, precision=HIGHEST, preferred_element_type=F32)
        for h in range(H_MLSTM):
            lanes = slice(h * DK_MLSTM, (h + 1) * DK_MLSTM)
            ig_row = g_t[h:h + 1, :]
            b_row = b_row_all[H_MLSTM + h:H_MLSTM + h + 1, :]
            ig = jnp.broadcast_to(g[:, h:h + 1], (L, LANE))
            b = jnp.broadcast_to(b_col_all[:, H_MLSTM + h:H_MLSTM + h + 1], (L, LANE))
            m_prev = m_sc[bi, h:h + 1, :]
            dmat = jnp.where(causal, b - b_row + ig_row, NEG)
            m_inter = b + m_prev
            m_t = jnp.maximum(m_inter, jnp.broadcast_to(jnp.max(dmat, axis=1, keepdims=True), (L, LANE)))
            w_inter = jnp.exp(m_inter - m_t)
            qh, kh, vh = q_ref[bi, :, lanes], k_ref[bi, :, lanes], v_ref[bi, :, lanes]
            a = jnp.exp(dmat - m_t) * lax.dot_general(qh, kh, _NT, preferred_element_type=F32)
            c_prev = c_sc[bi, h]
            n_prev = n_sc[bi, h:h + 1, :]
            state = jnp.concatenate([c_prev, jnp.broadcast_to(n_prev, (DV_MLSTM, DK_MLSTM))], axis=0).astype(BF16)
            num_den = (jnp.concatenate([w_inter, w_inter], axis=1)
                       * lax.dot_general(qh, state, _NT, preferred_element_type=F32)
                       + jnp.dot(a.astype(BF16), jnp.concatenate([vh, ones], axis=1), preferred_element_type=F32))
            hh = num_den[:, :DV_MLSTM] / jnp.maximum(jnp.abs(num_den[:, DV_MLSTM:]), jnp.exp(-m_t))
            mem_ref[bi, :, lanes] = _mlstm_head_out(
                hh, gain_ref[:, lanes], o_ref[bi, :, lanes].astype(F32)).astype(mem_ref.dtype)

            m_new = m_t[L - 1:L, :]
            b_last = b[L - 1:L, :]
            g_inter = jnp.exp(b_last + m_prev - m_new)
            g_in = jnp.exp(b_last - b + ig - m_new)
            v_scaled = (vh.astype(F32) * g_in).astype(BF16)
            c_sc[bi, h] = g_inter * c_prev + lax.dot_general(v_scaled, kh, _TN, preferred_element_type=F32)
            n_sc[bi, h:h + 1, :] = g_inter * n_prev + jnp.sum(kh.astype(F32) * g_in, axis=0, keepdims=True)
            m_sc[bi, h:h + 1, :] = m_new

    if not guest:
        for bi in range(nbat):
            batch_row(bi)
    else:
        def work(r, row, slot):
            _moba_sample_scores(row, qt_ref, knt_ref, [kbuf.at[slot, p] for p in range(kbuf.shape[1])],
                                pe_ref.at[r], stats_ref.at[r], idx_ref.at[r])
            batch_row(r)

        _guest_rows_ring(pl.program_id(0) * pl.num_programs(1) + c, pl.num_programs(0) * pl.num_programs(1),
                         nbat, *_key_page_ring(pt_ref, ck_ref, kbuf, sem), work)

    @pl.when(c == pl.num_programs(1) - 1)
    def _emit_state():
        c_out[...] = c_sc[...]
        n_out[...] = n_sc[...]
        m_out[...] = m_sc[...]


def _mlstm_prompt(mq, mk, mv, mo, gates, gain, guest=None):
    b, s, w = mq.shape
    L = MLSTM_CHUNK
    nbat = 2 if b % 2 == 0 else 1
    nc = s // L
    assert s % L == 0 and w == MLSTM_WIDTH and L == LANE == DK_MLSTM == DV_MLSTM
    tok = lambda width: pl.BlockSpec((nbat, L, width), lambda bi, c, *_: (bi, c, 0))
    state = lambda *dims: pl.BlockSpec((nbat,) + dims, lambda bi, c, *_: (bi,) + (0,) * len(dims))
    in_specs = [tok(w), tok(w), tok(w), tok(w), tok(LANE), pl.BlockSpec((1, w), lambda *_: (0, 0))]
    out_specs = [tok(w), state(H_MLSTM, DV_MLSTM, DK_MLSTM), state(SUBLANE, LANE), state(SUBLANE, LANE)]
    out_shape = [jax.ShapeDtypeStruct((b, s, w), BF16),
                 jax.ShapeDtypeStruct((b, H_MLSTM, DV_MLSTM, DK_MLSTM), F32),
                 jax.ShapeDtypeStruct((b, SUBLANE, LANE), F32),
                 jax.ShapeDtypeStruct((b, SUBLANE, LANE), F32)]
    scratch = [pltpu.VMEM((nbat, H_MLSTM, DV_MLSTM, DK_MLSTM), F32),
               pltpu.VMEM((nbat, SUBLANE, LANE), F32),
               pltpu.VMEM((nbat, SUBLANE, LANE), F32)]
    args = (mq, mk, mv, mo, gates, gain)
    assert len(in_specs) == N_MLSTM_INPUTS
    grid = (b // nbat, nc)
    if guest is None:
        return pl.pallas_call(
            functools.partial(_mlstm_prompt_kernel, guest=False), grid=grid, in_specs=in_specs,
            out_specs=out_specs, out_shape=out_shape, scratch_shapes=scratch,
            compiler_params=_params("arbitrary", "arbitrary"))(*args)

    qt, knt, cache_k, page_table = guest
    wa, bs = qt.shape
    n_pages = page_table.shape[1]
    assert bs == grid[0] * nc * nbat == LANE and n_pages % PAGES_PER_BLOCK == 0
    assert MOBA_TOPK <= n_pages // PAGES_PER_BLOCK <= LANE
    ck = _cache_pages(cache_k).reshape(cache_k.shape[0], wa, PAGE_SIZE)
    rows = lambda *dims: pl.BlockSpec((nbat,) + dims, lambda bi, c, *_: (bi * nc + c,) + (0,) * len(dims))
    whole = pl.BlockSpec((wa, bs), lambda *_: (0, 0))
    return pl.pallas_call(
        functools.partial(_mlstm_prompt_kernel, guest=True),
        grid_spec=pltpu.PrefetchScalarGridSpec(
            num_scalar_prefetch=1,
            grid=grid,
            in_specs=in_specs + [whole, whole, pl.BlockSpec(memory_space=pl.ANY)],
            out_specs=out_specs + [rows(n_pages, H_ATT, LANE), rows(H_ATT, LANE), rows(H_ATT, LANE)],
            scratch_shapes=scratch + [pltpu.VMEM((2 * nbat, n_pages, wa, PAGE_SIZE), F32),
                                      pltpu.SemaphoreType.DMA((2 * nbat,))],
        ),
        out_shape=out_shape + [jax.ShapeDtypeStruct((bs, n_pages, H_ATT, LANE), F32),
                               jax.ShapeDtypeStruct((bs, H_ATT, LANE), F32),
                               jax.ShapeDtypeStruct((bs, H_ATT, LANE), jnp.int32)],
        compiler_params=_params("arbitrary", "arbitrary"),
    )(page_table, *args, qt, knt, ck)


def _head_sublane(h):
    return (H_ATT // 2 - 1 - h) if h < H_ATT // 2 else (H_ATT + H_ATT // 2 - 1 - h)


def _head_rows(x):
    parts = []
    for h in range(H_ATT):
        tiles = [x[h * DH_ATT + SUBLANE * t:h * DH_ATT + SUBLANE * (t + 1), :] for t in range(DH_ATT // SUBLANE)]
        parts.append(sum(tiles[1:], tiles[0]))
    sub = lax.broadcasted_iota(jnp.int32, parts[0].shape, 0)
    folded = [p + pltpu.roll(p, 4, 0) for p in parts]
    quads = [jnp.where(sub < 4, folded[i], folded[i + 4]) for i in range(4)]
    take_up = (sub & 2) != 0
    pairs = [jnp.where(take_up, quads[i] + pltpu.roll(quads[i], 2, 0),
                       quads[i + 2] + pltpu.roll(quads[i + 2], 6, 0)) for i in range(2)]
    return jnp.where((sub & 1) != 0, pairs[0] + pltpu.roll(pairs[0], 1, 0), pairs[1] + pltpu.roll(pairs[1], 7, 0))


def _moba_sample_scores(b, qt_ref, knt_ref, kp_refs, pe_ref, stats_ref, idx_ref):
    n_pages = len(kp_refs)
    n_blocks = n_pages // PAGES_PER_BLOCK
    w = qt_ref.shape[0]
    on_b = lax.broadcasted_iota(jnp.int32, (w, LANE), 1) == b

    def column(ref):
        return jnp.sum(jnp.where(on_b, ref[...], 0.0), axis=1, keepdims=True)

    q_col = column(qt_ref) * (DH_ATT ** -0.5)
    q_wide = jnp.broadcast_to(q_col, (w, LANE))
    s_own = _head_rows(jnp.broadcast_to(q_col * column(knt_ref), (w, LANE)))[:, 0:1]
    s_pages = [_head_rows(kp_refs[p][...] * q_wide) for p in range(n_pages)]

    blk = [jnp.sum(sum(s_pages[n * PAGES_PER_BLOCK + 1:(n + 1) * PAGES_PER_BLOCK], s_pages[n * PAGES_PER_BLOCK]),
                   axis=1, keepdims=True) for n in range(n_blocks)]
    lane = lax.broadcasted_iota(jnp.int32, (H_ATT, LANE), 1)
    sel, ranked = [], jnp.zeros((H_ATT, LANE), jnp.int32)
    for n in range(n_blocks):
        rank = jnp.zeros((H_ATT, 1), jnp.int32)
        for o in range(n_blocks):
            if o != n:
                beats = (blk[o] >= blk[n]) if o < n else (blk[o] > blk[n])
                rank = rank + jnp.where(beats, 1, 0)
        sel.append(rank < MOBA_TOPK)
        ranked = jnp.where(rank == lane, n, ranked)
    m = s_own
    for p in range(n_pages):
        page_max = jnp.max(s_pages[p], axis=1, keepdims=True)
        m = jnp.maximum(m, jnp.where(sel[p // PAGES_PER_BLOCK], page_max, NEG))
    p_own = jnp.exp(s_own - m)
    total = jnp.zeros((H_ATT, LANE), F32)
    for p in range(n_pages):
        pe = jnp.where(sel[p // PAGES_PER_BLOCK], jnp.exp(s_pages[p] - m), 0.0)
        pe_ref[p] = pe
        total = total + pe
    row_sum = p_own + jnp.sum(total, axis=1, keepdims=True)
    stats_ref[...] = jnp.where(lane == 0, p_own, row_sum)
    idx_ref[...] = ranked


N_VALUE_CHUNKS = MOBA_TOPK * PAGES_PER_BLOCK


def _value_chunk_page(sel_ref, row, h, c):
    return sel_ref[row, h * MOBA_TOPK + c // PAGES_PER_BLOCK] * PAGES_PER_BLOCK + c % PAGES_PER_BLOCK


def _moba_sample_mix_row(row, sel_ref, pe_ref, stats_ref, vnt_ref, chunks_ref, o_ref):
    w = vnt_ref.shape[0]
    on_row = lax.broadcasted_iota(jnp.int32, (w, LANE), 1) == row
    vn_col = jnp.sum(jnp.where(on_row, vnt_ref[...], 0.0), axis=1, keepdims=True)
    stats = stats_ref[...]
    out_cols = []
    for h in range(H_ATT):
        r = _head_sublane(h)
        acc = jnp.zeros((DH_ATT, LANE), F32)
        for c in range(N_VALUE_CHUNKS):
            acc = acc + (pe_ref[_value_chunk_page(sel_ref, row, h, c), r:r + 1, :]
                         * chunks_ref[h * N_VALUE_CHUNKS + c])
        p_own, row_sum = stats[r:r + 1, 0:1], stats[r:r + 1, 1:2]
        rows = slice(h * DH_ATT, (h + 1) * DH_ATT)
        out_cols.append((jnp.sum(acc, axis=1, keepdims=True) + p_own * vn_col[rows, :]) / row_sum)
    o_ref[...] = jnp.where(on_row, jnp.concatenate(out_cols, axis=0), o_ref[...])


def _cache_pages(cache):
    return jnp.transpose(cache, (0, 2, 3, 1))


def _selected_blocks(ranked):
    b = ranked.shape[0]
    return jnp.stack([ranked[:, _head_sublane(h), :MOBA_TOPK] for h in range(H_ATT)], axis=1).reshape(b, -1)


def _mlstm_sample_kernel(q_ref, k_ref, v_ref, o_ref, g_ref, gain_ref, c0_ref, n0_ref, m0_ref,
                         mem_ref, c_ref, n_ref, m_ref):
    tb = q_ref.shape[0]
    g = g_ref[...]
    sub = lax.broadcasted_iota(jnp.int32, (2 * tb, LANE), 0)
    zrows = jnp.zeros((tb, LANE), F32)
    for h in range(H_MLSTM):
        lanes = slice(h * DK_MLSTM, (h + 1) * DK_MLSTM)
        ig, lf, m0 = g[:, h:h + 1], g[:, H_MLSTM + h:H_MLSTM + h + 1], m0_ref[:, h:h + 1]
        q, k, v = q_ref[:, lanes], k_ref[:, lanes], v_ref[:, lanes]
        n0 = n0_ref[:, lanes]
        m_t = jnp.maximum(lf + m0, ig)
        w_inter = jnp.exp(lf + m0 - m_t)
        g_in = jnp.exp(ig - m_t)
        a = g_in * jnp.sum(q * k, axis=1, keepdims=True)
        den = w_inter * jnp.sum(n0 * q, axis=1, keepdims=True) + a
        q_b = q.astype(BF16)
        gv = jnp.concatenate([g_in * v, zrows], axis=0)
        k_b = jnp.concatenate([k, zrows], axis=0).astype(BF16)
        cq_rows = []
        for r in range(tb):
            c_prev = c0_ref[r, h]
            cq_rows.append(lax.dot_general(q_b, c_prev.astype(BF16), _NT, preferred_element_type=F32)[r:r + 1, :])
            outer = lax.dot_general(jnp.where(sub == r, gv, 0.0).astype(BF16), k_b, _TN,
                                    preferred_element_type=F32)
            c_ref[r, h] = w_inter[r:r + 1, :] * c_prev + outer
        cq = jnp.concatenate(cq_rows, axis=0)
        hh = (w_inter * cq + a * v) / jnp.maximum(jnp.abs(den), jnp.exp(-m_t))
        mem_ref[:, lanes] = _mlstm_head_out(hh, gain_ref[:, lanes], o_ref[:, lanes]).astype(mem_ref.dtype)
        n_ref[:, lanes] = w_inter * n0 + g_in * k
        m_ref[:, h:h + 1] = m_t


def _mlstm_sample(mq, mk, mv, mo, gates, gain, c0, n0, m0):
    b, w = mq.shape
    tb = SUBLANE
    rows = lambda width: pl.BlockSpec((tb, width), lambda i: (i, 0))
    c_spec = pl.BlockSpec((tb, H_MLSTM, DV_MLSTM, DK_MLSTM), lambda i: (i, 0, 0, 0))
    return pl.pallas_call(
        _mlstm_sample_kernel,
        grid=(b // tb,),
        in_specs=[rows(w), rows(w), rows(w), rows(w), rows(LANE), pl.BlockSpec((1, w), lambda i: (0, 0)),
                  c_spec, rows(w), rows(H_MLSTM)],
        out_specs=[rows(w), c_spec, rows(w), rows(H_MLSTM)],
        out_shape=[jax.ShapeDtypeStruct((b, w), F32),
                   jax.ShapeDtypeStruct(c0.shape, F32),
                   jax.ShapeDtypeStruct((b, w), F32),
                   jax.ShapeDtypeStruct((b, H_MLSTM), F32)],
        compiler_params=_params("arbitrary"),
    )(mq, mk, mv, mo, gates, gain, c0, n0.reshape(b, w), m0)


N_FINISH_INPUTS = 15


def _finish_kernel(*refs, alpha, ff_chunk, att_transposed, guest_rows):
    if guest_rows:
        pt_ref, sel_ref, refs = refs[0], refs[1], refs[2:]
    (x_ref, att_ref, mem_ref, g1_ref, sh2_ref, sc2_ref, g2_ref, wo_ref, ln1g_ref, ln1b_ref,
     wg_ref, wu_ref, wd_ref, ln2g_ref, ln2b_ref) = refs[:N_FINISH_INPUTS]
    if guest_rows:
        pe_ref, stats_ref, vnt_ref, cv_ref, y_ref, o_ref, vbuf, sem = refs[N_FINISH_INPUTS:]
    else:
        (y_ref,) = refs[N_FINISH_INPUTS:]

    att = att_ref[...].T if att_transposed else att_ref[...]
    aw = att.shape[1]
    mix = (jnp.dot(att.astype(BF16), wo_ref[0:aw, :], preferred_element_type=F32)
           + jnp.dot(mem_ref[...].astype(BF16), wo_ref[aw:, :], preferred_element_type=F32))
    x1 = _layernorm(alpha * x_ref[...] + (1.0 + g1_ref[...]) * mix, ln1g_ref[...], ln1b_ref[...])
    h2 = (x1 * (1.0 + sc2_ref[...]) + sh2_ref[...]).astype(BF16)
    n_ff = wg_ref.shape[1] // ff_chunk

    def ffn_chunk(c):
        cols = slice(c * ff_chunk, (c + 1) * ff_chunk)
        gate = jnp.dot(h2, wg_ref[:, cols], preferred_element_type=F32)
        up = jnp.dot(h2, wu_ref[:, cols], preferred_element_type=F32)
        act = (gate * jax.nn.sigmoid(gate) * up).astype(BF16)
        return jnp.dot(act, wd_ref[cols, :], preferred_element_type=F32)

    ffn = [jnp.zeros(x1.shape, F32)]
    if not guest_rows:
        for c in range(n_ff):
            ffn[0] = ffn[0] + ffn_chunk(c)
    else:
        i = pl.program_id(0)
        per_row = -(-n_ff // guest_rows)

        @pl.when(i == 0)
        def _init_out():
            o_ref[...] = jnp.zeros_like(o_ref)

        def chunk_copy(row, slot, h, c):
            page = pt_ref[row, _value_chunk_page(sel_ref, row, h, c)]
            return pltpu.make_async_copy(cv_ref.at[page, h], vbuf.at[slot, h * N_VALUE_CHUNKS + c], sem.at[slot])

        def start_row(row, slot):
            for h in range(H_ATT):
                for c in range(N_VALUE_CHUNKS):
                    chunk_copy(row, slot, h, c).start()

        def wait_row(row, slot):
            for h in range(H_ATT):
                for c in range(N_VALUE_CHUNKS):
                    chunk_copy(row, slot, h, c).wait()

        def work(r, row, slot):
            _moba_sample_mix_row(row, sel_ref, pe_ref.at[r], stats_ref.at[r], vnt_ref, vbuf.at[slot], o_ref)
            for c in range(r * per_row, min((r + 1) * per_row, n_ff)):
                ffn[0] = ffn[0] + ffn_chunk(c)

        _guest_rows_ring(i, pl.num_programs(0), guest_rows, start_row, wait_row, work)
    y_ref[...] = _layernorm(alpha * x1 + (1.0 + g2_ref[...]) * ffn[0], ln2g_ref[...], ln2b_ref[...])


def _finish(x, att, mem, mods, weights, tm, rows_per_mod, alpha, att_transposed=False, guest=None):
    t, d = x.shape
    assert not att_transposed or tm == t
    steps = t // tm
    w_out, ln1_g, ln1_b, w_gate, w_up, w_down, ln2_g, ln2_b = weights
    if rows_per_mod is None:
        mod_spec = pl.BlockSpec((tm, d), lambda i, *_: (i, 0))
    else:
        per = rows_per_mod // tm
        mod_spec = pl.BlockSpec((None, 1, d), lambda i, *_: (i // per, 0, 0))
    tok = lambda width: pl.BlockSpec((tm, width), lambda i, *_: (i, 0))
    ff_chunk = 256
    assert w_gate.shape[1] % ff_chunk == 0
    in_specs = [tok(d), pl.BlockSpec(att.shape, lambda *_: (0, 0)) if att_transposed else tok(att.shape[1]),
                tok(mem.shape[1]), mod_spec, mod_spec, mod_spec, mod_spec,
                _const_spec(w_out.shape), _const_spec(ln1_g.shape), _const_spec(ln1_b.shape),
                _const_spec(w_gate.shape), _const_spec(w_up.shape), _const_spec(w_down.shape),
                _const_spec(ln2_g.shape), _const_spec(ln2_b.shape)]
    args = (x, att, mem, *mods, w_out, ln1_g, ln1_b, w_gate, w_up, w_down, ln2_g, ln2_b)
    assert len(in_specs) == N_FINISH_INPUTS
    body = functools.partial(_finish_kernel, alpha=alpha, ff_chunk=ff_chunk, att_transposed=att_transposed,
                             guest_rows=0)
    y_shape = jax.ShapeDtypeStruct((t, d), F32)
    if guest is None:
        return pl.pallas_call(body, grid=(steps,), in_specs=in_specs, out_specs=tok(d), out_shape=y_shape,
                              compiler_params=_params("arbitrary"))(*args)

    pe, stats, sel, vnt, cache_v, page_table = guest
    w, bs = vnt.shape
    n_pages = page_table.shape[1]
    guest_rows = bs // steps
    assert bs % steps == 0 and bs == LANE
    rows = lambda *dims: pl.BlockSpec((guest_rows,) + dims, lambda i, *_: (i,) + (0,) * len(dims))
    whole = pl.BlockSpec((w, bs), lambda *_: (0, 0))
    return pl.pallas_call(
        functools.partial(body, guest_rows=guest_rows),
        grid_spec=pltpu.PrefetchScalarGridSpec(
            num_scalar_prefetch=2,
            grid=(steps,),
            in_specs=in_specs + [rows(n_pages, H_ATT, LANE), rows(H_ATT, LANE), whole,
                                 pl.BlockSpec(memory_space=pl.ANY)],
            out_specs=[tok(d), whole],
            scratch_shapes=[pltpu.VMEM((2 * guest_rows, H_ATT * N_VALUE_CHUNKS, DH_ATT, PAGE_SIZE), F32),
                            pltpu.SemaphoreType.DMA((2 * guest_rows,))],
        ),
        out_shape=[y_shape, jax.ShapeDtypeStruct((w, bs), F32)],
        compiler_params=_params("arbitrary"),
    )(page_table, sel, *args, pe, stats, vnt, _cache_pages(cache_v))


def kernel(x_prompt, x_sample, cache_k, cache_v, state_C, state_n, state_m, page_table, c_prompt, c_sample,
           w_ada, b_ada, w_in, b_if, mlstm_norm_g, w_out, ln1_g, ln1_b, w_gate, w_up, w_down, ln2_g, ln2_b):
    depth = w_in.shape[0]
    assert depth == 1, "single-layer step"
    alpha = (2.0 * depth) ** 0.25
    bp, s, d = x_prompt.shape
    bs = x_sample.shape[0]
    assert x_sample.shape[1] == 1, "single-token decode step"
    n_main = N_PROJ_GROUPS * PROJ_GROUP

    w_main = w_in[0, :, :n_main].astype(BF16)
    w_att_t = w_in[0, :, :3 * PROJ_GROUP].T.astype(BF16)
    w_gates = jnp.pad(w_in[0, :, n_main:], ((0, 0), (0, LANE - 2 * H_MLSTM))).astype(BF16)
    b_gates = jnp.pad(b_if[0], (0, LANE - 2 * H_MLSTM)).reshape(1, LANE)
    gain = mlstm_norm_g[0].reshape(1, MLSTM_WIDTH)
    row = lambda a: a[0].reshape(1, -1)
    fin_w = (w_out[0].astype(BF16), row(ln1_g), row(ln1_b), w_gate[0].astype(BF16), w_up[0].astype(BF16),
             w_down[0].astype(BF16), row(ln2_g), row(ln2_b))

    c_all = jnp.concatenate([c_prompt, c_sample], axis=0)
    mod = _adaln(c_all, w_ada[0], b_ada[0])
    sh1, sc1, g1, sh2, sc2, g2 = (mod[:, i * d:(i + 1) * d] for i in range(6))
    pm = lambda a: a[:bp].reshape(bp, 1, d)
    sm = lambda a: a[bp:]

    xs = x_sample.reshape(bs, d)
    aq_s, ak_s, av_s, mq_s, mk_s, mv_s, mo_s, gates_s = _in_proj(
        xs, sm(sc1), sm(sh1), w_main, w_att_t, w_gates, b_gates, bs, None, (F32,) * N_PROJ_GROUPS,
        transposed=(0, 1, 2))

    xp = x_prompt.reshape(bp * s, d)
    tm = 512
    aq, ak_t, av_t, mq, mk, mv, mo, gates = _in_proj(
        xp, pm(sc1), pm(sh1), w_main, w_att_t, w_gates, b_gates, tm, s,
        (F32, F32, F32, BF16, BF16, BF16, F32), transposed=(1, 2))
    seq = lambda a: a.reshape(bp, s, a.shape[-1])
    att = _moba_prompt(seq(aq), ak_t, av_t)
    mem, c_p, n_p, m_p, pe, stats, ranked = _mlstm_prompt(
        seq(mq), seq(mk), seq(mv), seq(mo), seq(gates), gain, guest=(aq_s, ak_s, cache_k[0], page_table))
    y_p, att_s = _finish(
        xp, att.reshape(bp * s, -1), mem.reshape(bp * s, -1), (pm(g1), pm(sh2), pm(sc2), pm(g2)),
        fin_w, tm, s, alpha, guest=(pe, stats, _selected_blocks(ranked), av_s, cache_v[0], page_table))

    mem_s, c_s, n_s, m_s = _mlstm_sample(mq_s, mk_s, mv_s, mo_s, gates_s, gain,
                                         state_C[0], state_n[0], state_m[0])
    y_s = _finish(xs, att_s, mem_s, (sm(g1), sm(sh2), sm(sc2), sm(g2)), fin_w, bs, None, alpha,
                  att_transposed=True)

    rows_p = lambda a: jnp.transpose(a.reshape(bp, H_ATT, DH_ATT, s), (0, 3, 1, 2))[None]
    rows_s = lambda a: jnp.transpose(a.reshape(H_ATT, DH_ATT, bs), (2, 0, 1)).reshape(1, bs, 1, H_ATT, DH_ATT)
    return (y_p.reshape(bp, s, d), y_s.reshape(bs, 1, d),
            rows_p(ak_t), rows_p(av_t),
            c_p[None], n_p[None, :, :H_MLSTM, :], m_p[None, :, :H_MLSTM, 0],
            rows_s(ak_s), rows_s(av_s),
            c_s[None], n_s.reshape(1, bs, H_MLSTM, DK_MLSTM), m_s[None])
```

```python
import functools
import math

import jax
import jax.numpy as jnp
from jax import lax
from jax.experimental import pallas as pl
from jax.experimental.pallas import tpu as pltpu

F32 = jnp.float32
BF16 = jnp.bfloat16
HIGHEST = lax.Precision.HIGHEST

LANE = 128
SUBLANE = 8
VMEM_LIMIT_BYTES = 56 * 1024 * 1024

H_ATT = 8
DH_ATT = 64
ATT_WIDTH = H_ATT * DH_ATT
MOBA_BLOCK = 256
MOBA_TOPK = 3
H_MLSTM = 4
DK_MLSTM = 128
DV_MLSTM = 128
MLSTM_WIDTH = H_MLSTM * DV_MLSTM
MLSTM_CHUNK = LANE
PAGE_SIZE = 128
PAGES_PER_BLOCK = MOBA_BLOCK // PAGE_SIZE
LN_EPS = 1e-5
NEG = -1e30
LOG2E = math.log2(math.e)
N_PROJ_GROUPS = 7
PROJ_GROUP = 512
MK_GROUP = 4

_NT = (((1,), (1,)), ((), ()))
_TN = (((0,), (0,)), ((), ()))


def _params(*sem):
    return pltpu.CompilerParams(dimension_semantics=sem, vmem_limit_bytes=VMEM_LIMIT_BYTES)


def _const_spec(shape):
    return pl.BlockSpec(shape, lambda *_: (0,) * len(shape), pipeline_mode=pl.Buffered(1))


def _layernorm(x, g, b):
    mu = jnp.mean(x, axis=-1, keepdims=True)
    d = x - mu
    var = jnp.mean(d * d, axis=-1, keepdims=True)
    return d * lax.rsqrt(var + LN_EPS) * g + b


def _top_blocks(val, nidx):
    cnt = jnp.zeros(val.shape, jnp.int32)
    for r in range(1, SUBLANE):
        other = pltpu.roll(val, r, 0)
        oidx = pltpu.roll(nidx, r, 0)
        beats = (other > val) | ((other == val) & (oidx < nidx))
        cnt = cnt + jnp.where(beats, 1, 0)
    return cnt < MOBA_TOPK


def _adaln_kernel(c_ref, w_ref, b_ref, o_ref):
    c = c_ref[...]
    s = c * jax.nn.sigmoid(c)
    o_ref[...] = jnp.dot(s, w_ref[...], preferred_element_type=F32) + b_ref[...]


def _adaln(c, w_ada, b_ada):
    rows, d = c.shape
    n = w_ada.shape[1]
    tn = d
    return pl.pallas_call(
        _adaln_kernel,
        grid=(n // tn,),
        in_specs=[pl.BlockSpec((rows, d), lambda j: (0, 0)),
                  pl.BlockSpec((d, tn), lambda j: (0, j)),
                  pl.BlockSpec((1, tn), lambda j: (0, j))],
        out_specs=pl.BlockSpec((rows, tn), lambda j: (0, j)),
        out_shape=jax.ShapeDtypeStruct((rows, n), F32),
        compiler_params=_params("arbitrary"),
    )(c, w_ada, b_ada.reshape(1, n))


def _guest_rows_row_ahead(i, n_steps, guest_rows, start_row, wait_row, work):
    assert guest_rows % 2 == 0

    @pl.when(i == 0)
    def _first_row():
        start_row(0, 0)

    for r in range(guest_rows):
        row, slot = i * guest_rows + r, r % 2
        if r + 1 < guest_rows:
            start_row(row + 1, 1 - slot)
        else:
            @pl.when(i + 1 < n_steps)
            def _next_step_row():
                start_row(row + 1, 1 - slot)
        wait_row(row, slot)
        work(r, row, slot)


def _guest_rows_step_ahead(i, n_steps, guest_rows, start_row, wait_row, work):
    half = (i % 2) * guest_rows

    @pl.when(i == 0)
    def _first_step_rows():
        for r in range(guest_rows):
            start_row(r, r)

    @pl.when(i + 1 < n_steps)
    def _next_step_rows():
        for r in range(guest_rows):
            start_row((i + 1) * guest_rows + r, guest_rows - half + r)

    for r in range(guest_rows):
        row, slot = i * guest_rows + r, half + r
        wait_row(row, slot)
        work(r, row, slot)


def _key_page_ring(pt_ref, ck_ref, kbuf, sem):
    n_pages = kbuf.shape[1]

    def page_copy(row, slot, p):
        return pltpu.make_async_copy(ck_ref.at[pt_ref[row, p]], kbuf.at[slot, p], sem.at[slot])

    def start_row(row, slot):
        for p in range(n_pages):
            page_copy(row, slot, p).start()

    def wait_row(row, slot):
        for p in range(n_pages):
            page_copy(row, slot, p).wait()

    return start_row, wait_row


def _in_proj_kernel(x_ref, sc_ref, sh_ref, w_ref, wt_ref, wg_ref, bg_ref, *out_refs, transposed):
    proj_refs, g_ref = out_refs[:N_PROJ_GROUPS], out_refs[N_PROJ_GROUPS]
    h = (x_ref[...] * (1.0 + sc_ref[...]) + sh_ref[...]).astype(BF16)
    for gi, o_ref in enumerate(proj_refs):
        cols = slice(gi * PROJ_GROUP, (gi + 1) * PROJ_GROUP)
        if gi in transposed:
            y = lax.dot_general(wt_ref[cols, :], h, _NT, preferred_element_type=F32)
        else:
            y = jnp.dot(h, w_ref[:, cols], preferred_element_type=F32)
        if gi == MK_GROUP:
            y = y * (DK_MLSTM ** -0.5)
        o_ref[...] = y.astype(o_ref.dtype)
    g = jnp.dot(h, wg_ref[...], preferred_element_type=F32) + bg_ref[...]
    lane = lax.broadcasted_iota(jnp.int32, g.shape, 1)
    logsig = jnp.minimum(g, 0.0) - jnp.log1p(jnp.exp(-jnp.abs(g)))
    g_ref[...] = jnp.where(lane >= H_MLSTM, logsig, g)


def _in_proj(x, sc, sh, w_main, w_att_t, w_gate, b_gate, tm, rows_per_mod, out_dtypes, transposed=()):
    t, d = x.shape
    steps = t // tm
    assert MK_GROUP not in transposed
    if rows_per_mod is None:
        mod_spec = pl.BlockSpec((tm, d), lambda i, *_: (i, 0))
        t_shape, t_spec = (PROJ_GROUP, t), pl.BlockSpec((PROJ_GROUP, tm), lambda i, *_: (0, i))
    else:
        per = rows_per_mod // tm
        mod_spec = pl.BlockSpec((None, 1, d), lambda i, *_: (i // per, 0, 0))
        t_shape = (t // rows_per_mod, PROJ_GROUP, rows_per_mod)
        t_spec = pl.BlockSpec((None, PROJ_GROUP, tm), lambda i, *_: (i // per, 0, i % per))
    out_shape, out_specs = [], []
    for gi, dt in enumerate(out_dtypes):
        if gi in transposed:
            out_shape.append(jax.ShapeDtypeStruct(t_shape, dt))
            out_specs.append(t_spec)
        else:
            out_shape.append(jax.ShapeDtypeStruct((t, PROJ_GROUP), dt))
            out_specs.append(pl.BlockSpec((tm, PROJ_GROUP), lambda i, *_: (i, 0)))
    out_shape.append(jax.ShapeDtypeStruct((t, LANE), F32))
    out_specs.append(pl.BlockSpec((tm, LANE), lambda i, *_: (i, 0)))
    in_specs = [pl.BlockSpec((tm, d), lambda i, *_: (i, 0)), mod_spec, mod_spec,
                _const_spec(w_main.shape), _const_spec(w_att_t.shape),
                _const_spec(w_gate.shape), _const_spec(b_gate.shape)]
    return pl.pallas_call(
        functools.partial(_in_proj_kernel, transposed=tuple(transposed)),
        grid=(steps,), in_specs=in_specs, out_specs=out_specs, out_shape=out_shape,
        compiler_params=_params("arbitrary"),
    )(x, sc, sh, w_main, w_att_t, w_gate, b_gate)


def _moba_prompt_kernel(q_ref, kt_ref, vt_ref, o_ref, kaug_ref, vaug_ref, kmt_ref, lhs_sc, m_sc, acc_sc, *, nb):
    i = pl.program_id(1)
    blk = MOBA_BLOCK
    half = LANE // 2
    w = q_ref.shape[1]

    @pl.when(i == 0)
    def _prepare_batch():
        srow = lax.broadcasted_iota(jnp.int32, (LANE, blk), 0)
        in_lo = srow < half
        head_of_row = lax.broadcasted_iota(jnp.int32, (w, LANE), 0) // DH_ATT
        lane_w = lax.broadcasted_iota(jnp.int32, (w, LANE), 1)
        kmt = jnp.zeros((w, LANE), F32)
        for j in range(nb):
            ktj = kt_ref[:, j * blk:(j + 1) * blk]
            vtj = vt_ref[:, j * blk:(j + 1) * blk]
            col = jnp.mean(ktj, axis=1, keepdims=True)
            kmt = jnp.where((lane_w % SUBLANE == j) & (lane_w // SUBLANE == head_of_row), col, kmt)
            for p in range(H_ATT // 2):
                kp, vp = ktj[p * LANE:(p + 1) * LANE, :], vtj[p * LANE:(p + 1) * LANE, :]
                kaug_ref[2 * p, j] = jnp.where(in_lo, kp, jnp.where(srow == half + j, 1.0, 0.0)).astype(BF16)
                kaug_ref[2 * p + 1, j] = jnp.where(in_lo, jnp.where(srow == j, 1.0, 0.0), kp).astype(BF16)
                vaug_ref[2 * p, j] = jnp.where(in_lo, vp, 1.0).astype(BF16)
                vaug_ref[2 * p + 1, j] = jnp.where(in_lo, 1.0, vp).astype(BF16)
        km_hi = kmt.astype(BF16)
        kmt_ref[0] = km_hi
        kmt_ref[1] = (kmt - km_hi.astype(F32)).astype(BF16)

    lane = lax.broadcasted_iota(jnp.int32, (blk, LANE), 1)
    lo_lanes = lane < half

    def store_lhs(p, bias_p):
        qp = q_ref[:, p * LANE:(p + 1) * LANE] * (DH_ATT ** -0.5 * LOG2E)
        lhs_sc[2 * p] = jnp.where(lo_lanes, qp, bias_p).astype(BF16)
        lhs_sc[2 * p + 1] = jnp.where(lo_lanes, bias_p, qp).astype(BF16)

    @pl.when(i <= MOBA_TOPK)
    def _every_past_block_selected():
        block_of_lane = lane % half
        bias = jnp.where((block_of_lane < SUBLANE) & (block_of_lane > i), NEG, 0.0)
        for p in range(H_ATT // 2):
            store_lhs(p, bias)

    @pl.when(i > MOBA_TOPK)
    def _ranked_blocks():
        q32 = q_ref[...]
        q_hi = q32.astype(BF16)
        q_lo = (q32 - q_hi.astype(F32)).astype(BF16)
        sc = (jnp.dot(q_hi, kmt_ref[0], preferred_element_type=F32)
              + (jnp.dot(q_hi, kmt_ref[1], preferred_element_type=F32)
                 + jnp.dot(q_lo, kmt_ref[0], preferred_element_type=F32)))
        sc_t = sc.T
        nidx = lax.broadcasted_iota(jnp.int32, (SUBLANE, blk), 0)
        past = nidx < i
        biases = []
        for h in range(H_ATT):
            val = jnp.where(past, sc_t[h * SUBLANE:(h + 1) * SUBLANE, :], NEG)
            keep = (_top_blocks(val, nidx) & past) | (nidx == i)
            biases.append(jnp.where(keep, 0.0, NEG))
        zpad = jnp.zeros((half - SUBLANE, blk), F32)
        for p in range(H_ATT // 2):
            store_lhs(p, jnp.concatenate([biases[2 * p + 1], zpad, biases[2 * p], zpad], axis=0).T)

    def scores(h, j):
        return jnp.dot(lhs_sc[h], kaug_ref[h, j], preferred_element_type=F32)

    def row_max(s):
        return jnp.broadcast_to(jnp.max(s, axis=1, keepdims=True), (blk, LANE))

    def weights(s, m):
        return jnp.exp2(s - jnp.concatenate([m, m], axis=1)).astype(BF16)

    row = lax.broadcasted_iota(jnp.int32, (blk, blk), 0)
    col = lax.broadcasted_iota(jnp.int32, (blk, blk), 1)
    causal = col <= row
    for h in range(H_ATT):
        s = jnp.where(causal, scores(h, i), NEG)
        m = row_max(s)
        acc_sc[h] = lax.dot_general(weights(s, m), vaug_ref[h, i], _NT, preferred_element_type=F32)
        m_sc[h] = m

    def past_blocks(js):
        for h in range(H_ATT):
            ss = [scores(h, j) for j in js]
            m_old = m_sc[h]
            m_new = m_old
            for s in ss:
                m_new = jnp.maximum(m_new, row_max(s))
            acc = jnp.exp2(m_old - m_new) * acc_sc[h]
            for s, j in zip(ss, js):
                acc = acc + lax.dot_general(weights(s, m_new), vaug_ref[h, j], _NT, preferred_element_type=F32)
            acc_sc[h] = acc
            m_sc[h] = m_new

    def two_past_blocks(t, carry):
        past_blocks((2 * t, 2 * t + 1))
        return carry

    lax.fori_loop(0, lax.shift_right_logical(i, 1), two_past_blocks, 0)

    @pl.when((i & 1) == 1)
    def _last_past_block():
        past_blocks((i - 1,))

    for p in range(H_ATT // 2):
        acc_e, acc_o = acc_sc[2 * p], acc_sc[2 * p + 1]
        num = jnp.where(lo_lanes, acc_e, acc_o)
        den = pltpu.roll(jnp.where(lo_lanes, acc_o, acc_e), half, 1)
        o_ref[:, p * LANE:(p + 1) * LANE] = (num / den).astype(o_ref.dtype)


def _moba_prompt(q, kt, vt):
    b, s, w = q.shape
    nb = s // MOBA_BLOCK
    assert s % MOBA_BLOCK == 0 and nb <= SUBLANE and w == ATT_WIDTH
    blk_state = lambda dt: pltpu.VMEM((H_ATT, MOBA_BLOCK, LANE), dt)
    return pl.pallas_call(
        functools.partial(_moba_prompt_kernel, nb=nb),
        grid=(b, nb),
        in_specs=[pl.BlockSpec((None, MOBA_BLOCK, w), lambda bi, i: (bi, i, 0)),
                  pl.BlockSpec((None, w, s), lambda bi, i: (bi, 0, 0)),
                  pl.BlockSpec((None, w, s), lambda bi, i: (bi, 0, 0))],
        out_specs=pl.BlockSpec((None, MOBA_BLOCK, w), lambda bi, i: (bi, i, 0)),
        out_shape=jax.ShapeDtypeStruct((b, s, w), BF16),
        scratch_shapes=[pltpu.VMEM((H_ATT, nb, LANE, MOBA_BLOCK), BF16),
                        pltpu.VMEM((H_ATT, nb, LANE, MOBA_BLOCK), BF16),
                        pltpu.VMEM((2, w, LANE), BF16),
                        blk_state(BF16), blk_state(F32), blk_state(F32)],
        compiler_params=_params("arbitrary", "arbitrary"),
    )(q, kt, vt)


def _mlstm_head_out(hh, gain, ogate):
    mu = jnp.mean(hh, axis=-1, keepdims=True)
    d = hh - mu
    var = jnp.mean(d * d, axis=-1, keepdims=True)
    return d * lax.rsqrt(var + LN_EPS) * gain * jax.nn.sigmoid(ogate)


N_MLSTM_INPUTS = 6


def _mlstm_prompt_kernel(*refs, guest):
    if guest:
        pt_ref, refs = refs[0], refs[1:]
    q_ref, k_ref, v_ref, o_ref, g_ref, gain_ref = refs[:N_MLSTM_INPUTS]
    refs = refs[N_MLSTM_INPUTS:]
    if guest:
        (qt_ref, knt_ref, ck_ref), refs = refs[:3], refs[3:]
        mem_ref, c_out, n_out, m_out, pe_ref, stats_ref, idx_ref, c_sc, n_sc, m_sc, kbuf, sem = refs
    else:
        mem_ref, c_out, n_out, m_out, c_sc, n_sc, m_sc = refs
    c = pl.program_id(1)
    nbat, L = q_ref.shape[0], q_ref.shape[1]

    @pl.when(c == 0)
    def _reset_state():
        c_sc[...] = jnp.zeros_like(c_sc)
        n_sc[...] = jnp.zeros_like(n_sc)
        m_sc[...] = jnp.zeros_like(m_sc)

    row = lax.broadcasted_iota(jnp.int32, (L, L), 0)
    col = lax.broadcasted_iota(jnp.int32, (L, L), 1)
    causal = col <= row
    lower = jnp.where(causal, 1.0, 0.0)
    upper = jnp.where(row <= col, 1.0, 0.0)
    ones = jnp.ones((L, DV_MLSTM), BF16)

    def batch_row(bi):
        g = g_ref[bi]
        g_t = g.T
        b_col_all = jnp.dot(lower, g, precision=HIGHEST, preferred_element_type=F32)
        b_row_all = jnp.dot(g_t[0:SUBLANE, :], upper, precision=HIGHEST, preferred_element_type=F32)
        for h in range(H_MLSTM):
            lanes = slice(h * DK_MLSTM, (h + 1) * DK_MLSTM)
            ig_row = g_t[h:h + 1, :]
            b_row = b_row_all[H_MLSTM + h:H_MLSTM + h + 1, :]
            ig = jnp.broadcast_to(g[:, h:h + 1], (L, LANE))
            b = jnp.broadcast_to(b_col_all[:, H_MLSTM + h:H_MLSTM + h + 1], (L, LANE))
            m_prev = m_sc[bi, h:h + 1, :]
            dmat = jnp.where(causal, b - b_row + ig_row, NEG)
            m_inter = b + m_prev
            m_t = jnp.maximum(m_inter, jnp.broadcast_to(jnp.max(dmat, axis=1, keepdims=True), (L, LANE)))
            w_inter = jnp.exp(m_inter - m_t)
            qh, kh, vh = q_ref[bi, :, lanes], k_ref[bi, :, lanes], v_ref[bi, :, lanes]
            a = jnp.exp(dmat - m_t) * lax.dot_general(qh, kh, _NT, preferred_element_type=F32)
            c_prev = c_sc[bi, h]
            n_prev = n_sc[bi, h:h + 1, :]
            state = jnp.concatenate([c_prev, jnp.broadcast_to(n_prev, (DV_MLSTM, DK_MLSTM))], axis=0).astype(BF16)
            num_den = (jnp.concatenate([w_inter, w_inter], axis=1)
                       * lax.dot_general(qh, state, _NT, preferred_element_type=F32)
                       + jnp.dot(a.astype(BF16), jnp.concatenate([vh, ones], axis=1), preferred_element_type=F32))
            hh = num_den[:, :DV_MLSTM] / jnp.maximum(jnp.abs(num_den[:, DV_MLSTM:]), jnp.exp(-m_t))
            mem_ref[bi, :, lanes] = _mlstm_head_out(
                hh, gain_ref[:, lanes], o_ref[bi, :, lanes].astype(F32)).astype(mem_ref.dtype)

            m_new = m_t[L - 1:L, :]
            b_last = b[L - 1:L, :]
            g_inter = jnp.exp(b_last + m_prev - m_new)
            g_in = jnp.exp(b_last - b + ig - m_new)
            v_scaled = (vh.astype(F32) * g_in).astype(BF16)
            c_sc[bi, h] = g_inter * c_prev + lax.dot_general(v_scaled, kh, _TN, preferred_element_type=F32)
            n_sc[bi, h:h + 1, :] = g_inter * n_prev + jnp.sum(kh.astype(F32) * g_in, axis=0, keepdims=True)
            m_sc[bi, h:h + 1, :] = m_new

    if not guest:
        for bi in range(nbat):
            batch_row(bi)
    else:
        def work(r, row, slot):
            _moba_sample_scores(row, qt_ref, knt_ref, [kbuf.at[slot, p] for p in range(kbuf.shape[1])],
                                pe_ref.at[r], stats_ref.at[r], idx_ref.at[r])
            batch_row(r)

        _guest_rows_step_ahead(pl.program_id(0) * pl.num_programs(1) + c, pl.num_programs(0) * pl.num_programs(1),
                         nbat, *_key_page_ring(pt_ref, ck_ref, kbuf, sem), work)

    @pl.when(c == pl.num_programs(1) - 1)
    def _emit_state():
        c_out[...] = c_sc[...]
        n_out[...] = n_sc[...]
        m_out[...] = m_sc[...]


def _mlstm_prompt(mq, mk, mv, mo, gates, gain, guest=None):
    b, s, w = mq.shape
    L = MLSTM_CHUNK
    nbat = 2 if b % 2 == 0 else 1
    nc = s // L
    assert s % L == 0 and w == MLSTM_WIDTH and L == LANE == DK_MLSTM == DV_MLSTM
    tok = lambda width: pl.BlockSpec((nbat, L, width), lambda bi, c, *_: (bi, c, 0))
    state = lambda *dims: pl.BlockSpec((nbat,) + dims, lambda bi, c, *_: (bi,) + (0,) * len(dims))
    in_specs = [tok(w), tok(w), tok(w), tok(w), tok(LANE), pl.BlockSpec((1, w), lambda *_: (0, 0))]
    out_specs = [tok(w), state(H_MLSTM, DV_MLSTM, DK_MLSTM), state(SUBLANE, LANE), state(SUBLANE, LANE)]
    out_shape = [jax.ShapeDtypeStruct((b, s, w), BF16),
                 jax.ShapeDtypeStruct((b, H_MLSTM, DV_MLSTM, DK_MLSTM), F32),
                 jax.ShapeDtypeStruct((b, SUBLANE, LANE), F32),
                 jax.ShapeDtypeStruct((b, SUBLANE, LANE), F32)]
    scratch = [pltpu.VMEM((nbat, H_MLSTM, DV_MLSTM, DK_MLSTM), F32),
               pltpu.VMEM((nbat, SUBLANE, LANE), F32),
               pltpu.VMEM((nbat, SUBLANE, LANE), F32)]
    args = (mq, mk, mv, mo, gates, gain)
    assert len(in_specs) == N_MLSTM_INPUTS
    grid = (b // nbat, nc)
    if guest is None:
        return pl.pallas_call(
            functools.partial(_mlstm_prompt_kernel, guest=False), grid=grid, in_specs=in_specs,
            out_specs=out_specs, out_shape=out_shape, scratch_shapes=scratch,
            compiler_params=_params("arbitrary", "arbitrary"))(*args)

    qt, knt, cache_k, page_table = guest
    wa, bs = qt.shape
    n_pages = page_table.shape[1]
    assert bs == grid[0] * nc * nbat == LANE and n_pages % PAGES_PER_BLOCK == 0
    assert MOBA_TOPK <= n_pages // PAGES_PER_BLOCK <= LANE
    ck = _cache_pages(cache_k).reshape(cache_k.shape[0], wa, PAGE_SIZE)
    rows = lambda *dims: pl.BlockSpec((nbat,) + dims, lambda bi, c, *_: (bi * nc + c,) + (0,) * len(dims))
    whole = pl.BlockSpec((wa, bs), lambda *_: (0, 0))
    return pl.pallas_call(
        functools.partial(_mlstm_prompt_kernel, guest=True),
        grid_spec=pltpu.PrefetchScalarGridSpec(
            num_scalar_prefetch=1,
            grid=grid,
            in_specs=in_specs + [whole, whole, pl.BlockSpec(memory_space=pl.ANY)],
            out_specs=out_specs + [rows(n_pages, H_ATT, LANE), rows(H_ATT, LANE), rows(H_ATT, LANE)],
            scratch_shapes=scratch + [pltpu.VMEM((2 * nbat, n_pages, wa, PAGE_SIZE), F32),
                                      pltpu.SemaphoreType.DMA((2 * nbat,))],
        ),
        out_shape=out_shape + [jax.ShapeDtypeStruct((bs, n_pages, H_ATT, LANE), F32),
                               jax.ShapeDtypeStruct((bs, H_ATT, LANE), F32),
                               jax.ShapeDtypeStruct((bs, H_ATT, LANE), jnp.int32)],
        compiler_params=_params("arbitrary", "arbitrary"),
    )(page_table, *args, qt, knt, ck)


def _head_sublane(h):
    return (H_ATT // 2 - 1 - h) if h < H_ATT // 2 else (H_ATT + H_ATT // 2 - 1 - h)


def _head_rows(x):
    parts = []
    for h in range(H_ATT):
        tiles = [x[h * DH_ATT + SUBLANE * t:h * DH_ATT + SUBLANE * (t + 1), :] for t in range(DH_ATT // SUBLANE)]
        parts.append(sum(tiles[1:], tiles[0]))
    sub = lax.broadcasted_iota(jnp.int32, parts[0].shape, 0)
    folded = [p + pltpu.roll(p, 4, 0) for p in parts]
    quads = [jnp.where(sub < 4, folded[i], folded[i + 4]) for i in range(4)]
    take_up = (sub & 2) != 0
    pairs = [jnp.where(take_up, quads[i] + pltpu.roll(quads[i], 2, 0),
                       quads[i + 2] + pltpu.roll(quads[i + 2], 6, 0)) for i in range(2)]
    return jnp.where((sub & 1) != 0, pairs[0] + pltpu.roll(pairs[0], 1, 0), pairs[1] + pltpu.roll(pairs[1], 7, 0))


def _moba_sample_scores(b, qt_ref, knt_ref, kp_refs, pe_ref, stats_ref, idx_ref):
    n_pages = len(kp_refs)
    n_blocks = n_pages // PAGES_PER_BLOCK
    w = qt_ref.shape[0]
    on_b = lax.broadcasted_iota(jnp.int32, (w, LANE), 1) == b

    def column(ref):
        return jnp.sum(jnp.where(on_b, ref[...], 0.0), axis=1, keepdims=True)

    q_col = column(qt_ref) * (DH_ATT ** -0.5)
    q_wide = jnp.broadcast_to(q_col, (w, LANE))
    s_own = _head_rows(jnp.broadcast_to(q_col * column(knt_ref), (w, LANE)))[:, 0:1]
    s_pages = [_head_rows(kp_refs[p][...] * q_wide) for p in range(n_pages)]

    blk = [jnp.sum(sum(s_pages[n * PAGES_PER_BLOCK + 1:(n + 1) * PAGES_PER_BLOCK], s_pages[n * PAGES_PER_BLOCK]),
                   axis=1, keepdims=True) for n in range(n_blocks)]
    lane = lax.broadcasted_iota(jnp.int32, (H_ATT, LANE), 1)
    sel, ranked = [], jnp.zeros((H_ATT, LANE), jnp.int32)
    for n in range(n_blocks):
        rank = jnp.zeros((H_ATT, 1), jnp.int32)
        for o in range(n_blocks):
            if o != n:
                beats = (blk[o] >= blk[n]) if o < n else (blk[o] > blk[n])
                rank = rank + jnp.where(beats, 1, 0)
        sel.append(rank < MOBA_TOPK)
        ranked = jnp.where(rank == lane, n, ranked)
    m = s_own
    for p in range(n_pages):
        page_max = jnp.max(s_pages[p], axis=1, keepdims=True)
        m = jnp.maximum(m, jnp.where(sel[p // PAGES_PER_BLOCK], page_max, NEG))
    p_own = jnp.exp(s_own - m)
    total = jnp.zeros((H_ATT, LANE), F32)
    for p in range(n_pages):
        pe = jnp.where(sel[p // PAGES_PER_BLOCK], jnp.exp(s_pages[p] - m), 0.0)
        pe_ref[p] = pe
        total = total + pe
    row_sum = p_own + jnp.sum(total, axis=1, keepdims=True)
    stats_ref[...] = jnp.where(lane == 0, p_own, row_sum)
    idx_ref[...] = ranked


N_VALUE_CHUNKS = MOBA_TOPK * PAGES_PER_BLOCK


def _value_chunk_page(sel_ref, row, h, c):
    return sel_ref[row, h * MOBA_TOPK + c // PAGES_PER_BLOCK] * PAGES_PER_BLOCK + c % PAGES_PER_BLOCK


def _moba_sample_mix_row(row, sel_ref, pe_ref, stats_ref, vnt_ref, chunks_ref, o_ref):
    w = vnt_ref.shape[0]
    on_row = lax.broadcasted_iota(jnp.int32, (w, LANE), 1) == row
    vn_col = jnp.sum(jnp.where(on_row, vnt_ref[...], 0.0), axis=1, keepdims=True)
    stats = stats_ref[...]
    out_cols = []
    for h in range(H_ATT):
        r = _head_sublane(h)
        acc = jnp.zeros((DH_ATT, LANE), F32)
        for c in range(N_VALUE_CHUNKS):
            acc = acc + (pe_ref[_value_chunk_page(sel_ref, row, h, c), r:r + 1, :]
                         * chunks_ref[h * N_VALUE_CHUNKS + c])
        p_own, row_sum = stats[r:r + 1, 0:1], stats[r:r + 1, 1:2]
        rows = slice(h * DH_ATT, (h + 1) * DH_ATT)
        out_cols.append((jnp.sum(acc, axis=1, keepdims=True) + p_own * vn_col[rows, :]) / row_sum)
    o_ref[...] = jnp.where(on_row, jnp.concatenate(out_cols, axis=0), o_ref[...])


def _cache_pages(cache):
    return jnp.transpose(cache, (0, 2, 3, 1))


def _selected_blocks(ranked):
    b = ranked.shape[0]
    return jnp.stack([ranked[:, _head_sublane(h), :MOBA_TOPK] for h in range(H_ATT)], axis=1).reshape(b, -1)


def _mlstm_sample_kernel(q_ref, k_ref, v_ref, o_ref, g_ref, gain_ref, c0_ref, n0_ref, m0_ref,
                         mem_ref, c_ref, n_ref, m_ref):
    tb = q_ref.shape[0]
    g = g_ref[...]
    sub = lax.broadcasted_iota(jnp.int32, (2 * tb, LANE), 0)
    zrows = jnp.zeros((tb, LANE), F32)
    for h in range(H_MLSTM):
        lanes = slice(h * DK_MLSTM, (h + 1) * DK_MLSTM)
        ig, lf, m0 = g[:, h:h + 1], g[:, H_MLSTM + h:H_MLSTM + h + 1], m0_ref[:, h:h + 1]
        q, k, v = q_ref[:, lanes], k_ref[:, lanes], v_ref[:, lanes]
        n0 = n0_ref[:, lanes]
        m_t = jnp.maximum(lf + m0, ig)
        w_inter = jnp.exp(lf + m0 - m_t)
        g_in = jnp.exp(ig - m_t)
        a = g_in * jnp.sum(q * k, axis=1, keepdims=True)
        den = w_inter * jnp.sum(n0 * q, axis=1, keepdims=True) + a
        q_b = q.astype(BF16)
        gv = jnp.concatenate([g_in * v, zrows], axis=0)
        k_b = jnp.concatenate([k, zrows], axis=0).astype(BF16)
        cq_rows = []
        for r in range(tb):
            c_prev = c0_ref[r, h]
            cq_rows.append(lax.dot_general(q_b, c_prev.astype(BF16), _NT, preferred_element_type=F32)[r:r + 1, :])
            outer = lax.dot_general(jnp.where(sub == r, gv, 0.0).astype(BF16), k_b, _TN,
                                    preferred_element_type=F32)
            c_ref[r, h] = w_inter[r:r + 1, :] * c_prev + outer
        cq = jnp.concatenate(cq_rows, axis=0)
        hh = (w_inter * cq + a * v) / jnp.maximum(jnp.abs(den), jnp.exp(-m_t))
        mem_ref[:, lanes] = _mlstm_head_out(hh, gain_ref[:, lanes], o_ref[:, lanes]).astype(mem_ref.dtype)
        n_ref[:, lanes] = w_inter * n0 + g_in * k
        m_ref[:, h:h + 1] = m_t


def _mlstm_sample(mq, mk, mv, mo, gates, gain, c0, n0, m0):
    b, w = mq.shape
    tb = SUBLANE
    rows = lambda width: pl.BlockSpec((tb, width), lambda i: (i, 0))
    c_spec = pl.BlockSpec((tb, H_MLSTM, DV_MLSTM, DK_MLSTM), lambda i: (i, 0, 0, 0))
    return pl.pallas_call(
        _mlstm_sample_kernel,
        grid=(b // tb,),
        in_specs=[rows(w), rows(w), rows(w), rows(w), rows(LANE), pl.BlockSpec((1, w), lambda i: (0, 0)),
                  c_spec, rows(w), rows(H_MLSTM)],
        out_specs=[rows(w), c_spec, rows(w), rows(H_MLSTM)],
        out_shape=[jax.ShapeDtypeStruct((b, w), F32),
                   jax.ShapeDtypeStruct(c0.shape, F32),
                   jax.ShapeDtypeStruct((b, w), F32),
                   jax.ShapeDtypeStruct((b, H_MLSTM), F32)],
        compiler_params=_params("arbitrary"),
    )(mq, mk, mv, mo, gates, gain, c0, n0.reshape(b, w), m0)


N_FINISH_INPUTS = 15


def _finish_kernel(*refs, alpha, ff_chunk, att_transposed, guest_rows):
    if guest_rows:
        pt_ref, sel_ref, refs = refs[0], refs[1], refs[2:]
    (x_ref, att_ref, mem_ref, g1_ref, sh2_ref, sc2_ref, g2_ref, wo_ref, ln1g_ref, ln1b_ref,
     wg_ref, wu_ref, wd_ref, ln2g_ref, ln2b_ref) = refs[:N_FINISH_INPUTS]
    if guest_rows:
        pe_ref, stats_ref, vnt_ref, cv_ref, y_ref, o_ref, vbuf, sem = refs[N_FINISH_INPUTS:]
    else:
        (y_ref,) = refs[N_FINISH_INPUTS:]

    att = att_ref[...].T if att_transposed else att_ref[...]
    aw = att.shape[1]
    mix = (jnp.dot(att.astype(BF16), wo_ref[0:aw, :], preferred_element_type=F32)
           + jnp.dot(mem_ref[...].astype(BF16), wo_ref[aw:, :], preferred_element_type=F32))
    x1 = _layernorm(alpha * x_ref[...] + (1.0 + g1_ref[...]) * mix, ln1g_ref[...], ln1b_ref[...])
    h2 = (x1 * (1.0 + sc2_ref[...]) + sh2_ref[...]).astype(BF16)
    n_ff = wg_ref.shape[1] // ff_chunk

    def ffn_chunk(c):
        cols = slice(c * ff_chunk, (c + 1) * ff_chunk)
        gate = jnp.dot(h2, wg_ref[:, cols], preferred_element_type=F32)
        up = jnp.dot(h2, wu_ref[:, cols], preferred_element_type=F32)
        act = (gate * jax.nn.sigmoid(gate) * up).astype(BF16)
        return jnp.dot(act, wd_ref[cols, :], preferred_element_type=F32)

    ffn = [jnp.zeros(x1.shape, F32)]
    if not guest_rows:
        for c in range(n_ff):
            ffn[0] = ffn[0] + ffn_chunk(c)
    else:
        i = pl.program_id(0)
        per_row = -(-n_ff // guest_rows)

        @pl.when(i == 0)
        def _init_out():
            o_ref[...] = jnp.zeros_like(o_ref)

        def chunk_copy(row, slot, h, c):
            page = pt_ref[row, _value_chunk_page(sel_ref, row, h, c)]
            return pltpu.make_async_copy(cv_ref.at[page, h], vbuf.at[slot, h * N_VALUE_CHUNKS + c], sem.at[slot])

        def start_row(row, slot):
            for h in range(H_ATT):
                for c in range(N_VALUE_CHUNKS):
                    chunk_copy(row, slot, h, c).start()

        def wait_row(row, slot):
            for h in range(H_ATT):
                for c in range(N_VALUE_CHUNKS):
                    chunk_copy(row, slot, h, c).wait()

        def work(r, row, slot):
            _moba_sample_mix_row(row, sel_ref, pe_ref.at[r], stats_ref.at[r], vnt_ref, vbuf.at[slot], o_ref)
            for c in range(r * per_row, min((r + 1) * per_row, n_ff)):
                ffn[0] = ffn[0] + ffn_chunk(c)

        _guest_rows_row_ahead(i, pl.num_programs(0), guest_rows, start_row, wait_row, work)
    y_ref[...] = _layernorm(alpha * x1 + (1.0 + g2_ref[...]) * ffn[0], ln2g_ref[...], ln2b_ref[...])


def _finish(x, att, mem, mods, weights, tm, rows_per_mod, alpha, att_transposed=False, guest=None):
    t, d = x.shape
    assert not att_transposed or tm == t
    steps = t // tm
    w_out, ln1_g, ln1_b, w_gate, w_up, w_down, ln2_g, ln2_b = weights
    if rows_per_mod is None:
        mod_spec = pl.BlockSpec((tm, d), lambda i, *_: (i, 0))
    else:
        per = rows_per_mod // tm
        mod_spec = pl.BlockSpec((None, 1, d), lambda i, *_: (i // per, 0, 0))
    tok = lambda width: pl.BlockSpec((tm, width), lambda i, *_: (i, 0))
    ff_chunk = 256
    assert w_gate.shape[1] % ff_chunk == 0
    in_specs = [tok(d), pl.BlockSpec(att.shape, lambda *_: (0, 0)) if att_transposed else tok(att.shape[1]),
                tok(mem.shape[1]), mod_spec, mod_spec, mod_spec, mod_spec,
                _const_spec(w_out.shape), _const_spec(ln1_g.shape), _const_spec(ln1_b.shape),
                _const_spec(w_gate.shape), _const_spec(w_up.shape), _const_spec(w_down.shape),
                _const_spec(ln2_g.shape), _const_spec(ln2_b.shape)]
    args = (x, att, mem, *mods, w_out, ln1_g, ln1_b, w_gate, w_up, w_down, ln2_g, ln2_b)
    assert len(in_specs) == N_FINISH_INPUTS
    body = functools.partial(_finish_kernel, alpha=alpha, ff_chunk=ff_chunk, att_transposed=att_transposed,
                             guest_rows=0)
    y_shape = jax.ShapeDtypeStruct((t, d), F32)
    if guest is None:
        return pl.pallas_call(body, grid=(steps,), in_specs=in_specs, out_specs=tok(d), out_shape=y_shape,
                              compiler_params=_params("arbitrary"))(*args)

    pe, stats, sel, vnt, cache_v, page_table = guest
    w, bs = vnt.shape
    n_pages = page_table.shape[1]
    guest_rows = bs // steps
    assert bs % steps == 0 and bs == LANE
    rows = lambda *dims: pl.BlockSpec((guest_rows,) + dims, lambda i, *_: (i,) + (0,) * len(dims))
    whole = pl.BlockSpec((w, bs), lambda *_: (0, 0))
    return pl.pallas_call(
        functools.partial(body, guest_rows=guest_rows),
        grid_spec=pltpu.PrefetchScalarGridSpec(
            num_scalar_prefetch=2,
            grid=(steps,),
            in_specs=in_specs + [rows(n_pages, H_ATT, LANE), rows(H_ATT, LANE), whole,
                                 pl.BlockSpec(memory_space=pl.ANY)],
            out_specs=[tok(d), whole],
            scratch_shapes=[pltpu.VMEM((2, H_ATT * N_VALUE_CHUNKS, DH_ATT, PAGE_SIZE), F32),
                            pltpu.SemaphoreType.DMA((2,))],
        ),
        out_shape=[y_shape, jax.ShapeDtypeStruct((w, bs), F32)],
        compiler_params=_params("arbitrary"),
    )(page_table, sel, *args, pe, stats, vnt, _cache_pages(cache_v))


def kernel(x_prompt, x_sample, cache_k, cache_v, state_C, state_n, state_m, page_table, c_prompt, c_sample,
           w_ada, b_ada, w_in, b_if, mlstm_norm_g, w_out, ln1_g, ln1_b, w_gate, w_up, w_down, ln2_g, ln2_b):
    depth = w_in.shape[0]
    assert depth == 1, "single-layer step"
    alpha = (2.0 * depth) ** 0.25
    bp, s, d = x_prompt.shape
    bs = x_sample.shape[0]
    assert x_sample.shape[1] == 1, "single-token decode step"
    n_main = N_PROJ_GROUPS * PROJ_GROUP

    w_main = w_in[0, :, :n_main].astype(BF16)
    w_att_t = w_in[0, :, :3 * PROJ_GROUP].T.astype(BF16)
    w_gates = jnp.pad(w_in[0, :, n_main:], ((0, 0), (0, LANE - 2 * H_MLSTM))).astype(BF16)
    b_gates = jnp.pad(b_if[0], (0, LANE - 2 * H_MLSTM)).reshape(1, LANE)
    gain = mlstm_norm_g[0].reshape(1, MLSTM_WIDTH)
    row = lambda a: a[0].reshape(1, -1)
    fin_w = (w_out[0].astype(BF16), row(ln1_g), row(ln1_b), w_gate[0].astype(BF16), w_up[0].astype(BF16),
             w_down[0].astype(BF16), row(ln2_g), row(ln2_b))

    c_all = jnp.concatenate([c_prompt, c_sample], axis=0)
    mod = _adaln(c_all, w_ada[0], b_ada[0])
    sh1, sc1, g1, sh2, sc2, g2 = (mod[:, i * d:(i + 1) * d] for i in range(6))
    pm = lambda a: a[:bp].reshape(bp, 1, d)
    sm = lambda a: a[bp:]

    xs = x_sample.reshape(bs, d)
    aq_s, ak_s, av_s, mq_s, mk_s, mv_s, mo_s, gates_s = _in_proj(
        xs, sm(sc1), sm(sh1), w_main, w_att_t, w_gates, b_gates, bs, None, (F32,) * N_PROJ_GROUPS,
        transposed=(0, 1, 2))

    xp = x_prompt.reshape(bp * s, d)
    tm = 512
    aq, ak_t, av_t, mq, mk, mv, mo, gates = _in_proj(
        xp, pm(sc1), pm(sh1), w_main, w_att_t, w_gates, b_gates, tm, s,
        (F32, F32, F32, BF16, BF16, BF16, F32), transposed=(1, 2))
    seq = lambda a: a.reshape(bp, s, a.shape[-1])
    att = _moba_prompt(seq(aq), ak_t, av_t)
    mem, c_p, n_p, m_p, pe, stats, ranked = _mlstm_prompt(
        seq(mq), seq(mk), seq(mv), seq(mo), seq(gates), gain, guest=(aq_s, ak_s, cache_k[0], page_table))
    y_p, att_s = _finish(
        xp, att.reshape(bp * s, -1), mem.reshape(bp * s, -1), (pm(g1), pm(sh2), pm(sc2), pm(g2)),
        fin_w, tm, s, alpha, guest=(pe, stats, _selected_blocks(ranked), av_s, cache_v[0], page_table))

    mem_s, c_s, n_s, m_s = _mlstm_sample(mq_s, mk_s, mv_s, mo_s, gates_s, gain,
                                         state_C[0], state_n[0], state_m[0])
    y_s = _finish(xs, att_s, mem_s, (sm(g1), sm(sh2), sm(sc2), sm(g2)), fin_w, bs, None, alpha,
                  att_transposed=True)

    rows_p = lambda a: jnp.transpose(a.reshape(bp, H_ATT, DH_ATT, s), (0, 3, 1, 2))[None]
    rows_s = lambda a: jnp.transpose(a.reshape(H_ATT, DH_ATT, bs), (2, 0, 1)).reshape(1, bs, 1, H_ATT, DH_ATT)
    return (y_p.reshape(bp, s, d), y_s.reshape(bs, 1, d),
            rows_p(ak_t), rows_p(av_t),
            c_p[None], n_p[None, :, :H_MLSTM, :], m_p[None, :, :H_MLSTM, 0],
            rows_s(ak_s), rows_s(av_s),
            c_s[None], n_s.reshape(1, bs, H_MLSTM, DK_MLSTM), m_s[None])
```

```python
import functools
import math

import jax
import jax.numpy as jnp
from jax import lax
from jax.experimental import pallas as pl
from jax.experimental.pallas import tpu as pltpu

F32 = jnp.float32
BF16 = jnp.bfloat16
HIGHEST = lax.Precision.HIGHEST

LANE = 128
SUBLANE = 8
VMEM_LIMIT_BYTES = 56 * 1024 * 1024

H_ATT = 8
DH_ATT = 64
ATT_WIDTH = H_ATT * DH_ATT
MOBA_BLOCK = 256
MOBA_TOPK = 3
H_MLSTM = 4
DK_MLSTM = 128
DV_MLSTM = 128
MLSTM_WIDTH = H_MLSTM * DV_MLSTM
MLSTM_CHUNK = LANE
PAGE_SIZE = 128
PAGES_PER_BLOCK = MOBA_BLOCK // PAGE_SIZE
LN_EPS = 1e-5
NEG = -1e30
LOG2E = math.log2(math.e)
N_PROJ_GROUPS = 7
PROJ_GROUP = 512
MK_GROUP = 4

_NT = (((1,), (1,)), ((), ()))
_TN = (((0,), (0,)), ((), ()))


def _params(*sem):
    return pltpu.CompilerParams(dimension_semantics=sem, vmem_limit_bytes=VMEM_LIMIT_BYTES)


def _const_spec(shape):
    return pl.BlockSpec(shape, lambda *_: (0,) * len(shape), pipeline_mode=pl.Buffered(1))


def _layernorm(x, g, b):
    mu = jnp.mean(x, axis=-1, keepdims=True)
    d = x - mu
    var = jnp.mean(d * d, axis=-1, keepdims=True)
    return d * lax.rsqrt(var + LN_EPS) * g + b


def _top_blocks(val, nidx):
    cnt = jnp.zeros(val.shape, jnp.int32)
    for r in range(1, SUBLANE):
        other = pltpu.roll(val, r, 0)
        oidx = pltpu.roll(nidx, r, 0)
        beats = (other > val) | ((other == val) & (oidx < nidx))
        cnt = cnt + jnp.where(beats, 1, 0)
    return cnt < MOBA_TOPK


def _adaln_kernel(c_ref, w_ref, b_ref, o_ref):
    c = c_ref[...]
    s = c * jax.nn.sigmoid(c)
    o_ref[...] = jnp.dot(s, w_ref[...], preferred_element_type=F32) + b_ref[...]


def _adaln(c, w_ada, b_ada):
    rows, d = c.shape
    n = w_ada.shape[1]
    tn = d
    return pl.pallas_call(
        _adaln_kernel,
        grid=(n // tn,),
        in_specs=[pl.BlockSpec((rows, d), lambda j: (0, 0)),
                  pl.BlockSpec((d, tn), lambda j: (0, j)),
                  pl.BlockSpec((1, tn), lambda j: (0, j))],
        out_specs=pl.BlockSpec((rows, tn), lambda j: (0, j)),
        out_shape=jax.ShapeDtypeStruct((rows, n), F32),
        compiler_params=_params("arbitrary"),
    )(c, w_ada, b_ada.reshape(1, n))


def _guest_rows_row_ahead(i, n_steps, guest_rows, start_row, wait_row, work):
    assert guest_rows % 2 == 0

    @pl.when(i == 0)
    def _first_row():
        start_row(0, 0)

    for r in range(guest_rows):
        row, slot = i * guest_rows + r, r % 2
        if r + 1 < guest_rows:
            start_row(row + 1, 1 - slot)
        else:
            @pl.when(i + 1 < n_steps)
            def _next_step_row():
                start_row(row + 1, 1 - slot)
        wait_row(row, slot)
        work(r, row, slot)


def _guest_rows_step_ahead(i, n_steps, guest_rows, start_row, wait_row, work):
    half = (i % 2) * guest_rows

    @pl.when(i == 0)
    def _first_step_rows():
        for r in range(guest_rows):
            start_row(r, r)

    @pl.when(i + 1 < n_steps)
    def _next_step_rows():
        for r in range(guest_rows):
            start_row((i + 1) * guest_rows + r, guest_rows - half + r)

    for r in range(guest_rows):
        row, slot = i * guest_rows + r, half + r
        wait_row(row, slot)
        work(r, row, slot)


def _key_page_ring(pt_ref, ck_ref, kbuf, sem):
    n_pages = kbuf.shape[1]

    def page_copy(row, slot, p):
        return pltpu.make_async_copy(ck_ref.at[pt_ref[row, p]], kbuf.at[slot, p], sem.at[slot])

    def start_row(row, slot):
        for p in range(n_pages):
            page_copy(row, slot, p).start()

    def wait_row(row, slot):
        for p in range(n_pages):
            page_copy(row, slot, p).wait()

    return start_row, wait_row


def _in_proj_kernel(x_ref, sc_ref, sh_ref, w_ref, wt_ref, wg_ref, bg_ref, *out_refs, transposed):
    proj_refs, g_ref = out_refs[:N_PROJ_GROUPS], out_refs[N_PROJ_GROUPS]
    h = (x_ref[...] * (1.0 + sc_ref[...]) + sh_ref[...]).astype(BF16)
    for gi, o_ref in enumerate(proj_refs):
        cols = slice(gi * PROJ_GROUP, (gi + 1) * PROJ_GROUP)
        if gi in transposed:
            y = lax.dot_general(wt_ref[cols, :], h, _NT, preferred_element_type=F32)
        else:
            y = jnp.dot(h, w_ref[:, cols], preferred_element_type=F32)
        if gi == MK_GROUP:
            y = y * (DK_MLSTM ** -0.5)
        o_ref[...] = y.astype(o_ref.dtype)
    g = jnp.dot(h, wg_ref[...], preferred_element_type=F32) + bg_ref[...]
    lane = lax.broadcasted_iota(jnp.int32, g.shape, 1)
    logsig = jnp.minimum(g, 0.0) - jnp.log1p(jnp.exp(-jnp.abs(g)))
    g_ref[...] = jnp.where(lane >= H_MLSTM, logsig, g)


def _in_proj(x, sc, sh, w_main, w_att_t, w_gate, b_gate, tm, rows_per_mod, out_dtypes, transposed=()):
    t, d = x.shape
    steps = t // tm
    assert MK_GROUP not in transposed
    if rows_per_mod is None:
        mod_spec = pl.BlockSpec((tm, d), lambda i, *_: (i, 0))
        t_shape, t_spec = (PROJ_GROUP, t), pl.BlockSpec((PROJ_GROUP, tm), lambda i, *_: (0, i))
    else:
        per = rows_per_mod // tm
        mod_spec = pl.BlockSpec((None, 1, d), lambda i, *_: (i // per, 0, 0))
        t_shape = (t // rows_per_mod, PROJ_GROUP, rows_per_mod)
        t_spec = pl.BlockSpec((None, PROJ_GROUP, tm), lambda i, *_: (i // per, 0, i % per))
    out_shape, out_specs = [], []
    for gi, dt in enumerate(out_dtypes):
        if gi in transposed:
            out_shape.append(jax.ShapeDtypeStruct(t_shape, dt))
            out_specs.append(t_spec)
        else:
            out_shape.append(jax.ShapeDtypeStruct((t, PROJ_GROUP), dt))
            out_specs.append(pl.BlockSpec((tm, PROJ_GROUP), lambda i, *_: (i, 0)))
    out_shape.append(jax.ShapeDtypeStruct((t, LANE), F32))
    out_specs.append(pl.BlockSpec((tm, LANE), lambda i, *_: (i, 0)))
    in_specs = [pl.BlockSpec((tm, d), lambda i, *_: (i, 0)), mod_spec, mod_spec,
                _const_spec(w_main.shape), _const_spec(w_att_t.shape),
                _const_spec(w_gate.shape), _const_spec(b_gate.shape)]
    return pl.pallas_call(
        functools.partial(_in_proj_kernel, transposed=tuple(transposed)),
        grid=(steps,), in_specs=in_specs, out_specs=out_specs, out_shape=out_shape,
        compiler_params=_params("arbitrary"),
    )(x, sc, sh, w_main, w_att_t, w_gate, b_gate)


def _moba_step(i, q_ref, kt_ref, vt_ref, o_ref, kaug_ref, vaug_ref, kmt_ref, lhs_sc, m_sc, acc_sc, nb, beside_own_block):
    blk = MOBA_BLOCK
    half = LANE // 2
    w = q_ref.shape[1]

    @pl.when(i == 0)
    def _prepare_batch():
        srow = lax.broadcasted_iota(jnp.int32, (LANE, blk), 0)
        in_lo = srow < half
        head_of_row = lax.broadcasted_iota(jnp.int32, (w, LANE), 0) // DH_ATT
        lane_w = lax.broadcasted_iota(jnp.int32, (w, LANE), 1)
        kmt = jnp.zeros((w, LANE), F32)
        for j in range(nb):
            ktj = kt_ref[:, j * blk:(j + 1) * blk]
            vtj = vt_ref[:, j * blk:(j + 1) * blk]
            col = jnp.mean(ktj, axis=1, keepdims=True)
            kmt = jnp.where((lane_w % SUBLANE == j) & (lane_w // SUBLANE == head_of_row), col, kmt)
            for p in range(H_ATT // 2):
                kp, vp = ktj[p * LANE:(p + 1) * LANE, :], vtj[p * LANE:(p + 1) * LANE, :]
                kaug_ref[2 * p, j] = jnp.where(in_lo, kp, jnp.where(srow == half + j, 1.0, 0.0)).astype(BF16)
                kaug_ref[2 * p + 1, j] = jnp.where(in_lo, jnp.where(srow == j, 1.0, 0.0), kp).astype(BF16)
                vaug_ref[2 * p, j] = jnp.where(in_lo, vp, 1.0).astype(BF16)
                vaug_ref[2 * p + 1, j] = jnp.where(in_lo, 1.0, vp).astype(BF16)
        km_hi = kmt.astype(BF16)
        kmt_ref[0] = km_hi
        kmt_ref[1] = (kmt - km_hi.astype(F32)).astype(BF16)

    lane = lax.broadcasted_iota(jnp.int32, (blk, LANE), 1)
    lo_lanes = lane < half

    def store_lhs(p, bias_p):
        qp = q_ref[:, p * LANE:(p + 1) * LANE] * (DH_ATT ** -0.5 * LOG2E)
        lhs_sc[2 * p] = jnp.where(lo_lanes, qp, bias_p).astype(BF16)
        lhs_sc[2 * p + 1] = jnp.where(lo_lanes, bias_p, qp).astype(BF16)

    @pl.when(i <= MOBA_TOPK)
    def _every_past_block_selected():
        block_of_lane = lane % half
        bias = jnp.where((block_of_lane < SUBLANE) & (block_of_lane > i), NEG, 0.0)
        for p in range(H_ATT // 2):
            store_lhs(p, bias)

    @pl.when(i > MOBA_TOPK)
    def _ranked_blocks():
        q32 = q_ref[...]
        q_hi = q32.astype(BF16)
        q_lo = (q32 - q_hi.astype(F32)).astype(BF16)
        sc = (jnp.dot(q_hi, kmt_ref[0], preferred_element_type=F32)
              + (jnp.dot(q_hi, kmt_ref[1], preferred_element_type=F32)
                 + jnp.dot(q_lo, kmt_ref[0], preferred_element_type=F32)))
        sc_t = sc.T
        nidx = lax.broadcasted_iota(jnp.int32, (SUBLANE, blk), 0)
        past = nidx < i
        biases = []
        for h in range(H_ATT):
            val = jnp.where(past, sc_t[h * SUBLANE:(h + 1) * SUBLANE, :], NEG)
            keep = (_top_blocks(val, nidx) & past) | (nidx == i)
            biases.append(jnp.where(keep, 0.0, NEG))
        zpad = jnp.zeros((half - SUBLANE, blk), F32)
        for p in range(H_ATT // 2):
            store_lhs(p, jnp.concatenate([biases[2 * p + 1], zpad, biases[2 * p], zpad], axis=0).T)

    def scores(h, j):
        return jnp.dot(lhs_sc[h], kaug_ref[h, j], preferred_element_type=F32)

    def row_max(s):
        return jnp.broadcast_to(jnp.max(s, axis=1, keepdims=True), (blk, LANE))

    def weights(s, m):
        return jnp.exp2(s - jnp.concatenate([m, m], axis=1)).astype(BF16)

    row = lax.broadcasted_iota(jnp.int32, (blk, blk), 0)
    col = lax.broadcasted_iota(jnp.int32, (blk, blk), 1)
    causal = col <= row
    beside_own_block()
    for h in range(H_ATT):
        s = jnp.where(causal, scores(h, i), NEG)
        m = row_max(s)
        acc_sc[h] = lax.dot_general(weights(s, m), vaug_ref[h, i], _NT, preferred_element_type=F32)
        m_sc[h] = m

    def past_blocks(js):
        for h in range(H_ATT):
            ss = [scores(h, j) for j in js]
            m_old = m_sc[h]
            m_new = m_old
            for s in ss:
                m_new = jnp.maximum(m_new, row_max(s))
            acc = jnp.exp2(m_old - m_new) * acc_sc[h]
            for s, j in zip(ss, js):
                acc = acc + lax.dot_general(weights(s, m_new), vaug_ref[h, j], _NT, preferred_element_type=F32)
            acc_sc[h] = acc
            m_sc[h] = m_new

    def two_past_blocks(t, carry):
        past_blocks((2 * t, 2 * t + 1))
        return carry

    lax.fori_loop(0, lax.shift_right_logical(i, 1), two_past_blocks, 0)

    @pl.when((i & 1) == 1)
    def _last_past_block():
        past_blocks((i - 1,))

    for p in range(H_ATT // 2):
        acc_e, acc_o = acc_sc[2 * p], acc_sc[2 * p + 1]
        num = jnp.where(lo_lanes, acc_e, acc_o)
        den = pltpu.roll(jnp.where(lo_lanes, acc_o, acc_e), half, 1)
        o_ref[:, p * LANE:(p + 1) * LANE] = (num / den).astype(o_ref.dtype)


def _mlstm_head_out(hh, gain, ogate):
    mu = jnp.mean(hh, axis=-1, keepdims=True)
    d = hh - mu
    var = jnp.mean(d * d, axis=-1, keepdims=True)
    return d * lax.rsqrt(var + LN_EPS) * gain * jax.nn.sigmoid(ogate)


def _mlstm_chunk(tok, q_ref, k_ref, v_ref, o_ref, g_ref, gain_ref, mem_ref, c_sc, n_sc, m_sc):
    L = MLSTM_CHUNK
    row = lax.broadcasted_iota(jnp.int32, (L, L), 0)
    col = lax.broadcasted_iota(jnp.int32, (L, L), 1)
    causal = col <= row
    lower = jnp.where(causal, 1.0, 0.0)
    upper = jnp.where(row <= col, 1.0, 0.0)
    ones = jnp.ones((L, DV_MLSTM), BF16)

    g = g_ref[tok, :]
    g_t = g.T
    b_col_all = jnp.dot(lower, g, precision=HIGHEST, preferred_element_type=F32)
    b_row_all = jnp.dot(g_t[0:SUBLANE, :], upper, precision=HIGHEST, preferred_element_type=F32)
    for h in range(H_MLSTM):
        lanes = slice(h * DK_MLSTM, (h + 1) * DK_MLSTM)
        ig_row = g_t[h:h + 1, :]
        b_row = b_row_all[H_MLSTM + h:H_MLSTM + h + 1, :]
        ig = jnp.broadcast_to(g[:, h:h + 1], (L, LANE))
        b = jnp.broadcast_to(b_col_all[:, H_MLSTM + h:H_MLSTM + h + 1], (L, LANE))
        m_prev = m_sc[h:h + 1, :]
        dmat = jnp.where(causal, b - b_row + ig_row, NEG)
        m_inter = b + m_prev
        m_t = jnp.maximum(m_inter, jnp.broadcast_to(jnp.max(dmat, axis=1, keepdims=True), (L, LANE)))
        w_inter = jnp.exp(m_inter - m_t)
        qh, kh, vh = q_ref[tok, lanes], k_ref[tok, lanes], v_ref[tok, lanes]
        a = jnp.exp(dmat - m_t) * lax.dot_general(qh, kh, _NT, preferred_element_type=F32)
        c_prev = c_sc[h]
        n_prev = n_sc[h:h + 1, :]
        state = jnp.concatenate([c_prev, jnp.broadcast_to(n_prev, (DV_MLSTM, DK_MLSTM))], axis=0).astype(BF16)
        num_den = (jnp.concatenate([w_inter, w_inter], axis=1)
                   * lax.dot_general(qh, state, _NT, preferred_element_type=F32)
                   + jnp.dot(a.astype(BF16), jnp.concatenate([vh, ones], axis=1), preferred_element_type=F32))
        hh = num_den[:, :DV_MLSTM] / jnp.maximum(jnp.abs(num_den[:, DV_MLSTM:]), jnp.exp(-m_t))
        mem_ref[tok, lanes] = _mlstm_head_out(
            hh, gain_ref[:, lanes], o_ref[tok, lanes].astype(F32)).astype(mem_ref.dtype)

        m_new = m_t[L - 1:L, :]
        b_last = b[L - 1:L, :]
        g_inter = jnp.exp(b_last + m_prev - m_new)
        g_in = jnp.exp(b_last - b + ig - m_new)
        v_scaled = (vh.astype(F32) * g_in).astype(BF16)
        c_sc[h] = g_inter * c_prev + lax.dot_general(v_scaled, kh, _TN, preferred_element_type=F32)
        n_sc[h:h + 1, :] = g_inter * n_prev + jnp.sum(kh.astype(F32) * g_in, axis=0, keepdims=True)
        m_sc[h:h + 1, :] = m_new


def _prompt_mixers_kernel(pt_ref, q_ref, kt_ref, vt_ref, mq_ref, mk_ref, mv_ref, mo_ref, g_ref, gain_ref,
                          qt_ref, knt_ref, ck_ref,
                          att_ref, mem_ref, c_out, n_out, m_out, pe_ref, stats_ref, idx_ref,
                          kaug_ref, vaug_ref, kmt_ref, lhs_sc, m_att, acc_sc, c_sc, n_sc, m_sc, kbuf, sem,
                          *, nb, guest_rows):
    bi, i = pl.program_id(0), pl.program_id(1)

    @pl.when(i == 0)
    def _reset_state():
        c_sc[...] = jnp.zeros_like(c_sc)
        n_sc[...] = jnp.zeros_like(n_sc)
        m_sc[...] = jnp.zeros_like(m_sc)

    ready = []
    _guest_rows_step_ahead(bi * nb + i, pl.num_programs(0) * nb, guest_rows,
                           *_key_page_ring(pt_ref, ck_ref, kbuf, sem), lambda *row: ready.append(row))

    def recurrence_and_scores():
        for r, row, slot in ready:
            _moba_sample_scores(row, qt_ref, knt_ref, [kbuf.at[slot, p] for p in range(kbuf.shape[1])],
                                pe_ref.at[r], stats_ref.at[r], idx_ref.at[r])
        for c in range(MOBA_BLOCK // MLSTM_CHUNK):
            _mlstm_chunk(slice(c * MLSTM_CHUNK, (c + 1) * MLSTM_CHUNK), mq_ref, mk_ref, mv_ref, mo_ref, g_ref,
                         gain_ref, mem_ref, c_sc, n_sc, m_sc)

    _moba_step(i, q_ref, kt_ref, vt_ref, att_ref, kaug_ref, vaug_ref, kmt_ref, lhs_sc, m_att, acc_sc, nb,
               recurrence_and_scores)

    @pl.when(i == nb - 1)
    def _emit_state():
        c_out[...] = c_sc[...]
        n_out[...] = n_sc[...]
        m_out[...] = m_sc[...]


def _prompt_mixers(q, kt, vt, mq, mk, mv, mo, gates, gain, qt, knt, cache_k, page_table):
    b, s, w = q.shape
    nb = s // MOBA_BLOCK
    wa, bs = qt.shape
    n_pages = page_table.shape[1]
    guest_rows = bs // (b * nb)
    assert s % MOBA_BLOCK == 0 and nb <= SUBLANE and w == ATT_WIDTH == wa and mq.shape[2] == MLSTM_WIDTH
    assert MOBA_BLOCK % MLSTM_CHUNK == 0 and MLSTM_CHUNK == LANE == DK_MLSTM == DV_MLSTM
    assert bs == guest_rows * b * nb == LANE and n_pages % PAGES_PER_BLOCK == 0
    assert MOBA_TOPK <= n_pages // PAGES_PER_BLOCK <= LANE
    ck = _cache_pages(cache_k).reshape(cache_k.shape[0], wa, PAGE_SIZE)
    tok = lambda width: pl.BlockSpec((None, MOBA_BLOCK, width), lambda bi, i, *_: (bi, i, 0))
    seq = pl.BlockSpec((None, w, s), lambda bi, i, *_: (bi, 0, 0))
    state = lambda *dims: pl.BlockSpec((None,) + dims, lambda bi, i, *_: (bi,) + (0,) * len(dims))
    rows = lambda *dims: pl.BlockSpec((guest_rows,) + dims, lambda bi, i, *_: (bi * nb + i,) + (0,) * len(dims))
    whole = pl.BlockSpec((wa, bs), lambda *_: (0, 0))
    blk_state = lambda dt: pltpu.VMEM((H_ATT, MOBA_BLOCK, LANE), dt)
    return pl.pallas_call(
        functools.partial(_prompt_mixers_kernel, nb=nb, guest_rows=guest_rows),
        grid_spec=pltpu.PrefetchScalarGridSpec(
            num_scalar_prefetch=1,
            grid=(b, nb),
            in_specs=[tok(w), seq, seq, tok(MLSTM_WIDTH), tok(MLSTM_WIDTH), tok(MLSTM_WIDTH), tok(MLSTM_WIDTH),
                      tok(LANE), pl.BlockSpec((1, MLSTM_WIDTH), lambda *_: (0, 0)),
                      whole, whole, pl.BlockSpec(memory_space=pl.ANY)],
            out_specs=[tok(w), tok(MLSTM_WIDTH), state(H_MLSTM, DV_MLSTM, DK_MLSTM), state(SUBLANE, LANE),
                       state(SUBLANE, LANE), rows(n_pages, H_ATT, LANE), rows(H_ATT, LANE), rows(H_ATT, LANE)],
            scratch_shapes=[pltpu.VMEM((H_ATT, nb, LANE, MOBA_BLOCK), BF16),
                            pltpu.VMEM((H_ATT, nb, LANE, MOBA_BLOCK), BF16),
                            pltpu.VMEM((2, w, LANE), BF16),
                            blk_state(BF16), blk_state(F32), blk_state(F32),
                            pltpu.VMEM((H_MLSTM, DV_MLSTM, DK_MLSTM), F32),
                            pltpu.VMEM((SUBLANE, LANE), F32),
                            pltpu.VMEM((SUBLANE, LANE), F32),
                            pltpu.VMEM((2 * guest_rows, n_pages, wa, PAGE_SIZE), F32),
                            pltpu.SemaphoreType.DMA((2 * guest_rows,))],
        ),
        out_shape=[jax.ShapeDtypeStruct((b, s, w), BF16),
                   jax.ShapeDtypeStruct((b, s, MLSTM_WIDTH), BF16),
                   jax.ShapeDtypeStruct((b, H_MLSTM, DV_MLSTM, DK_MLSTM), F32),
                   jax.ShapeDtypeStruct((b, SUBLANE, LANE), F32),
                   jax.ShapeDtypeStruct((b, SUBLANE, LANE), F32),
                   jax.ShapeDtypeStruct((bs, n_pages, H_ATT, LANE), F32),
                   jax.ShapeDtypeStruct((bs, H_ATT, LANE), F32),
                   jax.ShapeDtypeStruct((bs, H_ATT, LANE), jnp.int32)],
        compiler_params=_params("arbitrary", "arbitrary"),
    )(page_table, q, kt, vt, mq, mk, mv, mo, gates, gain, qt, knt, ck)


def _head_sublane(h):
    return (H_ATT // 2 - 1 - h) if h < H_ATT // 2 else (H_ATT + H_ATT // 2 - 1 - h)


def _head_rows(x):
    parts = []
    for h in range(H_ATT):
        tiles = [x[h * DH_ATT + SUBLANE * t:h * DH_ATT + SUBLANE * (t + 1), :] for t in range(DH_ATT // SUBLANE)]
        parts.append(sum(tiles[1:], tiles[0]))
    sub = lax.broadcasted_iota(jnp.int32, parts[0].shape, 0)
    folded = [p + pltpu.roll(p, 4, 0) for p in parts]
    quads = [jnp.where(sub < 4, folded[i], folded[i + 4]) for i in range(4)]
    take_up = (sub & 2) != 0
    pairs = [jnp.where(take_up, quads[i] + pltpu.roll(quads[i], 2, 0),
                       quads[i + 2] + pltpu.roll(quads[i + 2], 6, 0)) for i in range(2)]
    return jnp.where((sub & 1) != 0, pairs[0] + pltpu.roll(pairs[0], 1, 0), pairs[1] + pltpu.roll(pairs[1], 7, 0))


def _moba_sample_scores(b, qt_ref, knt_ref, kp_refs, pe_ref, stats_ref, idx_ref):
    n_pages = len(kp_refs)
    n_blocks = n_pages // PAGES_PER_BLOCK
    w = qt_ref.shape[0]
    on_b = lax.broadcasted_iota(jnp.int32, (w, LANE), 1) == b

    def column(ref):
        return jnp.sum(jnp.where(on_b, ref[...], 0.0), axis=1, keepdims=True)

    q_col = column(qt_ref) * (DH_ATT ** -0.5)
    q_wide = jnp.broadcast_to(q_col, (w, LANE))
    s_own = _head_rows(jnp.broadcast_to(q_col * column(knt_ref), (w, LANE)))[:, 0:1]
    s_pages = [_head_rows(kp_refs[p][...] * q_wide) for p in range(n_pages)]

    blk = [jnp.sum(sum(s_pages[n * PAGES_PER_BLOCK + 1:(n + 1) * PAGES_PER_BLOCK], s_pages[n * PAGES_PER_BLOCK]),
                   axis=1, keepdims=True) for n in range(n_blocks)]
    lane = lax.broadcasted_iota(jnp.int32, (H_ATT, LANE), 1)
    sel, ranked = [], jnp.zeros((H_ATT, LANE), jnp.int32)
    for n in range(n_blocks):
        rank = jnp.zeros((H_ATT, 1), jnp.int32)
        for o in range(n_blocks):
            if o != n:
                beats = (blk[o] >= blk[n]) if o < n else (blk[o] > blk[n])
                rank = rank + jnp.where(beats, 1, 0)
        sel.append(rank < MOBA_TOPK)
        ranked = jnp.where(rank == lane, n, ranked)
    m = s_own
    for p in range(n_pages):
        page_max = jnp.max(s_pages[p], axis=1, keepdims=True)
        m = jnp.maximum(m, jnp.where(sel[p // PAGES_PER_BLOCK], page_max, NEG))
    p_own = jnp.exp(s_own - m)
    total = jnp.zeros((H_ATT, LANE), F32)
    for p in range(n_pages):
        pe = jnp.where(sel[p // PAGES_PER_BLOCK], jnp.exp(s_pages[p] - m), 0.0)
        pe_ref[p] = pe
        total = total + pe
    row_sum = p_own + jnp.sum(total, axis=1, keepdims=True)
    stats_ref[...] = jnp.where(lane == 0, p_own, row_sum)
    idx_ref[...] = ranked


N_VALUE_CHUNKS = MOBA_TOPK * PAGES_PER_BLOCK


def _value_chunk_page(sel_ref, row, h, c):
    return sel_ref[row, h * MOBA_TOPK + c // PAGES_PER_BLOCK] * PAGES_PER_BLOCK + c % PAGES_PER_BLOCK


def _moba_sample_mix_row(row, sel_ref, pe_ref, stats_ref, vnt_ref, chunks_ref, o_ref):
    w = vnt_ref.shape[0]
    on_row = lax.broadcasted_iota(jnp.int32, (w, LANE), 1) == row
    vn_col = jnp.sum(jnp.where(on_row, vnt_ref[...], 0.0), axis=1, keepdims=True)
    stats = stats_ref[...]
    out_cols = []
    for h in range(H_ATT):
        r = _head_sublane(h)
        acc = jnp.zeros((DH_ATT, LANE), F32)
        for c in range(N_VALUE_CHUNKS):
            acc = acc + (pe_ref[_value_chunk_page(sel_ref, row, h, c), r:r + 1, :]
                         * chunks_ref[h * N_VALUE_CHUNKS + c])
        p_own, row_sum = stats[r:r + 1, 0:1], stats[r:r + 1, 1:2]
        rows = slice(h * DH_ATT, (h + 1) * DH_ATT)
        out_cols.append((jnp.sum(acc, axis=1, keepdims=True) + p_own * vn_col[rows, :]) / row_sum)
    o_ref[...] = jnp.where(on_row, jnp.concatenate(out_cols, axis=0), o_ref[...])


def _cache_pages(cache):
    return jnp.transpose(cache, (0, 2, 3, 1))


def _selected_blocks(ranked):
    b = ranked.shape[0]
    return jnp.stack([ranked[:, _head_sublane(h), :MOBA_TOPK] for h in range(H_ATT)], axis=1).reshape(b, -1)


def _mlstm_sample_kernel(q_ref, k_ref, v_ref, o_ref, g_ref, gain_ref, c0_ref, n0_ref, m0_ref,
                         mem_ref, c_ref, n_ref, m_ref):
    tb = q_ref.shape[0]
    g = g_ref[...]
    sub = lax.broadcasted_iota(jnp.int32, (2 * tb, LANE), 0)
    zrows = jnp.zeros((tb, LANE), F32)
    for h in range(H_MLSTM):
        lanes = slice(h * DK_MLSTM, (h + 1) * DK_MLSTM)
        ig, lf, m0 = g[:, h:h + 1], g[:, H_MLSTM + h:H_MLSTM + h + 1], m0_ref[:, h:h + 1]
        q, k, v = q_ref[:, lanes], k_ref[:, lanes], v_ref[:, lanes]
        n0 = n0_ref[:, lanes]
        m_t = jnp.maximum(lf + m0, ig)
        w_inter = jnp.exp(lf + m0 - m_t)
        g_in = jnp.exp(ig - m_t)
        a = g_in * jnp.sum(q * k, axis=1, keepdims=True)
        den = w_inter * jnp.sum(n0 * q, axis=1, keepdims=True) + a
        q_b = q.astype(BF16)
        gv = jnp.concatenate([g_in * v, zrows], axis=0)
        k_b = jnp.concatenate([k, zrows], axis=0).astype(BF16)
        cq_rows = []
        for r in range(tb):
            c_prev = c0_ref[r, h]
            cq_rows.append(lax.dot_general(q_b, c_prev.astype(BF16), _NT, preferred_element_type=F32)[r:r + 1, :])
            outer = lax.dot_general(jnp.where(sub == r, gv, 0.0).astype(BF16), k_b, _TN,
                                    preferred_element_type=F32)
            c_ref[r, h] = w_inter[r:r + 1, :] * c_prev + outer
        cq = jnp.concatenate(cq_rows, axis=0)
        hh = (w_inter * cq + a * v) / jnp.maximum(jnp.abs(den), jnp.exp(-m_t))
        mem_ref[:, lanes] = _mlstm_head_out(hh, gain_ref[:, lanes], o_ref[:, lanes]).astype(mem_ref.dtype)
        n_ref[:, lanes] = w_inter * n0 + g_in * k
        m_ref[:, h:h + 1] = m_t


def _mlstm_sample(mq, mk, mv, mo, gates, gain, c0, n0, m0):
    b, w = mq.shape
    tb = SUBLANE
    rows = lambda width: pl.BlockSpec((tb, width), lambda i: (i, 0))
    c_spec = pl.BlockSpec((tb, H_MLSTM, DV_MLSTM, DK_MLSTM), lambda i: (i, 0, 0, 0))
    return pl.pallas_call(
        _mlstm_sample_kernel,
        grid=(b // tb,),
        in_specs=[rows(w), rows(w), rows(w), rows(w), rows(LANE), pl.BlockSpec((1, w), lambda i: (0, 0)),
                  c_spec, rows(w), rows(H_MLSTM)],
        out_specs=[rows(w), c_spec, rows(w), rows(H_MLSTM)],
        out_shape=[jax.ShapeDtypeStruct((b, w), F32),
                   jax.ShapeDtypeStruct(c0.shape, F32),
                   jax.ShapeDtypeStruct((b, w), F32),
                   jax.ShapeDtypeStruct((b, H_MLSTM), F32)],
        compiler_params=_params("arbitrary"),
    )(mq, mk, mv, mo, gates, gain, c0, n0.reshape(b, w), m0)


N_FINISH_INPUTS = 15


def _finish_kernel(*refs, alpha, ff_chunk, att_transposed, guest_rows):
    if guest_rows:
        pt_ref, sel_ref, refs = refs[0], refs[1], refs[2:]
    (x_ref, att_ref, mem_ref, g1_ref, sh2_ref, sc2_ref, g2_ref, wo_ref, ln1g_ref, ln1b_ref,
     wg_ref, wu_ref, wd_ref, ln2g_ref, ln2b_ref) = refs[:N_FINISH_INPUTS]
    if guest_rows:
        pe_ref, stats_ref, vnt_ref, cv_ref, y_ref, o_ref, vbuf, sem = refs[N_FINISH_INPUTS:]
    else:
        (y_ref,) = refs[N_FINISH_INPUTS:]

    att = att_ref[...].T if att_transposed else att_ref[...]
    aw = att.shape[1]
    mix = (jnp.dot(att.astype(BF16), wo_ref[0:aw, :], preferred_element_type=F32)
           + jnp.dot(mem_ref[...].astype(BF16), wo_ref[aw:, :], preferred_element_type=F32))
    x1 = _layernorm(alpha * x_ref[...] + (1.0 + g1_ref[...]) * mix, ln1g_ref[...], ln1b_ref[...])
    h2 = (x1 * (1.0 + sc2_ref[...]) + sh2_ref[...]).astype(BF16)
    n_ff = wg_ref.shape[1] // ff_chunk

    def ffn_chunk(c):
        cols = slice(c * ff_chunk, (c + 1) * ff_chunk)
        gate = jnp.dot(h2, wg_ref[:, cols], preferred_element_type=F32)
        up = jnp.dot(h2, wu_ref[:, cols], preferred_element_type=F32)
        act = (gate * jax.nn.sigmoid(gate) * up).astype(BF16)
        return jnp.dot(act, wd_ref[cols, :], preferred_element_type=F32)

    ffn = [jnp.zeros(x1.shape, F32)]
    if not guest_rows:
        for c in range(n_ff):
            ffn[0] = ffn[0] + ffn_chunk(c)
    else:
        i = pl.program_id(0)
        per_row = -(-n_ff // guest_rows)

        @pl.when(i == 0)
        def _init_out():
            o_ref[...] = jnp.zeros_like(o_ref)

        def chunk_copy(row, slot, h, c):
            page = pt_ref[row, _value_chunk_page(sel_ref, row, h, c)]
            return pltpu.make_async_copy(cv_ref.at[page, h], vbuf.at[slot, h * N_VALUE_CHUNKS + c], sem.at[slot])

        def start_row(row, slot):
            for h in range(H_ATT):
                for c in range(N_VALUE_CHUNKS):
                    chunk_copy(row, slot, h, c).start()

        def wait_row(row, slot):
            for h in range(H_ATT):
                for c in range(N_VALUE_CHUNKS):
                    chunk_copy(row, slot, h, c).wait()

        def work(r, row, slot):
            _moba_sample_mix_row(row, sel_ref, pe_ref.at[r], stats_ref.at[r], vnt_ref, vbuf.at[slot], o_ref)
            for c in range(r * per_row, min((r + 1) * per_row, n_ff)):
                ffn[0] = ffn[0] + ffn_chunk(c)

        _guest_rows_row_ahead(i, pl.num_programs(0), guest_rows, start_row, wait_row, work)
    y_ref[...] = _layernorm(alpha * x1 + (1.0 + g2_ref[...]) * ffn[0], ln2g_ref[...], ln2b_ref[...])


def _finish(x, att, mem, mods, weights, tm, rows_per_mod, alpha, att_transposed=False, guest=None):
    t, d = x.shape
    assert not att_transposed or tm == t
    steps = t // tm
    w_out, ln1_g, ln1_b, w_gate, w_up, w_down, ln2_g, ln2_b = weights
    if rows_per_mod is None:
        mod_spec = pl.BlockSpec((tm, d), lambda i, *_: (i, 0))
    else:
        per = rows_per_mod // tm
        mod_spec = pl.BlockSpec((None, 1, d), lambda i, *_: (i // per, 0, 0))
    tok = lambda width: pl.BlockSpec((tm, width), lambda i, *_: (i, 0))
    ff_chunk = 256
    assert w_gate.shape[1] % ff_chunk == 0
    in_specs = [tok(d), pl.BlockSpec(att.shape, lambda *_: (0, 0)) if att_transposed else tok(att.shape[1]),
                tok(mem.shape[1]), mod_spec, mod_spec, mod_spec, mod_spec,
                _const_spec(w_out.shape), _const_spec(ln1_g.shape), _const_spec(ln1_b.shape),
                _const_spec(w_gate.shape), _const_spec(w_up.shape), _const_spec(w_down.shape),
                _const_spec(ln2_g.shape), _const_spec(ln2_b.shape)]
    args = (x, att, mem, *mods, w_out, ln1_g, ln1_b, w_gate, w_up, w_down, ln2_g, ln2_b)
    assert len(in_specs) == N_FINISH_INPUTS
    body = functools.partial(_finish_kernel, alpha=alpha, ff_chunk=ff_chunk, att_transposed=att_transposed,
                             guest_rows=0)
    y_shape = jax.ShapeDtypeStruct((t, d), F32)
    if guest is None:
        return pl.pallas_call(body, grid=(steps,), in_specs=in_specs, out_specs=tok(d), out_shape=y_shape,
                              compiler_params=_params("arbitrary"))(*args)

    pe, stats, sel, vnt, cache_v, page_table = guest
    w, bs = vnt.shape
    n_pages = page_table.shape[1]
    guest_rows = bs // steps
    assert bs % steps == 0 and bs == LANE
    rows = lambda *dims: pl.BlockSpec((guest_rows,) + dims, lambda i, *_: (i,) + (0,) * len(dims))
    whole = pl.BlockSpec((w, bs), lambda *_: (0, 0))
    return pl.pallas_call(
        functools.partial(body, guest_rows=guest_rows),
        grid_spec=pltpu.PrefetchScalarGridSpec(
            num_scalar_prefetch=2,
            grid=(steps,),
            in_specs=in_specs + [rows(n_pages, H_ATT, LANE), rows(H_ATT, LANE), whole,
                                 pl.BlockSpec(memory_space=pl.ANY)],
            out_specs=[tok(d), whole],
            scratch_shapes=[pltpu.VMEM((2, H_ATT * N_VALUE_CHUNKS, DH_ATT, PAGE_SIZE), F32),
                            pltpu.SemaphoreType.DMA((2,))],
        ),
        out_shape=[y_shape, jax.ShapeDtypeStruct((w, bs), F32)],
        compiler_params=_params("arbitrary"),
    )(page_table, sel, *args, pe, stats, vnt, _cache_pages(cache_v))


def kernel(x_prompt, x_sample, cache_k, cache_v, state_C, state_n, state_m, page_table, c_prompt, c_sample,
           w_ada, b_ada, w_in, b_if, mlstm_norm_g, w_out, ln1_g, ln1_b, w_gate, w_up, w_down, ln2_g, ln2_b):
    depth = w_in.shape[0]
    assert depth == 1, "single-layer step"
    alpha = (2.0 * depth) ** 0.25
    bp, s, d = x_prompt.shape
    bs = x_sample.shape[0]
    assert x_sample.shape[1] == 1, "single-token decode step"
    n_main = N_PROJ_GROUPS * PROJ_GROUP

    w_main = w_in[0, :, :n_main].astype(BF16)
    w_att_t = w_in[0, :, :3 * PROJ_GROUP].T.astype(BF16)
    w_gates = jnp.pad(w_in[0, :, n_main:], ((0, 0), (0, LANE - 2 * H_MLSTM))).astype(BF16)
    b_gates = jnp.pad(b_if[0], (0, LANE - 2 * H_MLSTM)).reshape(1, LANE)
    gain = mlstm_norm_g[0].reshape(1, MLSTM_WIDTH)
    row = lambda a: a[0].reshape(1, -1)
    fin_w = (w_out[0].astype(BF16), row(ln1_g), row(ln1_b), w_gate[0].astype(BF16), w_up[0].astype(BF16),
             w_down[0].astype(BF16), row(ln2_g), row(ln2_b))

    c_all = jnp.concatenate([c_prompt, c_sample], axis=0)
    mod = _adaln(c_all, w_ada[0], b_ada[0])
    sh1, sc1, g1, sh2, sc2, g2 = (mod[:, i * d:(i + 1) * d] for i in range(6))
    pm = lambda a: a[:bp].reshape(bp, 1, d)
    sm = lambda a: a[bp:]

    xs = x_sample.reshape(bs, d)
    aq_s, ak_s, av_s, mq_s, mk_s, mv_s, mo_s, gates_s = _in_proj(
        xs, sm(sc1), sm(sh1), w_main, w_att_t, w_gates, b_gates, bs, None, (F32,) * N_PROJ_GROUPS,
        transposed=(0, 1, 2))

    xp = x_prompt.reshape(bp * s, d)
    tm = 512
    aq, ak_t, av_t, mq, mk, mv, mo, gates = _in_proj(
        xp, pm(sc1), pm(sh1), w_main, w_att_t, w_gates, b_gates, tm, s,
        (F32, F32, F32, BF16, BF16, BF16, F32), transposed=(1, 2))
    seq = lambda a: a.reshape(bp, s, a.shape[-1])
    att, mem, c_p, n_p, m_p, pe, stats, ranked = _prompt_mixers(
        seq(aq), ak_t, av_t, seq(mq), seq(mk), seq(mv), seq(mo), seq(gates), gain,
        aq_s, ak_s, cache_k[0], page_table)
    y_p, att_s = _finish(
        xp, att.reshape(bp * s, -1), mem.reshape(bp * s, -1), (pm(g1), pm(sh2), pm(sc2), pm(g2)),
        fin_w, tm, s, alpha, guest=(pe, stats, _selected_blocks(ranked), av_s, cache_v[0], page_table))

    mem_s, c_s, n_s, m_s = _mlstm_sample(mq_s, mk_s, mv_s, mo_s, gates_s, gain,
                                         state_C[0], state_n[0], state_m[0])
    y_s = _finish(xs, att_s, mem_s, (sm(g1), sm(sh2), sm(sc2), sm(g2)), fin_w, bs, None, alpha,
                  att_transposed=True)

    rows_p = lambda a: jnp.transpose(a.reshape(bp, H_ATT, DH_ATT, s), (0, 3, 1, 2))[None]
    rows_s = lambda a: jnp.transpose(a.reshape(H_ATT, DH_ATT, bs), (2, 0, 1)).reshape(1, bs, 1, H_ATT, DH_ATT)
    return (y_p.reshape(bp, s, d), y_s.reshape(bs, 1, d),
            rows_p(ak_t), rows_p(av_t),
            c_p[None], n_p[None, :, :H_MLSTM, :], m_p[None, :, :H_MLSTM, 0],
            rows_s(ak_s), rows_s(av_s),
            c_s[None], n_s.reshape(1, bs, H_MLSTM, DK_MLSTM), m_s[None])
```

```python
import functools
import math

import jax
import jax.numpy as jnp
from jax import lax
from jax.experimental import pallas as pl
from jax.experimental.pallas import tpu as pltpu

F32 = jnp.float32
BF16 = jnp.bfloat16
HIGHEST = lax.Precision.HIGHEST

LANE = 128
SUBLANE = 8
VMEM_LIMIT_BYTES = 56 * 1024 * 1024

H_ATT = 8
DH_ATT = 64
ATT_WIDTH = H_ATT * DH_ATT
MOBA_BLOCK = 256
MOBA_TOPK = 3
H_MLSTM = 4
DK_MLSTM = 128
DV_MLSTM = 128
MLSTM_WIDTH = H_MLSTM * DV_MLSTM
MLSTM_CHUNK = LANE
PAGE_SIZE = 128
PAGES_PER_BLOCK = MOBA_BLOCK // PAGE_SIZE
LN_EPS = 1e-5
NEG = -1e30
LOG2E = math.log2(math.e)
N_PROJ_GROUPS = 7
PROJ_GROUP = 512
MK_GROUP = 4

_NT = (((1,), (1,)), ((), ()))
_TN = (((0,), (0,)), ((), ()))


def _params(*sem):
    return pltpu.CompilerParams(dimension_semantics=sem, vmem_limit_bytes=VMEM_LIMIT_BYTES)


def _const_spec(shape):
    return pl.BlockSpec(shape, lambda *_: (0,) * len(shape), pipeline_mode=pl.Buffered(1))


def _layernorm(x, g, b):
    mu = jnp.mean(x, axis=-1, keepdims=True)
    d = x - mu
    var = jnp.mean(d * d, axis=-1, keepdims=True)
    return d * lax.rsqrt(var + LN_EPS) * g + b


def _top_blocks(val, nidx):
    cnt = jnp.zeros(val.shape, jnp.int32)
    for r in range(1, SUBLANE):
        other = pltpu.roll(val, r, 0)
        oidx = pltpu.roll(nidx, r, 0)
        beats = (other > val) | ((other == val) & (oidx < nidx))
        cnt = cnt + jnp.where(beats, 1, 0)
    return cnt < MOBA_TOPK


def _adaln_kernel(c_ref, w_ref, b_ref, o_ref):
    c = c_ref[...]
    s = c * jax.nn.sigmoid(c)
    o_ref[...] = jnp.dot(s, w_ref[...], preferred_element_type=F32) + b_ref[...]


def _adaln(c, w_ada, b_ada):
    rows, d = c.shape
    n = w_ada.shape[1]
    tn = d
    return pl.pallas_call(
        _adaln_kernel,
        grid=(n // tn,),
        in_specs=[pl.BlockSpec((rows, d), lambda j: (0, 0)),
                  pl.BlockSpec((d, tn), lambda j: (0, j)),
                  pl.BlockSpec((1, tn), lambda j: (0, j))],
        out_specs=pl.BlockSpec((rows, tn), lambda j: (0, j)),
        out_shape=jax.ShapeDtypeStruct((rows, n), F32),
        compiler_params=_params("arbitrary"),
    )(c, w_ada, b_ada.reshape(1, n))


def _proj_weights_kernel(w_ref, main_ref, att_t_ref, gates_ref):
    w = w_ref[...]
    n_main = main_ref.shape[1]
    main_ref[...] = w[:, :n_main].astype(BF16)
    att_t_ref[...] = w[:, :att_t_ref.shape[0]].T.astype(BF16)
    gate_cols = w[:, n_main:]
    pad = jnp.zeros((w.shape[0], LANE - gate_cols.shape[1]), F32)
    gates_ref[...] = jnp.concatenate([gate_cols, pad], axis=1).astype(BF16)


def _proj_weights(w_in):
    d, cols = w_in.shape
    n_main = N_PROJ_GROUPS * PROJ_GROUP
    n_att = 3 * PROJ_GROUP
    assert cols - n_main == 2 * H_MLSTM
    tr = 256
    return pl.pallas_call(
        _proj_weights_kernel,
        grid=(d // tr,),
        in_specs=[pl.BlockSpec((tr, cols), lambda i: (i, 0))],
        out_specs=[pl.BlockSpec((tr, n_main), lambda i: (i, 0)),
                   pl.BlockSpec((n_att, tr), lambda i: (0, i)),
                   pl.BlockSpec((tr, LANE), lambda i: (i, 0))],
        out_shape=[jax.ShapeDtypeStruct((d, n_main), BF16),
                   jax.ShapeDtypeStruct((n_att, d), BF16),
                   jax.ShapeDtypeStruct((d, LANE), BF16)],
        compiler_params=_params("arbitrary"),
    )(w_in)


def _guest_rows_row_ahead(i, n_steps, guest_rows, start_row, wait_row, work):
    assert guest_rows % 2 == 0

    @pl.when(i == 0)
    def _first_row():
        start_row(0, 0)

    for r in range(guest_rows):
        row, slot = i * guest_rows + r, r % 2
        if r + 1 < guest_rows:
            start_row(row + 1, 1 - slot)
        else:
            @pl.when(i + 1 < n_steps)
            def _next_step_row():
                start_row(row + 1, 1 - slot)
        wait_row(row, slot)
        work(r, row, slot)


def _guest_rows_step_ahead(i, n_steps, guest_rows, start_row, wait_row, work):
    half = (i % 2) * guest_rows

    @pl.when(i == 0)
    def _first_step_rows():
        for r in range(guest_rows):
            start_row(r, r)

    @pl.when(i + 1 < n_steps)
    def _next_step_rows():
        for r in range(guest_rows):
            start_row((i + 1) * guest_rows + r, guest_rows - half + r)

    for r in range(guest_rows):
        row, slot = i * guest_rows + r, half + r
        wait_row(row, slot)
        work(r, row, slot)


def _key_page_ring(pt_ref, ck_ref, kbuf, sem):
    n_pages = kbuf.shape[1]

    def page_copy(row, slot, p):
        return pltpu.make_async_copy(ck_ref.at[pt_ref[row, p]], kbuf.at[slot, p], sem.at[slot])

    def start_row(row, slot):
        for p in range(n_pages):
            page_copy(row, slot, p).start()

    def wait_row(row, slot):
        for p in range(n_pages):
            page_copy(row, slot, p).wait()

    return start_row, wait_row


def _in_proj_kernel(x_ref, sc_ref, sh_ref, w_ref, wt_ref, wg_ref, bg_ref, *out_refs, transposed):
    proj_refs, g_ref = out_refs[:N_PROJ_GROUPS], out_refs[N_PROJ_GROUPS]
    h = (x_ref[...] * (1.0 + sc_ref[...]) + sh_ref[...]).astype(BF16)
    for gi, o_ref in enumerate(proj_refs):
        cols = slice(gi * PROJ_GROUP, (gi + 1) * PROJ_GROUP)
        if gi in transposed:
            y = lax.dot_general(wt_ref[cols, :], h, _NT, preferred_element_type=F32)
        else:
            y = jnp.dot(h, w_ref[:, cols], preferred_element_type=F32)
        if gi == MK_GROUP:
            y = y * (DK_MLSTM ** -0.5)
        o_ref[...] = y.astype(o_ref.dtype)
    g = jnp.dot(h, wg_ref[...], preferred_element_type=F32) + bg_ref[...]
    lane = lax.broadcasted_iota(jnp.int32, g.shape, 1)
    logsig = jnp.minimum(g, 0.0) - jnp.log1p(jnp.exp(-jnp.abs(g)))
    g_ref[...] = jnp.where(lane >= H_MLSTM, logsig, g)


def _in_proj(x, sc, sh, w_main, w_att_t, w_gate, b_gate, tm, rows_per_mod, out_dtypes, transposed=()):
    t, d = x.shape
    steps = t // tm
    assert MK_GROUP not in transposed
    if rows_per_mod is None:
        mod_spec = pl.BlockSpec((tm, d), lambda i, *_: (i, 0))
        t_shape, t_spec = (PROJ_GROUP, t), pl.BlockSpec((PROJ_GROUP, tm), lambda i, *_: (0, i))
    else:
        per = rows_per_mod // tm
        mod_spec = pl.BlockSpec((None, 1, d), lambda i, *_: (i // per, 0, 0))
        t_shape = (t // rows_per_mod, PROJ_GROUP, rows_per_mod)
        t_spec = pl.BlockSpec((None, PROJ_GROUP, tm), lambda i, *_: (i // per, 0, i % per))
    out_shape, out_specs = [], []
    for gi, dt in enumerate(out_dtypes):
        if gi in transposed:
            out_shape.append(jax.ShapeDtypeStruct(t_shape, dt))
            out_specs.append(t_spec)
        else:
            out_shape.append(jax.ShapeDtypeStruct((t, PROJ_GROUP), dt))
            out_specs.append(pl.BlockSpec((tm, PROJ_GROUP), lambda i, *_: (i, 0)))
    out_shape.append(jax.ShapeDtypeStruct((t, LANE), F32))
    out_specs.append(pl.BlockSpec((tm, LANE), lambda i, *_: (i, 0)))
    in_specs = [pl.BlockSpec((tm, d), lambda i, *_: (i, 0)), mod_spec, mod_spec,
                _const_spec(w_main.shape), _const_spec(w_att_t.shape),
                _const_spec(w_gate.shape), _const_spec(b_gate.shape)]
    return pl.pallas_call(
        functools.partial(_in_proj_kernel, transposed=tuple(transposed)),
        grid=(steps,), in_specs=in_specs, out_specs=out_specs, out_shape=out_shape,
        compiler_params=_params("arbitrary"),
    )(x, sc, sh, w_main, w_att_t, w_gate, b_gate)


def _moba_step(i, q_ref, kt_ref, vt_ref, o_ref, kaug_ref, vaug_ref, kmt_ref, lhs_sc, m_sc, acc_sc, nb, beside_own_block):
    blk = MOBA_BLOCK
    half = LANE // 2
    w = q_ref.shape[1]

    @pl.when(i == 0)
    def _prepare_batch():
        srow = lax.broadcasted_iota(jnp.int32, (LANE, blk), 0)
        in_lo = srow < half
        head_of_row = lax.broadcasted_iota(jnp.int32, (w, LANE), 0) // DH_ATT
        lane_w = lax.broadcasted_iota(jnp.int32, (w, LANE), 1)
        kmt = jnp.zeros((w, LANE), F32)
        for j in range(nb):
            ktj = kt_ref[:, j * blk:(j + 1) * blk]
            vtj = vt_ref[:, j * blk:(j + 1) * blk]
            col = jnp.mean(ktj, axis=1, keepdims=True)
            kmt = jnp.where((lane_w % SUBLANE == j) & (lane_w // SUBLANE == head_of_row), col, kmt)
            for p in range(H_ATT // 2):
                kp, vp = ktj[p * LANE:(p + 1) * LANE, :], vtj[p * LANE:(p + 1) * LANE, :]
                kaug_ref[2 * p, j] = jnp.where(in_lo, kp, jnp.where(srow == half + j, 1.0, 0.0)).astype(BF16)
                kaug_ref[2 * p + 1, j] = jnp.where(in_lo, jnp.where(srow == j, 1.0, 0.0), kp).astype(BF16)
                vaug_ref[2 * p, j] = jnp.where(in_lo, vp, 1.0).astype(BF16)
                vaug_ref[2 * p + 1, j] = jnp.where(in_lo, 1.0, vp).astype(BF16)
        km_hi = kmt.astype(BF16)
        kmt_ref[0] = km_hi
        kmt_ref[1] = (kmt - km_hi.astype(F32)).astype(BF16)

    lane = lax.broadcasted_iota(jnp.int32, (blk, LANE), 1)
    lo_lanes = lane < half

    def store_lhs(p, bias_p):
        qp = q_ref[:, p * LANE:(p + 1) * LANE] * (DH_ATT ** -0.5 * LOG2E)
        lhs_sc[2 * p] = jnp.where(lo_lanes, qp, bias_p).astype(BF16)
        lhs_sc[2 * p + 1] = jnp.where(lo_lanes, bias_p, qp).astype(BF16)

    @pl.when(i <= MOBA_TOPK)
    def _every_past_block_selected():
        block_of_lane = lane % half
        bias = jnp.where((block_of_lane < SUBLANE) & (block_of_lane > i), NEG, 0.0)
        for p in range(H_ATT // 2):
            store_lhs(p, bias)

    @pl.when(i > MOBA_TOPK)
    def _ranked_blocks():
        q32 = q_ref[...]
        q_hi = q32.astype(BF16)
        q_lo = (q32 - q_hi.astype(F32)).astype(BF16)
        sc = (jnp.dot(q_hi, kmt_ref[0], preferred_element_type=F32)
              + (jnp.dot(q_hi, kmt_ref[1], preferred_element_type=F32)
                 + jnp.dot(q_lo, kmt_ref[0], preferred_element_type=F32)))
        sc_t = sc.T
        nidx = lax.broadcasted_iota(jnp.int32, (SUBLANE, blk), 0)
        past = nidx < i
        biases = []
        for h in range(H_ATT):
            val = jnp.where(past, sc_t[h * SUBLANE:(h + 1) * SUBLANE, :], NEG)
            keep = (_top_blocks(val, nidx) & past) | (nidx == i)
            biases.append(jnp.where(keep, 0.0, NEG))
        zpad = jnp.zeros((half - SUBLANE, blk), F32)
        for p in range(H_ATT // 2):
            store_lhs(p, jnp.concatenate([biases[2 * p + 1], zpad, biases[2 * p], zpad], axis=0).T)

    def scores(h, j):
        return jnp.dot(lhs_sc[h], kaug_ref[h, j], preferred_element_type=F32)

    def row_max(s):
        return jnp.broadcast_to(jnp.max(s, axis=1, keepdims=True), (blk, LANE))

    def weights(s, m):
        return jnp.exp2(s - jnp.concatenate([m, m], axis=1)).astype(BF16)

    row = lax.broadcasted_iota(jnp.int32, (blk, blk), 0)
    col = lax.broadcasted_iota(jnp.int32, (blk, blk), 1)
    causal = col <= row
    beside_own_block()
    for h in range(H_ATT):
        s = jnp.where(causal, scores(h, i), NEG)
        m = row_max(s)
        acc_sc[h] = lax.dot_general(weights(s, m), vaug_ref[h, i], _NT, preferred_element_type=F32)
        m_sc[h] = m

    def past_blocks(js):
        for h in range(H_ATT):
            ss = [scores(h, j) for j in js]
            m_old = m_sc[h]
            m_new = m_old
            for s in ss:
                m_new = jnp.maximum(m_new, row_max(s))
            acc = jnp.exp2(m_old - m_new) * acc_sc[h]
            for s, j in zip(ss, js):
                acc = acc + lax.dot_general(weights(s, m_new), vaug_ref[h, j], _NT, preferred_element_type=F32)
            acc_sc[h] = acc
            m_sc[h] = m_new

    def two_past_blocks(t, carry):
        past_blocks((2 * t, 2 * t + 1))
        return carry

    lax.fori_loop(0, lax.shift_right_logical(i, 1), two_past_blocks, 0)

    @pl.when((i & 1) == 1)
    def _last_past_block():
        past_blocks((i - 1,))

    for p in range(H_ATT // 2):
        acc_e, acc_o = acc_sc[2 * p], acc_sc[2 * p + 1]
        num = jnp.where(lo_lanes, acc_e, acc_o)
        den = pltpu.roll(jnp.where(lo_lanes, acc_o, acc_e), half, 1)
        o_ref[:, p * LANE:(p + 1) * LANE] = (num / den).astype(o_ref.dtype)


def _mlstm_head_out(hh, gain, ogate):
    mu = jnp.mean(hh, axis=-1, keepdims=True)
    d = hh - mu
    var = jnp.mean(d * d, axis=-1, keepdims=True)
    return d * lax.rsqrt(var + LN_EPS) * gain * jax.nn.sigmoid(ogate)


def _mlstm_chunk(tok, q_ref, k_ref, v_ref, o_ref, g_ref, gain_ref, mem_ref, c_sc, n_sc, m_sc):
    L = MLSTM_CHUNK
    row = lax.broadcasted_iota(jnp.int32, (L, L), 0)
    col = lax.broadcasted_iota(jnp.int32, (L, L), 1)
    causal = col <= row
    lower = jnp.where(causal, 1.0, 0.0)
    upper = jnp.where(row <= col, 1.0, 0.0)
    ones = jnp.ones((L, DV_MLSTM), BF16)

    g = g_ref[tok, :]
    g_t = g.T
    b_col_all = jnp.dot(lower, g, precision=HIGHEST, preferred_element_type=F32)
    b_row_all = jnp.dot(g_t[0:SUBLANE, :], upper, precision=HIGHEST, preferred_element_type=F32)
    for h in range(H_MLSTM):
        lanes = slice(h * DK_MLSTM, (h + 1) * DK_MLSTM)
        ig_row = g_t[h:h + 1, :]
        b_row = b_row_all[H_MLSTM + h:H_MLSTM + h + 1, :]
        ig = jnp.broadcast_to(g[:, h:h + 1], (L, LANE))
        b = jnp.broadcast_to(b_col_all[:, H_MLSTM + h:H_MLSTM + h + 1], (L, LANE))
        m_prev = m_sc[h:h + 1, :]
        dmat = jnp.where(causal, b - b_row + ig_row, NEG)
        m_inter = b + m_prev
        m_t = jnp.maximum(m_inter, jnp.broadcast_to(jnp.max(dmat, axis=1, keepdims=True), (L, LANE)))
        w_inter = jnp.exp(m_inter - m_t)
        qh, kh, vh = q_ref[tok, lanes], k_ref[tok, lanes], v_ref[tok, lanes]
        a = jnp.exp(dmat - m_t) * lax.dot_general(qh, kh, _NT, preferred_element_type=F32)
        c_prev = c_sc[h]
        n_prev = n_sc[h:h + 1, :]
        state = jnp.concatenate([c_prev, jnp.broadcast_to(n_prev, (DV_MLSTM, DK_MLSTM))], axis=0).astype(BF16)
        num_den = (jnp.concatenate([w_inter, w_inter], axis=1)
                   * lax.dot_general(qh, state, _NT, preferred_element_type=F32)
                   + jnp.dot(a.astype(BF16), jnp.concatenate([vh, ones], axis=1), preferred_element_type=F32))
        hh = num_den[:, :DV_MLSTM] / jnp.maximum(jnp.abs(num_den[:, DV_MLSTM:]), jnp.exp(-m_t))
        mem_ref[tok, lanes] = _mlstm_head_out(
            hh, gain_ref[:, lanes], o_ref[tok, lanes].astype(F32)).astype(mem_ref.dtype)

        m_new = m_t[L - 1:L, :]
        b_last = b[L - 1:L, :]
        g_inter = jnp.exp(b_last + m_prev - m_new)
        g_in = jnp.exp(b_last - b + ig - m_new)
        v_scaled = (vh.astype(F32) * g_in).astype(BF16)
        c_sc[h] = g_inter * c_prev + lax.dot_general(v_scaled, kh, _TN, preferred_element_type=F32)
        n_sc[h:h + 1, :] = g_inter * n_prev + jnp.sum(kh.astype(F32) * g_in, axis=0, keepdims=True)
        m_sc[h:h + 1, :] = m_new


def _prompt_mixers_kernel(pt_ref, q_ref, kt_ref, vt_ref, mq_ref, mk_ref, mv_ref, mo_ref, g_ref, gain_ref,
                          qt_ref, knt_ref, ck_ref,
                          att_ref, mem_ref, c_out, n_out, m_out, pe_ref, stats_ref, idx_ref,
                          kaug_ref, vaug_ref, kmt_ref, lhs_sc, m_att, acc_sc, c_sc, n_sc, m_sc, kbuf, sem,
                          *, nb, guest_rows):
    bi, i = pl.program_id(0), pl.program_id(1)

    @pl.when(i == 0)
    def _reset_state():
        c_sc[...] = jnp.zeros_like(c_sc)
        n_sc[...] = jnp.zeros_like(n_sc)
        m_sc[...] = jnp.zeros_like(m_sc)

    ready = []
    _guest_rows_step_ahead(bi * nb + i, pl.num_programs(0) * nb, guest_rows,
                           *_key_page_ring(pt_ref, ck_ref, kbuf, sem), lambda *row: ready.append(row))

    def recurrence_and_scores():
        for r, row, slot in ready:
            _moba_sample_scores(row, qt_ref, knt_ref, [kbuf.at[slot, p] for p in range(kbuf.shape[1])],
                                pe_ref.at[r], stats_ref.at[r], idx_ref.at[r])
        for c in range(MOBA_BLOCK // MLSTM_CHUNK):
            _mlstm_chunk(slice(c * MLSTM_CHUNK, (c + 1) * MLSTM_CHUNK), mq_ref, mk_ref, mv_ref, mo_ref, g_ref,
                         gain_ref, mem_ref, c_sc, n_sc, m_sc)

    _moba_step(i, q_ref, kt_ref, vt_ref, att_ref, kaug_ref, vaug_ref, kmt_ref, lhs_sc, m_att, acc_sc, nb,
               recurrence_and_scores)

    @pl.when(i == nb - 1)
    def _emit_state():
        c_out[...] = c_sc[...]
        n_out[...] = n_sc[...]
        m_out[...] = m_sc[...]


def _prompt_mixers(q, kt, vt, mq, mk, mv, mo, gates, gain, qt, knt, cache_k, page_table):
    b, s, w = q.shape
    nb = s // MOBA_BLOCK
    wa, bs = qt.shape
    n_pages = page_table.shape[1]
    guest_rows = bs // (b * nb)
    assert s % MOBA_BLOCK == 0 and nb <= SUBLANE and w == ATT_WIDTH == wa and mq.shape[2] == MLSTM_WIDTH
    assert MOBA_BLOCK % MLSTM_CHUNK == 0 and MLSTM_CHUNK == LANE == DK_MLSTM == DV_MLSTM
    assert bs == guest_rows * b * nb == LANE and n_pages % PAGES_PER_BLOCK == 0
    assert MOBA_TOPK <= n_pages // PAGES_PER_BLOCK <= LANE
    ck = _cache_pages(cache_k).reshape(cache_k.shape[0], wa, PAGE_SIZE)
    tok = lambda width: pl.BlockSpec((None, MOBA_BLOCK, width), lambda bi, i, *_: (bi, i, 0))
    seq = pl.BlockSpec((None, w, s), lambda bi, i, *_: (bi, 0, 0))
    state = lambda *dims: pl.BlockSpec((None,) + dims, lambda bi, i, *_: (bi,) + (0,) * len(dims))
    rows = lambda *dims: pl.BlockSpec((guest_rows,) + dims, lambda bi, i, *_: (bi * nb + i,) + (0,) * len(dims))
    whole = pl.BlockSpec((wa, bs), lambda *_: (0, 0))
    blk_state = lambda dt: pltpu.VMEM((H_ATT, MOBA_BLOCK, LANE), dt)
    return pl.pallas_call(
        functools.partial(_prompt_mixers_kernel, nb=nb, guest_rows=guest_rows),
        grid_spec=pltpu.PrefetchScalarGridSpec(
            num_scalar_prefetch=1,
            grid=(b, nb),
            in_specs=[tok(w), seq, seq, tok(MLSTM_WIDTH), tok(MLSTM_WIDTH), tok(MLSTM_WIDTH), tok(MLSTM_WIDTH),
                      tok(LANE), pl.BlockSpec((1, MLSTM_WIDTH), lambda *_: (0, 0)),
                      whole, whole, pl.BlockSpec(memory_space=pl.ANY)],
            out_specs=[tok(w), tok(MLSTM_WIDTH), state(H_MLSTM, DV_MLSTM, DK_MLSTM), state(SUBLANE, LANE),
                       state(SUBLANE, LANE), rows(n_pages, H_ATT, LANE), rows(H_ATT, LANE), rows(H_ATT, LANE)],
            scratch_shapes=[pltpu.VMEM((H_ATT, nb, LANE, MOBA_BLOCK), BF16),
                            pltpu.VMEM((H_ATT, nb, LANE, MOBA_BLOCK), BF16),
                            pltpu.VMEM((2, w, LANE), BF16),
                            blk_state(BF16), blk_state(F32), blk_state(F32),
                            pltpu.VMEM((H_MLSTM, DV_MLSTM, DK_MLSTM), F32),
                            pltpu.VMEM((SUBLANE, LANE), F32),
                            pltpu.VMEM((SUBLANE, LANE), F32),
                            pltpu.VMEM((2 * guest_rows, n_pages, wa, PAGE_SIZE), F32),
                            pltpu.SemaphoreType.DMA((2 * guest_rows,))],
        ),
        out_shape=[jax.ShapeDtypeStruct((b, s, w), BF16),
                   jax.ShapeDtypeStruct((b, s, MLSTM_WIDTH), BF16),
                   jax.ShapeDtypeStruct((b, H_MLSTM, DV_MLSTM, DK_MLSTM), F32),
                   jax.ShapeDtypeStruct((b, SUBLANE, LANE), F32),
                   jax.ShapeDtypeStruct((b, SUBLANE, LANE), F32),
                   jax.ShapeDtypeStruct((bs, n_pages, H_ATT, LANE), F32),
                   jax.ShapeDtypeStruct((bs, H_ATT, LANE), F32),
                   jax.ShapeDtypeStruct((bs, H_ATT, LANE), jnp.int32)],
        compiler_params=_params("arbitrary", "arbitrary"),
    )(page_table, q, kt, vt, mq, mk, mv, mo, gates, gain, qt, knt, ck)


def _head_sublane(h):
    return (H_ATT // 2 - 1 - h) if h < H_ATT // 2 else (H_ATT + H_ATT // 2 - 1 - h)


def _head_rows(x):
    parts = []
    for h in range(H_ATT):
        tiles = [x[h * DH_ATT + SUBLANE * t:h * DH_ATT + SUBLANE * (t + 1), :] for t in range(DH_ATT // SUBLANE)]
        parts.append(sum(tiles[1:], tiles[0]))
    sub = lax.broadcasted_iota(jnp.int32, parts[0].shape, 0)
    folded = [p + pltpu.roll(p, 4, 0) for p in parts]
    quads = [jnp.where(sub < 4, folded[i], folded[i + 4]) for i in range(4)]
    take_up = (sub & 2) != 0
    pairs = [jnp.where(take_up, quads[i] + pltpu.roll(quads[i], 2, 0),
                       quads[i + 2] + pltpu.roll(quads[i + 2], 6, 0)) for i in range(2)]
    return jnp.where((sub & 1) != 0, pairs[0] + pltpu.roll(pairs[0], 1, 0), pairs[1] + pltpu.roll(pairs[1], 7, 0))


def _moba_sample_scores(b, qt_ref, knt_ref, kp_refs, pe_ref, stats_ref, idx_ref):
    n_pages = len(kp_refs)
    n_blocks = n_pages // PAGES_PER_BLOCK
    w = qt_ref.shape[0]
    on_b = lax.broadcasted_iota(jnp.int32, (w, LANE), 1) == b

    def column(ref):
        return jnp.sum(jnp.where(on_b, ref[...], 0.0), axis=1, keepdims=True)

    q_col = column(qt_ref) * (DH_ATT ** -0.5)
    q_wide = jnp.broadcast_to(q_col, (w, LANE))
    s_own = _head_rows(jnp.broadcast_to(q_col * column(knt_ref), (w, LANE)))[:, 0:1]
    s_pages = [_head_rows(kp_refs[p][...] * q_wide) for p in range(n_pages)]

    blk = [jnp.sum(sum(s_pages[n * PAGES_PER_BLOCK + 1:(n + 1) * PAGES_PER_BLOCK], s_pages[n * PAGES_PER_BLOCK]),
                   axis=1, keepdims=True) for n in range(n_blocks)]
    lane = lax.broadcasted_iota(jnp.int32, (H_ATT, LANE), 1)
    sel, ranked = [], jnp.zeros((H_ATT, LANE), jnp.int32)
    for n in range(n_blocks):
        rank = jnp.zeros((H_ATT, 1), jnp.int32)
        for o in range(n_blocks):
            if o != n:
                beats = (blk[o] >= blk[n]) if o < n else (blk[o] > blk[n])
                rank = rank + jnp.where(beats, 1, 0)
        sel.append(rank < MOBA_TOPK)
        ranked = jnp.where(rank == lane, n, ranked)
    m = s_own
    for p in range(n_pages):
        page_max = jnp.max(s_pages[p], axis=1, keepdims=True)
        m = jnp.maximum(m, jnp.where(sel[p // PAGES_PER_BLOCK], page_max, NEG))
    p_own = jnp.exp(s_own - m)
    total = jnp.zeros((H_ATT, LANE), F32)
    for p in range(n_pages):
        pe = jnp.where(sel[p // PAGES_PER_BLOCK], jnp.exp(s_pages[p] - m), 0.0)
        pe_ref[p] = pe
        total = total + pe
    row_sum = p_own + jnp.sum(total, axis=1, keepdims=True)
    stats_ref[...] = jnp.where(lane == 0, p_own, row_sum)
    idx_ref[...] = ranked


N_VALUE_CHUNKS = MOBA_TOPK * PAGES_PER_BLOCK


def _value_chunk_page(sel_ref, row, h, c):
    return sel_ref[row, h * MOBA_TOPK + c // PAGES_PER_BLOCK] * PAGES_PER_BLOCK + c % PAGES_PER_BLOCK


def _moba_sample_mix_row(row, sel_ref, pe_ref, stats_ref, vnt_ref, chunks_ref, o_ref):
    w = vnt_ref.shape[0]
    on_row = lax.broadcasted_iota(jnp.int32, (w, LANE), 1) == row
    vn_col = jnp.sum(jnp.where(on_row, vnt_ref[...], 0.0), axis=1, keepdims=True)
    stats = stats_ref[...]
    out_cols = []
    for h in range(H_ATT):
        r = _head_sublane(h)
        acc = jnp.zeros((DH_ATT, LANE), F32)
        for c in range(N_VALUE_CHUNKS):
            acc = acc + (pe_ref[_value_chunk_page(sel_ref, row, h, c), r:r + 1, :]
                         * chunks_ref[h * N_VALUE_CHUNKS + c])
        p_own, row_sum = stats[r:r + 1, 0:1], stats[r:r + 1, 1:2]
        rows = slice(h * DH_ATT, (h + 1) * DH_ATT)
        out_cols.append((jnp.sum(acc, axis=1, keepdims=True) + p_own * vn_col[rows, :]) / row_sum)
    o_ref[...] = jnp.where(on_row, jnp.concatenate(out_cols, axis=0), o_ref[...])


def _cache_pages(cache):
    return jnp.transpose(cache, (0, 2, 3, 1))


def _selected_blocks(ranked):
    b = ranked.shape[0]
    return jnp.stack([ranked[:, _head_sublane(h), :MOBA_TOPK] for h in range(H_ATT)], axis=1).reshape(b, -1)


def _mlstm_sample_kernel(q_ref, k_ref, v_ref, o_ref, g_ref, gain_ref, c0_ref, n0_ref, m0_ref,
                         mem_ref, c_ref, n_ref, m_ref):
    tb = q_ref.shape[0]
    g = g_ref[...]
    sub = lax.broadcasted_iota(jnp.int32, (2 * tb, LANE), 0)
    zrows = jnp.zeros((tb, LANE), F32)
    for h in range(H_MLSTM):
        lanes = slice(h * DK_MLSTM, (h + 1) * DK_MLSTM)
        ig, lf, m0 = g[:, h:h + 1], g[:, H_MLSTM + h:H_MLSTM + h + 1], m0_ref[:, h:h + 1]
        q, k, v = q_ref[:, lanes], k_ref[:, lanes], v_ref[:, lanes]
        n0 = n0_ref[:, lanes]
        m_t = jnp.maximum(lf + m0, ig)
        w_inter = jnp.exp(lf + m0 - m_t)
        g_in = jnp.exp(ig - m_t)
        a = g_in * jnp.sum(q * k, axis=1, keepdims=True)
        den = w_inter * jnp.sum(n0 * q, axis=1, keepdims=True) + a
        q_b = q.astype(BF16)
        gv = jnp.concatenate([g_in * v, zrows], axis=0)
        k_b = jnp.concatenate([k, zrows], axis=0).astype(BF16)
        cq_rows = []
        for r in range(tb):
            c_prev = c0_ref[r, h]
            cq_rows.append(lax.dot_general(q_b, c_prev.astype(BF16), _NT, preferred_element_type=F32)[r:r + 1, :])
            outer = lax.dot_general(jnp.where(sub == r, gv, 0.0).astype(BF16), k_b, _TN,
                                    preferred_element_type=F32)
            c_ref[r, h] = w_inter[r:r + 1, :] * c_prev + outer
        cq = jnp.concatenate(cq_rows, axis=0)
        hh = (w_inter * cq + a * v) / jnp.maximum(jnp.abs(den), jnp.exp(-m_t))
        mem_ref[:, lanes] = _mlstm_head_out(hh, gain_ref[:, lanes], o_ref[:, lanes]).astype(mem_ref.dtype)
        n_ref[:, lanes] = w_inter * n0 + g_in * k
        m_ref[:, h:h + 1] = m_t


def _mlstm_sample(mq, mk, mv, mo, gates, gain, c0, n0, m0):
    b, w = mq.shape
    tb = 2 * SUBLANE
    rows = lambda width: pl.BlockSpec((tb, width), lambda i: (i, 0))
    c_spec = pl.BlockSpec((tb, H_MLSTM, DV_MLSTM, DK_MLSTM), lambda i: (i, 0, 0, 0))
    return pl.pallas_call(
        _mlstm_sample_kernel,
        grid=(b // tb,),
        in_specs=[rows(w), rows(w), rows(w), rows(w), rows(LANE), pl.BlockSpec((1, w), lambda i: (0, 0)),
                  c_spec, rows(w), rows(H_MLSTM)],
        out_specs=[rows(w), c_spec, rows(w), rows(H_MLSTM)],
        out_shape=[jax.ShapeDtypeStruct((b, w), F32),
                   jax.ShapeDtypeStruct(c0.shape, F32),
                   jax.ShapeDtypeStruct((b, w), F32),
                   jax.ShapeDtypeStruct((b, H_MLSTM), F32)],
        compiler_params=_params("arbitrary"),
    )(mq, mk, mv, mo, gates, gain, c0, n0.reshape(b, w), m0)


N_FINISH_INPUTS = 15


def _finish_kernel(*refs, alpha, ff_chunk, att_transposed, guest_rows):
    if guest_rows:
        pt_ref, sel_ref, refs = refs[0], refs[1], refs[2:]
    (x_ref, att_ref, mem_ref, g1_ref, sh2_ref, sc2_ref, g2_ref, wo_ref, ln1g_ref, ln1b_ref,
     wg_ref, wu_ref, wd_ref, ln2g_ref, ln2b_ref) = refs[:N_FINISH_INPUTS]
    if guest_rows:
        pe_ref, stats_ref, vnt_ref, cv_ref, y_ref, o_ref, vbuf, sem = refs[N_FINISH_INPUTS:]
    else:
        (y_ref,) = refs[N_FINISH_INPUTS:]

    att = att_ref[...].T if att_transposed else att_ref[...]
    aw = att.shape[1]
    mix = (jnp.dot(att.astype(BF16), wo_ref[0:aw, :], preferred_element_type=F32)
           + jnp.dot(mem_ref[...].astype(BF16), wo_ref[aw:, :], preferred_element_type=F32))
    x1 = _layernorm(alpha * x_ref[...] + (1.0 + g1_ref[...]) * mix, ln1g_ref[...], ln1b_ref[...])
    h2 = (x1 * (1.0 + sc2_ref[...]) + sh2_ref[...]).astype(BF16)
    n_ff = wg_ref.shape[1] // ff_chunk

    def ffn_chunk(c):
        cols = slice(c * ff_chunk, (c + 1) * ff_chunk)
        gate = jnp.dot(h2, wg_ref[:, cols], preferred_element_type=F32)
        up = jnp.dot(h2, wu_ref[:, cols], preferred_element_type=F32)
        act = (gate * jax.nn.sigmoid(gate) * up).astype(BF16)
        return jnp.dot(act, wd_ref[cols, :], preferred_element_type=F32)

    ffn = [jnp.zeros(x1.shape, F32)]
    if not guest_rows:
        for c in range(n_ff):
            ffn[0] = ffn[0] + ffn_chunk(c)
    else:
        i = pl.program_id(0)
        per_row = -(-n_ff // guest_rows)

        @pl.when(i == 0)
        def _init_out():
            o_ref[...] = jnp.zeros_like(o_ref)

        def chunk_copy(row, slot, h, c):
            page = pt_ref[row, _value_chunk_page(sel_ref, row, h, c)]
            return pltpu.make_async_copy(cv_ref.at[page, h], vbuf.at[slot, h * N_VALUE_CHUNKS + c], sem.at[slot])

        def start_row(row, slot):
            for h in range(H_ATT):
                for c in range(N_VALUE_CHUNKS):
                    chunk_copy(row, slot, h, c).start()

        def wait_row(row, slot):
            for h in range(H_ATT):
                for c in range(N_VALUE_CHUNKS):
                    chunk_copy(row, slot, h, c).wait()

        def work(r, row, slot):
            _moba_sample_mix_row(row, sel_ref, pe_ref.at[r], stats_ref.at[r], vnt_ref, vbuf.at[slot], o_ref)
            for c in range(r * per_row, min((r + 1) * per_row, n_ff)):
                ffn[0] = ffn[0] + ffn_chunk(c)

        _guest_rows_row_ahead(i, pl.num_programs(0), guest_rows, start_row, wait_row, work)
    y_ref[...] = _layernorm(alpha * x1 + (1.0 + g2_ref[...]) * ffn[0], ln2g_ref[...], ln2b_ref[...])


def _finish(x, att, mem, mods, weights, tm, rows_per_mod, alpha, att_transposed=False, guest=None):
    t, d = x.shape
    assert not att_transposed or tm == t
    steps = t // tm
    w_out, ln1_g, ln1_b, w_gate, w_up, w_down, ln2_g, ln2_b = weights
    if rows_per_mod is None:
        mod_spec = pl.BlockSpec((tm, d), lambda i, *_: (i, 0))
    else:
        per = rows_per_mod // tm
        mod_spec = pl.BlockSpec((None, 1, d), lambda i, *_: (i // per, 0, 0))
    tok = lambda width: pl.BlockSpec((tm, width), lambda i, *_: (i, 0))
    ff_chunk = 256
    assert w_gate.shape[1] % ff_chunk == 0
    in_specs = [tok(d), pl.BlockSpec(att.shape, lambda *_: (0, 0)) if att_transposed else tok(att.shape[1]),
                tok(mem.shape[1]), mod_spec, mod_spec, mod_spec, mod_spec,
                _const_spec(w_out.shape), _const_spec(ln1_g.shape), _const_spec(ln1_b.shape),
                _const_spec(w_gate.shape), _const_spec(w_up.shape), _const_spec(w_down.shape),
                _const_spec(ln2_g.shape), _const_spec(ln2_b.shape)]
    args = (x, att, mem, *mods, w_out, ln1_g, ln1_b, w_gate, w_up, w_down, ln2_g, ln2_b)
    assert len(in_specs) == N_FINISH_INPUTS
    body = functools.partial(_finish_kernel, alpha=alpha, ff_chunk=ff_chunk, att_transposed=att_transposed,
                             guest_rows=0)
    y_shape = jax.ShapeDtypeStruct((t, d), F32)
    if guest is None:
        return pl.pallas_call(body, grid=(steps,), in_specs=in_specs, out_specs=tok(d), out_shape=y_shape,
                              compiler_params=_params("arbitrary"))(*args)

    pe, stats, sel, vnt, cache_v, page_table = guest
    w, bs = vnt.shape
    n_pages = page_table.shape[1]
    guest_rows = bs // steps
    assert bs % steps == 0 and bs == LANE
    rows = lambda *dims: pl.BlockSpec((guest_rows,) + dims, lambda i, *_: (i,) + (0,) * len(dims))
    whole = pl.BlockSpec((w, bs), lambda *_: (0, 0))
    return pl.pallas_call(
        functools.partial(body, guest_rows=guest_rows),
        grid_spec=pltpu.PrefetchScalarGridSpec(
            num_scalar_prefetch=2,
            grid=(steps,),
            in_specs=in_specs + [rows(n_pages, H_ATT, LANE), rows(H_ATT, LANE), whole,
                                 pl.BlockSpec(memory_space=pl.ANY)],
            out_specs=[tok(d), whole],
            scratch_shapes=[pltpu.VMEM((2, H_ATT * N_VALUE_CHUNKS, DH_ATT, PAGE_SIZE), F32),
                            pltpu.SemaphoreType.DMA((2,))],
        ),
        out_shape=[y_shape, jax.ShapeDtypeStruct((w, bs), F32)],
        compiler_params=_params("arbitrary"),
    )(page_table, sel, *args, pe, stats, vnt, _cache_pages(cache_v))


def kernel(x_prompt, x_sample, cache_k, cache_v, state_C, state_n, state_m, page_table, c_prompt, c_sample,
           w_ada, b_ada, w_in, b_if, mlstm_norm_g, w_out, ln1_g, ln1_b, w_gate, w_up, w_down, ln2_g, ln2_b):
    depth = w_in.shape[0]
    assert depth == 1, "single-layer step"
    alpha = (2.0 * depth) ** 0.25
    bp, s, d = x_prompt.shape
    bs = x_sample.shape[0]
    assert x_sample.shape[1] == 1, "single-token decode step"

    w_main, w_att_t, w_gates = _proj_weights(w_in[0])
    b_gates = jnp.pad(b_if[0], (0, LANE - 2 * H_MLSTM)).reshape(1, LANE)
    gain = mlstm_norm_g[0].reshape(1, MLSTM_WIDTH)
    row = lambda a: a[0].reshape(1, -1)
    fin_w = (w_out[0].astype(BF16), row(ln1_g), row(ln1_b), w_gate[0].astype(BF16), w_up[0].astype(BF16),
             w_down[0].astype(BF16), row(ln2_g), row(ln2_b))

    c_all = jnp.concatenate([c_prompt, c_sample], axis=0)
    mod = _adaln(c_all, w_ada[0], b_ada[0])
    sh1, sc1, g1, sh2, sc2, g2 = (mod[:, i * d:(i + 1) * d] for i in range(6))
    pm = lambda a: a[:bp].reshape(bp, 1, d)
    sm = lambda a: a[bp:]

    xs = x_sample.reshape(bs, d)
    aq_s, ak_s, av_s, mq_s, mk_s, mv_s, mo_s, gates_s = _in_proj(
        xs, sm(sc1), sm(sh1), w_main, w_att_t, w_gates, b_gates, bs, None, (F32,) * N_PROJ_GROUPS,
        transposed=(0, 1, 2))

    xp = x_prompt.reshape(bp * s, d)
    tm = 512
    aq, ak_t, av_t, mq, mk, mv, mo, gates = _in_proj(
        xp, pm(sc1), pm(sh1), w_main, w_att_t, w_gates, b_gates, tm, s,
        (F32, F32, F32, BF16, BF16, BF16, F32), transposed=(1, 2))
    seq = lambda a: a.reshape(bp, s, a.shape[-1])
    att, mem, c_p, n_p, m_p, pe, stats, ranked = _prompt_mixers(
        seq(aq), ak_t, av_t, seq(mq), seq(mk), seq(mv), seq(mo), seq(gates), gain,
        aq_s, ak_s, cache_k[0], page_table)
    y_p, att_s = _finish(
        xp, att.reshape(bp * s, -1), mem.reshape(bp * s, -1), (pm(g1), pm(sh2), pm(sc2), pm(g2)),
        fin_w, tm, s, alpha, guest=(pe, stats, _selected_blocks(ranked), av_s, cache_v[0], page_table))

    mem_s, c_s, n_s, m_s = _mlstm_sample(mq_s, mk_s, mv_s, mo_s, gates_s, gain,
                                         state_C[0], state_n[0], state_m[0])
    y_s = _finish(xs, att_s, mem_s, (sm(g1), sm(sh2), sm(sc2), sm(g2)), fin_w, bs, None, alpha,
                  att_transposed=True)

    rows_p = lambda a: jnp.transpose(a.reshape(bp, H_ATT, DH_ATT, s), (0, 3, 1, 2))[None]
    rows_s = lambda a: jnp.transpose(a.reshape(H_ATT, DH_ATT, bs), (2, 0, 1)).reshape(1, bs, 1, H_ATT, DH_ATT)
    return (y_p.reshape(bp, s, d), y_s.reshape(bs, 1, d),
            rows_p(ak_t), rows_p(av_t),
            c_p[None], n_p[None, :, :H_MLSTM, :], m_p[None, :, :H_MLSTM, 0],
            rows_s(ak_s), rows_s(av_s),
            c_s[None], n_s.reshape(1, bs, H_MLSTM, DK_MLSTM), m_s[None])
```

```python
import functools
import math

import jax
import jax.numpy as jnp
from jax import lax
from jax.experimental import pallas as pl
from jax.experimental.pallas import tpu as pltpu

F32 = jnp.float32
BF16 = jnp.bfloat16
HIGHEST = lax.Precision.HIGHEST

LANE = 128
SUBLANE = 8
VMEM_LIMIT_BYTES = 56 * 1024 * 1024

H_ATT = 8
DH_ATT = 64
ATT_WIDTH = H_ATT * DH_ATT
MOBA_BLOCK = 256
MOBA_TOPK = 3
H_MLSTM = 4
DK_MLSTM = 128
DV_MLSTM = 128
MLSTM_WIDTH = H_MLSTM * DV_MLSTM
MLSTM_CHUNK = LANE
PAGE_SIZE = 128
PAGES_PER_BLOCK = MOBA_BLOCK // PAGE_SIZE
LN_EPS = 1e-5
NEG = -1e30
LOG2E = math.log2(math.e)
N_PROJ_GROUPS = 7
PROJ_GROUP = 512
MK_GROUP = 4

_NT = (((1,), (1,)), ((), ()))
_TN = (((0,), (0,)), ((), ()))


def _params(*sem):
    return pltpu.CompilerParams(dimension_semantics=sem, vmem_limit_bytes=VMEM_LIMIT_BYTES)


def _const_spec(shape):
    return pl.BlockSpec(shape, lambda *_: (0,) * len(shape), pipeline_mode=pl.Buffered(1))


def _layernorm(x, g, b):
    mu = jnp.mean(x, axis=-1, keepdims=True)
    d = x - mu
    var = jnp.mean(d * d, axis=-1, keepdims=True)
    return d * lax.rsqrt(var + LN_EPS) * g + b


def _top_blocks(val, nidx):
    cnt = jnp.zeros(val.shape, jnp.int32)
    for r in range(1, SUBLANE):
        other = pltpu.roll(val, r, 0)
        oidx = pltpu.roll(nidx, r, 0)
        beats = (other > val) | ((other == val) & (oidx < nidx))
        cnt = cnt + jnp.where(beats, 1, 0)
    return cnt < MOBA_TOPK


def _adaln_kernel(c_ref, w_ref, b_ref, o_ref):
    c = c_ref[...]
    s = c * jax.nn.sigmoid(c)
    o_ref[...] = jnp.dot(s, w_ref[...], preferred_element_type=F32) + b_ref[...]


def _adaln(c, w_ada, b_ada):
    rows, d = c.shape
    n = w_ada.shape[1]
    tn = d
    return pl.pallas_call(
        _adaln_kernel,
        grid=(n // tn,),
        in_specs=[pl.BlockSpec((rows, d), lambda j: (0, 0)),
                  pl.BlockSpec((d, tn), lambda j: (0, j)),
                  pl.BlockSpec((1, tn), lambda j: (0, j))],
        out_specs=pl.BlockSpec((rows, tn), lambda j: (0, j)),
        out_shape=jax.ShapeDtypeStruct((rows, n), F32),
        compiler_params=_params("arbitrary"),
    )(c, w_ada, b_ada.reshape(1, n))


N_ATT_GROUPS = 3


def _proj_weights_kernel(wt_ref, gate_rows_ref, main_ref, att_t_ref, gates_ref):
    g = pl.program_id(0)
    group_t = wt_ref[...]
    main_ref[...] = group_t.T.astype(BF16)

    @pl.when(g < N_ATT_GROUPS)
    def _attention_group():
        att_t_ref[...] = group_t.astype(BF16)

    @pl.when(g == 0)
    def _gate_columns():
        rows = gate_rows_ref[...]
        pad = jnp.zeros((LANE - rows.shape[0], rows.shape[1]), F32)
        gates_ref[...] = jnp.concatenate([rows, pad], axis=0).T.astype(BF16)


def _proj_weights(w_in_t):
    cols, d = w_in_t.shape
    n_main = N_PROJ_GROUPS * PROJ_GROUP
    n_gate = cols - n_main
    assert n_gate == 2 * H_MLSTM == SUBLANE
    return pl.pallas_call(
        _proj_weights_kernel,
        grid=(N_PROJ_GROUPS,),
        in_specs=[pl.BlockSpec((PROJ_GROUP, d), lambda g: (g, 0)),
                  pl.BlockSpec((n_gate, d), lambda g: (n_main // n_gate, 0))],
        out_specs=[pl.BlockSpec((d, PROJ_GROUP), lambda g: (0, g)),
                   pl.BlockSpec((PROJ_GROUP, d), lambda g: (jnp.minimum(g, N_ATT_GROUPS - 1), 0)),
                   pl.BlockSpec((d, LANE), lambda g: (0, 0))],
        out_shape=[jax.ShapeDtypeStruct((d, n_main), BF16),
                   jax.ShapeDtypeStruct((N_ATT_GROUPS * PROJ_GROUP, d), BF16),
                   jax.ShapeDtypeStruct((d, LANE), BF16)],
        compiler_params=_params("arbitrary"),
    )(w_in_t, w_in_t)


def _guest_rows_row_ahead(i, n_steps, guest_rows, start_row, wait_row, work):
    assert guest_rows % 2 == 0

    @pl.when(i == 0)
    def _first_row():
        start_row(0, 0)

    for r in range(guest_rows):
        row, slot = i * guest_rows + r, r % 2
        if r + 1 < guest_rows:
            start_row(row + 1, 1 - slot)
        else:
            @pl.when(i + 1 < n_steps)
            def _next_step_row():
                start_row(row + 1, 1 - slot)
        wait_row(row, slot)
        work(r, row, slot)


def _guest_rows_step_ahead(i, n_steps, guest_rows, start_row, wait_row, work):
    half = (i % 2) * guest_rows

    @pl.when(i == 0)
    def _first_step_rows():
        for r in range(guest_rows):
            start_row(r, r)

    @pl.when(i + 1 < n_steps)
    def _next_step_rows():
        for r in range(guest_rows):
            start_row((i + 1) * guest_rows + r, guest_rows - half + r)

    for r in range(guest_rows):
        row, slot = i * guest_rows + r, half + r
        wait_row(row, slot)
        work(r, row, slot)


def _key_page_ring(pt_ref, ck_ref, kbuf, sem):
    n_pages = kbuf.shape[1]

    def page_copy(row, slot, p):
        return pltpu.make_async_copy(ck_ref.at[pt_ref[row, p]], kbuf.at[slot, p], sem.at[slot])

    def start_row(row, slot):
        for p in range(n_pages):
            page_copy(row, slot, p).start()

    def wait_row(row, slot):
        for p in range(n_pages):
            page_copy(row, slot, p).wait()

    return start_row, wait_row


def _in_proj_kernel(x_ref, sc_ref, sh_ref, w_ref, wt_ref, wg_ref, bg_ref, *out_refs, transposed):
    proj_refs, g_ref = out_refs[:N_PROJ_GROUPS], out_refs[N_PROJ_GROUPS]
    h = (x_ref[...] * (1.0 + sc_ref[...]) + sh_ref[...]).astype(BF16)
    for gi, o_ref in enumerate(proj_refs):
        cols = slice(gi * PROJ_GROUP, (gi + 1) * PROJ_GROUP)
        if gi in transposed:
            y = lax.dot_general(wt_ref[cols, :], h, _NT, preferred_element_type=F32)
        else:
            y = jnp.dot(h, w_ref[:, cols], preferred_element_type=F32)
        if gi == MK_GROUP:
            y = y * (DK_MLSTM ** -0.5)
        o_ref[...] = y.astype(o_ref.dtype)
    g = jnp.dot(h, wg_ref[...], preferred_element_type=F32) + bg_ref[...]
    lane = lax.broadcasted_iota(jnp.int32, g.shape, 1)
    logsig = jnp.minimum(g, 0.0) - jnp.log1p(jnp.exp(-jnp.abs(g)))
    g_ref[...] = jnp.where(lane >= H_MLSTM, logsig, g)


def _in_proj(x, sc, sh, w_main, w_att_t, w_gate, b_gate, tm, rows_per_mod, out_dtypes, transposed=()):
    t, d = x.shape
    steps = t // tm
    assert MK_GROUP not in transposed
    if rows_per_mod is None:
        mod_spec = pl.BlockSpec((tm, d), lambda i, *_: (i, 0))
        t_shape, t_spec = (PROJ_GROUP, t), pl.BlockSpec((PROJ_GROUP, tm), lambda i, *_: (0, i))
    else:
        per = rows_per_mod // tm
        mod_spec = pl.BlockSpec((None, 1, d), lambda i, *_: (i // per, 0, 0))
        t_shape = (t // rows_per_mod, PROJ_GROUP, rows_per_mod)
        t_spec = pl.BlockSpec((None, PROJ_GROUP, tm), lambda i, *_: (i // per, 0, i % per))
    out_shape, out_specs = [], []
    for gi, dt in enumerate(out_dtypes):
        if gi in transposed:
            out_shape.append(jax.ShapeDtypeStruct(t_shape, dt))
            out_specs.append(t_spec)
        else:
            out_shape.append(jax.ShapeDtypeStruct((t, PROJ_GROUP), dt))
            out_specs.append(pl.BlockSpec((tm, PROJ_GROUP), lambda i, *_: (i, 0)))
    out_shape.append(jax.ShapeDtypeStruct((t, LANE), F32))
    out_specs.append(pl.BlockSpec((tm, LANE), lambda i, *_: (i, 0)))
    in_specs = [pl.BlockSpec((tm, d), lambda i, *_: (i, 0)), mod_spec, mod_spec,
                _const_spec(w_main.shape), _const_spec(w_att_t.shape),
                _const_spec(w_gate.shape), _const_spec(b_gate.shape)]
    return pl.pallas_call(
        functools.partial(_in_proj_kernel, transposed=tuple(transposed)),
        grid=(steps,), in_specs=in_specs, out_specs=out_specs, out_shape=out_shape,
        compiler_params=_params("arbitrary"),
    )(x, sc, sh, w_main, w_att_t, w_gate, b_gate)


def _moba_step(i, q_ref, kt_ref, vt_ref, o_ref, kaug_ref, vaug_ref, kmt_ref, lhs_sc, m_sc, acc_sc, nb, beside_own_block):
    blk = MOBA_BLOCK
    half = LANE // 2
    w = q_ref.shape[1]

    @pl.when(i == 0)
    def _prepare_batch():
        srow = lax.broadcasted_iota(jnp.int32, (LANE, blk), 0)
        in_lo = srow < half
        head_of_row = lax.broadcasted_iota(jnp.int32, (w, LANE), 0) // DH_ATT
        lane_w = lax.broadcasted_iota(jnp.int32, (w, LANE), 1)
        kmt = jnp.zeros((w, LANE), F32)
        for j in range(nb):
            ktj = kt_ref[:, j * blk:(j + 1) * blk]
            vtj = vt_ref[:, j * blk:(j + 1) * blk]
            col = jnp.mean(ktj, axis=1, keepdims=True)
            kmt = jnp.where((lane_w % SUBLANE == j) & (lane_w // SUBLANE == head_of_row), col, kmt)
            for p in range(H_ATT // 2):
                kp, vp = ktj[p * LANE:(p + 1) * LANE, :], vtj[p * LANE:(p + 1) * LANE, :]
                kaug_ref[2 * p, j] = jnp.where(in_lo, kp, jnp.where(srow == half + j, 1.0, 0.0)).astype(BF16)
                kaug_ref[2 * p + 1, j] = jnp.where(in_lo, jnp.where(srow == j, 1.0, 0.0), kp).astype(BF16)
                vaug_ref[2 * p, j] = jnp.where(in_lo, vp, 1.0).astype(BF16)
                vaug_ref[2 * p + 1, j] = jnp.where(in_lo, 1.0, vp).astype(BF16)
        km_hi = kmt.astype(BF16)
        kmt_ref[0] = km_hi
        kmt_ref[1] = (kmt - km_hi.astype(F32)).astype(BF16)

    lane = lax.broadcasted_iota(jnp.int32, (blk, LANE), 1)
    lo_lanes = lane < half

    def store_lhs(p, bias_p):
        qp = q_ref[:, p * LANE:(p + 1) * LANE] * (DH_ATT ** -0.5 * LOG2E)
        lhs_sc[2 * p] = jnp.where(lo_lanes, qp, bias_p).astype(BF16)
        lhs_sc[2 * p + 1] = jnp.where(lo_lanes, bias_p, qp).astype(BF16)

    @pl.when(i <= MOBA_TOPK)
    def _every_past_block_selected():
        block_of_lane = lane % half
        bias = jnp.where((block_of_lane < SUBLANE) & (block_of_lane > i), NEG, 0.0)
        for p in range(H_ATT // 2):
            store_lhs(p, bias)

    @pl.when(i > MOBA_TOPK)
    def _ranked_blocks():
        q32 = q_ref[...]
        q_hi = q32.astype(BF16)
        q_lo = (q32 - q_hi.astype(F32)).astype(BF16)
        sc = (jnp.dot(q_hi, kmt_ref[0], preferred_element_type=F32)
              + (jnp.dot(q_hi, kmt_ref[1], preferred_element_type=F32)
                 + jnp.dot(q_lo, kmt_ref[0], preferred_element_type=F32)))
        sc_t = sc.T
        nidx = lax.broadcasted_iota(jnp.int32, (SUBLANE, blk), 0)
        past = nidx < i
        biases = []
        for h in range(H_ATT):
            val = jnp.where(past, sc_t[h * SUBLANE:(h + 1) * SUBLANE, :], NEG)
            keep = (_top_blocks(val, nidx) & past) | (nidx == i)
            biases.append(jnp.where(keep, 0.0, NEG))
        zpad = jnp.zeros((half - SUBLANE, blk), F32)
        for p in range(H_ATT // 2):
            store_lhs(p, jnp.concatenate([biases[2 * p + 1], zpad, biases[2 * p], zpad], axis=0).T)

    def scores(h, j):
        return jnp.dot(lhs_sc[h], kaug_ref[h, j], preferred_element_type=F32)

    def row_max(s):
        return jnp.broadcast_to(jnp.max(s, axis=1, keepdims=True), (blk, LANE))

    def weights(s, m):
        return jnp.exp2(s - jnp.concatenate([m, m], axis=1)).astype(BF16)

    row = lax.broadcasted_iota(jnp.int32, (blk, blk), 0)
    col = lax.broadcasted_iota(jnp.int32, (blk, blk), 1)
    causal = col <= row
    beside_own_block()
    for h in range(H_ATT):
        s = jnp.where(causal, scores(h, i), NEG)
        m = row_max(s)
        acc_sc[h] = lax.dot_general(weights(s, m), vaug_ref[h, i], _NT, preferred_element_type=F32)
        m_sc[h] = m

    def past_blocks(js):
        for h in range(H_ATT):
            ss = [scores(h, j) for j in js]
            m_old = m_sc[h]
            m_new = m_old
            for s in ss:
                m_new = jnp.maximum(m_new, row_max(s))
            acc = jnp.exp2(m_old - m_new) * acc_sc[h]
            for s, j in zip(ss, js):
                acc = acc + lax.dot_general(weights(s, m_new), vaug_ref[h, j], _NT, preferred_element_type=F32)
            acc_sc[h] = acc
            m_sc[h] = m_new

    def two_past_blocks(t, carry):
        past_blocks((2 * t, 2 * t + 1))
        return carry

    lax.fori_loop(0, lax.shift_right_logical(i, 1), two_past_blocks, 0)

    @pl.when((i & 1) == 1)
    def _last_past_block():
        past_blocks((i - 1,))

    for p in range(H_ATT // 2):
        acc_e, acc_o = acc_sc[2 * p], acc_sc[2 * p + 1]
        num = jnp.where(lo_lanes, acc_e, acc_o)
        den = pltpu.roll(jnp.where(lo_lanes, acc_o, acc_e), half, 1)
        o_ref[:, p * LANE:(p + 1) * LANE] = (num / den).astype(o_ref.dtype)


def _mlstm_head_out(hh, gain, ogate):
    mu = jnp.mean(hh, axis=-1, keepdims=True)
    d = hh - mu
    var = jnp.mean(d * d, axis=-1, keepdims=True)
    return d * lax.rsqrt(var + LN_EPS) * gain * jax.nn.sigmoid(ogate)


def _mlstm_chunk(tok, q_ref, k_ref, v_ref, o_ref, g_ref, gain_ref, mem_ref, c_sc, n_sc, m_sc):
    L = MLSTM_CHUNK
    row = lax.broadcasted_iota(jnp.int32, (L, L), 0)
    col = lax.broadcasted_iota(jnp.int32, (L, L), 1)
    causal = col <= row
    lower = jnp.where(causal, 1.0, 0.0)
    upper = jnp.where(row <= col, 1.0, 0.0)
    ones = jnp.ones((L, DV_MLSTM), BF16)

    g = g_ref[tok, :]
    g_t = g.T
    b_col_all = jnp.dot(lower, g, precision=HIGHEST, preferred_element_type=F32)
    b_row_all = jnp.dot(g_t[0:SUBLANE, :], upper, precision=HIGHEST, preferred_element_type=F32)
    for h in range(H_MLSTM):
        lanes = slice(h * DK_MLSTM, (h + 1) * DK_MLSTM)
        ig_row = g_t[h:h + 1, :]
        b_row = b_row_all[H_MLSTM + h:H_MLSTM + h + 1, :]
        ig = jnp.broadcast_to(g[:, h:h + 1], (L, LANE))
        b = jnp.broadcast_to(b_col_all[:, H_MLSTM + h:H_MLSTM + h + 1], (L, LANE))
        m_prev = m_sc[h:h + 1, :]
        dmat = jnp.where(causal, b - b_row + ig_row, NEG)
        m_inter = b + m_prev
        m_t = jnp.maximum(m_inter, jnp.broadcast_to(jnp.max(dmat, axis=1, keepdims=True), (L, LANE)))
        w_inter = jnp.exp(m_inter - m_t)
        qh, kh, vh = q_ref[tok, lanes], k_ref[tok, lanes], v_ref[tok, lanes]
        a = jnp.exp(dmat - m_t) * lax.dot_general(qh, kh, _NT, preferred_element_type=F32)
        c_prev = c_sc[h]
        n_prev = n_sc[h:h + 1, :]
        state = jnp.concatenate([c_prev, jnp.broadcast_to(n_prev, (DV_MLSTM, DK_MLSTM))], axis=0).astype(BF16)
        num_den = (jnp.concatenate([w_inter, w_inter], axis=1)
                   * lax.dot_general(qh, state, _NT, preferred_element_type=F32)
                   + jnp.dot(a.astype(BF16), jnp.concatenate([vh, ones], axis=1), preferred_element_type=F32))
        hh = num_den[:, :DV_MLSTM] / jnp.maximum(jnp.abs(num_den[:, DV_MLSTM:]), jnp.exp(-m_t))
        mem_ref[tok, lanes] = _mlstm_head_out(
            hh, gain_ref[:, lanes], o_ref[tok, lanes].astype(F32)).astype(mem_ref.dtype)

        m_new = m_t[L - 1:L, :]
        b_last = b[L - 1:L, :]
        g_inter = jnp.exp(b_last + m_prev - m_new)
        g_in = jnp.exp(b_last - b + ig - m_new)
        v_scaled = (vh.astype(F32) * g_in).astype(BF16)
        c_sc[h] = g_inter * c_prev + lax.dot_general(v_scaled, kh, _TN, preferred_element_type=F32)
        n_sc[h:h + 1, :] = g_inter * n_prev + jnp.sum(kh.astype(F32) * g_in, axis=0, keepdims=True)
        m_sc[h:h + 1, :] = m_new


def _prompt_mixers_kernel(pt_ref, q_ref, kt_ref, vt_ref, mq_ref, mk_ref, mv_ref, mo_ref, g_ref, gain_ref,
                          qt_ref, knt_ref, ck_ref,
                          att_ref, mem_ref, c_out, n_out, m_out, pe_ref, stats_ref, idx_ref,
                          kaug_ref, vaug_ref, kmt_ref, lhs_sc, m_att, acc_sc, c_sc, n_sc, m_sc, kbuf, sem,
                          *, nb, guest_rows):
    bi, i = pl.program_id(0), pl.program_id(1)

    @pl.when(i == 0)
    def _reset_state():
        c_sc[...] = jnp.zeros_like(c_sc)
        n_sc[...] = jnp.zeros_like(n_sc)
        m_sc[...] = jnp.zeros_like(m_sc)

    ready = []
    _guest_rows_step_ahead(bi * nb + i, pl.num_programs(0) * nb, guest_rows,
                           *_key_page_ring(pt_ref, ck_ref, kbuf, sem), lambda *row: ready.append(row))

    def recurrence_and_scores():
        for r, row, slot in ready:
            _moba_sample_scores(row, qt_ref, knt_ref, [kbuf.at[slot, p] for p in range(kbuf.shape[1])],
                                pe_ref.at[r], stats_ref.at[r], idx_ref.at[r])
        for c in range(MOBA_BLOCK // MLSTM_CHUNK):
            _mlstm_chunk(slice(c * MLSTM_CHUNK, (c + 1) * MLSTM_CHUNK), mq_ref, mk_ref, mv_ref, mo_ref, g_ref,
                         gain_ref, mem_ref, c_sc, n_sc, m_sc)

    _moba_step(i, q_ref, kt_ref, vt_ref, att_ref, kaug_ref, vaug_ref, kmt_ref, lhs_sc, m_att, acc_sc, nb,
               recurrence_and_scores)

    @pl.when(i == nb - 1)
    def _emit_state():
        c_out[...] = c_sc[...]
        n_out[...] = n_sc[...]
        m_out[...] = m_sc[...]


def _prompt_mixers(q, kt, vt, mq, mk, mv, mo, gates, gain, qt, knt, cache_k, page_table):
    b, s, w = q.shape
    nb = s // MOBA_BLOCK
    wa, bs = qt.shape
    n_pages = page_table.shape[1]
    guest_rows = bs // (b * nb)
    assert s % MOBA_BLOCK == 0 and nb <= SUBLANE and w == ATT_WIDTH == wa and mq.shape[2] == MLSTM_WIDTH
    assert MOBA_BLOCK % MLSTM_CHUNK == 0 and MLSTM_CHUNK == LANE == DK_MLSTM == DV_MLSTM
    assert bs == guest_rows * b * nb == LANE and n_pages % PAGES_PER_BLOCK == 0
    assert MOBA_TOPK <= n_pages // PAGES_PER_BLOCK <= LANE
    ck = _cache_pages(cache_k).reshape(cache_k.shape[0], wa, PAGE_SIZE)
    tok = lambda width: pl.BlockSpec((None, MOBA_BLOCK, width), lambda bi, i, *_: (bi, i, 0))
    seq = pl.BlockSpec((None, w, s), lambda bi, i, *_: (bi, 0, 0))
    state = lambda *dims: pl.BlockSpec((None,) + dims, lambda bi, i, *_: (bi,) + (0,) * len(dims))
    rows = lambda *dims: pl.BlockSpec((guest_rows,) + dims, lambda bi, i, *_: (bi * nb + i,) + (0,) * len(dims))
    whole = pl.BlockSpec((wa, bs), lambda *_: (0, 0))
    blk_state = lambda dt: pltpu.VMEM((H_ATT, MOBA_BLOCK, LANE), dt)
    return pl.pallas_call(
        functools.partial(_prompt_mixers_kernel, nb=nb, guest_rows=guest_rows),
        grid_spec=pltpu.PrefetchScalarGridSpec(
            num_scalar_prefetch=1,
            grid=(b, nb),
            in_specs=[tok(w), seq, seq, tok(MLSTM_WIDTH), tok(MLSTM_WIDTH), tok(MLSTM_WIDTH), tok(MLSTM_WIDTH),
                      tok(LANE), pl.BlockSpec((1, MLSTM_WIDTH), lambda *_: (0, 0)),
                      whole, whole, pl.BlockSpec(memory_space=pl.ANY)],
            out_specs=[tok(w), tok(MLSTM_WIDTH), state(H_MLSTM, DV_MLSTM, DK_MLSTM), state(SUBLANE, LANE),
                       state(SUBLANE, LANE), rows(n_pages, H_ATT, LANE), rows(H_ATT, LANE), rows(H_ATT, LANE)],
            scratch_shapes=[pltpu.VMEM((H_ATT, nb, LANE, MOBA_BLOCK), BF16),
                            pltpu.VMEM((H_ATT, nb, LANE, MOBA_BLOCK), BF16),
                            pltpu.VMEM((2, w, LANE), BF16),
                            blk_state(BF16), blk_state(F32), blk_state(F32),
                            pltpu.VMEM((H_MLSTM, DV_MLSTM, DK_MLSTM), F32),
                            pltpu.VMEM((SUBLANE, LANE), F32),
                            pltpu.VMEM((SUBLANE, LANE), F32),
                            pltpu.VMEM((2 * guest_rows, n_pages, wa, PAGE_SIZE), F32),
                            pltpu.SemaphoreType.DMA((2 * guest_rows,))],
        ),
        out_shape=[jax.ShapeDtypeStruct((b, s, w), BF16),
                   jax.ShapeDtypeStruct((b, s, MLSTM_WIDTH), BF16),
                   jax.ShapeDtypeStruct((b, H_MLSTM, DV_MLSTM, DK_MLSTM), F32),
                   jax.ShapeDtypeStruct((b, SUBLANE, LANE), F32),
                   jax.ShapeDtypeStruct((b, SUBLANE, LANE), F32),
                   jax.ShapeDtypeStruct((bs, n_pages, H_ATT, LANE), F32),
                   jax.ShapeDtypeStruct((bs, H_ATT, LANE), F32),
                   jax.ShapeDtypeStruct((bs, H_ATT, LANE), jnp.int32)],
        compiler_params=_params("arbitrary", "arbitrary"),
    )(page_table, q, kt, vt, mq, mk, mv, mo, gates, gain, qt, knt, ck)


def _head_sublane(h):
    return (H_ATT // 2 - 1 - h) if h < H_ATT // 2 else (H_ATT + H_ATT // 2 - 1 - h)


def _head_rows(x):
    parts = []
    for h in range(H_ATT):
        tiles = [x[h * DH_ATT + SUBLANE * t:h * DH_ATT + SUBLANE * (t + 1), :] for t in range(DH_ATT // SUBLANE)]
        parts.append(sum(tiles[1:], tiles[0]))
    sub = lax.broadcasted_iota(jnp.int32, parts[0].shape, 0)
    folded = [p + pltpu.roll(p, 4, 0) for p in parts]
    quads = [jnp.where(sub < 4, folded[i], folded[i + 4]) for i in range(4)]
    take_up = (sub & 2) != 0
    pairs = [jnp.where(take_up, quads[i] + pltpu.roll(quads[i], 2, 0),
                       quads[i + 2] + pltpu.roll(quads[i + 2], 6, 0)) for i in range(2)]
    return jnp.where((sub & 1) != 0, pairs[0] + pltpu.roll(pairs[0], 1, 0), pairs[1] + pltpu.roll(pairs[1], 7, 0))


def _moba_sample_scores(b, qt_ref, knt_ref, kp_refs, pe_ref, stats_ref, idx_ref):
    n_pages = len(kp_refs)
    n_blocks = n_pages // PAGES_PER_BLOCK
    w = qt_ref.shape[0]
    on_b = lax.broadcasted_iota(jnp.int32, (w, LANE), 1) == b

    def column(ref):
        return jnp.sum(jnp.where(on_b, ref[...], 0.0), axis=1, keepdims=True)

    q_col = column(qt_ref) * (DH_ATT ** -0.5)
    q_wide = jnp.broadcast_to(q_col, (w, LANE))
    s_own = _head_rows(jnp.broadcast_to(q_col * column(knt_ref), (w, LANE)))[:, 0:1]
    s_pages = [_head_rows(kp_refs[p][...] * q_wide) for p in range(n_pages)]

    blk = [jnp.sum(sum(s_pages[n * PAGES_PER_BLOCK + 1:(n + 1) * PAGES_PER_BLOCK], s_pages[n * PAGES_PER_BLOCK]),
                   axis=1, keepdims=True) for n in range(n_blocks)]
    lane = lax.broadcasted_iota(jnp.int32, (H_ATT, LANE), 1)
    sel, ranked = [], jnp.zeros((H_ATT, LANE), jnp.int32)
    for n in range(n_blocks):
        rank = jnp.zeros((H_ATT, 1), jnp.int32)
        for o in range(n_blocks):
            if o != n:
                beats = (blk[o] >= blk[n]) if o < n else (blk[o] > blk[n])
                rank = rank + jnp.where(beats, 1, 0)
        sel.append(rank < MOBA_TOPK)
        ranked = jnp.where(rank == lane, n, ranked)
    m = s_own
    for p in range(n_pages):
        page_max = jnp.max(s_pages[p], axis=1, keepdims=True)
        m = jnp.maximum(m, jnp.where(sel[p // PAGES_PER_BLOCK], page_max, NEG))
    p_own = jnp.exp(s_own - m)
    total = jnp.zeros((H_ATT, LANE), F32)
    for p in range(n_pages):
        pe = jnp.where(sel[p // PAGES_PER_BLOCK], jnp.exp(s_pages[p] - m), 0.0)
        pe_ref[p] = pe
        total = total + pe
    row_sum = p_own + jnp.sum(total, axis=1, keepdims=True)
    stats_ref[...] = jnp.where(lane == 0, p_own, row_sum)
    idx_ref[...] = ranked


N_VALUE_CHUNKS = MOBA_TOPK * PAGES_PER_BLOCK


def _value_chunk_page(sel_ref, row, h, c):
    return sel_ref[row, h * MOBA_TOPK + c // PAGES_PER_BLOCK] * PAGES_PER_BLOCK + c % PAGES_PER_BLOCK


def _moba_sample_mix_row(row, sel_ref, pe_ref, stats_ref, vnt_ref, chunks_ref, o_ref):
    w = vnt_ref.shape[0]
    on_row = lax.broadcasted_iota(jnp.int32, (w, LANE), 1) == row
    vn_col = jnp.sum(jnp.where(on_row, vnt_ref[...], 0.0), axis=1, keepdims=True)
    stats = stats_ref[...]
    out_cols = []
    for h in range(H_ATT):
        r = _head_sublane(h)
        acc = jnp.zeros((DH_ATT, LANE), F32)
        for c in range(N_VALUE_CHUNKS):
            acc = acc + (pe_ref[_value_chunk_page(sel_ref, row, h, c), r:r + 1, :]
                         * chunks_ref[h * N_VALUE_CHUNKS + c])
        p_own, row_sum = stats[r:r + 1, 0:1], stats[r:r + 1, 1:2]
        rows = slice(h * DH_ATT, (h + 1) * DH_ATT)
        out_cols.append((jnp.sum(acc, axis=1, keepdims=True) + p_own * vn_col[rows, :]) / row_sum)
    o_ref[...] = jnp.where(on_row, jnp.concatenate(out_cols, axis=0), o_ref[...])


def _cache_pages(cache):
    return jnp.transpose(cache, (0, 2, 3, 1))


def _selected_blocks(ranked):
    b = ranked.shape[0]
    return jnp.stack([ranked[:, _head_sublane(h), :MOBA_TOPK] for h in range(H_ATT)], axis=1).reshape(b, -1)


def _mlstm_sample_kernel(q_ref, k_ref, v_ref, o_ref, g_ref, gain_ref, c0_ref, n0_ref, m0_ref,
                         mem_ref, c_ref, n_ref, m_ref):
    tb = q_ref.shape[0]
    g = g_ref[...]
    sub = lax.broadcasted_iota(jnp.int32, (2 * tb, LANE), 0)
    zrows = jnp.zeros((tb, LANE), F32)
    for h in range(H_MLSTM):
        lanes = slice(h * DK_MLSTM, (h + 1) * DK_MLSTM)
        ig, lf, m0 = g[:, h:h + 1], g[:, H_MLSTM + h:H_MLSTM + h + 1], m0_ref[:, h:h + 1]
        q, k, v = q_ref[:, lanes], k_ref[:, lanes], v_ref[:, lanes]
        n0 = n0_ref[:, lanes]
        m_t = jnp.maximum(lf + m0, ig)
        w_inter = jnp.exp(lf + m0 - m_t)
        g_in = jnp.exp(ig - m_t)
        a = g_in * jnp.sum(q * k, axis=1, keepdims=True)
        den = w_inter * jnp.sum(n0 * q, axis=1, keepdims=True) + a
        q_b = q.astype(BF16)
        gv = jnp.concatenate([g_in * v, zrows], axis=0)
        k_b = jnp.concatenate([k, zrows], axis=0).astype(BF16)
        cq_rows = []
        for r in range(tb):
            c_prev = c0_ref[r, h]
            cq_rows.append(lax.dot_general(q_b, c_prev.astype(BF16), _NT, preferred_element_type=F32)[r:r + 1, :])
            outer = lax.dot_general(jnp.where(sub == r, gv, 0.0).astype(BF16), k_b, _TN,
                                    preferred_element_type=F32)
            c_ref[r, h] = w_inter[r:r + 1, :] * c_prev + outer
        cq = jnp.concatenate(cq_rows, axis=0)
        hh = (w_inter * cq + a * v) / jnp.maximum(jnp.abs(den), jnp.exp(-m_t))
        mem_ref[:, lanes] = _mlstm_head_out(hh, gain_ref[:, lanes], o_ref[:, lanes]).astype(mem_ref.dtype)
        n_ref[:, lanes] = w_inter * n0 + g_in * k
        m_ref[:, h:h + 1] = m_t


def _mlstm_sample(mq, mk, mv, mo, gates, gain, c0, n0, m0):
    b, w = mq.shape
    tb = 2 * SUBLANE
    rows = lambda width: pl.BlockSpec((tb, width), lambda i: (i, 0))
    c_spec = pl.BlockSpec((tb, H_MLSTM, DV_MLSTM, DK_MLSTM), lambda i: (i, 0, 0, 0))
    return pl.pallas_call(
        _mlstm_sample_kernel,
        grid=(b // tb,),
        in_specs=[rows(w), rows(w), rows(w), rows(w), rows(LANE), pl.BlockSpec((1, w), lambda i: (0, 0)),
                  c_spec, rows(w), rows(H_MLSTM)],
        out_specs=[rows(w), c_spec, rows(w), rows(H_MLSTM)],
        out_shape=[jax.ShapeDtypeStruct((b, w), F32),
                   jax.ShapeDtypeStruct(c0.shape, F32),
                   jax.ShapeDtypeStruct((b, w), F32),
                   jax.ShapeDtypeStruct((b, H_MLSTM), F32)],
        compiler_params=_params("arbitrary"),
    )(mq, mk, mv, mo, gates, gain, c0, n0.reshape(b, w), m0)


N_FINISH_INPUTS = 15


def _finish_kernel(*refs, alpha, ff_chunk, att_transposed, guest_rows):
    if guest_rows:
        pt_ref, sel_ref, refs = refs[0], refs[1], refs[2:]
    (x_ref, att_ref, mem_ref, g1_ref, sh2_ref, sc2_ref, g2_ref, wo_ref, ln1g_ref, ln1b_ref,
     wg_ref, wu_ref, wd_ref, ln2g_ref, ln2b_ref) = refs[:N_FINISH_INPUTS]
    if guest_rows:
        pe_ref, stats_ref, vnt_ref, cv_ref, y_ref, o_ref, vbuf, sem = refs[N_FINISH_INPUTS:]
    else:
        (y_ref,) = refs[N_FINISH_INPUTS:]

    att = att_ref[...].T if att_transposed else att_ref[...]
    aw = att.shape[1]
    mix = (jnp.dot(att.astype(BF16), wo_ref[0:aw, :], preferred_element_type=F32)
           + jnp.dot(mem_ref[...].astype(BF16), wo_ref[aw:, :], preferred_element_type=F32))
    x1 = _layernorm(alpha * x_ref[...] + (1.0 + g1_ref[...]) * mix, ln1g_ref[...], ln1b_ref[...])
    h2 = (x1 * (1.0 + sc2_ref[...]) + sh2_ref[...]).astype(BF16)
    n_ff = wg_ref.shape[1] // ff_chunk

    def ffn_chunk(c):
        cols = slice(c * ff_chunk, (c + 1) * ff_chunk)
        gate = jnp.dot(h2, wg_ref[:, cols], preferred_element_type=F32)
        up = jnp.dot(h2, wu_ref[:, cols], preferred_element_type=F32)
        act = (gate * jax.nn.sigmoid(gate) * up).astype(BF16)
        return jnp.dot(act, wd_ref[cols, :], preferred_element_type=F32)

    ffn = [jnp.zeros(x1.shape, F32)]
    if not guest_rows:
        for c in range(n_ff):
            ffn[0] = ffn[0] + ffn_chunk(c)
    else:
        i = pl.program_id(0)
        per_row = -(-n_ff // guest_rows)

        @pl.when(i == 0)
        def _init_out():
            o_ref[...] = jnp.zeros_like(o_ref)

        def chunk_copy(row, slot, h, c):
            page = pt_ref[row, _value_chunk_page(sel_ref, row, h, c)]
            return pltpu.make_async_copy(cv_ref.at[page, h], vbuf.at[slot, h * N_VALUE_CHUNKS + c], sem.at[slot])

        def start_row(row, slot):
            for h in range(H_ATT):
                for c in range(N_VALUE_CHUNKS):
                    chunk_copy(row, slot, h, c).start()

        def wait_row(row, slot):
            for h in range(H_ATT):
                for c in range(N_VALUE_CHUNKS):
                    chunk_copy(row, slot, h, c).wait()

        def work(r, row, slot):
            _moba_sample_mix_row(row, sel_ref, pe_ref.at[r], stats_ref.at[r], vnt_ref, vbuf.at[slot], o_ref)
            for c in range(r * per_row, min((r + 1) * per_row, n_ff)):
                ffn[0] = ffn[0] + ffn_chunk(c)

        _guest_rows_row_ahead(i, pl.num_programs(0), guest_rows, start_row, wait_row, work)
    y_ref[...] = _layernorm(alpha * x1 + (1.0 + g2_ref[...]) * ffn[0], ln2g_ref[...], ln2b_ref[...])


def _finish(x, att, mem, mods, weights, tm, rows_per_mod, alpha, att_transposed=False, guest=None):
    t, d = x.shape
    assert not att_transposed or tm == t
    steps = t // tm
    w_out, ln1_g, ln1_b, w_gate, w_up, w_down, ln2_g, ln2_b = weights
    if rows_per_mod is None:
        mod_spec = pl.BlockSpec((tm, d), lambda i, *_: (i, 0))
    else:
        per = rows_per_mod // tm
        mod_spec = pl.BlockSpec((None, 1, d), lambda i, *_: (i // per, 0, 0))
    tok = lambda width: pl.BlockSpec((tm, width), lambda i, *_: (i, 0))
    ff_chunk = 256
    assert w_gate.shape[1] % ff_chunk == 0
    in_specs = [tok(d), pl.BlockSpec(att.shape, lambda *_: (0, 0)) if att_transposed else tok(att.shape[1]),
                tok(mem.shape[1]), mod_spec, mod_spec, mod_spec, mod_spec,
                _const_spec(w_out.shape), _const_spec(ln1_g.shape), _const_spec(ln1_b.shape),
                _const_spec(w_gate.shape), _const_spec(w_up.shape), _const_spec(w_down.shape),
                _const_spec(ln2_g.shape), _const_spec(ln2_b.shape)]
    args = (x, att, mem, *mods, w_out, ln1_g, ln1_b, w_gate, w_up, w_down, ln2_g, ln2_b)
    assert len(in_specs) == N_FINISH_INPUTS
    body = functools.partial(_finish_kernel, alpha=alpha, ff_chunk=ff_chunk, att_transposed=att_transposed,
                             guest_rows=0)
    y_shape = jax.ShapeDtypeStruct((t, d), F32)
    if guest is None:
        return pl.pallas_call(body, grid=(steps,), in_specs=in_specs, out_specs=tok(d), out_shape=y_shape,
                              compiler_params=_params("arbitrary"))(*args)

    pe, stats, sel, vnt, cache_v, page_table = guest
    w, bs = vnt.shape
    n_pages = page_table.shape[1]
    guest_rows = bs // steps
    assert bs % steps == 0 and bs == LANE
    rows = lambda *dims: pl.BlockSpec((guest_rows,) + dims, lambda i, *_: (i,) + (0,) * len(dims))
    whole = pl.BlockSpec((w, bs), lambda *_: (0, 0))
    return pl.pallas_call(
        functools.partial(body, guest_rows=guest_rows),
        grid_spec=pltpu.PrefetchScalarGridSpec(
            num_scalar_prefetch=2,
            grid=(steps,),
            in_specs=in_specs + [rows(n_pages, H_ATT, LANE), rows(H_ATT, LANE), whole,
                                 pl.BlockSpec(memory_space=pl.ANY)],
            out_specs=[tok(d), whole],
            scratch_shapes=[pltpu.VMEM((2, H_ATT * N_VALUE_CHUNKS, DH_ATT, PAGE_SIZE), F32),
                            pltpu.SemaphoreType.DMA((2,))],
        ),
        out_shape=[y_shape, jax.ShapeDtypeStruct((w, bs), F32)],
        compiler_params=_params("arbitrary"),
    )(page_table, sel, *args, pe, stats, vnt, _cache_pages(cache_v))


def kernel(x_prompt, x_sample, cache_k, cache_v, state_C, state_n, state_m, page_table, c_prompt, c_sample,
           w_ada, b_ada, w_in, b_if, mlstm_norm_g, w_out, ln1_g, ln1_b, w_gate, w_up, w_down, ln2_g, ln2_b):
    depth = w_in.shape[0]
    assert depth == 1, "single-layer step"
    alpha = (2.0 * depth) ** 0.25
    bp, s, d = x_prompt.shape
    bs = x_sample.shape[0]
    assert x_sample.shape[1] == 1, "single-token decode step"

    w_main, w_att_t, w_gates = _proj_weights(w_in[0].T)
    b_gates = jnp.pad(b_if[0], (0, LANE - 2 * H_MLSTM)).reshape(1, LANE)
    gain = mlstm_norm_g[0].reshape(1, MLSTM_WIDTH)
    row = lambda a: a[0].reshape(1, -1)
    fin_w = (w_out[0].astype(BF16), row(ln1_g), row(ln1_b), w_gate[0].astype(BF16), w_up[0].astype(BF16),
             w_down[0].astype(BF16), row(ln2_g), row(ln2_b))

    c_all = jnp.concatenate([c_prompt, c_sample], axis=0)
    mod = _adaln(c_all, w_ada[0], b_ada[0])
    sh1, sc1, g1, sh2, sc2, g2 = (mod[:, i * d:(i + 1) * d] for i in range(6))
    pm = lambda a: a[:bp].reshape(bp, 1, d)
    sm = lambda a: a[bp:]

    xs = x_sample.reshape(bs, d)
    aq_s, ak_s, av_s, mq_s, mk_s, mv_s, mo_s, gates_s = _in_proj(
        xs, sm(sc1), sm(sh1), w_main, w_att_t, w_gates, b_gates, bs, None, (F32,) * N_PROJ_GROUPS,
        transposed=(0, 1, 2))

    xp = x_prompt.reshape(bp * s, d)
    tm = 512
    aq, ak_t, av_t, mq, mk, mv, mo, gates = _in_proj(
        xp, pm(sc1), pm(sh1), w_main, w_att_t, w_gates, b_gates, tm, s,
        (F32, F32, F32, BF16, BF16, BF16, F32), transposed=(1, 2))
    seq = lambda a: a.reshape(bp, s, a.shape[-1])
    att, mem, c_p, n_p, m_p, pe, stats, ranked = _prompt_mixers(
        seq(aq), ak_t, av_t, seq(mq), seq(mk), seq(mv), seq(mo), seq(gates), gain,
        aq_s, ak_s, cache_k[0], page_table)
    y_p, att_s = _finish(
        xp, att.reshape(bp * s, -1), mem.reshape(bp * s, -1), (pm(g1), pm(sh2), pm(sc2), pm(g2)),
        fin_w, tm, s, alpha, guest=(pe, stats, _selected_blocks(ranked), av_s, cache_v[0], page_table))

    mem_s, c_s, n_s, m_s = _mlstm_sample(mq_s, mk_s, mv_s, mo_s, gates_s, gain,
                                         state_C[0], state_n[0], state_m[0])
    y_s = _finish(xs, att_s, mem_s, (sm(g1), sm(sh2), sm(sc2), sm(g2)), fin_w, bs, None, alpha,
                  att_transposed=True)

    rows_p = lambda a: jnp.transpose(a.reshape(bp, H_ATT, DH_ATT, s), (0, 3, 1, 2))[None]
    rows_s = lambda a: jnp.transpose(a.reshape(H_ATT, DH_ATT, bs), (2, 0, 1)).reshape(1, bs, 1, H_ATT, DH_ATT)
    return (y_p.reshape(bp, s, d), y_s.reshape(bs, 1, d),
            rows_p(ak_t), rows_p(av_t),
            c_p[None], n_p[None, :, :H_MLSTM, :], m_p[None, :, :H_MLSTM, 0],
            rows_s(ak_s), rows_s(av_s),
            c_s[None], n_s.reshape(1, bs, H_MLSTM, DK_MLSTM), m_s[None])
```

```python
import functools
import math

import jax
import jax.numpy as jnp
from jax import lax
from jax.experimental import pallas as pl
from jax.experimental.pallas import tpu as pltpu

F32 = jnp.float32
BF16 = jnp.bfloat16
HIGHEST = lax.Precision.HIGHEST

LANE = 128
SUBLANE = 8
VMEM_LIMIT_BYTES = 56 * 1024 * 1024

H_ATT = 8
DH_ATT = 64
ATT_WIDTH = H_ATT * DH_ATT
MOBA_BLOCK = 256
MOBA_TOPK = 3
H_MLSTM = 4
DK_MLSTM = 128
DV_MLSTM = 128
MLSTM_WIDTH = H_MLSTM * DV_MLSTM
MLSTM_CHUNK = LANE
PAGE_SIZE = 128
PAGES_PER_BLOCK = MOBA_BLOCK // PAGE_SIZE
LN_EPS = 1e-5
NEG = -1e30
LOG2E = math.log2(math.e)
N_PROJ_GROUPS = 7
PROJ_GROUP = 512
MK_GROUP = 4

_NT = (((1,), (1,)), ((), ()))
_TN = (((0,), (0,)), ((), ()))


def _params(*sem):
    return pltpu.CompilerParams(dimension_semantics=sem, vmem_limit_bytes=VMEM_LIMIT_BYTES)


def _const_spec(shape):
    return pl.BlockSpec(shape, lambda *_: (0,) * len(shape), pipeline_mode=pl.Buffered(1))


def _layernorm(x, g, b):
    mu = jnp.mean(x, axis=-1, keepdims=True)
    d = x - mu
    var = jnp.mean(d * d, axis=-1, keepdims=True)
    return d * lax.rsqrt(var + LN_EPS) * g + b


def _top_blocks(val, nidx):
    cnt = jnp.zeros(val.shape, jnp.int32)
    for r in range(1, SUBLANE):
        other = pltpu.roll(val, r, 0)
        oidx = pltpu.roll(nidx, r, 0)
        beats = (other > val) | ((other == val) & (oidx < nidx))
        cnt = cnt + jnp.where(beats, 1, 0)
    return cnt < MOBA_TOPK


def _adaln_kernel(c_ref, w_ref, b_ref, o_ref):
    c = c_ref[...]
    s = c * jax.nn.sigmoid(c)
    o_ref[...] = jnp.dot(s, w_ref[...], preferred_element_type=F32) + b_ref[...]


def _adaln(c, w_ada, b_ada):
    rows, d = c.shape
    n = w_ada.shape[1]
    tn = 2 * d
    return pl.pallas_call(
        _adaln_kernel,
        grid=(n // tn,),
        in_specs=[pl.BlockSpec((rows, d), lambda j: (0, 0)),
                  pl.BlockSpec((d, tn), lambda j: (0, j)),
                  pl.BlockSpec((1, tn), lambda j: (0, j))],
        out_specs=pl.BlockSpec((rows, tn), lambda j: (0, j)),
        out_shape=jax.ShapeDtypeStruct((rows, n), F32),
        compiler_params=_params("arbitrary"),
    )(c, w_ada, b_ada.reshape(1, n))


N_ATT_GROUPS = 3


def _proj_weights_kernel(wt_ref, gate_rows_ref, main_ref, att_t_ref, gates_ref):
    g = pl.program_id(0)
    group_t = wt_ref[...]
    main_ref[...] = group_t.T.astype(BF16)

    @pl.when(g < N_ATT_GROUPS)
    def _attention_group():
        att_t_ref[...] = group_t.astype(BF16)

    @pl.when(g == 0)
    def _gate_columns():
        rows = gate_rows_ref[...]
        pad = jnp.zeros((LANE - rows.shape[0], rows.shape[1]), F32)
        gates_ref[...] = jnp.concatenate([rows, pad], axis=0).T.astype(BF16)


def _proj_weights(w_in_t):
    cols, d = w_in_t.shape
    n_main = N_PROJ_GROUPS * PROJ_GROUP
    n_gate = cols - n_main
    assert n_gate == 2 * H_MLSTM == SUBLANE
    return pl.pallas_call(
        _proj_weights_kernel,
        grid=(N_PROJ_GROUPS,),
        in_specs=[pl.BlockSpec((PROJ_GROUP, d), lambda g: (g, 0)),
                  pl.BlockSpec((n_gate, d), lambda g: (n_main // n_gate, 0))],
        out_specs=[pl.BlockSpec((d, PROJ_GROUP), lambda g: (0, g)),
                   pl.BlockSpec((PROJ_GROUP, d), lambda g: (jnp.minimum(g, N_ATT_GROUPS - 1), 0)),
                   pl.BlockSpec((d, LANE), lambda g: (0, 0))],
        out_shape=[jax.ShapeDtypeStruct((d, n_main), BF16),
                   jax.ShapeDtypeStruct((N_ATT_GROUPS * PROJ_GROUP, d), BF16),
                   jax.ShapeDtypeStruct((d, LANE), BF16)],
        compiler_params=_params("arbitrary"),
    )(w_in_t, w_in_t)


def _guest_rows_row_ahead(i, n_steps, guest_rows, start_row, wait_row, work):
    assert guest_rows % 2 == 0

    @pl.when(i == 0)
    def _first_row():
        start_row(0, 0)

    for r in range(guest_rows):
        row, slot = i * guest_rows + r, r % 2
        if r + 1 < guest_rows:
            start_row(row + 1, 1 - slot)
        else:
            @pl.when(i + 1 < n_steps)
            def _next_step_row():
                start_row(row + 1, 1 - slot)
        wait_row(row, slot)
        work(r, row, slot)


def _guest_rows_step_ahead(i, n_steps, guest_rows, start_row, wait_row, work):
    half = (i % 2) * guest_rows

    @pl.when(i == 0)
    def _first_step_rows():
        for r in range(guest_rows):
            start_row(r, r)

    @pl.when(i + 1 < n_steps)
    def _next_step_rows():
        for r in range(guest_rows):
            start_row((i + 1) * guest_rows + r, guest_rows - half + r)

    for r in range(guest_rows):
        row, slot = i * guest_rows + r, half + r
        wait_row(row, slot)
        work(r, row, slot)


def _key_page_ring(pt_ref, ck_ref, kbuf, sem):
    n_pages = kbuf.shape[1]

    def page_copy(row, slot, p):
        return pltpu.make_async_copy(ck_ref.at[pt_ref[row, p]], kbuf.at[slot, p], sem.at[slot])

    def start_row(row, slot):
        for p in range(n_pages):
            page_copy(row, slot, p).start()

    def wait_row(row, slot):
        for p in range(n_pages):
            page_copy(row, slot, p).wait()

    return start_row, wait_row


def _in_proj_kernel(x_ref, sc_ref, sh_ref, w_ref, wt_ref, wg_ref, bg_ref, *out_refs, transposed):
    proj_refs, g_ref = out_refs[:N_PROJ_GROUPS], out_refs[N_PROJ_GROUPS]
    h = (x_ref[...] * (1.0 + sc_ref[...]) + sh_ref[...]).astype(BF16)
    for gi, o_ref in enumerate(proj_refs):
        cols = slice(gi * PROJ_GROUP, (gi + 1) * PROJ_GROUP)
        if gi in transposed:
            y = lax.dot_general(wt_ref[cols, :], h, _NT, preferred_element_type=F32)
        else:
            y = jnp.dot(h, w_ref[:, cols], preferred_element_type=F32)
        if gi == MK_GROUP:
            y = y * (DK_MLSTM ** -0.5)
        o_ref[...] = y.astype(o_ref.dtype)
    g = jnp.dot(h, wg_ref[...], preferred_element_type=F32) + bg_ref[...]
    lane = lax.broadcasted_iota(jnp.int32, g.shape, 1)
    logsig = jnp.minimum(g, 0.0) - jnp.log1p(jnp.exp(-jnp.abs(g)))
    g_ref[...] = jnp.where(lane >= H_MLSTM, logsig, g)


def _in_proj(x, sc, sh, w_main, w_att_t, w_gate, b_gate, tm, rows_per_mod, out_dtypes, transposed=()):
    t, d = x.shape
    steps = t // tm
    assert MK_GROUP not in transposed
    if rows_per_mod is None:
        mod_spec = pl.BlockSpec((tm, d), lambda i, *_: (i, 0))
        t_shape, t_spec = (PROJ_GROUP, t), pl.BlockSpec((PROJ_GROUP, tm), lambda i, *_: (0, i))
    else:
        per = rows_per_mod // tm
        mod_spec = pl.BlockSpec((None, 1, d), lambda i, *_: (i // per, 0, 0))
        t_shape = (t // rows_per_mod, PROJ_GROUP, rows_per_mod)
        t_spec = pl.BlockSpec((None, PROJ_GROUP, tm), lambda i, *_: (i // per, 0, i % per))
    out_shape, out_specs = [], []
    for gi, dt in enumerate(out_dtypes):
        if gi in transposed:
            out_shape.append(jax.ShapeDtypeStruct(t_shape, dt))
            out_specs.append(t_spec)
        else:
            out_shape.append(jax.ShapeDtypeStruct((t, PROJ_GROUP), dt))
            out_specs.append(pl.BlockSpec((tm, PROJ_GROUP), lambda i, *_: (i, 0)))
    out_shape.append(jax.ShapeDtypeStruct((t, LANE), F32))
    out_specs.append(pl.BlockSpec((tm, LANE), lambda i, *_: (i, 0)))
    in_specs = [pl.BlockSpec((tm, d), lambda i, *_: (i, 0)), mod_spec, mod_spec,
                _const_spec(w_main.shape), _const_spec(w_att_t.shape),
                _const_spec(w_gate.shape), _const_spec(b_gate.shape)]
    return pl.pallas_call(
        functools.partial(_in_proj_kernel, transposed=tuple(transposed)),
        grid=(steps,), in_specs=in_specs, out_specs=out_specs, out_shape=out_shape,
        compiler_params=_params("arbitrary"),
    )(x, sc, sh, w_main, w_att_t, w_gate, b_gate)


def _moba_step(i, q_ref, kt_ref, vt_ref, o_ref, kaug_ref, vaug_ref, kmt_ref, lhs_sc, m_sc, acc_sc, nb, beside_own_block):
    blk = MOBA_BLOCK
    half = LANE // 2
    w = q_ref.shape[1]

    @pl.when(i == 0)
    def _prepare_batch():
        srow = lax.broadcasted_iota(jnp.int32, (LANE, blk), 0)
        in_lo = srow < half
        head_of_row = lax.broadcasted_iota(jnp.int32, (w, LANE), 0) // DH_ATT
        lane_w = lax.broadcasted_iota(jnp.int32, (w, LANE), 1)
        kmt = jnp.zeros((w, LANE), F32)
        for j in range(nb):
            ktj = kt_ref[:, j * blk:(j + 1) * blk]
            vtj = vt_ref[:, j * blk:(j + 1) * blk]
            col = jnp.mean(ktj, axis=1, keepdims=True)
            kmt = jnp.where((lane_w % SUBLANE == j) & (lane_w // SUBLANE == head_of_row), col, kmt)
            for p in range(H_ATT // 2):
                kp, vp = ktj[p * LANE:(p + 1) * LANE, :], vtj[p * LANE:(p + 1) * LANE, :]
                kaug_ref[2 * p, j] = jnp.where(in_lo, kp, jnp.where(srow == half + j, 1.0, 0.0)).astype(BF16)
                kaug_ref[2 * p + 1, j] = jnp.where(in_lo, jnp.where(srow == j, 1.0, 0.0), kp).astype(BF16)
                vaug_ref[2 * p, j] = jnp.where(in_lo, vp, 1.0).astype(BF16)
                vaug_ref[2 * p + 1, j] = jnp.where(in_lo, 1.0, vp).astype(BF16)
        km_hi = kmt.astype(BF16)
        kmt_ref[0] = km_hi
        kmt_ref[1] = (kmt - km_hi.astype(F32)).astype(BF16)

    lane = lax.broadcasted_iota(jnp.int32, (blk, LANE), 1)
    lo_lanes = lane < half

    def store_lhs(p, bias_p):
        qp = q_ref[:, p * LANE:(p + 1) * LANE] * (DH_ATT ** -0.5 * LOG2E)
        lhs_sc[2 * p] = jnp.where(lo_lanes, qp, bias_p).astype(BF16)
        lhs_sc[2 * p + 1] = jnp.where(lo_lanes, bias_p, qp).astype(BF16)

    @pl.when(i <= MOBA_TOPK)
    def _every_past_block_selected():
        block_of_lane = lane % half
        bias = jnp.where((block_of_lane < SUBLANE) & (block_of_lane > i), NEG, 0.0)
        for p in range(H_ATT // 2):
            store_lhs(p, bias)

    @pl.when(i > MOBA_TOPK)
    def _ranked_blocks():
        q32 = q_ref[...]
        q_hi = q32.astype(BF16)
        q_lo = (q32 - q_hi.astype(F32)).astype(BF16)
        sc = (jnp.dot(q_hi, kmt_ref[0], preferred_element_type=F32)
              + (jnp.dot(q_hi, kmt_ref[1], preferred_element_type=F32)
                 + jnp.dot(q_lo, kmt_ref[0], preferred_element_type=F32)))
        sc_t = sc.T
        nidx = lax.broadcasted_iota(jnp.int32, (SUBLANE, blk), 0)
        past = nidx < i
        biases = []
        for h in range(H_ATT):
            val = jnp.where(past, sc_t[h * SUBLANE:(h + 1) * SUBLANE, :], NEG)
            keep = (_top_blocks(val, nidx) & past) | (nidx == i)
            biases.append(jnp.where(keep, 0.0, NEG))
        zpad = jnp.zeros((half - SUBLANE, blk), F32)
        for p in range(H_ATT // 2):
            store_lhs(p, jnp.concatenate([biases[2 * p + 1], zpad, biases[2 * p], zpad], axis=0).T)

    def scores(h, j):
        return jnp.dot(lhs_sc[h], kaug_ref[h, j], preferred_element_type=F32)

    def row_max(s):
        return jnp.broadcast_to(jnp.max(s, axis=1, keepdims=True), (blk, LANE))

    def weights(s, m):
        return jnp.exp2(s - jnp.concatenate([m, m], axis=1)).astype(BF16)

    row = lax.broadcasted_iota(jnp.int32, (blk, blk), 0)
    col = lax.broadcasted_iota(jnp.int32, (blk, blk), 1)
    causal = col <= row
    beside_own_block()
    for h in range(H_ATT):
        s = jnp.where(causal, scores(h, i), NEG)
        m = row_max(s)
        acc_sc[h] = lax.dot_general(weights(s, m), vaug_ref[h, i], _NT, preferred_element_type=F32)
        m_sc[h] = m

    def past_blocks(js):
        for h in range(H_ATT):
            ss = [scores(h, j) for j in js]
            m_old = m_sc[h]
            m_new = m_old
            for s in ss:
                m_new = jnp.maximum(m_new, row_max(s))
            acc = jnp.exp2(m_old - m_new) * acc_sc[h]
            for s, j in zip(ss, js):
                acc = acc + lax.dot_general(weights(s, m_new), vaug_ref[h, j], _NT, preferred_element_type=F32)
            acc_sc[h] = acc
            m_sc[h] = m_new

    def two_past_blocks(t, carry):
        past_blocks((2 * t, 2 * t + 1))
        return carry

    lax.fori_loop(0, lax.shift_right_logical(i, 1), two_past_blocks, 0)

    @pl.when((i & 1) == 1)
    def _last_past_block():
        past_blocks((i - 1,))

    for p in range(H_ATT // 2):
        acc_e, acc_o = acc_sc[2 * p], acc_sc[2 * p + 1]
        num = jnp.where(lo_lanes, acc_e, acc_o)
        den = pltpu.roll(jnp.where(lo_lanes, acc_o, acc_e), half, 1)
        o_ref[:, p * LANE:(p + 1) * LANE] = (num / den).astype(o_ref.dtype)


def _mlstm_head_out(hh, gain, ogate):
    mu = jnp.mean(hh, axis=-1, keepdims=True)
    d = hh - mu
    var = jnp.mean(d * d, axis=-1, keepdims=True)
    return d * lax.rsqrt(var + LN_EPS) * gain * jax.nn.sigmoid(ogate)


def _mlstm_chunk(tok, q_ref, k_ref, v_ref, o_ref, g_ref, gain_ref, mem_ref, c_sc, n_sc, m_sc):
    L = MLSTM_CHUNK
    row = lax.broadcasted_iota(jnp.int32, (L, L), 0)
    col = lax.broadcasted_iota(jnp.int32, (L, L), 1)
    causal = col <= row
    lower = jnp.where(causal, 1.0, 0.0)
    upper = jnp.where(row <= col, 1.0, 0.0)
    ones = jnp.ones((L, DV_MLSTM), BF16)

    g = g_ref[tok, :]
    g_t = g.T
    b_col_all = jnp.dot(lower, g, precision=HIGHEST, preferred_element_type=F32)
    b_row_all = jnp.dot(g_t[0:SUBLANE, :], upper, precision=HIGHEST, preferred_element_type=F32)
    for h in range(H_MLSTM):
        lanes = slice(h * DK_MLSTM, (h + 1) * DK_MLSTM)
        ig_row = g_t[h:h + 1, :]
        b_row = b_row_all[H_MLSTM + h:H_MLSTM + h + 1, :]
        ig = jnp.broadcast_to(g[:, h:h + 1], (L, LANE))
        b = jnp.broadcast_to(b_col_all[:, H_MLSTM + h:H_MLSTM + h + 1], (L, LANE))
        m_prev = m_sc[h:h + 1, :]
        dmat = jnp.where(causal, b - b_row + ig_row, NEG)
        m_inter = b + m_prev
        m_t = jnp.maximum(m_inter, jnp.broadcast_to(jnp.max(dmat, axis=1, keepdims=True), (L, LANE)))
        w_inter = jnp.exp(m_inter - m_t)
        qh, kh, vh = q_ref[tok, lanes], k_ref[tok, lanes], v_ref[tok, lanes]
        a = jnp.exp(dmat - m_t) * lax.dot_general(qh, kh, _NT, preferred_element_type=F32)
        c_prev = c_sc[h]
        n_prev = n_sc[h:h + 1, :]
        state = jnp.concatenate([c_prev, jnp.broadcast_to(n_prev, (DV_MLSTM, DK_MLSTM))], axis=0).astype(BF16)
        num_den = (jnp.concatenate([w_inter, w_inter], axis=1)
                   * lax.dot_general(qh, state, _NT, preferred_element_type=F32)
                   + jnp.dot(a.astype(BF16), jnp.concatenate([vh, ones], axis=1), preferred_element_type=F32))
        hh = num_den[:, :DV_MLSTM] / jnp.maximum(jnp.abs(num_den[:, DV_MLSTM:]), jnp.exp(-m_t))
        mem_ref[tok, lanes] = _mlstm_head_out(
            hh, gain_ref[:, lanes], o_ref[tok, lanes].astype(F32)).astype(mem_ref.dtype)

        m_new = m_t[L - 1:L, :]
        b_last = b[L - 1:L, :]
        g_inter = jnp.exp(b_last + m_prev - m_new)
        g_in = jnp.exp(b_last - b + ig - m_new)
        v_scaled = (vh.astype(F32) * g_in).astype(BF16)
        c_sc[h] = g_inter * c_prev + lax.dot_general(v_scaled, kh, _TN, preferred_element_type=F32)
        n_sc[h:h + 1, :] = g_inter * n_prev + jnp.sum(kh.astype(F32) * g_in, axis=0, keepdims=True)
        m_sc[h:h + 1, :] = m_new


def _prompt_mixers_kernel(pt_ref, q_ref, kt_ref, vt_ref, mq_ref, mk_ref, mv_ref, mo_ref, g_ref, gain_ref,
                          qt_ref, knt_ref, ck_ref,
                          att_ref, mem_ref, c_out, n_out, m_out, pe_ref, stats_ref, idx_ref,
                          kaug_ref, vaug_ref, kmt_ref, lhs_sc, m_att, acc_sc, c_sc, n_sc, m_sc, kbuf, sem,
                          *, nb, guest_rows):
    bi, i = pl.program_id(0), pl.program_id(1)

    @pl.when(i == 0)
    def _reset_state():
        c_sc[...] = jnp.zeros_like(c_sc)
        n_sc[...] = jnp.zeros_like(n_sc)
        m_sc[...] = jnp.zeros_like(m_sc)

    ready = []
    _guest_rows_step_ahead(bi * nb + i, pl.num_programs(0) * nb, guest_rows,
                           *_key_page_ring(pt_ref, ck_ref, kbuf, sem), lambda *row: ready.append(row))

    def recurrence_and_scores():
        for r, row, slot in ready:
            _moba_sample_scores(row, qt_ref, knt_ref, [kbuf.at[slot, p] for p in range(kbuf.shape[1])],
                                pe_ref.at[r], stats_ref.at[r], idx_ref.at[r])
        for c in range(MOBA_BLOCK // MLSTM_CHUNK):
            _mlstm_chunk(slice(c * MLSTM_CHUNK, (c + 1) * MLSTM_CHUNK), mq_ref, mk_ref, mv_ref, mo_ref, g_ref,
                         gain_ref, mem_ref, c_sc, n_sc, m_sc)

    _moba_step(i, q_ref, kt_ref, vt_ref, att_ref, kaug_ref, vaug_ref, kmt_ref, lhs_sc, m_att, acc_sc, nb,
               recurrence_and_scores)

    @pl.when(i == nb - 1)
    def _emit_state():
        c_out[...] = c_sc[...]
        n_out[...] = n_sc[...]
        m_out[...] = m_sc[...]


def _prompt_mixers(q, kt, vt, mq, mk, mv, mo, gates, gain, qt, knt, cache_k, page_table):
    b, s, w = q.shape
    nb = s // MOBA_BLOCK
    wa, bs = qt.shape
    n_pages = page_table.shape[1]
    guest_rows = bs // (b * nb)
    assert s % MOBA_BLOCK == 0 and nb <= SUBLANE and w == ATT_WIDTH == wa and mq.shape[2] == MLSTM_WIDTH
    assert MOBA_BLOCK % MLSTM_CHUNK == 0 and MLSTM_CHUNK == LANE == DK_MLSTM == DV_MLSTM
    assert bs == guest_rows * b * nb == LANE and n_pages % PAGES_PER_BLOCK == 0
    assert MOBA_TOPK <= n_pages // PAGES_PER_BLOCK <= LANE
    ck = _cache_pages(cache_k).reshape(cache_k.shape[0], wa, PAGE_SIZE)
    tok = lambda width: pl.BlockSpec((None, MOBA_BLOCK, width), lambda bi, i, *_: (bi, i, 0))
    seq = pl.BlockSpec((None, w, s), lambda bi, i, *_: (bi, 0, 0))
    state = lambda *dims: pl.BlockSpec((None,) + dims, lambda bi, i, *_: (bi,) + (0,) * len(dims))
    rows = lambda *dims: pl.BlockSpec((guest_rows,) + dims, lambda bi, i, *_: (bi * nb + i,) + (0,) * len(dims))
    whole = pl.BlockSpec((wa, bs), lambda *_: (0, 0))
    blk_state = lambda dt: pltpu.VMEM((H_ATT, MOBA_BLOCK, LANE), dt)
    return pl.pallas_call(
        functools.partial(_prompt_mixers_kernel, nb=nb, guest_rows=guest_rows),
        grid_spec=pltpu.PrefetchScalarGridSpec(
            num_scalar_prefetch=1,
            grid=(b, nb),
            in_specs=[tok(w), seq, seq, tok(MLSTM_WIDTH), tok(MLSTM_WIDTH), tok(MLSTM_WIDTH), tok(MLSTM_WIDTH),
                      tok(LANE), pl.BlockSpec((1, MLSTM_WIDTH), lambda *_: (0, 0)),
                      whole, whole, pl.BlockSpec(memory_space=pl.ANY)],
            out_specs=[tok(w), tok(MLSTM_WIDTH), state(H_MLSTM, DV_MLSTM, DK_MLSTM), state(SUBLANE, LANE),
                       state(SUBLANE, LANE), rows(n_pages, H_ATT, LANE), rows(H_ATT, LANE), rows(H_ATT, LANE)],
            scratch_shapes=[pltpu.VMEM((H_ATT, nb, LANE, MOBA_BLOCK), BF16),
                            pltpu.VMEM((H_ATT, nb, LANE, MOBA_BLOCK), BF16),
                            pltpu.VMEM((2, w, LANE), BF16),
                            blk_state(BF16), blk_state(F32), blk_state(F32),
                            pltpu.VMEM((H_MLSTM, DV_MLSTM, DK_MLSTM), F32),
                            pltpu.VMEM((SUBLANE, LANE), F32),
                            pltpu.VMEM((SUBLANE, LANE), F32),
                            pltpu.VMEM((2 * guest_rows, n_pages, wa, PAGE_SIZE), F32),
                            pltpu.SemaphoreType.DMA((2 * guest_rows,))],
        ),
        out_shape=[jax.ShapeDtypeStruct((b, s, w), BF16),
                   jax.ShapeDtypeStruct((b, s, MLSTM_WIDTH), BF16),
                   jax.ShapeDtypeStruct((b, H_MLSTM, DV_MLSTM, DK_MLSTM), F32),
                   jax.ShapeDtypeStruct((b, SUBLANE, LANE), F32),
                   jax.ShapeDtypeStruct((b, SUBLANE, LANE), F32),
                   jax.ShapeDtypeStruct((bs, n_pages, H_ATT, LANE), F32),
                   jax.ShapeDtypeStruct((bs, H_ATT, LANE), F32),
                   jax.ShapeDtypeStruct((bs, H_ATT, LANE), jnp.int32)],
        compiler_params=_params("arbitrary", "arbitrary"),
    )(page_table, q, kt, vt, mq, mk, mv, mo, gates, gain, qt, knt, ck)


def _head_sublane(h):
    return (H_ATT // 2 - 1 - h) if h < H_ATT // 2 else (H_ATT + H_ATT // 2 - 1 - h)


def _head_rows(x):
    parts = []
    for h in range(H_ATT):
        tiles = [x[h * DH_ATT + SUBLANE * t:h * DH_ATT + SUBLANE * (t + 1), :] for t in range(DH_ATT // SUBLANE)]
        parts.append(sum(tiles[1:], tiles[0]))
    sub = lax.broadcasted_iota(jnp.int32, parts[0].shape, 0)
    folded = [p + pltpu.roll(p, 4, 0) for p in parts]
    quads = [jnp.where(sub < 4, folded[i], folded[i + 4]) for i in range(4)]
    take_up = (sub & 2) != 0
    pairs = [jnp.where(take_up, quads[i] + pltpu.roll(quads[i], 2, 0),
                       quads[i + 2] + pltpu.roll(quads[i + 2], 6, 0)) for i in range(2)]
    return jnp.where((sub & 1) != 0, pairs[0] + pltpu.roll(pairs[0], 1, 0), pairs[1] + pltpu.roll(pairs[1], 7, 0))


def _moba_sample_scores(b, qt_ref, knt_ref, kp_refs, pe_ref, stats_ref, idx_ref):
    n_pages = len(kp_refs)
    n_blocks = n_pages // PAGES_PER_BLOCK
    w = qt_ref.shape[0]
    on_b = lax.broadcasted_iota(jnp.int32, (w, LANE), 1) == b

    def column(ref):
        return jnp.sum(jnp.where(on_b, ref[...], 0.0), axis=1, keepdims=True)

    q_col = column(qt_ref) * (DH_ATT ** -0.5)
    q_wide = jnp.broadcast_to(q_col, (w, LANE))
    s_own = _head_rows(jnp.broadcast_to(q_col * column(knt_ref), (w, LANE)))[:, 0:1]
    s_pages = [_head_rows(kp_refs[p][...] * q_wide) for p in range(n_pages)]

    blk = [jnp.sum(sum(s_pages[n * PAGES_PER_BLOCK + 1:(n + 1) * PAGES_PER_BLOCK], s_pages[n * PAGES_PER_BLOCK]),
                   axis=1, keepdims=True) for n in range(n_blocks)]
    lane = lax.broadcasted_iota(jnp.int32, (H_ATT, LANE), 1)
    sel, ranked = [], jnp.zeros((H_ATT, LANE), jnp.int32)
    for n in range(n_blocks):
        rank = jnp.zeros((H_ATT, 1), jnp.int32)
        for o in range(n_blocks):
            if o != n:
                beats = (blk[o] >= blk[n]) if o < n else (blk[o] > blk[n])
                rank = rank + jnp.where(beats, 1, 0)
        sel.append(rank < MOBA_TOPK)
        ranked = jnp.where(rank == lane, n, ranked)
    m = s_own
    for p in range(n_pages):
        page_max = jnp.max(s_pages[p], axis=1, keepdims=True)
        m = jnp.maximum(m, jnp.where(sel[p // PAGES_PER_BLOCK], page_max, NEG))
    p_own = jnp.exp(s_own - m)
    total = jnp.zeros((H_ATT, LANE), F32)
    for p in range(n_pages):
        pe = jnp.where(sel[p // PAGES_PER_BLOCK], jnp.exp(s_pages[p] - m), 0.0)
        pe_ref[p] = pe
        total = total + pe
    row_sum = p_own + jnp.sum(total, axis=1, keepdims=True)
    stats_ref[...] = jnp.where(lane == 0, p_own, row_sum)
    idx_ref[...] = ranked


N_VALUE_CHUNKS = MOBA_TOPK * PAGES_PER_BLOCK


def _value_chunk_page(sel_ref, row, h, c):
    return sel_ref[row, h * MOBA_TOPK + c // PAGES_PER_BLOCK] * PAGES_PER_BLOCK + c % PAGES_PER_BLOCK


def _moba_sample_mix_row(row, sel_ref, pe_ref, stats_ref, vnt_ref, chunks_ref, o_ref):
    w = vnt_ref.shape[0]
    on_row = lax.broadcasted_iota(jnp.int32, (w, LANE), 1) == row
    vn_col = jnp.sum(jnp.where(on_row, vnt_ref[...], 0.0), axis=1, keepdims=True)
    stats = stats_ref[...]
    out_cols = []
    for h in range(H_ATT):
        r = _head_sublane(h)
        acc = jnp.zeros((DH_ATT, LANE), F32)
        for c in range(N_VALUE_CHUNKS):
            acc = acc + (pe_ref[_value_chunk_page(sel_ref, row, h, c), r:r + 1, :]
                         * chunks_ref[h * N_VALUE_CHUNKS + c])
        p_own, row_sum = stats[r:r + 1, 0:1], stats[r:r + 1, 1:2]
        rows = slice(h * DH_ATT, (h + 1) * DH_ATT)
        out_cols.append((jnp.sum(acc, axis=1, keepdims=True) + p_own * vn_col[rows, :]) / row_sum)
    o_ref[...] = jnp.where(on_row, jnp.concatenate(out_cols, axis=0), o_ref[...])


def _cache_pages(cache):
    return jnp.transpose(cache, (0, 2, 3, 1))


def _selected_blocks(ranked):
    b = ranked.shape[0]
    return jnp.stack([ranked[:, _head_sublane(h), :MOBA_TOPK] for h in range(H_ATT)], axis=1).reshape(b, -1)


def _mlstm_sample_kernel(q_ref, k_ref, v_ref, o_ref, g_ref, gain_ref, c0_ref, n0_ref, m0_ref,
                         mem_ref, c_ref, n_ref, m_ref):
    tb = q_ref.shape[0]
    g = g_ref[...]
    sub = lax.broadcasted_iota(jnp.int32, (2 * tb, LANE), 0)
    zrows = jnp.zeros((tb, LANE), F32)
    for h in range(H_MLSTM):
        lanes = slice(h * DK_MLSTM, (h + 1) * DK_MLSTM)
        ig, lf, m0 = g[:, h:h + 1], g[:, H_MLSTM + h:H_MLSTM + h + 1], m0_ref[:, h:h + 1]
        q, k, v = q_ref[:, lanes], k_ref[:, lanes], v_ref[:, lanes]
        n0 = n0_ref[:, lanes]
        m_t = jnp.maximum(lf + m0, ig)
        w_inter = jnp.exp(lf + m0 - m_t)
        g_in = jnp.exp(ig - m_t)
        a = g_in * jnp.sum(q * k, axis=1, keepdims=True)
        den = w_inter * jnp.sum(n0 * q, axis=1, keepdims=True) + a
        q_b = q.astype(BF16)
        gv = jnp.concatenate([g_in * v, zrows], axis=0)
        k_b = jnp.concatenate([k, zrows], axis=0).astype(BF16)
        cq_rows = []
        for r in range(tb):
            c_prev = c0_ref[r, h]
            cq_rows.append(lax.dot_general(q_b, c_prev.astype(BF16), _NT, preferred_element_type=F32)[r:r + 1, :])
            outer = lax.dot_general(jnp.where(sub == r, gv, 0.0).astype(BF16), k_b, _TN,
                                    preferred_element_type=F32)
            c_ref[r, h] = w_inter[r:r + 1, :] * c_prev + outer
        cq = jnp.concatenate(cq_rows, axis=0)
        hh = (w_inter * cq + a * v) / jnp.maximum(jnp.abs(den), jnp.exp(-m_t))
        mem_ref[:, lanes] = _mlstm_head_out(hh, gain_ref[:, lanes], o_ref[:, lanes]).astype(mem_ref.dtype)
        n_ref[:, lanes] = w_inter * n0 + g_in * k
        m_ref[:, h:h + 1] = m_t


def _mlstm_sample(mq, mk, mv, mo, gates, gain, c0, n0, m0):
    b, w = mq.shape
    tb = 2 * SUBLANE
    rows = lambda width: pl.BlockSpec((tb, width), lambda i: (i, 0))
    c_spec = pl.BlockSpec((tb, H_MLSTM, DV_MLSTM, DK_MLSTM), lambda i: (i, 0, 0, 0))
    return pl.pallas_call(
        _mlstm_sample_kernel,
        grid=(b // tb,),
        in_specs=[rows(w), rows(w), rows(w), rows(w), rows(LANE), pl.BlockSpec((1, w), lambda i: (0, 0)),
                  c_spec, rows(w), rows(H_MLSTM)],
        out_specs=[rows(w), c_spec, rows(w), rows(H_MLSTM)],
        out_shape=[jax.ShapeDtypeStruct((b, w), F32),
                   jax.ShapeDtypeStruct(c0.shape, F32),
                   jax.ShapeDtypeStruct((b, w), F32),
                   jax.ShapeDtypeStruct((b, H_MLSTM), F32)],
        compiler_params=_params("arbitrary"),
    )(mq, mk, mv, mo, gates, gain, c0, n0.reshape(b, w), m0)


N_FINISH_INPUTS = 15


def _finish_kernel(*refs, alpha, ff_chunk, att_transposed, guest_rows):
    if guest_rows:
        pt_ref, sel_ref, refs = refs[0], refs[1], refs[2:]
    (x_ref, att_ref, mem_ref, g1_ref, sh2_ref, sc2_ref, g2_ref, wo_ref, ln1g_ref, ln1b_ref,
     wg_ref, wu_ref, wd_ref, ln2g_ref, ln2b_ref) = refs[:N_FINISH_INPUTS]
    if guest_rows:
        pe_ref, stats_ref, vnt_ref, cv_ref, y_ref, o_ref, vbuf, sem = refs[N_FINISH_INPUTS:]
    else:
        (y_ref,) = refs[N_FINISH_INPUTS:]

    att = att_ref[...].T if att_transposed else att_ref[...]
    aw = att.shape[1]
    mix = (jnp.dot(att.astype(BF16), wo_ref[0:aw, :], preferred_element_type=F32)
           + jnp.dot(mem_ref[...].astype(BF16), wo_ref[aw:, :], preferred_element_type=F32))
    x1 = _layernorm(alpha * x_ref[...] + (1.0 + g1_ref[...]) * mix, ln1g_ref[...], ln1b_ref[...])
    h2 = (x1 * (1.0 + sc2_ref[...]) + sh2_ref[...]).astype(BF16)
    n_ff = wg_ref.shape[1] // ff_chunk

    def ffn_chunk(c):
        cols = slice(c * ff_chunk, (c + 1) * ff_chunk)
        gate = jnp.dot(h2, wg_ref[:, cols], preferred_element_type=F32)
        up = jnp.dot(h2, wu_ref[:, cols], preferred_element_type=F32)
        act = (gate * jax.nn.sigmoid(gate) * up).astype(BF16)
        return jnp.dot(act, wd_ref[cols, :], preferred_element_type=F32)

    ffn = [jnp.zeros(x1.shape, F32)]
    if not guest_rows:
        for c in range(n_ff):
            ffn[0] = ffn[0] + ffn_chunk(c)
    else:
        i = pl.program_id(0)
        per_row = -(-n_ff // guest_rows)

        @pl.when(i == 0)
        def _init_out():
            o_ref[...] = jnp.zeros_like(o_ref)

        def chunk_copy(row, slot, h, c):
            page = pt_ref[row, _value_chunk_page(sel_ref, row, h, c)]
            return pltpu.make_async_copy(cv_ref.at[page, h], vbuf.at[slot, h * N_VALUE_CHUNKS + c], sem.at[slot])

        def start_row(row, slot):
            for h in range(H_ATT):
                for c in range(N_VALUE_CHUNKS):
                    chunk_copy(row, slot, h, c).start()

        def wait_row(row, slot):
            for h in range(H_ATT):
                for c in range(N_VALUE_CHUNKS):
                    chunk_copy(row, slot, h, c).wait()

        def work(r, row, slot):
            _moba_sample_mix_row(row, sel_ref, pe_ref.at[r], stats_ref.at[r], vnt_ref, vbuf.at[slot], o_ref)
            for c in range(r * per_row, min((r + 1) * per_row, n_ff)):
                ffn[0] = ffn[0] + ffn_chunk(c)

        _guest_rows_row_ahead(i, pl.num_programs(0), guest_rows, start_row, wait_row, work)
    y_ref[...] = _layernorm(alpha * x1 + (1.0 + g2_ref[...]) * ffn[0], ln2g_ref[...], ln2b_ref[...])


def _finish(x, att, mem, mods, weights, tm, rows_per_mod, alpha, att_transposed=False, guest=None):
    t, d = x.shape
    assert not att_transposed or tm == t
    steps = t // tm
    w_out, ln1_g, ln1_b, w_gate, w_up, w_down, ln2_g, ln2_b = weights
    if rows_per_mod is None:
        mod_spec = pl.BlockSpec((tm, d), lambda i, *_: (i, 0))
    else:
        per = rows_per_mod // tm
        mod_spec = pl.BlockSpec((None, 1, d), lambda i, *_: (i // per, 0, 0))
    tok = lambda width: pl.BlockSpec((tm, width), lambda i, *_: (i, 0))
    ff_chunk = 256
    assert w_gate.shape[1] % ff_chunk == 0
    in_specs = [tok(d), pl.BlockSpec(att.shape, lambda *_: (0, 0)) if att_transposed else tok(att.shape[1]),
                tok(mem.shape[1]), mod_spec, mod_spec, mod_spec, mod_spec,
                _const_spec(w_out.shape), _const_spec(ln1_g.shape), _const_spec(ln1_b.shape),
                _const_spec(w_gate.shape), _const_spec(w_up.shape), _const_spec(w_down.shape),
                _const_spec(ln2_g.shape), _const_spec(ln2_b.shape)]
    args = (x, att, mem, *mods, w_out, ln1_g, ln1_b, w_gate, w_up, w_down, ln2_g, ln2_b)
    assert len(in_specs) == N_FINISH_INPUTS
    body = functools.partial(_finish_kernel, alpha=alpha, ff_chunk=ff_chunk, att_transposed=att_transposed,
                             guest_rows=0)
    y_shape = jax.ShapeDtypeStruct((t, d), F32)
    if guest is None:
        return pl.pallas_call(body, grid=(steps,), in_specs=in_specs, out_specs=tok(d), out_shape=y_shape,
                              compiler_params=_params("arbitrary"))(*args)

    pe, stats, sel, vnt, cache_v, page_table = guest
    w, bs = vnt.shape
    n_pages = page_table.shape[1]
    guest_rows = bs // steps
    assert bs % steps == 0 and bs == LANE
    rows = lambda *dims: pl.BlockSpec((guest_rows,) + dims, lambda i, *_: (i,) + (0,) * len(dims))
    whole = pl.BlockSpec((w, bs), lambda *_: (0, 0))
    return pl.pallas_call(
        functools.partial(body, guest_rows=guest_rows),
        grid_spec=pltpu.PrefetchScalarGridSpec(
            num_scalar_prefetch=2,
            grid=(steps,),
            in_specs=in_specs + [rows(n_pages, H_ATT, LANE), rows(H_ATT, LANE), whole,
                                 pl.BlockSpec(memory_space=pl.ANY)],
            out_specs=[tok(d), whole],
            scratch_shapes=[pltpu.VMEM((2, H_ATT * N_VALUE_CHUNKS, DH_ATT, PAGE_SIZE), F32),
                            pltpu.SemaphoreType.DMA((2,))],
        ),
        out_shape=[y_shape, jax.ShapeDtypeStruct((w, bs), F32)],
        compiler_params=_params("arbitrary"),
    )(page_table, sel, *args, pe, stats, vnt, _cache_pages(cache_v))


def kernel(x_prompt, x_sample, cache_k, cache_v, state_C, state_n, state_m, page_table, c_prompt, c_sample,
           w_ada, b_ada, w_in, b_if, mlstm_norm_g, w_out, ln1_g, ln1_b, w_gate, w_up, w_down, ln2_g, ln2_b):
    depth = w_in.shape[0]
    assert depth == 1, "single-layer step"
    alpha = (2.0 * depth) ** 0.25
    bp, s, d = x_prompt.shape
    bs = x_sample.shape[0]
    assert x_sample.shape[1] == 1, "single-token decode step"

    w_main, w_att_t, w_gates = _proj_weights(w_in[0].T)
    b_gates = jnp.pad(b_if[0], (0, LANE - 2 * H_MLSTM)).reshape(1, LANE)
    gain = mlstm_norm_g[0].reshape(1, MLSTM_WIDTH)
    row = lambda a: a[0].reshape(1, -1)
    fin_w = (w_out[0].astype(BF16), row(ln1_g), row(ln1_b), w_gate[0].astype(BF16), w_up[0].astype(BF16),
             w_down[0].astype(BF16), row(ln2_g), row(ln2_b))

    c_all = jnp.concatenate([c_prompt, c_sample], axis=0)
    mod = _adaln(c_all, w_ada[0], b_ada[0])
    sh1, sc1, g1, sh2, sc2, g2 = (mod[:, i * d:(i + 1) * d] for i in range(6))
    pm = lambda a: a[:bp].reshape(bp, 1, d)
    sm = lambda a: a[bp:]

    xs = x_sample.reshape(bs, d)
    aq_s, ak_s, av_s, mq_s, mk_s, mv_s, mo_s, gates_s = _in_proj(
        xs, sm(sc1), sm(sh1), w_main, w_att_t, w_gates, b_gates, bs, None, (F32,) * N_PROJ_GROUPS,
        transposed=(0, 1, 2))

    xp = x_prompt.reshape(bp * s, d)
    tm = 512
    aq, ak_t, av_t, mq, mk, mv, mo, gates = _in_proj(
        xp, pm(sc1), pm(sh1), w_main, w_att_t, w_gates, b_gates, 2 * tm, s,
        (F32, F32, F32, BF16, BF16, BF16, F32), transposed=(1, 2))
    seq = lambda a: a.reshape(bp, s, a.shape[-1])
    att, mem, c_p, n_p, m_p, pe, stats, ranked = _prompt_mixers(
        seq(aq), ak_t, av_t, seq(mq), seq(mk), seq(mv), seq(mo), seq(gates), gain,
        aq_s, ak_s, cache_k[0], page_table)
    y_p, att_s = _finish(
        xp, att.reshape(bp * s, -1), mem.reshape(bp * s, -1), (pm(g1), pm(sh2), pm(sc2), pm(g2)),
        fin_w, tm, s, alpha, guest=(pe, stats, _selected_blocks(ranked), av_s, cache_v[0], page_table))

    mem_s, c_s, n_s, m_s = _mlstm_sample(mq_s, mk_s, mv_s, mo_s, gates_s, gain,
                                         state_C[0], state_n[0], state_m[0])
    y_s = _finish(xs, att_s, mem_s, (sm(g1), sm(sh2), sm(sc2), sm(g2)), fin_w, bs, None, alpha,
                  att_transposed=True)

    rows_p = lambda a: jnp.transpose(a.reshape(bp, H_ATT, DH_ATT, s), (0, 3, 1, 2))[None]
    rows_s = lambda a: jnp.transpose(a.reshape(H_ATT, DH_ATT, bs), (2, 0, 1)).reshape(1, bs, 1, H_ATT, DH_ATT)
    return (y_p.reshape(bp, s, d), y_s.reshape(bs, 1, d),
            rows_p(ak_t), rows_p(av_t),
            c_p[None], n_p[None, :, :H_MLSTM, :], m_p[None, :, :H_MLSTM, 0],
            rows_s(ak_s), rows_s(av_s),
            c_s[None], n_s.reshape(1, bs, H_MLSTM, DK_MLSTM), m_s[None])
```

```python
import functools
import math

import jax
import jax.numpy as jnp
from jax import lax
from jax.experimental import pallas as pl
from jax.experimental.pallas import tpu as pltpu

F32 = jnp.float32
BF16 = jnp.bfloat16
HIGHEST = lax.Precision.HIGHEST

LANE = 128
SUBLANE = 8
VMEM_LIMIT_BYTES = 56 * 1024 * 1024

H_ATT = 8
DH_ATT = 64
ATT_WIDTH = H_ATT * DH_ATT
MOBA_BLOCK = 256
MOBA_TOPK = 3
H_MLSTM = 4
DK_MLSTM = 128
DV_MLSTM = 128
MLSTM_WIDTH = H_MLSTM * DV_MLSTM
MLSTM_CHUNK = LANE
PAGE_SIZE = 128
PAGES_PER_BLOCK = MOBA_BLOCK // PAGE_SIZE
LN_EPS = 1e-5
NEG = -1e30
LOG2E = math.log2(math.e)
N_PROJ_GROUPS = 7
PROJ_GROUP = 512
MK_GROUP = 4

_NT = (((1,), (1,)), ((), ()))
_TN = (((0,), (0,)), ((), ()))


def _params(*sem):
    return pltpu.CompilerParams(dimension_semantics=sem, vmem_limit_bytes=VMEM_LIMIT_BYTES)


def _const_spec(shape):
    return pl.BlockSpec(shape, lambda *_: (0,) * len(shape), pipeline_mode=pl.Buffered(1))


def _layernorm(x, g, b):
    mu = jnp.mean(x, axis=-1, keepdims=True)
    d = x - mu
    var = jnp.mean(d * d, axis=-1, keepdims=True)
    return d * lax.rsqrt(var + LN_EPS) * g + b


def _top_blocks(val, nidx):
    cnt = jnp.zeros(val.shape, jnp.int32)
    for r in range(1, SUBLANE):
        other = pltpu.roll(val, r, 0)
        oidx = pltpu.roll(nidx, r, 0)
        beats = (other > val) | ((other == val) & (oidx < nidx))
        cnt = cnt + jnp.where(beats, 1, 0)
    return cnt < MOBA_TOPK


def _adaln_kernel(c_ref, w_ref, b_ref, o_ref):
    c = c_ref[...]
    s = c * jax.nn.sigmoid(c)
    o_ref[...] = jnp.dot(s, w_ref[...], preferred_element_type=F32) + b_ref[...]


def _adaln(c, w_ada, b_ada):
    rows, d = c.shape
    n = w_ada.shape[1]
    tn = d
    return pl.pallas_call(
        _adaln_kernel,
        grid=(n // tn,),
        in_specs=[pl.BlockSpec((rows, d), lambda j: (0, 0)),
                  pl.BlockSpec((d, tn), lambda j: (0, j)),
                  pl.BlockSpec((1, tn), lambda j: (0, j))],
        out_specs=pl.BlockSpec((rows, tn), lambda j: (0, j)),
        out_shape=jax.ShapeDtypeStruct((rows, n), F32),
        compiler_params=_params("arbitrary"),
    )(c, w_ada, b_ada.reshape(1, n))


N_ATT_GROUPS = 3


def _proj_weights_kernel(wt_ref, gate_rows_ref, main_ref, att_t_ref, gates_ref):
    g = pl.program_id(0)
    group_t = wt_ref[...]
    main_ref[...] = group_t.T.astype(BF16)

    @pl.when(g < N_ATT_GROUPS)
    def _attention_group():
        att_t_ref[...] = group_t.astype(BF16)

    @pl.when(g == 0)
    def _gate_columns():
        rows = gate_rows_ref[...]
        pad = jnp.zeros((LANE - rows.shape[0], rows.shape[1]), F32)
        gates_ref[...] = jnp.concatenate([rows, pad], axis=0).T.astype(BF16)


def _proj_weights(w_in_t):
    cols, d = w_in_t.shape
    n_main = N_PROJ_GROUPS * PROJ_GROUP
    n_gate = cols - n_main
    assert n_gate == 2 * H_MLSTM == SUBLANE
    return pl.pallas_call(
        _proj_weights_kernel,
        grid=(N_PROJ_GROUPS,),
        in_specs=[pl.BlockSpec((PROJ_GROUP, d), lambda g: (g, 0)),
                  pl.BlockSpec((n_gate, d), lambda g: (n_main // n_gate, 0))],
        out_specs=[pl.BlockSpec((d, PROJ_GROUP), lambda g: (0, g)),
                   pl.BlockSpec((PROJ_GROUP, d), lambda g: (jnp.minimum(g, N_ATT_GROUPS - 1), 0)),
                   pl.BlockSpec((d, LANE), lambda g: (0, 0))],
        out_shape=[jax.ShapeDtypeStruct((d, n_main), BF16),
                   jax.ShapeDtypeStruct((N_ATT_GROUPS * PROJ_GROUP, d), BF16),
                   jax.ShapeDtypeStruct((d, LANE), BF16)],
        compiler_params=_params("arbitrary"),
    )(w_in_t, w_in_t)


def _guest_rows_row_ahead(i, n_steps, guest_rows, start_row, wait_row, work):
    assert guest_rows % 2 == 0

    @pl.when(i == 0)
    def _first_row():
        start_row(0, 0)

    for r in range(guest_rows):
        row, slot = i * guest_rows + r, r % 2
        if r + 1 < guest_rows:
            start_row(row + 1, 1 - slot)
        else:
            @pl.when(i + 1 < n_steps)
            def _next_step_row():
                start_row(row + 1, 1 - slot)
        wait_row(row, slot)
        work(r, row, slot)


def _guest_rows_step_ahead(i, n_steps, guest_rows, start_row, wait_row, work):
    half = (i % 2) * guest_rows

    @pl.when(i == 0)
    def _first_step_rows():
        for r in range(guest_rows):
            start_row(r, r)

    @pl.when(i + 1 < n_steps)
    def _next_step_rows():
        for r in range(guest_rows):
            start_row((i + 1) * guest_rows + r, guest_rows - half + r)

    for r in range(guest_rows):
        row, slot = i * guest_rows + r, half + r
        wait_row(row, slot)
        work(r, row, slot)


def _key_page_ring(pt_ref, ck_ref, kbuf, sem):
    n_pages = kbuf.shape[1]

    def page_copy(row, slot, p):
        return pltpu.make_async_copy(ck_ref.at[pt_ref[row, p]], kbuf.at[slot, p], sem.at[slot])

    def start_row(row, slot):
        for p in range(n_pages):
            page_copy(row, slot, p).start()

    def wait_row(row, slot):
        for p in range(n_pages):
            page_copy(row, slot, p).wait()

    return start_row, wait_row


def _in_proj_kernel(x_ref, sc_ref, sh_ref, w_ref, wt_ref, wg_ref, bg_ref, *out_refs, transposed):
    proj_refs, g_ref = out_refs[:N_PROJ_GROUPS], out_refs[N_PROJ_GROUPS]
    h = (x_ref[...] * (1.0 + sc_ref[...]) + sh_ref[...]).astype(BF16)
    for gi, o_ref in enumerate(proj_refs):
        cols = slice(gi * PROJ_GROUP, (gi + 1) * PROJ_GROUP)
        if gi in transposed:
            y = lax.dot_general(wt_ref[cols, :], h, _NT, preferred_element_type=F32)
        else:
            y = jnp.dot(h, w_ref[:, cols], preferred_element_type=F32)
        if gi == MK_GROUP:
            y = y * (DK_MLSTM ** -0.5)
        o_ref[...] = y.astype(o_ref.dtype)
    g = jnp.dot(h, wg_ref[...], preferred_element_type=F32) + bg_ref[...]
    lane = lax.broadcasted_iota(jnp.int32, g.shape, 1)
    logsig = jnp.minimum(g, 0.0) - jnp.log1p(jnp.exp(-jnp.abs(g)))
    g_ref[...] = jnp.where(lane >= H_MLSTM, logsig, g)


def _in_proj(x, sc, sh, w_main, w_att_t, w_gate, b_gate, tm, rows_per_mod, out_dtypes, transposed=()):
    t, d = x.shape
    steps = t // tm
    assert MK_GROUP not in transposed
    if rows_per_mod is None:
        mod_spec = pl.BlockSpec((tm, d), lambda i, *_: (i, 0))
        t_shape, t_spec = (PROJ_GROUP, t), pl.BlockSpec((PROJ_GROUP, tm), lambda i, *_: (0, i))
    else:
        per = rows_per_mod // tm
        mod_spec = pl.BlockSpec((None, 1, d), lambda i, *_: (i // per, 0, 0))
        t_shape = (t // rows_per_mod, PROJ_GROUP, rows_per_mod)
        t_spec = pl.BlockSpec((None, PROJ_GROUP, tm), lambda i, *_: (i // per, 0, i % per))
    out_shape, out_specs = [], []
    for gi, dt in enumerate(out_dtypes):
        if gi in transposed:
            out_shape.append(jax.ShapeDtypeStruct(t_shape, dt))
            out_specs.append(t_spec)
        else:
            out_shape.append(jax.ShapeDtypeStruct((t, PROJ_GROUP), dt))
            out_specs.append(pl.BlockSpec((tm, PROJ_GROUP), lambda i, *_: (i, 0)))
    out_shape.append(jax.ShapeDtypeStruct((t, LANE), F32))
    out_specs.append(pl.BlockSpec((tm, LANE), lambda i, *_: (i, 0)))
    in_specs = [pl.BlockSpec((tm, d), lambda i, *_: (i, 0)), mod_spec, mod_spec,
                _const_spec(w_main.shape), _const_spec(w_att_t.shape),
                _const_spec(w_gate.shape), _const_spec(b_gate.shape)]
    return pl.pallas_call(
        functools.partial(_in_proj_kernel, transposed=tuple(transposed)),
        grid=(steps,), in_specs=in_specs, out_specs=out_specs, out_shape=out_shape,
        compiler_params=_params("arbitrary"),
    )(x, sc, sh, w_main, w_att_t, w_gate, b_gate)


def _moba_block_pair(i, nb, q_refs, kt_ref, vt_ref, o_refs, kaug_ref, vaug_ref, kmt_ref, lhs_sc, m_sc, acc_sc):
    blk = MOBA_BLOCK
    half = LANE // 2
    w = q_refs[0].shape[1]
    own = (i, nb - 1 - i)
    assert nb // 2 - 1 <= MOBA_TOPK

    @pl.when(i == 0)
    def _prepare_batch():
        srow = lax.broadcasted_iota(jnp.int32, (LANE, blk), 0)
        in_lo = srow < half
        head_of_row = lax.broadcasted_iota(jnp.int32, (w, LANE), 0) // DH_ATT
        lane_w = lax.broadcasted_iota(jnp.int32, (w, LANE), 1)
        kmt = jnp.zeros((w, LANE), F32)
        for j in range(nb):
            ktj = kt_ref[:, j * blk:(j + 1) * blk]
            vtj = vt_ref[:, j * blk:(j + 1) * blk]
            col = jnp.mean(ktj, axis=1, keepdims=True)
            kmt = jnp.where((lane_w % SUBLANE == j) & (lane_w // SUBLANE == head_of_row), col, kmt)
            for p in range(H_ATT // 2):
                kp, vp = ktj[p * LANE:(p + 1) * LANE, :], vtj[p * LANE:(p + 1) * LANE, :]
                kaug_ref[2 * p, j] = jnp.where(in_lo, kp, jnp.where(srow == half + j, 1.0, 0.0)).astype(BF16)
                kaug_ref[2 * p + 1, j] = jnp.where(in_lo, jnp.where(srow == j, 1.0, 0.0), kp).astype(BF16)
                vaug_ref[2 * p, j] = jnp.where(in_lo, vp, 1.0).astype(BF16)
                vaug_ref[2 * p + 1, j] = jnp.where(in_lo, 1.0, vp).astype(BF16)
        km_hi = kmt.astype(BF16)
        kmt_ref[0] = km_hi
        kmt_ref[1] = (kmt - km_hi.astype(F32)).astype(BF16)

    lane = lax.broadcasted_iota(jnp.int32, (blk, LANE), 1)
    lo_lanes = lane < half

    def store_lhs(side, p, bias_p):
        qp = q_refs[side][:, p * LANE:(p + 1) * LANE] * (DH_ATT ** -0.5 * LOG2E)
        lhs_sc[side * H_ATT + 2 * p] = jnp.where(lo_lanes, qp, bias_p).astype(BF16)
        lhs_sc[side * H_ATT + 2 * p + 1] = jnp.where(lo_lanes, bias_p, qp).astype(BF16)

    block_of_lane = lane % half
    bias_lo = jnp.where((block_of_lane < SUBLANE) & (block_of_lane > own[0]), NEG, 0.0)
    for p in range(H_ATT // 2):
        store_lhs(0, p, bias_lo)

    q32 = q_refs[1][...]
    q_hi = q32.astype(BF16)
    q_lo = (q32 - q_hi.astype(F32)).astype(BF16)
    sc = (jnp.dot(q_hi, kmt_ref[0], preferred_element_type=F32)
          + (jnp.dot(q_hi, kmt_ref[1], preferred_element_type=F32)
             + jnp.dot(q_lo, kmt_ref[0], preferred_element_type=F32)))
    sc_t = sc.T
    nidx = lax.broadcasted_iota(jnp.int32, (SUBLANE, blk), 0)
    past = nidx < own[1]
    biases = []
    for h in range(H_ATT):
        val = jnp.where(past, sc_t[h * SUBLANE:(h + 1) * SUBLANE, :], NEG)
        keep = (_top_blocks(val, nidx) & past) | (nidx == own[1])
        biases.append(jnp.where(keep, 0.0, NEG))
    zpad = jnp.zeros((half - SUBLANE, blk), F32)
    for p in range(H_ATT // 2):
        store_lhs(1, p, jnp.concatenate([biases[2 * p + 1], zpad, biases[2 * p], zpad], axis=0).T)

    def scores(idx, h, j):
        return jnp.dot(lhs_sc[idx], kaug_ref[h, j], preferred_element_type=F32)

    def row_max(s):
        return jnp.broadcast_to(jnp.max(s, axis=1, keepdims=True), (blk, LANE))

    def weights(s, m):
        return jnp.exp2(s - jnp.concatenate([m, m], axis=1)).astype(BF16)

    row = lax.broadcasted_iota(jnp.int32, (blk, blk), 0)
    col = lax.broadcasted_iota(jnp.int32, (blk, blk), 1)
    causal = col <= row

    def own_block(side):
        for h in range(H_ATT):
            idx = side * H_ATT + h
            s = jnp.where(causal, scores(idx, h, own[side]), NEG)
            m = row_max(s)
            acc_sc[idx] = lax.dot_general(weights(s, m), vaug_ref[h, own[side]], _NT, preferred_element_type=F32)
            m_sc[idx] = m

    def past_block(u):
        side = jnp.where(u >= own[0], 1, 0)
        j = u - side * own[0]
        for h in range(H_ATT):
            idx = side * H_ATT + h
            s = scores(idx, h, j)
            m_old = m_sc[idx]
            m_new = jnp.maximum(m_old, row_max(s))
            acc_sc[idx] = (jnp.exp2(m_old - m_new) * acc_sc[idx]
                           + lax.dot_general(weights(s, m_new), vaug_ref[h, j], _NT, preferred_element_type=F32))
            m_sc[idx] = m_new

    units = [functools.partial(own_block, 0), functools.partial(own_block, 1)]
    units += [functools.partial(past_block, u) for u in range(nb - 1)]

    def write_outputs():
        for side in range(2):
            for p in range(H_ATT // 2):
                acc_e, acc_o = acc_sc[side * H_ATT + 2 * p], acc_sc[side * H_ATT + 2 * p + 1]
                num = jnp.where(lo_lanes, acc_e, acc_o)
                den = pltpu.roll(jnp.where(lo_lanes, acc_o, acc_e), half, 1)
                o_refs[side][:, p * LANE:(p + 1) * LANE] = (num / den).astype(o_refs[side].dtype)

    return units, write_outputs


def _mlstm_head_out(hh, gain, ogate):
    mu = jnp.mean(hh, axis=-1, keepdims=True)
    d = hh - mu
    var = jnp.mean(d * d, axis=-1, keepdims=True)
    return d * lax.rsqrt(var + LN_EPS) * gain * jax.nn.sigmoid(ogate)


def _mlstm_chunk(tok, q_ref, k_ref, v_ref, o_ref, g_ref, gain_ref, mem_ref, c_sc, n_sc, m_sc):
    L = MLSTM_CHUNK
    row = lax.broadcasted_iota(jnp.int32, (L, L), 0)
    col = lax.broadcasted_iota(jnp.int32, (L, L), 1)
    causal = col <= row
    lower = jnp.where(causal, 1.0, 0.0)
    upper = jnp.where(row <= col, 1.0, 0.0)
    ones = jnp.ones((L, DV_MLSTM), BF16)

    g = g_ref[tok, :]
    g_t = g.T
    b_col_all = jnp.dot(lower, g, precision=HIGHEST, preferred_element_type=F32)
    b_row_all = jnp.dot(g_t[0:SUBLANE, :], upper, precision=HIGHEST, preferred_element_type=F32)
    for h in range(H_MLSTM):
        lanes = slice(h * DK_MLSTM, (h + 1) * DK_MLSTM)
        ig_row = g_t[h:h + 1, :]
        b_row = b_row_all[H_MLSTM + h:H_MLSTM + h + 1, :]
        ig = jnp.broadcast_to(g[:, h:h + 1], (L, LANE))
        b = jnp.broadcast_to(b_col_all[:, H_MLSTM + h:H_MLSTM + h + 1], (L, LANE))
        m_prev = m_sc[h:h + 1, :]
        dmat = jnp.where(causal, b - b_row + ig_row, NEG)
        m_inter = b + m_prev
        m_t = jnp.maximum(m_inter, jnp.broadcast_to(jnp.max(dmat, axis=1, keepdims=True), (L, LANE)))
        w_inter = jnp.exp(m_inter - m_t)
        qh, kh, vh = q_ref[tok, lanes], k_ref[tok, lanes], v_ref[tok, lanes]
        a = jnp.exp(dmat - m_t) * lax.dot_general(qh, kh, _NT, preferred_element_type=F32)
        c_prev = c_sc[h]
        n_prev = n_sc[h:h + 1, :]
        state = jnp.concatenate([c_prev, jnp.broadcast_to(n_prev, (DV_MLSTM, DK_MLSTM))], axis=0).astype(BF16)
        num_den = (jnp.concatenate([w_inter, w_inter], axis=1)
                   * lax.dot_general(qh, state, _NT, preferred_element_type=F32)
                   + jnp.dot(a.astype(BF16), jnp.concatenate([vh, ones], axis=1), preferred_element_type=F32))
        hh = num_den[:, :DV_MLSTM] / jnp.maximum(jnp.abs(num_den[:, DV_MLSTM:]), jnp.exp(-m_t))
        mem_ref[tok, lanes] = _mlstm_head_out(
            hh, gain_ref[:, lanes], o_ref[tok, lanes].astype(F32)).astype(mem_ref.dtype)

        m_new = m_t[L - 1:L, :]
        b_last = b[L - 1:L, :]
        g_inter = jnp.exp(b_last + m_prev - m_new)
        g_in = jnp.exp(b_last - b + ig - m_new)
        v_scaled = (vh.astype(F32) * g_in).astype(BF16)
        c_sc[h] = g_inter * c_prev + lax.dot_general(v_scaled, kh, _TN, preferred_element_type=F32)
        n_sc[h:h + 1, :] = g_inter * n_prev + jnp.sum(kh.astype(F32) * g_in, axis=0, keepdims=True)
        m_sc[h:h + 1, :] = m_new


def _prompt_mixers_kernel(pt_ref, q_lo_ref, q_hi_ref, kt_ref, vt_ref, mq_ref, mk_ref, mv_ref, mo_ref, g_ref, gain_ref,
                          qt_ref, knt_ref, ck_ref,
                          att_lo_ref, att_hi_ref, mem_ref, c_out, n_out, m_out, pe_ref, stats_ref, idx_ref,
                          kaug_ref, vaug_ref, kmt_ref, lhs_sc, m_att, acc_sc, c_sc, n_sc, m_sc, kbuf, sem,
                          *, nb, guest_rows):
    bi, i = pl.program_id(0), pl.program_id(1)
    steps = nb // 2

    @pl.when(i == 0)
    def _reset_state():
        c_sc[...] = jnp.zeros_like(c_sc)
        n_sc[...] = jnp.zeros_like(n_sc)
        m_sc[...] = jnp.zeros_like(m_sc)

    units, write_outputs = _moba_block_pair(i, nb, (q_lo_ref, q_hi_ref), kt_ref, vt_ref, (att_lo_ref, att_hi_ref),
                                            kaug_ref, vaug_ref, kmt_ref, lhs_sc, m_att, acc_sc)
    per_row = -(-len(units) // guest_rows)

    def work(r, row, slot):
        _moba_sample_scores(row, qt_ref, knt_ref, [kbuf.at[slot, p] for p in range(kbuf.shape[1])],
                            pe_ref.at[r], stats_ref.at[r], idx_ref.at[r])
        _mlstm_chunk(slice(r * MLSTM_CHUNK, (r + 1) * MLSTM_CHUNK), mq_ref, mk_ref, mv_ref, mo_ref, g_ref,
                     gain_ref, mem_ref, c_sc, n_sc, m_sc)
        for unit in units[r * per_row:(r + 1) * per_row]:
            unit()

    _guest_rows_row_ahead(bi * steps + i, pl.num_programs(0) * steps, guest_rows,
                          *_key_page_ring(pt_ref, ck_ref, kbuf, sem), work)
    write_outputs()

    @pl.when(i == steps - 1)
    def _emit_state():
        c_out[...] = c_sc[...]
        n_out[...] = n_sc[...]
        m_out[...] = m_sc[...]


def _prompt_mixers(q, kt, vt, mq, mk, mv, mo, gates, gain, qt, knt, cache_k, page_table):
    b, s, w = q.shape
    nb = s // MOBA_BLOCK
    steps = nb // 2
    tok_rows = 2 * MOBA_BLOCK
    wa, bs = qt.shape
    n_pages = page_table.shape[1]
    guest_rows = bs // (b * steps)
    assert s % tok_rows == 0 and nb <= SUBLANE and w == ATT_WIDTH == wa and mq.shape[2] == MLSTM_WIDTH
    assert MLSTM_CHUNK == LANE == DK_MLSTM == DV_MLSTM and guest_rows == tok_rows // MLSTM_CHUNK
    assert bs == guest_rows * b * steps == LANE and n_pages % PAGES_PER_BLOCK == 0
    assert MOBA_TOPK <= n_pages // PAGES_PER_BLOCK <= LANE
    ck = _cache_pages(cache_k).reshape(cache_k.shape[0], wa, PAGE_SIZE)
    q_lo = pl.BlockSpec((None, MOBA_BLOCK, w), lambda bi, i, *_: (bi, i, 0))
    q_hi = pl.BlockSpec((None, MOBA_BLOCK, w), lambda bi, i, *_: (bi, nb - 1 - i, 0))
    o_hi = pl.BlockSpec((None, MOBA_BLOCK, w), lambda bi, i, *_: (bi, steps - 1 - i, 0))
    tok = lambda width: pl.BlockSpec((None, tok_rows, width), lambda bi, i, *_: (bi, i, 0))
    seq = pl.BlockSpec((None, w, s), lambda bi, i, *_: (bi, 0, 0))
    state = lambda *dims: pl.BlockSpec((None,) + dims, lambda bi, i, *_: (bi,) + (0,) * len(dims))
    rows = lambda *dims: pl.BlockSpec((guest_rows,) + dims, lambda bi, i, *_: (bi * steps + i,) + (0,) * len(dims))
    whole = pl.BlockSpec((wa, bs), lambda *_: (0, 0))
    pair_state = lambda dt: pltpu.VMEM((2 * H_ATT, MOBA_BLOCK, LANE), dt)
    att_lo, att_hi, mem, c_p, n_p, m_p, pe, stats, ranked = pl.pallas_call(
        functools.partial(_prompt_mixers_kernel, nb=nb, guest_rows=guest_rows),
        grid_spec=pltpu.PrefetchScalarGridSpec(
            num_scalar_prefetch=1,
            grid=(b, steps),
            in_specs=[q_lo, q_hi, seq, seq, tok(MLSTM_WIDTH), tok(MLSTM_WIDTH), tok(MLSTM_WIDTH), tok(MLSTM_WIDTH),
                      tok(LANE), pl.BlockSpec((1, MLSTM_WIDTH), lambda *_: (0, 0)),
                      whole, whole, pl.BlockSpec(memory_space=pl.ANY)],
            out_specs=[q_lo, o_hi, tok(MLSTM_WIDTH), state(H_MLSTM, DV_MLSTM, DK_MLSTM), state(SUBLANE, LANE),
                       state(SUBLANE, LANE), rows(n_pages, H_ATT, LANE), rows(H_ATT, LANE), rows(H_ATT, LANE)],
            scratch_shapes=[pltpu.VMEM((H_ATT, nb, LANE, MOBA_BLOCK), BF16),
                            pltpu.VMEM((H_ATT, nb, LANE, MOBA_BLOCK), BF16),
                            pltpu.VMEM((2, w, LANE), BF16),
                            pair_state(BF16), pair_state(F32), pair_state(F32),
                            pltpu.VMEM((H_MLSTM, DV_MLSTM, DK_MLSTM), F32),
                            pltpu.VMEM((SUBLANE, LANE), F32),
                            pltpu.VMEM((SUBLANE, LANE), F32),
                            pltpu.VMEM((2, n_pages, wa, PAGE_SIZE), F32),
                            pltpu.SemaphoreType.DMA((2,))],
        ),
        out_shape=[jax.ShapeDtypeStruct((b, s // 2, w), BF16),
                   jax.ShapeDtypeStruct((b, s // 2, w), BF16),
                   jax.ShapeDtypeStruct((b, s, MLSTM_WIDTH), BF16),
                   jax.ShapeDtypeStruct((b, H_MLSTM, DV_MLSTM, DK_MLSTM), F32),
                   jax.ShapeDtypeStruct((b, SUBLANE, LANE), F32),
                   jax.ShapeDtypeStruct((b, SUBLANE, LANE), F32),
                   jax.ShapeDtypeStruct((bs, n_pages, H_ATT, LANE), F32),
                   jax.ShapeDtypeStruct((bs, H_ATT, LANE), F32),
                   jax.ShapeDtypeStruct((bs, H_ATT, LANE), jnp.int32)],
        compiler_params=_params("arbitrary", "arbitrary"),
    )(page_table, q, q, kt, vt, mq, mk, mv, mo, gates, gain, qt, knt, ck)
    att = jnp.concatenate([att_lo, att_hi], axis=1)
    return att, mem, c_p, n_p, m_p, pe, stats, ranked


def _head_sublane(h):
    return (H_ATT // 2 - 1 - h) if h < H_ATT // 2 else (H_ATT + H_ATT // 2 - 1 - h)


def _head_rows(x):
    parts = []
    for h in range(H_ATT):
        tiles = [x[h * DH_ATT + SUBLANE * t:h * DH_ATT + SUBLANE * (t + 1), :] for t in range(DH_ATT // SUBLANE)]
        parts.append(sum(tiles[1:], tiles[0]))
    sub = lax.broadcasted_iota(jnp.int32, parts[0].shape, 0)
    folded = [p + pltpu.roll(p, 4, 0) for p in parts]
    quads = [jnp.where(sub < 4, folded[i], folded[i + 4]) for i in range(4)]
    take_up = (sub & 2) != 0
    pairs = [jnp.where(take_up, quads[i] + pltpu.roll(quads[i], 2, 0),
                       quads[i + 2] + pltpu.roll(quads[i + 2], 6, 0)) for i in range(2)]
    return jnp.where((sub & 1) != 0, pairs[0] + pltpu.roll(pairs[0], 1, 0), pairs[1] + pltpu.roll(pairs[1], 7, 0))


def _moba_sample_scores(b, qt_ref, knt_ref, kp_refs, pe_ref, stats_ref, idx_ref):
    n_pages = len(kp_refs)
    n_blocks = n_pages // PAGES_PER_BLOCK
    w = qt_ref.shape[0]
    on_b = lax.broadcasted_iota(jnp.int32, (w, LANE), 1) == b

    def column(ref):
        return jnp.sum(jnp.where(on_b, ref[...], 0.0), axis=1, keepdims=True)

    q_col = column(qt_ref) * (DH_ATT ** -0.5)
    q_wide = jnp.broadcast_to(q_col, (w, LANE))
    s_own = _head_rows(jnp.broadcast_to(q_col * column(knt_ref), (w, LANE)))[:, 0:1]
    s_pages = [_head_rows(kp_refs[p][...] * q_wide) for p in range(n_pages)]

    blk = [jnp.sum(sum(s_pages[n * PAGES_PER_BLOCK + 1:(n + 1) * PAGES_PER_BLOCK], s_pages[n * PAGES_PER_BLOCK]),
                   axis=1, keepdims=True) for n in range(n_blocks)]
    lane = lax.broadcasted_iota(jnp.int32, (H_ATT, LANE), 1)
    sel, ranked = [], jnp.zeros((H_ATT, LANE), jnp.int32)
    for n in range(n_blocks):
        rank = jnp.zeros((H_ATT, 1), jnp.int32)
        for o in range(n_blocks):
            if o != n:
                beats = (blk[o] >= blk[n]) if o < n else (blk[o] > blk[n])
                rank = rank + jnp.where(beats, 1, 0)
        sel.append(rank < MOBA_TOPK)
        ranked = jnp.where(rank == lane, n, ranked)
    m = s_own
    for p in range(n_pages):
        page_max = jnp.max(s_pages[p], axis=1, keepdims=True)
        m = jnp.maximum(m, jnp.where(sel[p // PAGES_PER_BLOCK], page_max, NEG))
    p_own = jnp.exp(s_own - m)
    total = jnp.zeros((H_ATT, LANE), F32)
    for p in range(n_pages):
        pe = jnp.where(sel[p // PAGES_PER_BLOCK], jnp.exp(s_pages[p] - m), 0.0)
        pe_ref[p] = pe
        total = total + pe
    row_sum = p_own + jnp.sum(total, axis=1, keepdims=True)
    stats_ref[...] = jnp.where(lane == 0, p_own, row_sum)
    idx_ref[...] = ranked


N_VALUE_CHUNKS = MOBA_TOPK * PAGES_PER_BLOCK


def _value_chunk_page(sel_ref, row, h, c):
    return sel_ref[row, h * MOBA_TOPK + c // PAGES_PER_BLOCK] * PAGES_PER_BLOCK + c % PAGES_PER_BLOCK


def _moba_sample_mix_row(row, sel_ref, pe_ref, stats_ref, vnt_ref, chunks_ref, o_ref):
    w = vnt_ref.shape[0]
    on_row = lax.broadcasted_iota(jnp.int32, (w, LANE), 1) == row
    vn_col = jnp.sum(jnp.where(on_row, vnt_ref[...], 0.0), axis=1, keepdims=True)
    stats = stats_ref[...]
    out_cols = []
    for h in range(H_ATT):
        r = _head_sublane(h)
        acc = jnp.zeros((DH_ATT, LANE), F32)
        for c in range(N_VALUE_CHUNKS):
            acc = acc + (pe_ref[_value_chunk_page(sel_ref, row, h, c), r:r + 1, :]
                         * chunks_ref[h * N_VALUE_CHUNKS + c])
        p_own, row_sum = stats[r:r + 1, 0:1], stats[r:r + 1, 1:2]
        rows = slice(h * DH_ATT, (h + 1) * DH_ATT)
        out_cols.append((jnp.sum(acc, axis=1, keepdims=True) + p_own * vn_col[rows, :]) / row_sum)
    o_ref[...] = jnp.where(on_row, jnp.concatenate(out_cols, axis=0), o_ref[...])


def _cache_pages(cache):
    return jnp.transpose(cache, (0, 2, 3, 1))


def _selected_blocks(ranked):
    b = ranked.shape[0]
    return jnp.stack([ranked[:, _head_sublane(h), :MOBA_TOPK] for h in range(H_ATT)], axis=1).reshape(b, -1)


def _mlstm_sample_kernel(q_ref, k_ref, v_ref, o_ref, g_ref, gain_ref, c0_ref, n0_ref, m0_ref,
                         mem_ref, c_ref, n_ref, m_ref):
    tb = q_ref.shape[0]
    g = g_ref[...]
    sub = lax.broadcasted_iota(jnp.int32, (2 * tb, LANE), 0)
    zrows = jnp.zeros((tb, LANE), F32)
    for h in range(H_MLSTM):
        lanes = slice(h * DK_MLSTM, (h + 1) * DK_MLSTM)
        ig, lf, m0 = g[:, h:h + 1], g[:, H_MLSTM + h:H_MLSTM + h + 1], m0_ref[:, h:h + 1]
        q, k, v = q_ref[:, lanes], k_ref[:, lanes], v_ref[:, lanes]
        n0 = n0_ref[:, lanes]
        m_t = jnp.maximum(lf + m0, ig)
        w_inter = jnp.exp(lf + m0 - m_t)
        g_in = jnp.exp(ig - m_t)
        a = g_in * jnp.sum(q * k, axis=1, keepdims=True)
        den = w_inter * jnp.sum(n0 * q, axis=1, keepdims=True) + a
        q_b = q.astype(BF16)
        gv = jnp.concatenate([g_in * v, zrows], axis=0)
        k_b = jnp.concatenate([k, zrows], axis=0).astype(BF16)
        cq_rows = []
        for r in range(tb):
            c_prev = c0_ref[r, h]
            cq_rows.append(lax.dot_general(q_b, c_prev.astype(BF16), _NT, preferred_element_type=F32)[r:r + 1, :])
            outer = lax.dot_general(jnp.where(sub == r, gv, 0.0).astype(BF16), k_b, _TN,
                                    preferred_element_type=F32)
            c_ref[r, h] = w_inter[r:r + 1, :] * c_prev + outer
        cq = jnp.concatenate(cq_rows, axis=0)
        hh = (w_inter * cq + a * v) / jnp.maximum(jnp.abs(den), jnp.exp(-m_t))
        mem_ref[:, lanes] = _mlstm_head_out(hh, gain_ref[:, lanes], o_ref[:, lanes]).astype(mem_ref.dtype)
        n_ref[:, lanes] = w_inter * n0 + g_in * k
        m_ref[:, h:h + 1] = m_t


def _mlstm_sample(mq, mk, mv, mo, gates, gain, c0, n0, m0):
    b, w = mq.shape
    tb = 2 * SUBLANE
    rows = lambda width: pl.BlockSpec((tb, width), lambda i: (i, 0))
    c_spec = pl.BlockSpec((tb, H_MLSTM, DV_MLSTM, DK_MLSTM), lambda i: (i, 0, 0, 0))
    return pl.pallas_call(
        _mlstm_sample_kernel,
        grid=(b // tb,),
        in_specs=[rows(w), rows(w), rows(w), rows(w), rows(LANE), pl.BlockSpec((1, w), lambda i: (0, 0)),
                  c_spec, rows(w), rows(H_MLSTM)],
        out_specs=[rows(w), c_spec, rows(w), rows(H_MLSTM)],
        out_shape=[jax.ShapeDtypeStruct((b, w), F32),
                   jax.ShapeDtypeStruct(c0.shape, F32),
                   jax.ShapeDtypeStruct((b, w), F32),
                   jax.ShapeDtypeStruct((b, H_MLSTM), F32)],
        compiler_params=_params("arbitrary"),
    )(mq, mk, mv, mo, gates, gain, c0, n0.reshape(b, w), m0)


N_FINISH_INPUTS = 15


def _finish_kernel(*refs, alpha, ff_chunk, att_transposed, guest_rows):
    if guest_rows:
        pt_ref, sel_ref, refs = refs[0], refs[1], refs[2:]
    (x_ref, att_ref, mem_ref, g1_ref, sh2_ref, sc2_ref, g2_ref, wo_ref, ln1g_ref, ln1b_ref,
     wg_ref, wu_ref, wd_ref, ln2g_ref, ln2b_ref) = refs[:N_FINISH_INPUTS]
    if guest_rows:
        pe_ref, stats_ref, vnt_ref, cv_ref, y_ref, o_ref, vbuf, sem = refs[N_FINISH_INPUTS:]
    else:
        (y_ref,) = refs[N_FINISH_INPUTS:]

    att = att_ref[...].T if att_transposed else att_ref[...]
    aw = att.shape[1]
    mix = (jnp.dot(att.astype(BF16), wo_ref[0:aw, :], preferred_element_type=F32)
           + jnp.dot(mem_ref[...].astype(BF16), wo_ref[aw:, :], preferred_element_type=F32))
    x1 = _layernorm(alpha * x_ref[...] + (1.0 + g1_ref[...]) * mix, ln1g_ref[...], ln1b_ref[...])
    h2 = (x1 * (1.0 + sc2_ref[...]) + sh2_ref[...]).astype(BF16)
    n_ff = wg_ref.shape[1] // ff_chunk

    def ffn_chunk(c):
        cols = slice(c * ff_chunk, (c + 1) * ff_chunk)
        gate = jnp.dot(h2, wg_ref[:, cols], preferred_element_type=F32)
        up = jnp.dot(h2, wu_ref[:, cols], preferred_element_type=F32)
        act = (gate * jax.nn.sigmoid(gate) * up).astype(BF16)
        return jnp.dot(act, wd_ref[cols, :], preferred_element_type=F32)

    ffn = [jnp.zeros(x1.shape, F32)]
    if not guest_rows:
        for c in range(n_ff):
            ffn[0] = ffn[0] + ffn_chunk(c)
    else:
        i = pl.program_id(0)
        per_row = -(-n_ff // guest_rows)

        @pl.when(i == 0)
        def _init_out():
            o_ref[...] = jnp.zeros_like(o_ref)

        def chunk_copy(row, slot, h, c):
            page = pt_ref[row, _value_chunk_page(sel_ref, row, h, c)]
            return pltpu.make_async_copy(cv_ref.at[page, h], vbuf.at[slot, h * N_VALUE_CHUNKS + c], sem.at[slot])

        def start_row(row, slot):
            for h in range(H_ATT):
                for c in range(N_VALUE_CHUNKS):
                    chunk_copy(row, slot, h, c).start()

        def wait_row(row, slot):
            for h in range(H_ATT):
                for c in range(N_VALUE_CHUNKS):
                    chunk_copy(row, slot, h, c).wait()

        def work(r, row, slot):
            _moba_sample_mix_row(row, sel_ref, pe_ref.at[r], stats_ref.at[r], vnt_ref, vbuf.at[slot], o_ref)
            for c in range(r * per_row, min((r + 1) * per_row, n_ff)):
                ffn[0] = ffn[0] + ffn_chunk(c)

        _guest_rows_row_ahead(i, pl.num_programs(0), guest_rows, start_row, wait_row, work)
    y_ref[...] = _layernorm(alpha * x1 + (1.0 + g2_ref[...]) * ffn[0], ln2g_ref[...], ln2b_ref[...])


def _finish(x, att, mem, mods, weights, tm, rows_per_mod, alpha, att_transposed=False, guest=None):
    t, d = x.shape
    assert not att_transposed or tm == t
    steps = t // tm
    w_out, ln1_g, ln1_b, w_gate, w_up, w_down, ln2_g, ln2_b = weights
    if rows_per_mod is None:
        mod_spec = pl.BlockSpec((tm, d), lambda i, *_: (i, 0))
    else:
        per = rows_per_mod // tm
        mod_spec = pl.BlockSpec((None, 1, d), lambda i, *_: (i // per, 0, 0))
    tok = lambda width: pl.BlockSpec((tm, width), lambda i, *_: (i, 0))
    ff_chunk = 256
    assert w_gate.shape[1] % ff_chunk == 0
    in_specs = [tok(d), pl.BlockSpec(att.shape, lambda *_: (0, 0)) if att_transposed else tok(att.shape[1]),
                tok(mem.shape[1]), mod_spec, mod_spec, mod_spec, mod_spec,
                _const_spec(w_out.shape), _const_spec(ln1_g.shape), _const_spec(ln1_b.shape),
                _const_spec(w_gate.shape), _const_spec(w_up.shape), _const_spec(w_down.shape),
                _const_spec(ln2_g.shape), _const_spec(ln2_b.shape)]
    args = (x, att, mem, *mods, w_out, ln1_g, ln1_b, w_gate, w_up, w_down, ln2_g, ln2_b)
    assert len(in_specs) == N_FINISH_INPUTS
    body = functools.partial(_finish_kernel, alpha=alpha, ff_chunk=ff_chunk, att_transposed=att_transposed,
                             guest_rows=0)
    y_shape = jax.ShapeDtypeStruct((t, d), F32)
    if guest is None:
        return pl.pallas_call(body, grid=(steps,), in_specs=in_specs, out_specs=tok(d), out_shape=y_shape,
                              compiler_params=_params("arbitrary"))(*args)

    pe, stats, sel, vnt, cache_v, page_table = guest
    w, bs = vnt.shape
    n_pages = page_table.shape[1]
    guest_rows = bs // steps
    assert bs % steps == 0 and bs == LANE
    rows = lambda *dims: pl.BlockSpec((guest_rows,) + dims, lambda i, *_: (i,) + (0,) * len(dims))
    whole = pl.BlockSpec((w, bs), lambda *_: (0, 0))
    return pl.pallas_call(
        functools.partial(body, guest_rows=guest_rows),
        grid_spec=pltpu.PrefetchScalarGridSpec(
            num_scalar_prefetch=2,
            grid=(steps,),
            in_specs=in_specs + [rows(n_pages, H_ATT, LANE), rows(H_ATT, LANE), whole,
                                 pl.BlockSpec(memory_space=pl.ANY)],
            out_specs=[tok(d), whole],
            scratch_shapes=[pltpu.VMEM((2, H_ATT * N_VALUE_CHUNKS, DH_ATT, PAGE_SIZE), F32),
                            pltpu.SemaphoreType.DMA((2,))],
        ),
        out_shape=[y_shape, jax.ShapeDtypeStruct((w, bs), F32)],
        compiler_params=_params("arbitrary"),
    )(page_table, sel, *args, pe, stats, vnt, _cache_pages(cache_v))


def kernel(x_prompt, x_sample, cache_k, cache_v, state_C, state_n, state_m, page_table, c_prompt, c_sample,
           w_ada, b_ada, w_in, b_if, mlstm_norm_g, w_out, ln1_g, ln1_b, w_gate, w_up, w_down, ln2_g, ln2_b):
    depth = w_in.shape[0]
    assert depth == 1, "single-layer step"
    alpha = (2.0 * depth) ** 0.25
    bp, s, d = x_prompt.shape
    bs = x_sample.shape[0]
    assert x_sample.shape[1] == 1, "single-token decode step"

    w_main, w_att_t, w_gates = _proj_weights(w_in[0].T)
    b_gates = jnp.pad(b_if[0], (0, LANE - 2 * H_MLSTM)).reshape(1, LANE)
    gain = mlstm_norm_g[0].reshape(1, MLSTM_WIDTH)
    row = lambda a: a[0].reshape(1, -1)
    fin_w = (w_out[0].astype(BF16), row(ln1_g), row(ln1_b), w_gate[0].astype(BF16), w_up[0].astype(BF16),
             w_down[0].astype(BF16), row(ln2_g), row(ln2_b))

    c_all = jnp.concatenate([c_prompt, c_sample], axis=0)
    mod = _adaln(c_all, w_ada[0], b_ada[0])
    sh1, sc1, g1, sh2, sc2, g2 = (mod[:, i * d:(i + 1) * d] for i in range(6))
    pm = lambda a: a[:bp].reshape(bp, 1, d)
    sm = lambda a: a[bp:]

    xs = x_sample.reshape(bs, d)
    aq_s, ak_s, av_s, mq_s, mk_s, mv_s, mo_s, gates_s = _in_proj(
        xs, sm(sc1), sm(sh1), w_main, w_att_t, w_gates, b_gates, bs, None, (F32,) * N_PROJ_GROUPS,
        transposed=(0, 1, 2))

    xp = x_prompt.reshape(bp * s, d)
    tm = 512
    aq, ak_t, av_t, mq, mk, mv, mo, gates = _in_proj(
        xp, pm(sc1), pm(sh1), w_main, w_att_t, w_gates, b_gates, tm, s,
        (F32, F32, F32, BF16, BF16, BF16, F32), transposed=(1, 2))
    seq = lambda a: a.reshape(bp, s, a.shape[-1])
    att, mem, c_p, n_p, m_p, pe, stats, ranked = _prompt_mixers(
        seq(aq), ak_t, av_t, seq(mq), seq(mk), seq(mv), seq(mo), seq(gates), gain,
        aq_s, ak_s, cache_k[0], page_table)
    y_p, att_s = _finish(
        xp, att.reshape(bp * s, -1), mem.reshape(bp * s, -1), (pm(g1), pm(sh2), pm(sc2), pm(g2)),
        fin_w, tm, s, alpha, guest=(pe, stats, _selected_blocks(ranked), av_s, cache_v[0], page_table))

    mem_s, c_s, n_s, m_s = _mlstm_sample(mq_s, mk_s, mv_s, mo_s, gates_s, gain,
                                         state_C[0], state_n[0], state_m[0])
    y_s = _finish(xs, att_s, mem_s, (sm(g1), sm(sh2), sm(sc2), sm(g2)), fin_w, bs, None, alpha,
                  att_transposed=True)

    rows_p = lambda a: jnp.transpose(a.reshape(bp, H_ATT, DH_ATT, s), (0, 3, 1, 2))[None]
    rows_s = lambda a: jnp.transpose(a.reshape(H_ATT, DH_ATT, bs), (2, 0, 1)).reshape(1, bs, 1, H_ATT, DH_ATT)
    return (y_p.reshape(bp, s, d), y_s.reshape(bs, 1, d),
            rows_p(ak_t), rows_p(av_t),
            c_p[None], n_p[None, :, :H_MLSTM, :], m_p[None, :, :H_MLSTM, 0],
            rows_s(ak_s), rows_s(av_s),
            c_s[None], n_s.reshape(1, bs, H_MLSTM, DK_MLSTM), m_s[None])
```

```python
import functools
import math

import jax
import jax.numpy as jnp
from jax import lax
from jax.experimental import pallas as pl
from jax.experimental.pallas import tpu as pltpu

F32 = jnp.float32
BF16 = jnp.bfloat16

LANE = 128
SUBLANE = 8
VMEM_LIMIT_BYTES = 56 * 1024 * 1024

H_ATT = 8
DH_ATT = 64
ATT_WIDTH = H_ATT * DH_ATT
MOBA_BLOCK = 256
MOBA_TOPK = 3
H_MLSTM = 4
DK_MLSTM = 128
DV_MLSTM = 128
MLSTM_WIDTH = H_MLSTM * DV_MLSTM
MLSTM_CHUNK = LANE
PAGE_SIZE = 128
PAGES_PER_BLOCK = MOBA_BLOCK // PAGE_SIZE
LN_EPS = 1e-5
NEG = -1e30
LOG2E = math.log2(math.e)
N_PROJ_GROUPS = 7
PROJ_GROUP = 512
MK_GROUP = 4
PROMPT_TOKEN_TILE = 512
FF_CHUNK = 2 * LANE
SAMPLE_STATE_TILE = 2 * SUBLANE

_NT = (((1,), (1,)), ((), ()))
_TN = (((0,), (0,)), ((), ()))


def _params(*sem):
    return pltpu.CompilerParams(dimension_semantics=sem, vmem_limit_bytes=VMEM_LIMIT_BYTES)


def _const_spec(shape):
    return pl.BlockSpec(shape, lambda *_: (0,) * len(shape), pipeline_mode=pl.Buffered(1))


def _layernorm(x, g, b):
    mu = jnp.mean(x, axis=-1, keepdims=True)
    d = x - mu
    var = jnp.mean(d * d, axis=-1, keepdims=True)
    return d * lax.rsqrt(var + LN_EPS) * g + b


def _top_blocks(val, nidx):
    cnt = jnp.zeros(val.shape, jnp.int32)
    for r in range(1, SUBLANE):
        other = pltpu.roll(val, r, 0)
        oidx = pltpu.roll(nidx, r, 0)
        beats = (other > val) | ((other == val) & (oidx < nidx))
        cnt = cnt + jnp.where(beats, 1, 0)
    return cnt < MOBA_TOPK


def _adaln_kernel(c_ref, w_ref, b_ref, o_ref):
    c = c_ref[...]
    s = c * jax.nn.sigmoid(c)
    o_ref[...] = jnp.dot(s, w_ref[...], preferred_element_type=F32) + b_ref[...]


def _adaln(c, w_ada, b_ada):
    rows, d = c.shape
    n = w_ada.shape[1]
    tn = d
    return pl.pallas_call(
        _adaln_kernel,
        grid=(n // tn,),
        in_specs=[pl.BlockSpec((rows, d), lambda j: (0, 0)),
                  pl.BlockSpec((d, tn), lambda j: (0, j)),
                  pl.BlockSpec((1, tn), lambda j: (0, j))],
        out_specs=pl.BlockSpec((rows, tn), lambda j: (0, j)),
        out_shape=jax.ShapeDtypeStruct((rows, n), F32),
        compiler_params=_params("arbitrary"),
    )(c, w_ada, b_ada.reshape(1, n))


N_ATT_GROUPS = 3


def _proj_weights_kernel(wt_ref, gate_rows_ref, main_ref, att_t_ref, gates_ref):
    g = pl.program_id(0)
    group_t = wt_ref[...]
    main_ref[...] = group_t.T.astype(BF16)

    @pl.when(g < N_ATT_GROUPS)
    def _attention_group():
        att_t_ref[...] = group_t.astype(BF16)

    @pl.when(g == 0)
    def _gate_columns():
        rows = gate_rows_ref[...]
        pad = jnp.zeros((LANE - rows.shape[0], rows.shape[1]), F32)
        gates_ref[...] = jnp.concatenate([rows, pad], axis=0).T.astype(BF16)


def _proj_weights(w_in_t):
    cols, d = w_in_t.shape
    n_main = N_PROJ_GROUPS * PROJ_GROUP
    n_gate = cols - n_main
    assert n_gate == 2 * H_MLSTM == SUBLANE
    return pl.pallas_call(
        _proj_weights_kernel,
        grid=(N_PROJ_GROUPS,),
        in_specs=[pl.BlockSpec((PROJ_GROUP, d), lambda g: (g, 0)),
                  pl.BlockSpec((n_gate, d), lambda g: (n_main // n_gate, 0))],
        out_specs=[pl.BlockSpec((d, PROJ_GROUP), lambda g: (0, g)),
                   pl.BlockSpec((PROJ_GROUP, d), lambda g: (jnp.minimum(g, N_ATT_GROUPS - 1), 0)),
                   pl.BlockSpec((d, LANE), lambda g: (0, 0))],
        out_shape=[jax.ShapeDtypeStruct((d, n_main), BF16),
                   jax.ShapeDtypeStruct((N_ATT_GROUPS * PROJ_GROUP, d), BF16),
                   jax.ShapeDtypeStruct((d, LANE), BF16)],
        compiler_params=_params("arbitrary"),
    )(w_in_t, w_in_t)


def _guest_rows_row_ahead(i, n_steps, guest_rows, start_row, wait_row, work):
    assert guest_rows % 2 == 0

    @pl.when(i == 0)
    def _first_row():
        start_row(0, 0)

    for r in range(guest_rows):
        row, slot = i * guest_rows + r, r % 2
        if r + 1 < guest_rows:
            start_row(row + 1, 1 - slot)
        else:
            @pl.when(i + 1 < n_steps)
            def _next_step_row():
                start_row(row + 1, 1 - slot)
        wait_row(row, slot)
        work(r, row, slot)


def _key_page_ring(pt_ref, ck_ref, kbuf, sem):
    n_pages = kbuf.shape[1]

    def page_copy(row, slot, p):
        return pltpu.make_async_copy(ck_ref.at[pt_ref[row, p]], kbuf.at[slot, p], sem.at[slot])

    def start_row(row, slot):
        for p in range(n_pages):
            page_copy(row, slot, p).start()

    def wait_row(row, slot):
        for p in range(n_pages):
            page_copy(row, slot, p).wait()

    return start_row, wait_row


def _in_proj_kernel(x_ref, sc_ref, sh_ref, w_ref, wt_ref, wg_ref, bg_ref, *out_refs, transposed):
    proj_refs, g_ref = out_refs[:N_PROJ_GROUPS], out_refs[N_PROJ_GROUPS]
    h = (x_ref[...] * (1.0 + sc_ref[...]) + sh_ref[...]).astype(BF16)
    for gi, o_ref in enumerate(proj_refs):
        cols = slice(gi * PROJ_GROUP, (gi + 1) * PROJ_GROUP)
        if gi in transposed:
            y = lax.dot_general(wt_ref[cols, :], h, _NT, preferred_element_type=F32)
        else:
            y = jnp.dot(h, w_ref[:, cols], preferred_element_type=F32)
        if gi == MK_GROUP:
            y = y * (DK_MLSTM ** -0.5)
        o_ref[...] = y.astype(o_ref.dtype)
    g = jnp.dot(h, wg_ref[...], preferred_element_type=F32) + bg_ref[...]
    lane = lax.broadcasted_iota(jnp.int32, g.shape, 1)
    logsig = jnp.minimum(g, 0.0) - jnp.log1p(jnp.exp(-jnp.abs(g)))
    g_ref[...] = jnp.where(lane >= H_MLSTM, logsig, g)


def _in_proj(x, sc, sh, w_main, w_att_t, w_gate, b_gate, tm, rows_per_mod, out_dtypes, transposed=()):
    t, d = x.shape
    steps = t // tm
    assert MK_GROUP not in transposed
    if rows_per_mod is None:
        mod_spec = pl.BlockSpec((tm, d), lambda i, *_: (i, 0))
        t_shape, t_spec = (PROJ_GROUP, t), pl.BlockSpec((PROJ_GROUP, tm), lambda i, *_: (0, i))
    else:
        per = rows_per_mod // tm
        mod_spec = pl.BlockSpec((None, 1, d), lambda i, *_: (i // per, 0, 0))
        t_shape = (t // rows_per_mod, PROJ_GROUP, rows_per_mod)
        t_spec = pl.BlockSpec((None, PROJ_GROUP, tm), lambda i, *_: (i // per, 0, i % per))
    out_shape, out_specs = [], []
    for gi, dt in enumerate(out_dtypes):
        if gi in transposed:
            out_shape.append(jax.ShapeDtypeStruct(t_shape, dt))
            out_specs.append(t_spec)
        else:
            out_shape.append(jax.ShapeDtypeStruct((t, PROJ_GROUP), dt))
            out_specs.append(pl.BlockSpec((tm, PROJ_GROUP), lambda i, *_: (i, 0)))
    out_shape.append(jax.ShapeDtypeStruct((t, LANE), F32))
    out_specs.append(pl.BlockSpec((tm, LANE), lambda i, *_: (i, 0)))
    in_specs = [pl.BlockSpec((tm, d), lambda i, *_: (i, 0)), mod_spec, mod_spec,
                _const_spec(w_main.shape), _const_spec(w_att_t.shape),
                _const_spec(w_gate.shape), _const_spec(b_gate.shape)]
    return pl.pallas_call(
        functools.partial(_in_proj_kernel, transposed=tuple(transposed)),
        grid=(steps,), in_specs=in_specs, out_specs=out_specs, out_shape=out_shape,
        compiler_params=_params("arbitrary"),
    )(x, sc, sh, w_main, w_att_t, w_gate, b_gate)


def _moba_block_pair(i, nb, q_refs, kt_ref, vt_ref, o_refs, kaug_ref, vaug_ref, kmt_ref, lhs_sc, m_sc, acc_sc):
    blk = MOBA_BLOCK
    half = LANE // 2
    w = q_refs[0].shape[1]
    own = (i, nb - 1 - i)
    assert nb // 2 - 1 <= MOBA_TOPK

    @pl.when(i == 0)
    def _prepare_batch():
        srow = lax.broadcasted_iota(jnp.int32, (LANE, blk), 0)
        in_lo = srow < half
        head_of_row = lax.broadcasted_iota(jnp.int32, (w, LANE), 0) // DH_ATT
        lane_w = lax.broadcasted_iota(jnp.int32, (w, LANE), 1)
        kmt = jnp.zeros((w, LANE), F32)
        for j in range(nb):
            ktj = kt_ref[:, j * blk:(j + 1) * blk]
            vtj = vt_ref[:, j * blk:(j + 1) * blk]
            col = jnp.mean(ktj, axis=1, keepdims=True)
            kmt = jnp.where((lane_w % SUBLANE == j) & (lane_w // SUBLANE == head_of_row), col, kmt)
            for p in range(H_ATT // 2):
                kp, vp = ktj[p * LANE:(p + 1) * LANE, :], vtj[p * LANE:(p + 1) * LANE, :]
                kaug_ref[2 * p, j] = jnp.where(in_lo, kp, jnp.where(srow == half + j, 1.0, 0.0)).astype(BF16)
                kaug_ref[2 * p + 1, j] = jnp.where(in_lo, jnp.where(srow == j, 1.0, 0.0), kp).astype(BF16)
                vaug_ref[2 * p, j] = jnp.where(in_lo, vp, 1.0).astype(BF16)
                vaug_ref[2 * p + 1, j] = jnp.where(in_lo, 1.0, vp).astype(BF16)
        km_hi = kmt.astype(BF16)
        kmt_ref[0] = km_hi
        kmt_ref[1] = (kmt - km_hi.astype(F32)).astype(BF16)

    lane = lax.broadcasted_iota(jnp.int32, (blk, LANE), 1)
    lo_lanes = lane < half

    def store_lhs(side, p, bias_p):
        qp = q_refs[side][:, p * LANE:(p + 1) * LANE] * (DH_ATT ** -0.5 * LOG2E)
        lhs_sc[side * H_ATT + 2 * p] = jnp.where(lo_lanes, qp, bias_p).astype(BF16)
        lhs_sc[side * H_ATT + 2 * p + 1] = jnp.where(lo_lanes, bias_p, qp).astype(BF16)

    block_of_lane = lane % half
    bias_lo = jnp.where((block_of_lane < SUBLANE) & (block_of_lane > own[0]), NEG, 0.0)
    for p in range(H_ATT // 2):
        store_lhs(0, p, bias_lo)

    q32 = q_refs[1][...]
    q_hi = q32.astype(BF16)
    q_lo = (q32 - q_hi.astype(F32)).astype(BF16)
    sc = (jnp.dot(q_hi, kmt_ref[0], preferred_element_type=F32)
          + (jnp.dot(q_hi, kmt_ref[1], preferred_element_type=F32)
             + jnp.dot(q_lo, kmt_ref[0], preferred_element_type=F32)))
    sc_t = sc.T
    nidx = lax.broadcasted_iota(jnp.int32, (SUBLANE, blk), 0)
    past = nidx < own[1]
    biases = []
    for h in range(H_ATT):
        val = jnp.where(past, sc_t[h * SUBLANE:(h + 1) * SUBLANE, :], NEG)
        keep = (_top_blocks(val, nidx) & past) | (nidx == own[1])
        biases.append(jnp.where(keep, 0.0, NEG))
    zpad = jnp.zeros((half - SUBLANE, blk), F32)
    for p in range(H_ATT // 2):
        store_lhs(1, p, jnp.concatenate([biases[2 * p + 1], zpad, biases[2 * p], zpad], axis=0).T)

    def scores(idx, h, j):
        return jnp.dot(lhs_sc[idx], kaug_ref[h, j], preferred_element_type=F32)

    def row_max(s):
        return jnp.broadcast_to(jnp.max(s, axis=1, keepdims=True), (blk, LANE))

    def weights(s, m):
        return jnp.exp2(s - jnp.concatenate([m, m], axis=1)).astype(BF16)

    row = lax.broadcasted_iota(jnp.int32, (blk, blk), 0)
    col = lax.broadcasted_iota(jnp.int32, (blk, blk), 1)
    causal = col <= row

    def own_block(side):
        for h in range(H_ATT):
            idx = side * H_ATT + h
            s = jnp.where(causal, scores(idx, h, own[side]), NEG)
            m = row_max(s)
            acc_sc[idx] = lax.dot_general(weights(s, m), vaug_ref[h, own[side]], _NT, preferred_element_type=F32)
            m_sc[idx] = m

    def past_block(u):
        side = jnp.where(u >= own[0], 1, 0)
        j = u - side * own[0]
        for h in range(H_ATT):
            idx = side * H_ATT + h
            s = scores(idx, h, j)
            m_old = m_sc[idx]
            m_new = jnp.maximum(m_old, row_max(s))
            acc_sc[idx] = (jnp.exp2(m_old - m_new) * acc_sc[idx]
                           + lax.dot_general(weights(s, m_new), vaug_ref[h, j], _NT, preferred_element_type=F32))
            m_sc[idx] = m_new

    units = [functools.partial(own_block, 0), functools.partial(own_block, 1)]
    units += [functools.partial(past_block, u) for u in range(nb - 1)]

    def write_outputs():
        for side in range(2):
            for p in range(H_ATT // 2):
                acc_e, acc_o = acc_sc[side * H_ATT + 2 * p], acc_sc[side * H_ATT + 2 * p + 1]
                num = jnp.where(lo_lanes, acc_e, acc_o)
                den = pltpu.roll(jnp.where(lo_lanes, acc_o, acc_e), half, 1)
                o_refs[side][:, p * LANE:(p + 1) * LANE] = (num / den).astype(o_refs[side].dtype)

    return units, write_outputs


def _mlstm_head_out(hh, gain, ogate):
    mu = jnp.mean(hh, axis=-1, keepdims=True)
    d = hh - mu
    var = jnp.mean(d * d, axis=-1, keepdims=True)
    return d * lax.rsqrt(var + LN_EPS) * gain * jax.nn.sigmoid(ogate)


def _mlstm_chunk(tok, q_ref, k_ref, v_ref, o_ref, g_ref, gain_ref, mem_ref, c_sc, n_sc, m_sc):
    L = MLSTM_CHUNK
    row = lax.broadcasted_iota(jnp.int32, (L, L), 0)
    col = lax.broadcasted_iota(jnp.int32, (L, L), 1)
    causal = col <= row
    lower = jnp.where(causal, 1.0, 0.0)
    upper = jnp.where(row <= col, 1.0, 0.0)
    ones = jnp.ones((L, DV_MLSTM), BF16)

    g = g_ref[tok, :]
    g_t = g.T

    def split3(x):
        hi = x.astype(BF16)
        r = x - hi.astype(F32)
        mid = r.astype(BF16)
        return hi, mid, (r - mid.astype(F32)).astype(BF16)

    b_col_all = sum(jnp.dot(lower.astype(BF16), part, preferred_element_type=F32) for part in split3(g))
    b_row_all = sum(jnp.dot(part, upper.astype(BF16), preferred_element_type=F32)
                    for part in split3(g_t[0:2 * SUBLANE, :]))
    for h in range(H_MLSTM):
        lanes = slice(h * DK_MLSTM, (h + 1) * DK_MLSTM)
        ig_row = g_t[h:h + 1, :]
        b_row = b_row_all[H_MLSTM + h:H_MLSTM + h + 1, :]
        ig = jnp.broadcast_to(g[:, h:h + 1], (L, LANE))
        b = jnp.broadcast_to(b_col_all[:, H_MLSTM + h:H_MLSTM + h + 1], (L, LANE))
        m_prev = m_sc[h:h + 1, :]
        dmat = jnp.where(causal, b - b_row + ig_row, NEG)
        m_inter = b + m_prev
        m_t = jnp.maximum(m_inter, jnp.broadcast_to(jnp.max(dmat, axis=1, keepdims=True), (L, LANE)))
        w_inter = jnp.exp(m_inter - m_t)
        qh, kh, vh = q_ref[tok, lanes], k_ref[tok, lanes], v_ref[tok, lanes]
        a = jnp.exp(dmat - m_t) * lax.dot_general(qh, kh, _NT, preferred_element_type=F32)
        c_prev = c_sc[h]
        n_prev = n_sc[h:h + 1, :]
        state = jnp.concatenate([c_prev, jnp.broadcast_to(n_prev, (DV_MLSTM, DK_MLSTM))], axis=0).astype(BF16)
        num_den = (jnp.concatenate([w_inter, w_inter], axis=1)
                   * lax.dot_general(qh, state, _NT, preferred_element_type=F32)
                   + jnp.dot(a.astype(BF16), jnp.concatenate([vh, ones], axis=1), preferred_element_type=F32))
        hh = num_den[:, :DV_MLSTM] / jnp.maximum(jnp.abs(num_den[:, DV_MLSTM:]), jnp.exp(-m_t))
        mem_ref[tok, lanes] = _mlstm_head_out(
            hh, gain_ref[:, lanes], o_ref[tok, lanes].astype(F32)).astype(mem_ref.dtype)

        m_new = m_t[L - 1:L, :]
        b_last = b[L - 1:L, :]
        g_inter = jnp.exp(b_last + m_prev - m_new)
        g_in = jnp.exp(b_last - b + ig - m_new)
        v_scaled = (vh.astype(F32) * g_in).astype(BF16)
        c_sc[h] = g_inter * c_prev + lax.dot_general(v_scaled, kh, _TN, preferred_element_type=F32)
        n_sc[h:h + 1, :] = g_inter * n_prev + jnp.sum(kh.astype(F32) * g_in, axis=0, keepdims=True)
        m_sc[h:h + 1, :] = m_new


def _prompt_mixers_kernel(pt_ref, q_lo_ref, q_hi_ref, kt_ref, vt_ref, mq_ref, mk_ref, mv_ref, mo_ref, g_ref, gain_ref,
                          qt_ref, knt_ref, ck_ref,
                          att_lo_ref, att_hi_ref, mem_ref, c_out, n_out, m_out, pe_ref, stats_ref, idx_ref,
                          kaug_ref, vaug_ref, kmt_ref, lhs_sc, m_att, acc_sc, c_sc, n_sc, m_sc, kbuf, sem,
                          *, nb, guest_rows):
    bi, i = pl.program_id(0), pl.program_id(1)
    steps = nb // 2

    @pl.when(i == 0)
    def _reset_state():
        c_sc[...] = jnp.zeros_like(c_sc)
        n_sc[...] = jnp.zeros_like(n_sc)
        m_sc[...] = jnp.zeros_like(m_sc)

    units, write_outputs = _moba_block_pair(i, nb, (q_lo_ref, q_hi_ref), kt_ref, vt_ref, (att_lo_ref, att_hi_ref),
                                            kaug_ref, vaug_ref, kmt_ref, lhs_sc, m_att, acc_sc)
    per_row = -(-len(units) // guest_rows)

    def work(r, row, slot):
        _moba_sample_scores(row, qt_ref, knt_ref, [kbuf.at[slot, p] for p in range(kbuf.shape[1])],
                            pe_ref.at[r], stats_ref.at[r], idx_ref.at[r])
        _mlstm_chunk(slice(r * MLSTM_CHUNK, (r + 1) * MLSTM_CHUNK), mq_ref, mk_ref, mv_ref, mo_ref, g_ref,
                     gain_ref, mem_ref, c_sc, n_sc, m_sc)
        for unit in units[r * per_row:(r + 1) * per_row]:
            unit()

    _guest_rows_row_ahead(bi * steps + i, pl.num_programs(0) * steps, guest_rows,
                          *_key_page_ring(pt_ref, ck_ref, kbuf, sem), work)
    write_outputs()

    @pl.when(i == steps - 1)
    def _emit_state():
        c_out[...] = c_sc[...]
        n_out[...] = n_sc[...]
        m_out[...] = m_sc[...]


def _prompt_mixers(q, kt, vt, mq, mk, mv, mo, gates, gain, qt, knt, cache_k, page_table):
    b, s, w = q.shape
    nb = s // MOBA_BLOCK
    steps = nb // 2
    tok_rows = 2 * MOBA_BLOCK
    wa, bs = qt.shape
    n_pages = page_table.shape[1]
    guest_rows = bs // (b * steps)
    assert s % tok_rows == 0 and nb <= SUBLANE and w == ATT_WIDTH == wa and mq.shape[2] == MLSTM_WIDTH
    assert MLSTM_CHUNK == LANE == DK_MLSTM == DV_MLSTM and guest_rows == tok_rows // MLSTM_CHUNK
    assert bs == guest_rows * b * steps == LANE and n_pages % PAGES_PER_BLOCK == 0
    assert MOBA_TOPK <= n_pages // PAGES_PER_BLOCK <= LANE
    ck = _cache_pages(cache_k).reshape(cache_k.shape[0], wa, PAGE_SIZE)
    q_lo = pl.BlockSpec((None, MOBA_BLOCK, w), lambda bi, i, *_: (bi, i, 0))
    q_hi = pl.BlockSpec((None, MOBA_BLOCK, w), lambda bi, i, *_: (bi, nb - 1 - i, 0))
    o_hi = pl.BlockSpec((None, MOBA_BLOCK, w), lambda bi, i, *_: (bi, steps - 1 - i, 0))
    tok = lambda width: pl.BlockSpec((None, tok_rows, width), lambda bi, i, *_: (bi, i, 0))
    seq = pl.BlockSpec((None, w, s), lambda bi, i, *_: (bi, 0, 0))
    state = lambda *dims: pl.BlockSpec((None,) + dims, lambda bi, i, *_: (bi,) + (0,) * len(dims))
    rows = lambda *dims: pl.BlockSpec((guest_rows,) + dims, lambda bi, i, *_: (bi * steps + i,) + (0,) * len(dims))
    whole = pl.BlockSpec((wa, bs), lambda *_: (0, 0))
    pair_state = lambda dt: pltpu.VMEM((2 * H_ATT, MOBA_BLOCK, LANE), dt)
    att_lo, att_hi, mem, c_p, n_p, m_p, pe, stats, ranked = pl.pallas_call(
        functools.partial(_prompt_mixers_kernel, nb=nb, guest_rows=guest_rows),
        grid_spec=pltpu.PrefetchScalarGridSpec(
            num_scalar_prefetch=1,
            grid=(b, steps),
            in_specs=[q_lo, q_hi, seq, seq, tok(MLSTM_WIDTH), tok(MLSTM_WIDTH), tok(MLSTM_WIDTH), tok(MLSTM_WIDTH),
                      tok(LANE), pl.BlockSpec((1, MLSTM_WIDTH), lambda *_: (0, 0)),
                      whole, whole, pl.BlockSpec(memory_space=pl.ANY)],
            out_specs=[q_lo, o_hi, tok(MLSTM_WIDTH), state(H_MLSTM, DV_MLSTM, DK_MLSTM), state(SUBLANE, LANE),
                       state(SUBLANE, LANE), rows(n_pages, H_ATT, LANE), rows(H_ATT, LANE), rows(H_ATT, LANE)],
            scratch_shapes=[pltpu.VMEM((H_ATT, nb, LANE, MOBA_BLOCK), BF16),
                            pltpu.VMEM((H_ATT, nb, LANE, MOBA_BLOCK), BF16),
                            pltpu.VMEM((2, w, LANE), BF16),
                            pair_state(BF16), pair_state(F32), pair_state(F32),
                            pltpu.VMEM((H_MLSTM, DV_MLSTM, DK_MLSTM), F32),
                            pltpu.VMEM((SUBLANE, LANE), F32),
                            pltpu.VMEM((SUBLANE, LANE), F32),
                            pltpu.VMEM((2, n_pages, wa, PAGE_SIZE), F32),
                            pltpu.SemaphoreType.DMA((2,))],
        ),
        out_shape=[jax.ShapeDtypeStruct((b, s // 2, w), BF16),
                   jax.ShapeDtypeStruct((b, s // 2, w), BF16),
                   jax.ShapeDtypeStruct((b, s, MLSTM_WIDTH), BF16),
                   jax.ShapeDtypeStruct((b, H_MLSTM, DV_MLSTM, DK_MLSTM), F32),
                   jax.ShapeDtypeStruct((b, SUBLANE, LANE), F32),
                   jax.ShapeDtypeStruct((b, SUBLANE, LANE), F32),
                   jax.ShapeDtypeStruct((bs, n_pages, H_ATT, LANE), F32),
                   jax.ShapeDtypeStruct((bs, H_ATT, LANE), F32),
                   jax.ShapeDtypeStruct((bs, H_ATT, LANE), jnp.int32)],
        compiler_params=_params("arbitrary", "arbitrary"),
    )(page_table, q, q, kt, vt, mq, mk, mv, mo, gates, gain, qt, knt, ck)
    return (att_lo, att_hi), mem, c_p, n_p, m_p, pe, stats, ranked


def _head_sublane(h):
    return (H_ATT // 2 - 1 - h) if h < H_ATT // 2 else (H_ATT + H_ATT // 2 - 1 - h)


def _head_rows(x):
    parts = []
    for h in range(H_ATT):
        tiles = [x[h * DH_ATT + SUBLANE * t:h * DH_ATT + SUBLANE * (t + 1), :] for t in range(DH_ATT // SUBLANE)]
        parts.append(sum(tiles[1:], tiles[0]))
    sub = lax.broadcasted_iota(jnp.int32, parts[0].shape, 0)
    folded = [p + pltpu.roll(p, 4, 0) for p in parts]
    quads = [jnp.where(sub < 4, folded[i], folded[i + 4]) for i in range(4)]
    take_up = (sub & 2) != 0
    pairs = [jnp.where(take_up, quads[i] + pltpu.roll(quads[i], 2, 0),
                       quads[i + 2] + pltpu.roll(quads[i + 2], 6, 0)) for i in range(2)]
    return jnp.where((sub & 1) != 0, pairs[0] + pltpu.roll(pairs[0], 1, 0), pairs[1] + pltpu.roll(pairs[1], 7, 0))


def _moba_sample_scores(b, qt_ref, knt_ref, kp_refs, pe_ref, stats_ref, idx_ref):
    n_pages = len(kp_refs)
    n_blocks = n_pages // PAGES_PER_BLOCK
    w = qt_ref.shape[0]
    on_b = lax.broadcasted_iota(jnp.int32, (w, LANE), 1) == b

    def column(ref):
        return jnp.sum(jnp.where(on_b, ref[...], 0.0), axis=1, keepdims=True)

    q_col = column(qt_ref) * (DH_ATT ** -0.5)
    q_wide = jnp.broadcast_to(q_col, (w, LANE))
    s_own = _head_rows(jnp.broadcast_to(q_col * column(knt_ref), (w, LANE)))[:, 0:1]
    s_pages = [_head_rows(kp_refs[p][...] * q_wide) for p in range(n_pages)]

    blk = [jnp.sum(sum(s_pages[n * PAGES_PER_BLOCK + 1:(n + 1) * PAGES_PER_BLOCK], s_pages[n * PAGES_PER_BLOCK]),
                   axis=1, keepdims=True) for n in range(n_blocks)]
    lane = lax.broadcasted_iota(jnp.int32, (H_ATT, LANE), 1)
    sel, ranked = [], jnp.zeros((H_ATT, LANE), jnp.int32)
    for n in range(n_blocks):
        rank = jnp.zeros((H_ATT, 1), jnp.int32)
        for o in range(n_blocks):
            if o != n:
                beats = (blk[o] >= blk[n]) if o < n else (blk[o] > blk[n])
                rank = rank + jnp.where(beats, 1, 0)
        sel.append(rank < MOBA_TOPK)
        ranked = jnp.where(rank == lane, n, ranked)
    m = s_own
    for p in range(n_pages):
        page_max = jnp.max(s_pages[p], axis=1, keepdims=True)
        m = jnp.maximum(m, jnp.where(sel[p // PAGES_PER_BLOCK], page_max, NEG))
    p_own = jnp.exp(s_own - m)
    total = jnp.zeros((H_ATT, LANE), F32)
    for p in range(n_pages):
        pe = jnp.where(sel[p // PAGES_PER_BLOCK], jnp.exp(s_pages[p] - m), 0.0)
        pe_ref[p] = pe
        total = total + pe
    row_sum = p_own + jnp.sum(total, axis=1, keepdims=True)
    stats_ref[...] = jnp.where(lane == 0, p_own, row_sum)
    idx_ref[...] = ranked


N_VALUE_CHUNKS = MOBA_TOPK * PAGES_PER_BLOCK


def _value_chunk_page(sel_ref, row, h, c):
    return sel_ref[row, h * MOBA_TOPK + c // PAGES_PER_BLOCK] * PAGES_PER_BLOCK + c % PAGES_PER_BLOCK


def _moba_sample_mix_row(row, sel_ref, pe_ref, stats_ref, vnt_ref, chunks_ref, o_ref):
    w = vnt_ref.shape[0]
    on_row = lax.broadcasted_iota(jnp.int32, (w, LANE), 1) == row
    vn_col = jnp.sum(jnp.where(on_row, vnt_ref[...], 0.0), axis=1, keepdims=True)
    stats = stats_ref[...]
    out_cols = []
    for h in range(H_ATT):
        r = _head_sublane(h)
        acc = jnp.zeros((DH_ATT, LANE), F32)
        for c in range(N_VALUE_CHUNKS):
            acc = acc + (pe_ref[_value_chunk_page(sel_ref, row, h, c), r:r + 1, :]
                         * chunks_ref[h * N_VALUE_CHUNKS + c])
        p_own, row_sum = stats[r:r + 1, 0:1], stats[r:r + 1, 1:2]
        rows = slice(h * DH_ATT, (h + 1) * DH_ATT)
        out_cols.append((jnp.sum(acc, axis=1, keepdims=True) + p_own * vn_col[rows, :]) / row_sum)
    o_ref[...] = jnp.where(on_row, jnp.concatenate(out_cols, axis=0), o_ref[...])


def _cache_pages(cache):
    return jnp.transpose(cache, (0, 2, 3, 1))


def _selected_blocks(ranked):
    b = ranked.shape[0]
    return jnp.stack([ranked[:, _head_sublane(h), :MOBA_TOPK] for h in range(H_ATT)], axis=1).reshape(b, -1)


def _mlstm_sample_kernel(q_ref, k_ref, v_ref, o_ref, g_ref, gain_ref, c0_ref, n0_ref, m0_ref,
                         mem_ref, c_ref, n_ref, m_ref):
    tb = q_ref.shape[0]
    g = g_ref[...]
    sub = lax.broadcasted_iota(jnp.int32, (2 * tb, LANE), 0)
    zrows = jnp.zeros((tb, LANE), F32)
    for h in range(H_MLSTM):
        lanes = slice(h * DK_MLSTM, (h + 1) * DK_MLSTM)
        ig, lf, m0 = g[:, h:h + 1], g[:, H_MLSTM + h:H_MLSTM + h + 1], m0_ref[:, h:h + 1]
        q, k, v = q_ref[:, lanes], k_ref[:, lanes], v_ref[:, lanes]
        n0 = n0_ref[:, lanes]
        m_t = jnp.maximum(lf + m0, ig)
        w_inter = jnp.exp(lf + m0 - m_t)
        g_in = jnp.exp(ig - m_t)
        a = g_in * jnp.sum(q * k, axis=1, keepdims=True)
        den = w_inter * jnp.sum(n0 * q, axis=1, keepdims=True) + a
        q_b = q.astype(BF16)
        gv = jnp.concatenate([g_in * v, zrows], axis=0)
        k_b = jnp.concatenate([k, zrows], axis=0).astype(BF16)
        cq_rows = []
        for r in range(tb):
            c_prev = c0_ref[r, h]
            cq_rows.append(lax.dot_general(q_b, c_prev.astype(BF16), _NT, preferred_element_type=F32)[r:r + 1, :])
            outer = lax.dot_general(jnp.where(sub == r, gv, 0.0).astype(BF16), k_b, _TN,
                                    preferred_element_type=F32)
            c_ref[r, h] = w_inter[r:r + 1, :] * c_prev + outer
        cq = jnp.concatenate(cq_rows, axis=0)
        hh = (w_inter * cq + a * v) / jnp.maximum(jnp.abs(den), jnp.exp(-m_t))
        mem_ref[:, lanes] = _mlstm_head_out(hh, gain_ref[:, lanes], o_ref[:, lanes]).astype(mem_ref.dtype)
        n_ref[:, lanes] = w_inter * n0 + g_in * k
        m_ref[:, h:h + 1] = m_t


def _mlstm_sample(mq, mk, mv, mo, gates, gain, c0, n0, m0):
    b, w = mq.shape
    tb = SAMPLE_STATE_TILE
    rows = lambda width: pl.BlockSpec((tb, width), lambda i: (i, 0))
    c_spec = pl.BlockSpec((tb, H_MLSTM, DV_MLSTM, DK_MLSTM), lambda i: (i, 0, 0, 0))
    return pl.pallas_call(
        _mlstm_sample_kernel,
        grid=(b // tb,),
        in_specs=[rows(w), rows(w), rows(w), rows(w), rows(LANE), pl.BlockSpec((1, w), lambda i: (0, 0)),
                  c_spec, rows(w), rows(H_MLSTM)],
        out_specs=[rows(w), c_spec, rows(w), rows(H_MLSTM)],
        out_shape=[jax.ShapeDtypeStruct((b, w), F32),
                   jax.ShapeDtypeStruct(c0.shape, F32),
                   jax.ShapeDtypeStruct((b, w), F32),
                   jax.ShapeDtypeStruct((b, H_MLSTM), F32)],
        compiler_params=_params("arbitrary"),
    )(mq, mk, mv, mo, gates, gain, c0, n0.reshape(b, w), m0)


N_FINISH_INPUTS = 15


def _finish_kernel(*refs, alpha, ff_chunk, att_transposed, guest_rows, seq_tiles):
    if guest_rows:
        pt_ref, sel_ref, refs = refs[0], refs[1], refs[2:]
    if seq_tiles:
        att_hi_ref, refs = refs[2], refs[:2] + refs[3:]
    (x_ref, att_ref, mem_ref, g1_ref, sh2_ref, sc2_ref, g2_ref, wo_ref, ln1g_ref, ln1b_ref,
     wg_ref, wu_ref, wd_ref, ln2g_ref, ln2b_ref) = refs[:N_FINISH_INPUTS]
    if guest_rows:
        pe_ref, stats_ref, vnt_ref, cv_ref, y_ref, o_ref, vbuf, sem = refs[N_FINISH_INPUTS:]
    else:
        (y_ref,) = refs[N_FINISH_INPUTS:]

    if seq_tiles:
        in_upper_half = (pl.program_id(0) % seq_tiles) >= seq_tiles // 2
        att = jnp.where(in_upper_half, att_hi_ref[...], att_ref[...])
    else:
        att = att_ref[...].T if att_transposed else att_ref[...]
    aw = att.shape[1]
    mix = (jnp.dot(att.astype(BF16), wo_ref[0:aw, :], preferred_element_type=F32)
           + jnp.dot(mem_ref[...].astype(BF16), wo_ref[aw:, :], preferred_element_type=F32))
    x1 = _layernorm(alpha * x_ref[...] + (1.0 + g1_ref[...]) * mix, ln1g_ref[...], ln1b_ref[...])
    h2 = (x1 * (1.0 + sc2_ref[...]) + sh2_ref[...]).astype(BF16)
    n_ff = wg_ref.shape[1] // ff_chunk

    def ffn_chunk(c):
        cols = slice(c * ff_chunk, (c + 1) * ff_chunk)
        gate = jnp.dot(h2, wg_ref[:, cols], preferred_element_type=F32)
        up = jnp.dot(h2, wu_ref[:, cols], preferred_element_type=F32)
        act = (gate * jax.nn.sigmoid(gate) * up).astype(BF16)
        return jnp.dot(act, wd_ref[cols, :], preferred_element_type=F32)

    ffn = [jnp.zeros(x1.shape, F32)]
    if not guest_rows:
        for c in range(n_ff):
            ffn[0] = ffn[0] + ffn_chunk(c)
    else:
        i = pl.program_id(0)
        per_row = -(-n_ff // guest_rows)

        @pl.when(i == 0)
        def _init_out():
            o_ref[...] = jnp.zeros_like(o_ref)

        def chunk_copy(row, slot, h, c):
            page = pt_ref[row, _value_chunk_page(sel_ref, row, h, c)]
            return pltpu.make_async_copy(cv_ref.at[page, h], vbuf.at[slot, h * N_VALUE_CHUNKS + c], sem.at[slot])

        def start_row(row, slot):
            for h in range(H_ATT):
                for c in range(N_VALUE_CHUNKS):
                    chunk_copy(row, slot, h, c).start()

        def wait_row(row, slot):
            for h in range(H_ATT):
                for c in range(N_VALUE_CHUNKS):
                    chunk_copy(row, slot, h, c).wait()

        def work(r, row, slot):
            _moba_sample_mix_row(row, sel_ref, pe_ref.at[r], stats_ref.at[r], vnt_ref, vbuf.at[slot], o_ref)
            for c in range(r * per_row, min((r + 1) * per_row, n_ff)):
                ffn[0] = ffn[0] + ffn_chunk(c)

        _guest_rows_row_ahead(i, pl.num_programs(0), guest_rows, start_row, wait_row, work)
    y_ref[...] = _layernorm(alpha * x1 + (1.0 + g2_ref[...]) * ffn[0], ln2g_ref[...], ln2b_ref[...])


def _finish(x, att, mem, mods, weights, tm, rows_per_mod, alpha, att_transposed=False, guest=None):
    t, d = x.shape
    assert not att_transposed or tm == t
    steps = t // tm
    w_out, ln1_g, ln1_b, w_gate, w_up, w_down, ln2_g, ln2_b = weights
    if rows_per_mod is None:
        mod_spec = pl.BlockSpec((tm, d), lambda i, *_: (i, 0))
    else:
        per = rows_per_mod // tm
        mod_spec = pl.BlockSpec((None, 1, d), lambda i, *_: (i // per, 0, 0))
    tok = lambda width: pl.BlockSpec((tm, width), lambda i, *_: (i, 0))
    ff_chunk = FF_CHUNK
    assert w_gate.shape[1] % ff_chunk == 0
    seq_tiles = 0
    if isinstance(att, tuple):
        seq_tiles = rows_per_mod // tm
        half = seq_tiles // 2
        assert not att_transposed and seq_tiles % 2 == 0
        att_specs = [pl.BlockSpec((tm, att[0].shape[1]),
                                  lambda i, *_: ((i // seq_tiles) * half + jnp.minimum(i % seq_tiles, half - 1), 0)),
                     pl.BlockSpec((tm, att[1].shape[1]),
                                  lambda i, *_: ((i // seq_tiles) * half + jnp.maximum(i % seq_tiles - half, 0), 0))]
        att_args = att
    else:
        att_specs = [pl.BlockSpec(att.shape, lambda *_: (0, 0)) if att_transposed else tok(att.shape[1])]
        att_args = (att,)
    in_specs = [tok(d), *att_specs, tok(mem.shape[1]), mod_spec, mod_spec, mod_spec, mod_spec,
                _const_spec(w_out.shape), _const_spec(ln1_g.shape), _const_spec(ln1_b.shape),
                _const_spec(w_gate.shape), _const_spec(w_up.shape), _const_spec(w_down.shape),
                _const_spec(ln2_g.shape), _const_spec(ln2_b.shape)]
    args = (x, *att_args, mem, *mods, w_out, ln1_g, ln1_b, w_gate, w_up, w_down, ln2_g, ln2_b)
    assert len(in_specs) == N_FINISH_INPUTS + len(att_args) - 1
    body = functools.partial(_finish_kernel, alpha=alpha, ff_chunk=ff_chunk, att_transposed=att_transposed,
                             guest_rows=0, seq_tiles=seq_tiles)
    y_shape = jax.ShapeDtypeStruct((t, d), F32)
    if guest is None:
        return pl.pallas_call(body, grid=(steps,), in_specs=in_specs, out_specs=tok(d), out_shape=y_shape,
                              compiler_params=_params("arbitrary"))(*args)

    pe, stats, sel, vnt, cache_v, page_table = guest
    w, bs = vnt.shape
    n_pages = page_table.shape[1]
    guest_rows = bs // steps
    assert bs % steps == 0 and bs == LANE
    rows = lambda *dims: pl.BlockSpec((guest_rows,) + dims, lambda i, *_: (i,) + (0,) * len(dims))
    whole = pl.BlockSpec((w, bs), lambda *_: (0, 0))
    return pl.pallas_call(
        functools.partial(body, guest_rows=guest_rows),
        grid_spec=pltpu.PrefetchScalarGridSpec(
            num_scalar_prefetch=2,
            grid=(steps,),
            in_specs=in_specs + [rows(n_pages, H_ATT, LANE), rows(H_ATT, LANE), whole,
                                 pl.BlockSpec(memory_space=pl.ANY)],
            out_specs=[tok(d), whole],
            scratch_shapes=[pltpu.VMEM((2, H_ATT * N_VALUE_CHUNKS, DH_ATT, PAGE_SIZE), F32),
                            pltpu.SemaphoreType.DMA((2,))],
        ),
        out_shape=[y_shape, jax.ShapeDtypeStruct((w, bs), F32)],
        compiler_params=_params("arbitrary"),
    )(page_table, sel, *args, pe, stats, vnt, _cache_pages(cache_v))


def kernel(x_prompt, x_sample, cache_k, cache_v, state_C, state_n, state_m, page_table, c_prompt, c_sample,
           w_ada, b_ada, w_in, b_if, mlstm_norm_g, w_out, ln1_g, ln1_b, w_gate, w_up, w_down, ln2_g, ln2_b):
    depth = w_in.shape[0]
    assert depth == 1, "single-layer step"
    alpha = (2.0 * depth) ** 0.25
    bp, s, d = x_prompt.shape
    bs = x_sample.shape[0]
    assert x_sample.shape[1] == 1, "single-token decode step"

    w_main, w_att_t, w_gates = _proj_weights(w_in[0].T)
    b_gates = jnp.pad(b_if[0], (0, LANE - 2 * H_MLSTM)).reshape(1, LANE)
    gain = mlstm_norm_g[0].reshape(1, MLSTM_WIDTH)
    row = lambda a: a[0].reshape(1, -1)
    fin_w = (w_out[0].astype(BF16), row(ln1_g), row(ln1_b), w_gate[0].astype(BF16), w_up[0].astype(BF16),
             w_down[0].astype(BF16), row(ln2_g), row(ln2_b))

    c_all = jnp.concatenate([c_prompt, c_sample], axis=0)
    mod = _adaln(c_all, w_ada[0], b_ada[0])
    sh1, sc1, g1, sh2, sc2, g2 = (mod[:, i * d:(i + 1) * d] for i in range(6))
    pm = lambda a: a[:bp].reshape(bp, 1, d)
    sm = lambda a: a[bp:]

    xs = x_sample.reshape(bs, d)
    aq_s, ak_s, av_s, mq_s, mk_s, mv_s, mo_s, gates_s = _in_proj(
        xs, sm(sc1), sm(sh1), w_main, w_att_t, w_gates, b_gates, bs, None, (F32,) * N_PROJ_GROUPS,
        transposed=(0, 1, 2))

    xp = x_prompt.reshape(bp * s, d)
    tm = PROMPT_TOKEN_TILE
    aq, ak_t, av_t, mq, mk, mv, mo, gates = _in_proj(
        xp, pm(sc1), pm(sh1), w_main, w_att_t, w_gates, b_gates, tm, s,
        (F32, F32, F32, BF16, BF16, BF16, F32), transposed=(1, 2))
    seq = lambda a: a.reshape(bp, s, a.shape[-1])
    att, mem, c_p, n_p, m_p, pe, stats, ranked = _prompt_mixers(
        seq(aq), ak_t, av_t, seq(mq), seq(mk), seq(mv), seq(mo), seq(gates), gain,
        aq_s, ak_s, cache_k[0], page_table)
    y_p, att_s = _finish(
        xp, tuple(a.reshape(bp * s // 2, -1) for a in att), mem.reshape(bp * s, -1),
        (pm(g1), pm(sh2), pm(sc2), pm(g2)),
        fin_w, tm, s, alpha, guest=(pe, stats, _selected_blocks(ranked), av_s, cache_v[0], page_table))

    mem_s, c_s, n_s, m_s = _mlstm_sample(mq_s, mk_s, mv_s, mo_s, gates_s, gain,
                                         state_C[0], state_n[0], state_m[0])
    y_s = _finish(xs, att_s, mem_s, (sm(g1), sm(sh2), sm(sc2), sm(g2)), fin_w, bs, None, alpha,
                  att_transposed=True)

    rows_p = lambda a: jnp.transpose(a.reshape(bp, H_ATT, DH_ATT, s), (0, 3, 1, 2))[None]
    rows_s = lambda a: jnp.transpose(a.reshape(H_ATT, DH_ATT, bs), (2, 0, 1)).reshape(1, bs, 1, H_ATT, DH_ATT)
    return (y_p.reshape(bp, s, d), y_s.reshape(bs, 1, d),
            rows_p(ak_t), rows_p(av_t),
            c_p[None], n_p[None, :, :H_MLSTM, :], m_p[None, :, :H_MLSTM, 0],
            rows_s(ak_s), rows_s(av_s),
            c_s[None], n_s.reshape(1, bs, H_MLSTM, DK_MLSTM), m_s[None])
```

```python
import functools
import math

import jax
import jax.numpy as jnp
from jax import lax
from jax.experimental import pallas as pl
from jax.experimental.pallas import tpu as pltpu

F32 = jnp.float32
BF16 = jnp.bfloat16

LANE = 128
SUBLANE = 8
VMEM_LIMIT_BYTES = 56 * 1024 * 1024

H_ATT = 8
DH_ATT = 64
ATT_WIDTH = H_ATT * DH_ATT
MOBA_BLOCK = 256
MOBA_TOPK = 3
H_MLSTM = 4
DK_MLSTM = 128
DV_MLSTM = 128
MLSTM_WIDTH = H_MLSTM * DV_MLSTM
MLSTM_CHUNK = LANE
PAGE_SIZE = 128
PAGES_PER_BLOCK = MOBA_BLOCK // PAGE_SIZE
LN_EPS = 1e-5
NEG = -1e30
LOG2E = math.log2(math.e)
N_PROJ_GROUPS = 7
PROJ_GROUP = 512
MK_GROUP = 4
PROMPT_TOKEN_TILE = 512
FF_CHUNK = 2 * LANE
SAMPLE_STATE_TILE = SUBLANE

_NT = (((1,), (1,)), ((), ()))
_TN = (((0,), (0,)), ((), ()))


def _params(*sem):
    return pltpu.CompilerParams(dimension_semantics=sem, vmem_limit_bytes=VMEM_LIMIT_BYTES)


def _const_spec(shape):
    return pl.BlockSpec(shape, lambda *_: (0,) * len(shape), pipeline_mode=pl.Buffered(1))


def _layernorm(x, g, b):
    mu = jnp.mean(x, axis=-1, keepdims=True)
    d = x - mu
    var = jnp.mean(d * d, axis=-1, keepdims=True)
    return d * lax.rsqrt(var + LN_EPS) * g + b


def _top_blocks(val, nidx):
    cnt = jnp.zeros(val.shape, jnp.int32)
    for r in range(1, SUBLANE):
        other = pltpu.roll(val, r, 0)
        oidx = pltpu.roll(nidx, r, 0)
        beats = (other > val) | ((other == val) & (oidx < nidx))
        cnt = cnt + jnp.where(beats, 1, 0)
    return cnt < MOBA_TOPK


def _adaln_kernel(c_ref, w_ref, b_ref, o_ref):
    c = c_ref[...]
    s = c * jax.nn.sigmoid(c)
    o_ref[...] = jnp.dot(s, w_ref[...], preferred_element_type=F32) + b_ref[...]


def _adaln(c, w_ada, b_ada):
    rows, d = c.shape
    n = w_ada.shape[1]
    tn = d
    return pl.pallas_call(
        _adaln_kernel,
        grid=(n // tn,),
        in_specs=[pl.BlockSpec((rows, d), lambda j: (0, 0)),
                  pl.BlockSpec((d, tn), lambda j: (0, j)),
                  pl.BlockSpec((1, tn), lambda j: (0, j))],
        out_specs=pl.BlockSpec((rows, tn), lambda j: (0, j)),
        out_shape=jax.ShapeDtypeStruct((rows, n), F32),
        compiler_params=_params("arbitrary"),
    )(c, w_ada, b_ada.reshape(1, n))


N_ATT_GROUPS = 3


def _proj_weights_kernel(wt_ref, gate_rows_ref, main_ref, att_t_ref, gates_ref):
    g = pl.program_id(0)
    group_t = wt_ref[...]
    main_ref[...] = group_t.T.astype(BF16)

    @pl.when(g < N_ATT_GROUPS)
    def _attention_group():
        att_t_ref[...] = group_t.astype(BF16)

    @pl.when(g == 0)
    def _gate_columns():
        rows = gate_rows_ref[...]
        pad = jnp.zeros((LANE - rows.shape[0], rows.shape[1]), F32)
        gates_ref[...] = jnp.concatenate([rows, pad], axis=0).T.astype(BF16)


def _proj_weights(w_in_t):
    cols, d = w_in_t.shape
    n_main = N_PROJ_GROUPS * PROJ_GROUP
    n_gate = cols - n_main
    assert n_gate == 2 * H_MLSTM == SUBLANE
    return pl.pallas_call(
        _proj_weights_kernel,
        grid=(N_PROJ_GROUPS,),
        in_specs=[pl.BlockSpec((PROJ_GROUP, d), lambda g: (g, 0)),
                  pl.BlockSpec((n_gate, d), lambda g: (n_main // n_gate, 0))],
        out_specs=[pl.BlockSpec((d, PROJ_GROUP), lambda g: (0, g)),
                   pl.BlockSpec((PROJ_GROUP, d), lambda g: (jnp.minimum(g, N_ATT_GROUPS - 1), 0)),
                   pl.BlockSpec((d, LANE), lambda g: (0, 0))],
        out_shape=[jax.ShapeDtypeStruct((d, n_main), BF16),
                   jax.ShapeDtypeStruct((N_ATT_GROUPS * PROJ_GROUP, d), BF16),
                   jax.ShapeDtypeStruct((d, LANE), BF16)],
        compiler_params=_params("arbitrary"),
    )(w_in_t, w_in_t)


def _guest_rows_row_ahead(i, n_steps, guest_rows, start_row, wait_row, work):
    assert guest_rows % 2 == 0

    @pl.when(i == 0)
    def _first_row():
        start_row(0, 0)

    for r in range(guest_rows):
        row, slot = i * guest_rows + r, r % 2
        if r + 1 < guest_rows:
            start_row(row + 1, 1 - slot)
        else:
            @pl.when(i + 1 < n_steps)
            def _next_step_row():
                start_row(row + 1, 1 - slot)
        wait_row(row, slot)
        work(r, row, slot)


def _key_page_ring(pt_ref, ck_ref, kbuf, sem):
    n_pages = kbuf.shape[1]

    def page_copy(row, slot, p):
        return pltpu.make_async_copy(ck_ref.at[pt_ref[row, p]], kbuf.at[slot, p], sem.at[slot])

    def start_row(row, slot):
        for p in range(n_pages):
            page_copy(row, slot, p).start()

    def wait_row(row, slot):
        for p in range(n_pages):
            page_copy(row, slot, p).wait()

    return start_row, wait_row


N_STATE_GUEST_INPUTS = 9


def _in_proj_kernel(x_ref, sc_ref, sh_ref, w_ref, wt_ref, wg_ref, bg_ref, *refs, transposed, state_guest):
    if state_guest:
        guest_in, refs = refs[:N_STATE_GUEST_INPUTS], refs[N_STATE_GUEST_INPUTS:]
        _mlstm_sample_tile(*guest_in, *refs[N_PROJ_GROUPS + 1:])
    out_refs = refs[:N_PROJ_GROUPS + 1]
    proj_refs, g_ref = out_refs[:N_PROJ_GROUPS], out_refs[N_PROJ_GROUPS]
    h = (x_ref[...] * (1.0 + sc_ref[...]) + sh_ref[...]).astype(BF16)
    for gi, o_ref in enumerate(proj_refs):
        cols = slice(gi * PROJ_GROUP, (gi + 1) * PROJ_GROUP)
        if gi in transposed:
            y = lax.dot_general(wt_ref[cols, :], h, _NT, preferred_element_type=F32)
        else:
            y = jnp.dot(h, w_ref[:, cols], preferred_element_type=F32)
        if gi == MK_GROUP:
            y = y * (DK_MLSTM ** -0.5)
        o_ref[...] = y.astype(o_ref.dtype)
    g = jnp.dot(h, wg_ref[...], preferred_element_type=F32) + bg_ref[...]
    lane = lax.broadcasted_iota(jnp.int32, g.shape, 1)
    logsig = jnp.minimum(g, 0.0) - jnp.log1p(jnp.exp(-jnp.abs(g)))
    g_ref[...] = jnp.where(lane >= H_MLSTM, logsig, g)


def _in_proj(x, sc, sh, w_main, w_att_t, w_gate, b_gate, tm, rows_per_mod, out_dtypes, transposed=(),
             state_guest=None):
    t, d = x.shape
    steps = t // tm
    assert MK_GROUP not in transposed
    if rows_per_mod is None:
        mod_spec = pl.BlockSpec((tm, d), lambda i, *_: (i, 0))
        t_shape, t_spec = (PROJ_GROUP, t), pl.BlockSpec((PROJ_GROUP, tm), lambda i, *_: (0, i))
    else:
        per = rows_per_mod // tm
        mod_spec = pl.BlockSpec((None, 1, d), lambda i, *_: (i // per, 0, 0))
        t_shape = (t // rows_per_mod, PROJ_GROUP, rows_per_mod)
        t_spec = pl.BlockSpec((None, PROJ_GROUP, tm), lambda i, *_: (i // per, 0, i % per))
    out_shape, out_specs = [], []
    for gi, dt in enumerate(out_dtypes):
        if gi in transposed:
            out_shape.append(jax.ShapeDtypeStruct(t_shape, dt))
            out_specs.append(t_spec)
        else:
            out_shape.append(jax.ShapeDtypeStruct((t, PROJ_GROUP), dt))
            out_specs.append(pl.BlockSpec((tm, PROJ_GROUP), lambda i, *_: (i, 0)))
    out_shape.append(jax.ShapeDtypeStruct((t, LANE), F32))
    out_specs.append(pl.BlockSpec((tm, LANE), lambda i, *_: (i, 0)))
    in_specs = [pl.BlockSpec((tm, d), lambda i, *_: (i, 0)), mod_spec, mod_spec,
                _const_spec(w_main.shape), _const_spec(w_att_t.shape),
                _const_spec(w_gate.shape), _const_spec(b_gate.shape)]
    args = (x, sc, sh, w_main, w_att_t, w_gate, b_gate)
    if state_guest is not None:
        mq, mk, mv, mo, gates, gain, c0, n0, m0 = state_guest
        b, w = mq.shape
        tb = b // steps
        assert b % steps == 0 and tb % SUBLANE == 0
        rows = lambda width: pl.BlockSpec((tb, width), lambda i, *_: (i, 0))
        c_spec = pl.BlockSpec((tb,) + c0.shape[1:], lambda i, *_: (i, 0, 0, 0))
        in_specs += [rows(w), rows(w), rows(w), rows(w), rows(LANE), pl.BlockSpec((1, w), lambda *_: (0, 0)),
                     c_spec, rows(w), rows(H_MLSTM)]
        assert len(in_specs) == len(args) + N_STATE_GUEST_INPUTS
        args += (mq, mk, mv, mo, gates, gain, c0, n0, m0)
        out_specs += [rows(w), c_spec, rows(w), rows(H_MLSTM)]
        out_shape += [jax.ShapeDtypeStruct((b, w), F32), jax.ShapeDtypeStruct(c0.shape, F32),
                      jax.ShapeDtypeStruct((b, w), F32), jax.ShapeDtypeStruct((b, H_MLSTM), F32)]
    return pl.pallas_call(
        functools.partial(_in_proj_kernel, transposed=tuple(transposed), state_guest=state_guest is not None),
        grid=(steps,), in_specs=in_specs, out_specs=out_specs, out_shape=out_shape,
        compiler_params=_params("arbitrary"),
    )(*args)


def _moba_block_pair(i, nb, q_refs, kt_ref, vt_ref, o_refs, kaug_ref, vaug_ref, kmt_ref, lhs_sc, m_sc, acc_sc):
    blk = MOBA_BLOCK
    half = LANE // 2
    w = q_refs[0].shape[1]
    own = (i, nb - 1 - i)
    assert nb // 2 - 1 <= MOBA_TOPK

    @pl.when(i == 0)
    def _prepare_batch():
        srow = lax.broadcasted_iota(jnp.int32, (LANE, blk), 0)
        in_lo = srow < half
        head_of_row = lax.broadcasted_iota(jnp.int32, (w, LANE), 0) // DH_ATT
        lane_w = lax.broadcasted_iota(jnp.int32, (w, LANE), 1)
        kmt = jnp.zeros((w, LANE), F32)
        for j in range(nb):
            ktj = kt_ref[:, j * blk:(j + 1) * blk]
            vtj = vt_ref[:, j * blk:(j + 1) * blk]
            col = jnp.mean(ktj, axis=1, keepdims=True)
            kmt = jnp.where((lane_w % SUBLANE == j) & (lane_w // SUBLANE == head_of_row), col, kmt)
            for p in range(H_ATT // 2):
                kp, vp = ktj[p * LANE:(p + 1) * LANE, :], vtj[p * LANE:(p + 1) * LANE, :]
                kaug_ref[2 * p, j] = jnp.where(in_lo, kp, jnp.where(srow == half + j, 1.0, 0.0)).astype(BF16)
                kaug_ref[2 * p + 1, j] = jnp.where(in_lo, jnp.where(srow == j, 1.0, 0.0), kp).astype(BF16)
                vaug_ref[2 * p, j] = jnp.where(in_lo, vp, 1.0).astype(BF16)
                vaug_ref[2 * p + 1, j] = jnp.where(in_lo, 1.0, vp).astype(BF16)
        km_hi = kmt.astype(BF16)
        kmt_ref[0] = km_hi
        kmt_ref[1] = (kmt - km_hi.astype(F32)).astype(BF16)

    lane = lax.broadcasted_iota(jnp.int32, (blk, LANE), 1)
    lo_lanes = lane < half

    def store_lhs(side, p, bias_p):
        qp = q_refs[side][:, p * LANE:(p + 1) * LANE] * (DH_ATT ** -0.5 * LOG2E)
        lhs_sc[side * H_ATT + 2 * p] = jnp.where(lo_lanes, qp, bias_p).astype(BF16)
        lhs_sc[side * H_ATT + 2 * p + 1] = jnp.where(lo_lanes, bias_p, qp).astype(BF16)

    block_of_lane = lane % half
    bias_lo = jnp.where((block_of_lane < SUBLANE) & (block_of_lane > own[0]), NEG, 0.0)
    for p in range(H_ATT // 2):
        store_lhs(0, p, bias_lo)

    q32 = q_refs[1][...]
    q_hi = q32.astype(BF16)
    q_lo = (q32 - q_hi.astype(F32)).astype(BF16)
    sc = (jnp.dot(q_hi, kmt_ref[0], preferred_element_type=F32)
          + (jnp.dot(q_hi, kmt_ref[1], preferred_element_type=F32)
             + jnp.dot(q_lo, kmt_ref[0], preferred_element_type=F32)))
    sc_t = sc.T
    nidx = lax.broadcasted_iota(jnp.int32, (SUBLANE, blk), 0)
    past = nidx < own[1]
    biases = []
    for h in range(H_ATT):
        val = jnp.where(past, sc_t[h * SUBLANE:(h + 1) * SUBLANE, :], NEG)
        keep = (_top_blocks(val, nidx) & past) | (nidx == own[1])
        biases.append(jnp.where(keep, 0.0, NEG))
    zpad = jnp.zeros((half - SUBLANE, blk), F32)
    for p in range(H_ATT // 2):
        store_lhs(1, p, jnp.concatenate([biases[2 * p + 1], zpad, biases[2 * p], zpad], axis=0).T)

    def scores(idx, h, j):
        return jnp.dot(lhs_sc[idx], kaug_ref[h, j], preferred_element_type=F32)

    def row_max(s):
        return jnp.broadcast_to(jnp.max(s, axis=1, keepdims=True), (blk, LANE))

    def weights(s, m):
        return jnp.exp2(s - jnp.concatenate([m, m], axis=1)).astype(BF16)

    row = lax.broadcasted_iota(jnp.int32, (blk, blk), 0)
    col = lax.broadcasted_iota(jnp.int32, (blk, blk), 1)
    causal = col <= row

    def own_block(side):
        for h in range(H_ATT):
            idx = side * H_ATT + h
            s = jnp.where(causal, scores(idx, h, own[side]), NEG)
            m = row_max(s)
            acc_sc[idx] = lax.dot_general(weights(s, m), vaug_ref[h, own[side]], _NT, preferred_element_type=F32)
            m_sc[idx] = m

    def past_block(u):
        side = jnp.where(u >= own[0], 1, 0)
        j = u - side * own[0]
        for h in range(H_ATT):
            idx = side * H_ATT + h
            s = scores(idx, h, j)
            m_old = m_sc[idx]
            m_new = jnp.maximum(m_old, row_max(s))
            acc_sc[idx] = (jnp.exp2(m_old - m_new) * acc_sc[idx]
                           + lax.dot_general(weights(s, m_new), vaug_ref[h, j], _NT, preferred_element_type=F32))
            m_sc[idx] = m_new

    units = [functools.partial(own_block, 0), functools.partial(own_block, 1)]
    units += [functools.partial(past_block, u) for u in range(nb - 1)]

    def write_outputs():
        for side in range(2):
            for p in range(H_ATT // 2):
                acc_e, acc_o = acc_sc[side * H_ATT + 2 * p], acc_sc[side * H_ATT + 2 * p + 1]
                num = jnp.where(lo_lanes, acc_e, acc_o)
                den = pltpu.roll(jnp.where(lo_lanes, acc_o, acc_e), half, 1)
                o_refs[side][:, p * LANE:(p + 1) * LANE] = (num / den).astype(o_refs[side].dtype)

    return units, write_outputs


def _mlstm_head_out(hh, gain, ogate):
    mu = jnp.mean(hh, axis=-1, keepdims=True)
    d = hh - mu
    var = jnp.mean(d * d, axis=-1, keepdims=True)
    return d * lax.rsqrt(var + LN_EPS) * gain * jax.nn.sigmoid(ogate)


def _mlstm_chunk(tok, q_ref, k_ref, v_ref, o_ref, g_ref, gain_ref, mem_ref, c_sc, n_sc, m_sc):
    L = MLSTM_CHUNK
    row = lax.broadcasted_iota(jnp.int32, (L, L), 0)
    col = lax.broadcasted_iota(jnp.int32, (L, L), 1)
    causal = col <= row
    lower = jnp.where(causal, 1.0, 0.0)
    upper = jnp.where(row <= col, 1.0, 0.0)
    ones = jnp.ones((L, DV_MLSTM), BF16)

    g = g_ref[tok, :]
    g_t = g.T

    def split3(x):
        hi = x.astype(BF16)
        r = x - hi.astype(F32)
        mid = r.astype(BF16)
        return hi, mid, (r - mid.astype(F32)).astype(BF16)

    b_col_all = sum(jnp.dot(lower.astype(BF16), part, preferred_element_type=F32) for part in split3(g))
    b_row_all = sum(jnp.dot(part, upper.astype(BF16), preferred_element_type=F32)
                    for part in split3(g_t[0:2 * SUBLANE, :]))
    for h in range(H_MLSTM):
        lanes = slice(h * DK_MLSTM, (h + 1) * DK_MLSTM)
        ig_row = g_t[h:h + 1, :]
        b_row = b_row_all[H_MLSTM + h:H_MLSTM + h + 1, :]
        ig = jnp.broadcast_to(g[:, h:h + 1], (L, LANE))
        b = jnp.broadcast_to(b_col_all[:, H_MLSTM + h:H_MLSTM + h + 1], (L, LANE))
        m_prev = m_sc[h:h + 1, :]
        dmat = jnp.where(causal, b - b_row + ig_row, NEG)
        m_inter = b + m_prev
        m_t = jnp.maximum(m_inter, jnp.broadcast_to(jnp.max(dmat, axis=1, keepdims=True), (L, LANE)))
        w_inter = jnp.exp(m_inter - m_t)
        qh, kh, vh = q_ref[tok, lanes], k_ref[tok, lanes], v_ref[tok, lanes]
        a = jnp.exp(dmat - m_t) * lax.dot_general(qh, kh, _NT, preferred_element_type=F32)
        c_prev = c_sc[h]
        n_prev = n_sc[h:h + 1, :]
        state = jnp.concatenate([c_prev, jnp.broadcast_to(n_prev, (DV_MLSTM, DK_MLSTM))], axis=0).astype(BF16)
        num_den = (jnp.concatenate([w_inter, w_inter], axis=1)
                   * lax.dot_general(qh, state, _NT, preferred_element_type=F32)
                   + jnp.dot(a.astype(BF16), jnp.concatenate([vh, ones], axis=1), preferred_element_type=F32))
        hh = num_den[:, :DV_MLSTM] / jnp.maximum(jnp.abs(num_den[:, DV_MLSTM:]), jnp.exp(-m_t))
        mem_ref[tok, lanes] = _mlstm_head_out(
            hh, gain_ref[:, lanes], o_ref[tok, lanes].astype(F32)).astype(mem_ref.dtype)

        m_new = m_t[L - 1:L, :]
        b_last = b[L - 1:L, :]
        g_inter = jnp.exp(b_last + m_prev - m_new)
        g_in = jnp.exp(b_last - b + ig - m_new)
        v_scaled = (vh.astype(F32) * g_in).astype(BF16)
        c_sc[h] = g_inter * c_prev + lax.dot_general(v_scaled, kh, _TN, preferred_element_type=F32)
        n_sc[h:h + 1, :] = g_inter * n_prev + jnp.sum(kh.astype(F32) * g_in, axis=0, keepdims=True)
        m_sc[h:h + 1, :] = m_new


def _prompt_mixers_kernel(pt_ref, q_lo_ref, q_hi_ref, kt_ref, vt_ref, mq_ref, mk_ref, mv_ref, mo_ref, g_ref, gain_ref,
                          qt_ref, knt_ref, ck_ref,
                          att_lo_ref, att_hi_ref, mem_ref, c_out, n_out, m_out, pe_ref, stats_ref, idx_ref,
                          kaug_ref, vaug_ref, kmt_ref, lhs_sc, m_att, acc_sc, c_sc, n_sc, m_sc, kbuf, sem,
                          *, nb, guest_rows):
    bi, i = pl.program_id(0), pl.program_id(1)
    steps = nb // 2

    @pl.when(i == 0)
    def _reset_state():
        c_sc[...] = jnp.zeros_like(c_sc)
        n_sc[...] = jnp.zeros_like(n_sc)
        m_sc[...] = jnp.zeros_like(m_sc)

    units, write_outputs = _moba_block_pair(i, nb, (q_lo_ref, q_hi_ref), kt_ref, vt_ref, (att_lo_ref, att_hi_ref),
                                            kaug_ref, vaug_ref, kmt_ref, lhs_sc, m_att, acc_sc)
    per_row = -(-len(units) // guest_rows)

    def work(r, row, slot):
        _moba_sample_scores(row, qt_ref, knt_ref, [kbuf.at[slot, p] for p in range(kbuf.shape[1])],
                            pe_ref.at[r], stats_ref.at[r], idx_ref.at[r])
        _mlstm_chunk(slice(r * MLSTM_CHUNK, (r + 1) * MLSTM_CHUNK), mq_ref, mk_ref, mv_ref, mo_ref, g_ref,
                     gain_ref, mem_ref, c_sc, n_sc, m_sc)
        for unit in units[r * per_row:(r + 1) * per_row]:
            unit()

    _guest_rows_row_ahead(bi * steps + i, pl.num_programs(0) * steps, guest_rows,
                          *_key_page_ring(pt_ref, ck_ref, kbuf, sem), work)
    write_outputs()

    @pl.when(i == steps - 1)
    def _emit_state():
        c_out[...] = c_sc[...]
        n_out[...] = n_sc[...]
        m_out[...] = m_sc[...]


def _prompt_mixers(q, kt, vt, mq, mk, mv, mo, gates, gain, qt, knt, cache_k, page_table):
    b, s, w = q.shape
    nb = s // MOBA_BLOCK
    steps = nb // 2
    tok_rows = 2 * MOBA_BLOCK
    wa, bs = qt.shape
    n_pages = page_table.shape[1]
    guest_rows = bs // (b * steps)
    assert s % tok_rows == 0 and nb <= SUBLANE and w == ATT_WIDTH == wa and mq.shape[2] == MLSTM_WIDTH
    assert MLSTM_CHUNK == LANE == DK_MLSTM == DV_MLSTM and guest_rows == tok_rows // MLSTM_CHUNK
    assert bs == guest_rows * b * steps == LANE and n_pages % PAGES_PER_BLOCK == 0
    assert MOBA_TOPK <= n_pages // PAGES_PER_BLOCK <= LANE
    ck = _cache_pages(cache_k).reshape(cache_k.shape[0], wa, PAGE_SIZE)
    q_lo = pl.BlockSpec((None, MOBA_BLOCK, w), lambda bi, i, *_: (bi, i, 0))
    q_hi = pl.BlockSpec((None, MOBA_BLOCK, w), lambda bi, i, *_: (bi, nb - 1 - i, 0))
    o_hi = pl.BlockSpec((None, MOBA_BLOCK, w), lambda bi, i, *_: (bi, steps - 1 - i, 0))
    tok = lambda width: pl.BlockSpec((None, tok_rows, width), lambda bi, i, *_: (bi, i, 0))
    seq = pl.BlockSpec((None, w, s), lambda bi, i, *_: (bi, 0, 0))
    state = lambda *dims: pl.BlockSpec((None,) + dims, lambda bi, i, *_: (bi,) + (0,) * len(dims))
    rows = lambda *dims: pl.BlockSpec((guest_rows,) + dims, lambda bi, i, *_: (bi * steps + i,) + (0,) * len(dims))
    whole = pl.BlockSpec((wa, bs), lambda *_: (0, 0))
    pair_state = lambda dt: pltpu.VMEM((2 * H_ATT, MOBA_BLOCK, LANE), dt)
    att_lo, att_hi, mem, c_p, n_p, m_p, pe, stats, ranked = pl.pallas_call(
        functools.partial(_prompt_mixers_kernel, nb=nb, guest_rows=guest_rows),
        grid_spec=pltpu.PrefetchScalarGridSpec(
            num_scalar_prefetch=1,
            grid=(b, steps),
            in_specs=[q_lo, q_hi, seq, seq, tok(MLSTM_WIDTH), tok(MLSTM_WIDTH), tok(MLSTM_WIDTH), tok(MLSTM_WIDTH),
                      tok(LANE), pl.BlockSpec((1, MLSTM_WIDTH), lambda *_: (0, 0)),
                      whole, whole, pl.BlockSpec(memory_space=pl.ANY)],
            out_specs=[q_lo, o_hi, tok(MLSTM_WIDTH), state(H_MLSTM, DV_MLSTM, DK_MLSTM), state(SUBLANE, LANE),
                       state(SUBLANE, LANE), rows(n_pages, H_ATT, LANE), rows(H_ATT, LANE), rows(H_ATT, LANE)],
            scratch_shapes=[pltpu.VMEM((H_ATT, nb, LANE, MOBA_BLOCK), BF16),
                            pltpu.VMEM((H_ATT, nb, LANE, MOBA_BLOCK), BF16),
                            pltpu.VMEM((2, w, LANE), BF16),
                            pair_state(BF16), pair_state(F32), pair_state(F32),
                            pltpu.VMEM((H_MLSTM, DV_MLSTM, DK_MLSTM), F32),
                            pltpu.VMEM((SUBLANE, LANE), F32),
                            pltpu.VMEM((SUBLANE, LANE), F32),
                            pltpu.VMEM((2, n_pages, wa, PAGE_SIZE), F32),
                            pltpu.SemaphoreType.DMA((2,))],
        ),
        out_shape=[jax.ShapeDtypeStruct((b, s // 2, w), BF16),
                   jax.ShapeDtypeStruct((b, s // 2, w), BF16),
                   jax.ShapeDtypeStruct((b, s, MLSTM_WIDTH), BF16),
                   jax.ShapeDtypeStruct((b, H_MLSTM, DV_MLSTM, DK_MLSTM), F32),
                   jax.ShapeDtypeStruct((b, SUBLANE, LANE), F32),
                   jax.ShapeDtypeStruct((b, SUBLANE, LANE), F32),
                   jax.ShapeDtypeStruct((bs, n_pages, H_ATT, LANE), F32),
                   jax.ShapeDtypeStruct((bs, H_ATT, LANE), F32),
                   jax.ShapeDtypeStruct((bs, H_ATT, LANE), jnp.int32)],
        compiler_params=_params("arbitrary", "arbitrary"),
    )(page_table, q, q, kt, vt, mq, mk, mv, mo, gates, gain, qt, knt, ck)
    return (att_lo, att_hi), mem, c_p, n_p, m_p, pe, stats, ranked


def _head_sublane(h):
    return (H_ATT // 2 - 1 - h) if h < H_ATT // 2 else (H_ATT + H_ATT // 2 - 1 - h)


def _head_rows(x):
    parts = []
    for h in range(H_ATT):
        tiles = [x[h * DH_ATT + SUBLANE * t:h * DH_ATT + SUBLANE * (t + 1), :] for t in range(DH_ATT // SUBLANE)]
        parts.append(sum(tiles[1:], tiles[0]))
    sub = lax.broadcasted_iota(jnp.int32, parts[0].shape, 0)
    folded = [p + pltpu.roll(p, 4, 0) for p in parts]
    quads = [jnp.where(sub < 4, folded[i], folded[i + 4]) for i in range(4)]
    take_up = (sub & 2) != 0
    pairs = [jnp.where(take_up, quads[i] + pltpu.roll(quads[i], 2, 0),
                       quads[i + 2] + pltpu.roll(quads[i + 2], 6, 0)) for i in range(2)]
    return jnp.where((sub & 1) != 0, pairs[0] + pltpu.roll(pairs[0], 1, 0), pairs[1] + pltpu.roll(pairs[1], 7, 0))


def _moba_sample_scores(b, qt_ref, knt_ref, kp_refs, pe_ref, stats_ref, idx_ref):
    n_pages = len(kp_refs)
    n_blocks = n_pages // PAGES_PER_BLOCK
    w = qt_ref.shape[0]
    on_b = lax.broadcasted_iota(jnp.int32, (w, LANE), 1) == b

    def column(ref):
        return jnp.sum(jnp.where(on_b, ref[...], 0.0), axis=1, keepdims=True)

    q_col = column(qt_ref) * (DH_ATT ** -0.5)
    q_wide = jnp.broadcast_to(q_col, (w, LANE))
    s_own = _head_rows(jnp.broadcast_to(q_col * column(knt_ref), (w, LANE)))[:, 0:1]
    s_pages = [_head_rows(kp_refs[p][...] * q_wide) for p in range(n_pages)]

    blk = [jnp.sum(sum(s_pages[n * PAGES_PER_BLOCK + 1:(n + 1) * PAGES_PER_BLOCK], s_pages[n * PAGES_PER_BLOCK]),
                   axis=1, keepdims=True) for n in range(n_blocks)]
    lane = lax.broadcasted_iota(jnp.int32, (H_ATT, LANE), 1)
    sel, ranked = [], jnp.zeros((H_ATT, LANE), jnp.int32)
    for n in range(n_blocks):
        rank = jnp.zeros((H_ATT, 1), jnp.int32)
        for o in range(n_blocks):
            if o != n:
                beats = (blk[o] >= blk[n]) if o < n else (blk[o] > blk[n])
                rank = rank + jnp.where(beats, 1, 0)
        sel.append(rank < MOBA_TOPK)
        ranked = jnp.where(rank == lane, n, ranked)
    m = s_own
    for p in range(n_pages):
        page_max = jnp.max(s_pages[p], axis=1, keepdims=True)
        m = jnp.maximum(m, jnp.where(sel[p // PAGES_PER_BLOCK], page_max, NEG))
    p_own = jnp.exp(s_own - m)
    total = jnp.zeros((H_ATT, LANE), F32)
    for p in range(n_pages):
        pe = jnp.where(sel[p // PAGES_PER_BLOCK], jnp.exp(s_pages[p] - m), 0.0)
        pe_ref[p] = pe
        total = total + pe
    row_sum = p_own + jnp.sum(total, axis=1, keepdims=True)
    stats_ref[...] = jnp.where(lane == 0, p_own, row_sum)
    idx_ref[...] = ranked


N_VALUE_CHUNKS = MOBA_TOPK * PAGES_PER_BLOCK


def _value_chunk_page(sel_ref, row, h, c):
    return sel_ref[row, h * MOBA_TOPK + c // PAGES_PER_BLOCK] * PAGES_PER_BLOCK + c % PAGES_PER_BLOCK


def _moba_sample_mix_row(row, sel_ref, pe_ref, stats_ref, vnt_ref, chunks_ref, o_ref):
    w = vnt_ref.shape[0]
    on_row = lax.broadcasted_iota(jnp.int32, (w, LANE), 1) == row
    vn_col = jnp.sum(jnp.where(on_row, vnt_ref[...], 0.0), axis=1, keepdims=True)
    stats = stats_ref[...]
    out_cols = []
    for h in range(H_ATT):
        r = _head_sublane(h)
        acc = jnp.zeros((DH_ATT, LANE), F32)
        for c in range(N_VALUE_CHUNKS):
            acc = acc + (pe_ref[_value_chunk_page(sel_ref, row, h, c), r:r + 1, :]
                         * chunks_ref[h * N_VALUE_CHUNKS + c])
        p_own, row_sum = stats[r:r + 1, 0:1], stats[r:r + 1, 1:2]
        rows = slice(h * DH_ATT, (h + 1) * DH_ATT)
        out_cols.append((jnp.sum(acc, axis=1, keepdims=True) + p_own * vn_col[rows, :]) / row_sum)
    o_ref[...] = jnp.where(on_row, jnp.concatenate(out_cols, axis=0), o_ref[...])


def _cache_pages(cache):
    return jnp.transpose(cache, (0, 2, 3, 1))


def _selected_blocks(ranked):
    b = ranked.shape[0]
    return jnp.stack([ranked[:, _head_sublane(h), :MOBA_TOPK] for h in range(H_ATT)], axis=1).reshape(b, -1)


def _mlstm_sample_tile(q_ref, k_ref, v_ref, o_ref, g_ref, gain_ref, c0_ref, n0_ref, m0_ref,
                       mem_ref, c_ref, n_ref, m_ref):
    tb = q_ref.shape[0]
    g = g_ref[...]
    sub = lax.broadcasted_iota(jnp.int32, (2 * tb, LANE), 0)
    zrows = jnp.zeros((tb, LANE), F32)
    for h in range(H_MLSTM):
        lanes = slice(h * DK_MLSTM, (h + 1) * DK_MLSTM)
        ig, lf, m0 = g[:, h:h + 1], g[:, H_MLSTM + h:H_MLSTM + h + 1], m0_ref[:, h:h + 1]
        q, k, v = q_ref[:, lanes], k_ref[:, lanes], v_ref[:, lanes]
        n0 = n0_ref[:, lanes]
        m_t = jnp.maximum(lf + m0, ig)
        w_inter = jnp.exp(lf + m0 - m_t)
        g_in = jnp.exp(ig - m_t)
        a = g_in * jnp.sum(q * k, axis=1, keepdims=True)
        den = w_inter * jnp.sum(n0 * q, axis=1, keepdims=True) + a
        q_b = q.astype(BF16)
        gv = jnp.concatenate([g_in * v, zrows], axis=0)
        k_b = jnp.concatenate([k, zrows], axis=0).astype(BF16)
        cq_rows = []
        for r in range(tb):
            c_prev = c0_ref[r, h]
            cq_rows.append(lax.dot_general(q_b, c_prev.astype(BF16), _NT, preferred_element_type=F32)[r:r + 1, :])
            outer = lax.dot_general(jnp.where(sub == r, gv, 0.0).astype(BF16), k_b, _TN,
                                    preferred_element_type=F32)
            c_ref[r, h] = w_inter[r:r + 1, :] * c_prev + outer
        cq = jnp.concatenate(cq_rows, axis=0)
        hh = (w_inter * cq + a * v) / jnp.maximum(jnp.abs(den), jnp.exp(-m_t))
        mem_ref[:, lanes] = _mlstm_head_out(hh, gain_ref[:, lanes], o_ref[:, lanes]).astype(mem_ref.dtype)
        n_ref[:, lanes] = w_inter * n0 + g_in * k
        m_ref[:, h:h + 1] = m_t


N_FINISH_INPUTS = 15


def _finish_kernel(*refs, alpha, ff_chunk, att_transposed, guest_rows, seq_tiles):
    if guest_rows:
        pt_ref, sel_ref, refs = refs[0], refs[1], refs[2:]
    if seq_tiles:
        att_hi_ref, refs = refs[2], refs[:2] + refs[3:]
    (x_ref, att_ref, mem_ref, g1_ref, sh2_ref, sc2_ref, g2_ref, wo_ref, ln1g_ref, ln1b_ref,
     wg_ref, wu_ref, wd_ref, ln2g_ref, ln2b_ref) = refs[:N_FINISH_INPUTS]
    if guest_rows:
        pe_ref, stats_ref, vnt_ref, cv_ref, y_ref, o_ref, vbuf, sem = refs[N_FINISH_INPUTS:]
    else:
        (y_ref,) = refs[N_FINISH_INPUTS:]

    if seq_tiles:
        in_upper_half = (pl.program_id(0) % seq_tiles) >= seq_tiles // 2
        att = jnp.where(in_upper_half, att_hi_ref[...], att_ref[...])
    else:
        att = att_ref[...].T if att_transposed else att_ref[...]
    aw = att.shape[1]
    mix = (jnp.dot(att.astype(BF16), wo_ref[0:aw, :], preferred_element_type=F32)
           + jnp.dot(mem_ref[...].astype(BF16), wo_ref[aw:, :], preferred_element_type=F32))
    x1 = _layernorm(alpha * x_ref[...] + (1.0 + g1_ref[...]) * mix, ln1g_ref[...], ln1b_ref[...])
    h2 = (x1 * (1.0 + sc2_ref[...]) + sh2_ref[...]).astype(BF16)
    n_ff = wg_ref.shape[1] // ff_chunk

    def ffn_chunk(c):
        cols = slice(c * ff_chunk, (c + 1) * ff_chunk)
        gate = jnp.dot(h2, wg_ref[:, cols], preferred_element_type=F32)
        up = jnp.dot(h2, wu_ref[:, cols], preferred_element_type=F32)
        act = (gate * jax.nn.sigmoid(gate) * up).astype(BF16)
        return jnp.dot(act, wd_ref[cols, :], preferred_element_type=F32)

    ffn = [jnp.zeros(x1.shape, F32)]
    if not guest_rows:
        for c in range(n_ff):
            ffn[0] = ffn[0] + ffn_chunk(c)
    else:
        i = pl.program_id(0)
        per_row = -(-n_ff // guest_rows)

        @pl.when(i == 0)
        def _init_out():
            o_ref[...] = jnp.zeros_like(o_ref)

        def chunk_copy(row, slot, h, c):
            page = pt_ref[row, _value_chunk_page(sel_ref, row, h, c)]
            return pltpu.make_async_copy(cv_ref.at[page, h], vbuf.at[slot, h * N_VALUE_CHUNKS + c], sem.at[slot])

        def start_row(row, slot):
            for h in range(H_ATT):
                for c in range(N_VALUE_CHUNKS):
                    chunk_copy(row, slot, h, c).start()

        def wait_row(row, slot):
            for h in range(H_ATT):
                for c in range(N_VALUE_CHUNKS):
                    chunk_copy(row, slot, h, c).wait()

        def work(r, row, slot):
            _moba_sample_mix_row(row, sel_ref, pe_ref.at[r], stats_ref.at[r], vnt_ref, vbuf.at[slot], o_ref)
            for c in range(r * per_row, min((r + 1) * per_row, n_ff)):
                ffn[0] = ffn[0] + ffn_chunk(c)

        _guest_rows_row_ahead(i, pl.num_programs(0), guest_rows, start_row, wait_row, work)
    y_ref[...] = _layernorm(alpha * x1 + (1.0 + g2_ref[...]) * ffn[0], ln2g_ref[...], ln2b_ref[...])


def _finish(x, att, mem, mods, weights, tm, rows_per_mod, alpha, att_transposed=False, guest=None):
    t, d = x.shape
    assert not att_transposed or tm == t
    steps = t // tm
    w_out, ln1_g, ln1_b, w_gate, w_up, w_down, ln2_g, ln2_b = weights
    if rows_per_mod is None:
        mod_spec = pl.BlockSpec((tm, d), lambda i, *_: (i, 0))
    else:
        per = rows_per_mod // tm
        mod_spec = pl.BlockSpec((None, 1, d), lambda i, *_: (i // per, 0, 0))
    tok = lambda width: pl.BlockSpec((tm, width), lambda i, *_: (i, 0))
    ff_chunk = FF_CHUNK
    assert w_gate.shape[1] % ff_chunk == 0
    seq_tiles = 0
    if isinstance(att, tuple):
        seq_tiles = rows_per_mod // tm
        half = seq_tiles // 2
        assert not att_transposed and seq_tiles % 2 == 0
        att_specs = [pl.BlockSpec((tm, att[0].shape[1]),
                                  lambda i, *_: ((i // seq_tiles) * half + jnp.minimum(i % seq_tiles, half - 1), 0)),
                     pl.BlockSpec((tm, att[1].shape[1]),
                                  lambda i, *_: ((i // seq_tiles) * half + jnp.maximum(i % seq_tiles - half, 0), 0))]
        att_args = att
    else:
        att_specs = [pl.BlockSpec(att.shape, lambda *_: (0, 0)) if att_transposed else tok(att.shape[1])]
        att_args = (att,)
    in_specs = [tok(d), *att_specs, tok(mem.shape[1]), mod_spec, mod_spec, mod_spec, mod_spec,
                _const_spec(w_out.shape), _const_spec(ln1_g.shape), _const_spec(ln1_b.shape),
                _const_spec(w_gate.shape), _const_spec(w_up.shape), _const_spec(w_down.shape),
                _const_spec(ln2_g.shape), _const_spec(ln2_b.shape)]
    args = (x, *att_args, mem, *mods, w_out, ln1_g, ln1_b, w_gate, w_up, w_down, ln2_g, ln2_b)
    assert len(in_specs) == N_FINISH_INPUTS + len(att_args) - 1
    body = functools.partial(_finish_kernel, alpha=alpha, ff_chunk=ff_chunk, att_transposed=att_transposed,
                             guest_rows=0, seq_tiles=seq_tiles)
    y_shape = jax.ShapeDtypeStruct((t, d), F32)
    if guest is None:
        return pl.pallas_call(body, grid=(steps,), in_specs=in_specs, out_specs=tok(d), out_shape=y_shape,
                              compiler_params=_params("arbitrary"))(*args)

    pe, stats, sel, vnt, cache_v, page_table = guest
    w, bs = vnt.shape
    n_pages = page_table.shape[1]
    guest_rows = bs // steps
    assert bs % steps == 0 and bs == LANE
    rows = lambda *dims: pl.BlockSpec((guest_rows,) + dims, lambda i, *_: (i,) + (0,) * len(dims))
    whole = pl.BlockSpec((w, bs), lambda *_: (0, 0))
    return pl.pallas_call(
        functools.partial(body, guest_rows=guest_rows),
        grid_spec=pltpu.PrefetchScalarGridSpec(
            num_scalar_prefetch=2,
            grid=(steps,),
            in_specs=in_specs + [rows(n_pages, H_ATT, LANE), rows(H_ATT, LANE), whole,
                                 pl.BlockSpec(memory_space=pl.ANY)],
            out_specs=[tok(d), whole],
            scratch_shapes=[pltpu.VMEM((2, H_ATT * N_VALUE_CHUNKS, DH_ATT, PAGE_SIZE), F32),
                            pltpu.SemaphoreType.DMA((2,))],
        ),
        out_shape=[y_shape, jax.ShapeDtypeStruct((w, bs), F32)],
        compiler_params=_params("arbitrary"),
    )(page_table, sel, *args, pe, stats, vnt, _cache_pages(cache_v))


def kernel(x_prompt, x_sample, cache_k, cache_v, state_C, state_n, state_m, page_table, c_prompt, c_sample,
           w_ada, b_ada, w_in, b_if, mlstm_norm_g, w_out, ln1_g, ln1_b, w_gate, w_up, w_down, ln2_g, ln2_b):
    depth = w_in.shape[0]
    assert depth == 1, "single-layer step"
    alpha = (2.0 * depth) ** 0.25
    bp, s, d = x_prompt.shape
    bs = x_sample.shape[0]
    assert x_sample.shape[1] == 1, "single-token decode step"

    w_main, w_att_t, w_gates = _proj_weights(w_in[0].T)
    b_gates = jnp.pad(b_if[0], (0, LANE - 2 * H_MLSTM)).reshape(1, LANE)
    gain = mlstm_norm_g[0].reshape(1, MLSTM_WIDTH)
    row = lambda a: a[0].reshape(1, -1)
    fin_w = (w_out[0].astype(BF16), row(ln1_g), row(ln1_b), w_gate[0].astype(BF16), w_up[0].astype(BF16),
             w_down[0].astype(BF16), row(ln2_g), row(ln2_b))

    c_all = jnp.concatenate([c_prompt, c_sample], axis=0)
    mod = _adaln(c_all, w_ada[0], b_ada[0])
    sh1, sc1, g1, sh2, sc2, g2 = (mod[:, i * d:(i + 1) * d] for i in range(6))
    pm = lambda a: a[:bp].reshape(bp, 1, d)
    sm = lambda a: a[bp:]

    xs = x_sample.reshape(bs, d)
    aq_s, ak_s, av_s, mq_s, mk_s, mv_s, mo_s, gates_s = _in_proj(
        xs, sm(sc1), sm(sh1), w_main, w_att_t, w_gates, b_gates, bs, None, (F32,) * N_PROJ_GROUPS,
        transposed=(0, 1, 2))

    xp = x_prompt.reshape(bp * s, d)
    tm = PROMPT_TOKEN_TILE
    aq, ak_t, av_t, mq, mk, mv, mo, gates, mem_s, c_s, n_s, m_s = _in_proj(
        xp, pm(sc1), pm(sh1), w_main, w_att_t, w_gates, b_gates, bp * s * SAMPLE_STATE_TILE // bs, s,
        (F32, F32, F32, BF16, BF16, BF16, F32), transposed=(1, 2),
        state_guest=(mq_s, mk_s, mv_s, mo_s, gates_s, gain, state_C[0], state_n[0].reshape(bs, -1), state_m[0]))
    seq = lambda a: a.reshape(bp, s, a.shape[-1])
    att, mem, c_p, n_p, m_p, pe, stats, ranked = _prompt_mixers(
        seq(aq), ak_t, av_t, seq(mq), seq(mk), seq(mv), seq(mo), seq(gates), gain,
        aq_s, ak_s, cache_k[0], page_table)
    y_p, att_s = _finish(
        xp, tuple(a.reshape(bp * s // 2, -1) for a in att), mem.reshape(bp * s, -1),
        (pm(g1), pm(sh2), pm(sc2), pm(g2)),
        fin_w, tm, s, alpha, guest=(pe, stats, _selected_blocks(ranked), av_s, cache_v[0], page_table))

    y_s = _finish(xs, att_s, mem_s, (sm(g1), sm(sh2), sm(sc2), sm(g2)), fin_w, bs, None, alpha,
                  att_transposed=True)

    rows_p = lambda a: jnp.transpose(a.reshape(bp, H_ATT, DH_ATT, s), (0, 3, 1, 2))[None]
    rows_s = lambda a: jnp.transpose(a.reshape(H_ATT, DH_ATT, bs), (2, 0, 1)).reshape(1, bs, 1, H_ATT, DH_ATT)
    return (y_p.reshape(bp, s, d), y_s.reshape(bs, 1, d),
            rows_p(ak_t), rows_p(av_t),
            c_p[None], n_p[None, :, :H_MLSTM, :], m_p[None, :, :H_MLSTM, 0],
            rows_s(ak_s), rows_s(av_s),
            c_s[None], n_s.reshape(1, bs, H_MLSTM, DK_MLSTM), m_s[None])
```

```python
import functools
import math

import jax
import jax.numpy as jnp
from jax import lax
from jax.experimental import pallas as pl
from jax.experimental.pallas import tpu as pltpu

F32 = jnp.float32
BF16 = jnp.bfloat16

LANE = 128
SUBLANE = 8
VMEM_LIMIT_BYTES = 56 * 1024 * 1024

H_ATT = 8
DH_ATT = 64
ATT_WIDTH = H_ATT * DH_ATT
MOBA_BLOCK = 256
MOBA_TOPK = 3
H_MLSTM = 4
DK_MLSTM = 128
DV_MLSTM = 128
MLSTM_WIDTH = H_MLSTM * DV_MLSTM
MLSTM_CHUNK = LANE
PAGE_SIZE = 128
PAGES_PER_BLOCK = MOBA_BLOCK // PAGE_SIZE
LN_EPS = 1e-5
NEG = -1e30
LOG2E = math.log2(math.e)
N_PROJ_GROUPS = 7
PROJ_GROUP = 512
MK_GROUP = 4
PROMPT_TOKEN_TILE = 512
FF_CHUNK = 2 * LANE
SAMPLE_STATE_TILE = SUBLANE

_NT = (((1,), (1,)), ((), ()))
_TN = (((0,), (0,)), ((), ()))


def _params(*sem):
    return pltpu.CompilerParams(dimension_semantics=sem, vmem_limit_bytes=VMEM_LIMIT_BYTES)


def _const_spec(shape):
    return pl.BlockSpec(shape, lambda *_: (0,) * len(shape), pipeline_mode=pl.Buffered(1))


def _layernorm(x, g, b):
    mu = jnp.mean(x, axis=-1, keepdims=True)
    d = x - mu
    var = jnp.mean(d * d, axis=-1, keepdims=True)
    return d * lax.rsqrt(var + LN_EPS) * g + b


def _top_blocks(val, nidx):
    cnt = jnp.zeros(val.shape, jnp.int32)
    for r in range(1, SUBLANE):
        other = pltpu.roll(val, r, 0)
        oidx = pltpu.roll(nidx, r, 0)
        beats = (other > val) | ((other == val) & (oidx < nidx))
        cnt = cnt + jnp.where(beats, 1, 0)
    return cnt < MOBA_TOPK


def _adaln_kernel(c_ref, w_ref, b_ref, o_ref):
    c = c_ref[...]
    s = c * jax.nn.sigmoid(c)
    o_ref[...] = jnp.dot(s, w_ref[...], preferred_element_type=F32) + b_ref[...]


def _adaln(c, w_ada, b_ada):
    rows, d = c.shape
    n = w_ada.shape[1]
    tn = d
    return pl.pallas_call(
        _adaln_kernel,
        grid=(n // tn,),
        in_specs=[pl.BlockSpec((rows, d), lambda j: (0, 0)),
                  pl.BlockSpec((d, tn), lambda j: (0, j)),
                  pl.BlockSpec((1, tn), lambda j: (0, j))],
        out_specs=pl.BlockSpec((rows, tn), lambda j: (0, j)),
        out_shape=jax.ShapeDtypeStruct((rows, n), F32),
        compiler_params=_params("arbitrary"),
    )(c, w_ada, b_ada.reshape(1, n))


N_ATT_GROUPS = 3


def _proj_weights_kernel(wt_ref, gate_rows_ref, main_ref, att_t_ref, gates_ref):
    g = pl.program_id(0)
    group_t = wt_ref[...]
    main_ref[...] = group_t.T.astype(BF16)

    @pl.when(g < N_ATT_GROUPS)
    def _attention_group():
        att_t_ref[...] = group_t.astype(BF16)

    @pl.when(g == 0)
    def _gate_columns():
        rows = gate_rows_ref[...]
        pad = jnp.zeros((LANE - rows.shape[0], rows.shape[1]), F32)
        gates_ref[...] = jnp.concatenate([rows, pad], axis=0).T.astype(BF16)


def _proj_weights(w_in_t):
    cols, d = w_in_t.shape
    n_main = N_PROJ_GROUPS * PROJ_GROUP
    n_gate = cols - n_main
    assert n_gate == 2 * H_MLSTM == SUBLANE
    return pl.pallas_call(
        _proj_weights_kernel,
        grid=(N_PROJ_GROUPS,),
        in_specs=[pl.BlockSpec((PROJ_GROUP, d), lambda g: (g, 0)),
                  pl.BlockSpec((n_gate, d), lambda g: (n_main // n_gate, 0))],
        out_specs=[pl.BlockSpec((d, PROJ_GROUP), lambda g: (0, g)),
                   pl.BlockSpec((PROJ_GROUP, d), lambda g: (jnp.minimum(g, N_ATT_GROUPS - 1), 0)),
                   pl.BlockSpec((d, LANE), lambda g: (0, 0))],
        out_shape=[jax.ShapeDtypeStruct((d, n_main), BF16),
                   jax.ShapeDtypeStruct((N_ATT_GROUPS * PROJ_GROUP, d), BF16),
                   jax.ShapeDtypeStruct((d, LANE), BF16)],
        compiler_params=_params("arbitrary"),
    )(w_in_t, w_in_t)


def _guest_rows_row_ahead(i, n_steps, guest_rows, start_row, wait_row, work):
    assert guest_rows % 2 == 0

    @pl.when(i == 0)
    def _first_row():
        start_row(0, 0)

    for r in range(guest_rows):
        row, slot = i * guest_rows + r, r % 2
        if r + 1 < guest_rows:
            start_row(row + 1, 1 - slot)
        else:
            @pl.when(i + 1 < n_steps)
            def _next_step_row():
                start_row(row + 1, 1 - slot)
        wait_row(row, slot)
        work(r, row, slot)


def _key_page_ring(pt_ref, ck_ref, kbuf, sem):
    n_pages = kbuf.shape[1]

    def page_copy(row, slot, p):
        return pltpu.make_async_copy(ck_ref.at[pt_ref[row, p]], kbuf.at[slot, p], sem.at[slot])

    def start_row(row, slot):
        for p in range(n_pages):
            page_copy(row, slot, p).start()

    def wait_row(row, slot):
        for p in range(n_pages):
            page_copy(row, slot, p).wait()

    return start_row, wait_row


N_STATE_GUEST_INPUTS = 9


def _in_proj_kernel(x_ref, sc_ref, sh_ref, w_ref, wt_ref, wg_ref, bg_ref, *refs, transposed, state_guest):
    if state_guest:
        guest_in, refs = refs[:N_STATE_GUEST_INPUTS], refs[N_STATE_GUEST_INPUTS:]
        _mlstm_sample_tile(*guest_in, *refs[N_PROJ_GROUPS + 1:])
    out_refs = refs[:N_PROJ_GROUPS + 1]
    proj_refs, g_ref = out_refs[:N_PROJ_GROUPS], out_refs[N_PROJ_GROUPS]
    h = (x_ref[...] * (1.0 + sc_ref[...]) + sh_ref[...]).astype(BF16)
    for gi, o_ref in enumerate(proj_refs):
        cols = slice(gi * PROJ_GROUP, (gi + 1) * PROJ_GROUP)
        if gi in transposed:
            y = lax.dot_general(wt_ref[cols, :], h, _NT, preferred_element_type=F32)
        else:
            y = jnp.dot(h, w_ref[:, cols], preferred_element_type=F32)
        if gi == MK_GROUP:
            y = y * (DK_MLSTM ** -0.5)
        o_ref[...] = y.astype(o_ref.dtype)
    g = jnp.dot(h, wg_ref[...], preferred_element_type=F32) + bg_ref[...]
    lane = lax.broadcasted_iota(jnp.int32, g.shape, 1)
    logsig = jnp.minimum(g, 0.0) - jnp.log1p(jnp.exp(-jnp.abs(g)))
    g_ref[...] = jnp.where(lane >= H_MLSTM, logsig, g)


def _in_proj(x, sc, sh, w_main, w_att_t, w_gate, b_gate, tm, rows_per_mod, out_dtypes, transposed=(),
             state_guest=None):
    t, d = x.shape
    steps = t // tm
    assert MK_GROUP not in transposed
    if rows_per_mod is None:
        mod_spec = pl.BlockSpec((tm, d), lambda i, *_: (i, 0))
        t_shape, t_spec = (PROJ_GROUP, t), pl.BlockSpec((PROJ_GROUP, tm), lambda i, *_: (0, i))
    else:
        per = rows_per_mod // tm
        mod_spec = pl.BlockSpec((None, 1, d), lambda i, *_: (i // per, 0, 0))
        t_shape = (t // rows_per_mod, PROJ_GROUP, rows_per_mod)
        t_spec = pl.BlockSpec((None, PROJ_GROUP, tm), lambda i, *_: (i // per, 0, i % per))
    out_shape, out_specs = [], []
    for gi, dt in enumerate(out_dtypes):
        if gi in transposed:
            out_shape.append(jax.ShapeDtypeStruct(t_shape, dt))
            out_specs.append(t_spec)
        else:
            out_shape.append(jax.ShapeDtypeStruct((t, PROJ_GROUP), dt))
            out_specs.append(pl.BlockSpec((tm, PROJ_GROUP), lambda i, *_: (i, 0)))
    out_shape.append(jax.ShapeDtypeStruct((t, LANE), F32))
    out_specs.append(pl.BlockSpec((tm, LANE), lambda i, *_: (i, 0)))
    in_specs = [pl.BlockSpec((tm, d), lambda i, *_: (i, 0)), mod_spec, mod_spec,
                _const_spec(w_main.shape), _const_spec(w_att_t.shape),
                _const_spec(w_gate.shape), _const_spec(b_gate.shape)]
    args = (x, sc, sh, w_main, w_att_t, w_gate, b_gate)
    if state_guest is not None:
        mq, mk, mv, mo, gates, gain, c0, n0, m0 = state_guest
        b, w = mq.shape
        tb = b // steps
        assert b % steps == 0 and tb % SUBLANE == 0
        rows = lambda width: pl.BlockSpec((tb, width), lambda i, *_: (i, 0))
        c_spec = pl.BlockSpec((tb,) + c0.shape[1:], lambda i, *_: (i, 0, 0, 0))
        in_specs += [rows(w), rows(w), rows(w), rows(w), rows(LANE), pl.BlockSpec((1, w), lambda *_: (0, 0)),
                     c_spec, rows(w), rows(H_MLSTM)]
        assert len(in_specs) == len(args) + N_STATE_GUEST_INPUTS
        args += (mq, mk, mv, mo, gates, gain, c0, n0, m0)
        out_specs += [rows(w), c_spec, rows(w), rows(H_MLSTM)]
        out_shape += [jax.ShapeDtypeStruct((b, w), F32), jax.ShapeDtypeStruct(c0.shape, F32),
                      jax.ShapeDtypeStruct((b, w), F32), jax.ShapeDtypeStruct((b, H_MLSTM), F32)]
    return pl.pallas_call(
        functools.partial(_in_proj_kernel, transposed=tuple(transposed), state_guest=state_guest is not None),
        grid=(steps,), in_specs=in_specs, out_specs=out_specs, out_shape=out_shape,
        compiler_params=_params("arbitrary"),
    )(*args)


def _moba_block_pair(i, nb, q_refs, kt_ref, vt_ref, o_refs, kaug_ref, vaug_ref, kmt_ref, lhs_sc, m_sc, acc_sc):
    blk = MOBA_BLOCK
    half = LANE // 2
    w = q_refs[0].shape[1]
    own = (i, nb - 1 - i)
    assert nb // 2 - 1 <= MOBA_TOPK

    @pl.when(i == 0)
    def _prepare_batch():
        srow = lax.broadcasted_iota(jnp.int32, (LANE, blk), 0)
        in_lo = srow < half
        head_of_row = lax.broadcasted_iota(jnp.int32, (w, LANE), 0) // DH_ATT
        lane_w = lax.broadcasted_iota(jnp.int32, (w, LANE), 1)
        kmt = jnp.zeros((w, LANE), F32)
        for j in range(nb):
            ktj = kt_ref[:, j * blk:(j + 1) * blk]
            vtj = vt_ref[:, j * blk:(j + 1) * blk]
            col = jnp.mean(ktj, axis=1, keepdims=True)
            kmt = jnp.where((lane_w % SUBLANE == j) & (lane_w // SUBLANE == head_of_row), col, kmt)
            for p in range(H_ATT // 2):
                kp, vp = ktj[p * LANE:(p + 1) * LANE, :], vtj[p * LANE:(p + 1) * LANE, :]
                kaug_ref[2 * p, j] = jnp.where(in_lo, kp, jnp.where(srow == half + j, 1.0, 0.0)).astype(BF16)
                kaug_ref[2 * p + 1, j] = jnp.where(in_lo, jnp.where(srow == j, 1.0, 0.0), kp).astype(BF16)
                vaug_ref[2 * p, j] = jnp.where(in_lo, vp, 1.0).astype(BF16)
                vaug_ref[2 * p + 1, j] = jnp.where(in_lo, 1.0, vp).astype(BF16)
        km_hi = kmt.astype(BF16)
        kmt_ref[0] = km_hi
        kmt_ref[1] = (kmt - km_hi.astype(F32)).astype(BF16)

    lane = lax.broadcasted_iota(jnp.int32, (blk, LANE), 1)
    lo_lanes = lane < half

    def store_lhs(side, p, bias_p):
        qp = q_refs[side][:, p * LANE:(p + 1) * LANE] * (DH_ATT ** -0.5 * LOG2E)
        lhs_sc[side * H_ATT + 2 * p] = jnp.where(lo_lanes, qp, bias_p).astype(BF16)
        lhs_sc[side * H_ATT + 2 * p + 1] = jnp.where(lo_lanes, bias_p, qp).astype(BF16)

    block_of_lane = lane % half
    bias_lo = jnp.where((block_of_lane < SUBLANE) & (block_of_lane > own[0]), NEG, 0.0)
    for p in range(H_ATT // 2):
        store_lhs(0, p, bias_lo)

    q32 = q_refs[1][...]
    q_hi = q32.astype(BF16)
    q_lo = (q32 - q_hi.astype(F32)).astype(BF16)
    sc = (jnp.dot(q_hi, kmt_ref[0], preferred_element_type=F32)
          + (jnp.dot(q_hi, kmt_ref[1], preferred_element_type=F32)
             + jnp.dot(q_lo, kmt_ref[0], preferred_element_type=F32)))
    sc_t = sc.T
    nidx = lax.broadcasted_iota(jnp.int32, (SUBLANE, blk), 0)
    past = nidx < own[1]
    biases = []
    for h in range(H_ATT):
        val = jnp.where(past, sc_t[h * SUBLANE:(h + 1) * SUBLANE, :], NEG)
        keep = (_top_blocks(val, nidx) & past) | (nidx == own[1])
        biases.append(jnp.where(keep, 0.0, NEG))
    zpad = jnp.zeros((half - SUBLANE, blk), F32)
    for p in range(H_ATT // 2):
        store_lhs(1, p, jnp.concatenate([biases[2 * p + 1], zpad, biases[2 * p], zpad], axis=0).T)

    def scores(idx, h, j):
        return jnp.dot(lhs_sc[idx], kaug_ref[h, j], preferred_element_type=F32)

    def row_max(s):
        return jnp.broadcast_to(jnp.max(s, axis=1, keepdims=True), (blk, LANE))

    def weights(s, m):
        return jnp.exp2(s - jnp.concatenate([m, m], axis=1)).astype(BF16)

    row = lax.broadcasted_iota(jnp.int32, (blk, blk), 0)
    col = lax.broadcasted_iota(jnp.int32, (blk, blk), 1)
    causal = col <= row

    def own_block(side):
        for h in range(H_ATT):
            idx = side * H_ATT + h
            s = jnp.where(causal, scores(idx, h, own[side]), NEG)
            m = row_max(s)
            acc_sc[idx] = lax.dot_general(weights(s, m), vaug_ref[h, own[side]], _NT, preferred_element_type=F32)
            m_sc[idx] = m

    def past_block(u):
        side = jnp.where(u >= own[0], 1, 0)
        j = u - side * own[0]
        for h in range(H_ATT):
            idx = side * H_ATT + h
            s = scores(idx, h, j)
            m_old = m_sc[idx]
            m_new = jnp.maximum(m_old, row_max(s))
            acc_sc[idx] = (jnp.exp2(m_old - m_new) * acc_sc[idx]
                           + lax.dot_general(weights(s, m_new), vaug_ref[h, j], _NT, preferred_element_type=F32))
            m_sc[idx] = m_new

    units = [functools.partial(own_block, 0), functools.partial(own_block, 1)]
    units += [functools.partial(past_block, u) for u in range(nb - 1)]

    def write_outputs():
        for side in range(2):
            for p in range(H_ATT // 2):
                acc_e, acc_o = acc_sc[side * H_ATT + 2 * p], acc_sc[side * H_ATT + 2 * p + 1]
                num = jnp.where(lo_lanes, acc_e, acc_o)
                den = pltpu.roll(jnp.where(lo_lanes, acc_o, acc_e), half, 1)
                o_refs[side][:, p * LANE:(p + 1) * LANE] = (num / den).astype(o_refs[side].dtype)

    return units, write_outputs


def _mlstm_head_out(hh, gain, ogate):
    mu = jnp.mean(hh, axis=-1, keepdims=True)
    d = hh - mu
    var = jnp.mean(d * d, axis=-1, keepdims=True)
    return d * lax.rsqrt(var + LN_EPS) * gain * jax.nn.sigmoid(ogate)


def _mlstm_chunk(tok, q_ref, k_ref, v_ref, o_ref, g_ref, gain_ref, mem_ref, c_sc, n_sc, m_sc):
    L = MLSTM_CHUNK
    row = lax.broadcasted_iota(jnp.int32, (L, L), 0)
    col = lax.broadcasted_iota(jnp.int32, (L, L), 1)
    causal = col <= row
    lower = jnp.where(causal, 1.0, 0.0)
    upper = jnp.where(row <= col, 1.0, 0.0)
    ones = jnp.ones((L, DV_MLSTM), BF16)

    g = g_ref[tok, :]
    g_t = g.T

    def split3(x):
        hi = x.astype(BF16)
        r = x - hi.astype(F32)
        mid = r.astype(BF16)
        return hi, mid, (r - mid.astype(F32)).astype(BF16)

    b_col_all = sum(jnp.dot(lower.astype(BF16), part, preferred_element_type=F32) for part in split3(g))
    b_row_all = sum(jnp.dot(part, upper.astype(BF16), preferred_element_type=F32)
                    for part in split3(g_t[0:2 * SUBLANE, :]))
    for h in range(H_MLSTM):
        lanes = slice(h * DK_MLSTM, (h + 1) * DK_MLSTM)
        ig_row = g_t[h:h + 1, :]
        b_row = b_row_all[H_MLSTM + h:H_MLSTM + h + 1, :]
        ig = jnp.broadcast_to(g[:, h:h + 1], (L, LANE))
        b = jnp.broadcast_to(b_col_all[:, H_MLSTM + h:H_MLSTM + h + 1], (L, LANE))
        m_prev = m_sc[h:h + 1, :]
        dmat = jnp.where(causal, b - b_row + ig_row, NEG)
        m_inter = b + m_prev
        m_t = jnp.maximum(m_inter, jnp.broadcast_to(jnp.max(dmat, axis=1, keepdims=True), (L, LANE)))
        w_inter = jnp.exp(m_inter - m_t)
        qh, kh, vh = q_ref[tok, lanes], k_ref[tok, lanes], v_ref[tok, lanes]
        a = jnp.exp(dmat - m_t) * lax.dot_general(qh, kh, _NT, preferred_element_type=F32)
        c_prev = c_sc[h]
        n_prev = n_sc[h:h + 1, :]
        state = jnp.concatenate([c_prev, jnp.broadcast_to(n_prev, (DV_MLSTM, DK_MLSTM))], axis=0).astype(BF16)
        num_den = (jnp.concatenate([w_inter, w_inter], axis=1)
                   * lax.dot_general(qh, state, _NT, preferred_element_type=F32)
                   + jnp.dot(a.astype(BF16), jnp.concatenate([vh, ones], axis=1), preferred_element_type=F32))
        hh = num_den[:, :DV_MLSTM] / jnp.maximum(jnp.abs(num_den[:, DV_MLSTM:]), jnp.exp(-m_t))
        mem_ref[tok, lanes] = _mlstm_head_out(
            hh, gain_ref[:, lanes], o_ref[tok, lanes].astype(F32)).astype(mem_ref.dtype)

        m_new = m_t[L - 1:L, :]
        b_last = b[L - 1:L, :]
        g_inter = jnp.exp(b_last + m_prev - m_new)
        g_in = jnp.exp(b_last - b + ig - m_new)
        v_scaled = (vh.astype(F32) * g_in).astype(BF16)
        c_sc[h] = g_inter * c_prev + lax.dot_general(v_scaled, kh, _TN, preferred_element_type=F32)
        n_sc[h:h + 1, :] = g_inter * n_prev + jnp.sum(kh.astype(F32) * g_in, axis=0, keepdims=True)
        m_sc[h:h + 1, :] = m_new


def _prompt_mixers_kernel(pt_ref, q_lo_ref, q_hi_ref, kt_ref, vt_ref, mq_ref, mk_ref, mv_ref, mo_ref, g_ref, gain_ref,
                          qt_ref, knt_ref, ck_ref,
                          att_lo_ref, att_hi_ref, mem_ref, c_out, n_out, m_out, pe_ref, stats_ref, idx_ref,
                          kaug_ref, vaug_ref, kmt_ref, lhs_sc, m_att, acc_sc, c_sc, n_sc, m_sc, kbuf, sem,
                          *, nb, guest_rows):
    bi, i = pl.program_id(0), pl.program_id(1)
    steps = nb // 2

    @pl.when(i == 0)
    def _reset_state():
        c_sc[...] = jnp.zeros_like(c_sc)
        n_sc[...] = jnp.zeros_like(n_sc)
        m_sc[...] = jnp.zeros_like(m_sc)

    units, write_outputs = _moba_block_pair(i, nb, (q_lo_ref, q_hi_ref), kt_ref, vt_ref, (att_lo_ref, att_hi_ref),
                                            kaug_ref, vaug_ref, kmt_ref, lhs_sc, m_att, acc_sc)
    per_row = -(-len(units) // guest_rows)

    def work(r, row, slot):
        _moba_sample_scores(row, qt_ref, knt_ref, [kbuf.at[slot, p] for p in range(kbuf.shape[1])],
                            pe_ref.at[r], stats_ref.at[r], idx_ref.at[r])
        _mlstm_chunk(slice(r * MLSTM_CHUNK, (r + 1) * MLSTM_CHUNK), mq_ref, mk_ref, mv_ref, mo_ref, g_ref,
                     gain_ref, mem_ref, c_sc, n_sc, m_sc)
        for unit in units[r * per_row:(r + 1) * per_row]:
            unit()

    _guest_rows_row_ahead(bi * steps + i, pl.num_programs(0) * steps, guest_rows,
                          *_key_page_ring(pt_ref, ck_ref, kbuf, sem), work)
    write_outputs()

    @pl.when(i == steps - 1)
    def _emit_state():
        c_out[...] = c_sc[...]
        n_out[...] = n_sc[...]
        m_out[...] = m_sc[...]


def _prompt_mixers(q, kt, vt, mq, mk, mv, mo, gates, gain, qt, knt, cache_k, page_table):
    b, s, w = q.shape
    nb = s // MOBA_BLOCK
    steps = nb // 2
    tok_rows = 2 * MOBA_BLOCK
    wa, bs = qt.shape
    n_pages = page_table.shape[1]
    guest_rows = bs // (b * steps)
    assert s % tok_rows == 0 and nb <= SUBLANE and w == ATT_WIDTH == wa and mq.shape[2] == MLSTM_WIDTH
    assert MLSTM_CHUNK == LANE == DK_MLSTM == DV_MLSTM and guest_rows == tok_rows // MLSTM_CHUNK
    assert bs == guest_rows * b * steps == LANE and n_pages % PAGES_PER_BLOCK == 0
    assert MOBA_TOPK <= n_pages // PAGES_PER_BLOCK <= LANE
    ck = _cache_pages(cache_k).reshape(cache_k.shape[0], wa, PAGE_SIZE)
    q_lo = pl.BlockSpec((None, MOBA_BLOCK, w), lambda bi, i, *_: (bi, i, 0))
    q_hi = pl.BlockSpec((None, MOBA_BLOCK, w), lambda bi, i, *_: (bi, nb - 1 - i, 0))
    o_hi = pl.BlockSpec((None, MOBA_BLOCK, w), lambda bi, i, *_: (bi, steps - 1 - i, 0))
    tok = lambda width: pl.BlockSpec((None, tok_rows, width), lambda bi, i, *_: (bi, i, 0))
    seq = pl.BlockSpec((None, w, s), lambda bi, i, *_: (bi, 0, 0))
    state = lambda *dims: pl.BlockSpec((None,) + dims, lambda bi, i, *_: (bi,) + (0,) * len(dims))
    rows = lambda *dims: pl.BlockSpec((guest_rows,) + dims, lambda bi, i, *_: (bi * steps + i,) + (0,) * len(dims))
    whole = pl.BlockSpec((wa, bs), lambda *_: (0, 0))
    pair_state = lambda dt: pltpu.VMEM((2 * H_ATT, MOBA_BLOCK, LANE), dt)
    att_lo, att_hi, mem, c_p, n_p, m_p, pe, stats, ranked = pl.pallas_call(
        functools.partial(_prompt_mixers_kernel, nb=nb, guest_rows=guest_rows),
        grid_spec=pltpu.PrefetchScalarGridSpec(
            num_scalar_prefetch=1,
            grid=(b, steps),
            in_specs=[q_lo, q_hi, seq, seq, tok(MLSTM_WIDTH), tok(MLSTM_WIDTH), tok(MLSTM_WIDTH), tok(MLSTM_WIDTH),
                      tok(LANE), pl.BlockSpec((1, MLSTM_WIDTH), lambda *_: (0, 0)),
                      whole, whole, pl.BlockSpec(memory_space=pl.ANY)],
            out_specs=[q_lo, o_hi, tok(MLSTM_WIDTH), state(H_MLSTM, DV_MLSTM, DK_MLSTM), state(SUBLANE, LANE),
                       state(SUBLANE, LANE), rows(n_pages, H_ATT, LANE), rows(H_ATT, LANE), rows(H_ATT, LANE)],
            scratch_shapes=[pltpu.VMEM((H_ATT, nb, LANE, MOBA_BLOCK), BF16),
                            pltpu.VMEM((H_ATT, nb, LANE, MOBA_BLOCK), BF16),
                            pltpu.VMEM((2, w, LANE), BF16),
                            pair_state(BF16), pair_state(F32), pair_state(F32),
                            pltpu.VMEM((H_MLSTM, DV_MLSTM, DK_MLSTM), F32),
                            pltpu.VMEM((SUBLANE, LANE), F32),
                            pltpu.VMEM((SUBLANE, LANE), F32),
                            pltpu.VMEM((2, n_pages, wa, PAGE_SIZE), F32),
                            pltpu.SemaphoreType.DMA((2,))],
        ),
        out_shape=[jax.ShapeDtypeStruct((b, s // 2, w), BF16),
                   jax.ShapeDtypeStruct((b, s // 2, w), BF16),
                   jax.ShapeDtypeStruct((b, s, MLSTM_WIDTH), BF16),
                   jax.ShapeDtypeStruct((b, H_MLSTM, DV_MLSTM, DK_MLSTM), F32),
                   jax.ShapeDtypeStruct((b, SUBLANE, LANE), F32),
                   jax.ShapeDtypeStruct((b, SUBLANE, LANE), F32),
                   jax.ShapeDtypeStruct((bs, n_pages, H_ATT, LANE), F32),
                   jax.ShapeDtypeStruct((bs, H_ATT, LANE), F32),
                   jax.ShapeDtypeStruct((bs, H_ATT, LANE), jnp.int32)],
        compiler_params=_params("arbitrary", "arbitrary"),
    )(page_table, q, q, kt, vt, mq, mk, mv, mo, gates, gain, qt, knt, ck)
    return (att_lo, att_hi), mem, c_p, n_p, m_p, pe, stats, ranked


def _head_sublane(h):
    return (H_ATT // 2 - 1 - h) if h < H_ATT // 2 else (H_ATT + H_ATT // 2 - 1 - h)


def _head_rows(x):
    parts = []
    for h in range(H_ATT):
        tiles = [x[h * DH_ATT + SUBLANE * t:h * DH_ATT + SUBLANE * (t + 1), :] for t in range(DH_ATT // SUBLANE)]
        parts.append(sum(tiles[1:], tiles[0]))
    sub = lax.broadcasted_iota(jnp.int32, parts[0].shape, 0)
    folded = [p + pltpu.roll(p, 4, 0) for p in parts]
    quads = [jnp.where(sub < 4, folded[i], folded[i + 4]) for i in range(4)]
    take_up = (sub & 2) != 0
    pairs = [jnp.where(take_up, quads[i] + pltpu.roll(quads[i], 2, 0),
                       quads[i + 2] + pltpu.roll(quads[i + 2], 6, 0)) for i in range(2)]
    return jnp.where((sub & 1) != 0, pairs[0] + pltpu.roll(pairs[0], 1, 0), pairs[1] + pltpu.roll(pairs[1], 7, 0))


def _moba_sample_scores(b, qt_ref, knt_ref, kp_refs, pe_ref, stats_ref, idx_ref):
    n_pages = len(kp_refs)
    n_blocks = n_pages // PAGES_PER_BLOCK
    w = qt_ref.shape[0]
    on_b = lax.broadcasted_iota(jnp.int32, (w, LANE), 1) == b

    def column(ref):
        return jnp.sum(jnp.where(on_b, ref[...], 0.0), axis=1, keepdims=True)

    q_col = column(qt_ref) * (DH_ATT ** -0.5)
    q_wide = jnp.broadcast_to(q_col, (w, LANE))
    s_own = _head_rows(jnp.broadcast_to(q_col * column(knt_ref), (w, LANE)))[:, 0:1]
    s_pages = [_head_rows(kp_refs[p][...] * q_wide) for p in range(n_pages)]

    blk = [jnp.sum(sum(s_pages[n * PAGES_PER_BLOCK + 1:(n + 1) * PAGES_PER_BLOCK], s_pages[n * PAGES_PER_BLOCK]),
                   axis=1, keepdims=True) for n in range(n_blocks)]
    lane = lax.broadcasted_iota(jnp.int32, (H_ATT, LANE), 1)
    sel, ranked = [], jnp.zeros((H_ATT, LANE), jnp.int32)
    for n in range(n_blocks):
        rank = jnp.zeros((H_ATT, 1), jnp.int32)
        for o in range(n_blocks):
            if o != n:
                beats = (blk[o] >= blk[n]) if o < n else (blk[o] > blk[n])
                rank = rank + jnp.where(beats, 1, 0)
        sel.append(rank < MOBA_TOPK)
        ranked = jnp.where(rank == lane, n, ranked)
    m = s_own
    for p in range(n_pages):
        page_max = jnp.max(s_pages[p], axis=1, keepdims=True)
        m = jnp.maximum(m, jnp.where(sel[p // PAGES_PER_BLOCK], page_max, NEG))
    p_own = jnp.exp(s_own - m)
    total = jnp.zeros((H_ATT, LANE), F32)
    for p in range(n_pages):
        pe = jnp.where(sel[p // PAGES_PER_BLOCK], jnp.exp(s_pages[p] - m), 0.0)
        pe_ref[p] = pe
        total = total + pe
    row_sum = p_own + jnp.sum(total, axis=1, keepdims=True)
    stats_ref[...] = jnp.where(lane == 0, p_own, row_sum)
    idx_ref[...] = ranked


N_VALUE_CHUNKS = MOBA_TOPK * PAGES_PER_BLOCK


def _value_chunk_page(sel_ref, row, h, c):
    return sel_ref[row, h * MOBA_TOPK + c // PAGES_PER_BLOCK] * PAGES_PER_BLOCK + c % PAGES_PER_BLOCK


def _moba_sample_mix_row(row, sel_ref, pe_ref, stats_ref, vnt_ref, chunks_ref, o_ref):
    w = vnt_ref.shape[0]
    on_row = lax.broadcasted_iota(jnp.int32, (w, LANE), 1) == row
    vn_col = jnp.sum(jnp.where(on_row, vnt_ref[...], 0.0), axis=1, keepdims=True)
    stats = stats_ref[...]
    out_cols = []
    for h in range(H_ATT):
        r = _head_sublane(h)
        acc = jnp.zeros((DH_ATT, LANE), F32)
        for c in range(N_VALUE_CHUNKS):
            acc = acc + (pe_ref[_value_chunk_page(sel_ref, row, h, c), r:r + 1, :]
                         * chunks_ref[h * N_VALUE_CHUNKS + c])
        p_own, row_sum = stats[r:r + 1, 0:1], stats[r:r + 1, 1:2]
        rows = slice(h * DH_ATT, (h + 1) * DH_ATT)
        out_cols.append((jnp.sum(acc, axis=1, keepdims=True) + p_own * vn_col[rows, :]) / row_sum)
    o_ref[...] = jnp.where(on_row, jnp.concatenate(out_cols, axis=0), o_ref[...])


def _cache_pages(cache):
    return jnp.transpose(cache, (0, 2, 3, 1))


def _selected_blocks(ranked):
    b = ranked.shape[0]
    return jnp.stack([ranked[:, _head_sublane(h), :MOBA_TOPK] for h in range(H_ATT)], axis=1).reshape(b, -1)


def _mlstm_sample_tile(q_ref, k_ref, v_ref, o_ref, g_ref, gain_ref, c0_ref, n0_ref, m0_ref,
                       mem_ref, c_ref, n_ref, m_ref):
    tb = q_ref.shape[0]
    g = g_ref[...]
    sub = lax.broadcasted_iota(jnp.int32, (2 * tb, LANE), 0)
    zrows = jnp.zeros((tb, LANE), F32)
    for h in range(H_MLSTM):
        lanes = slice(h * DK_MLSTM, (h + 1) * DK_MLSTM)
        ig, lf, m0 = g[:, h:h + 1], g[:, H_MLSTM + h:H_MLSTM + h + 1], m0_ref[:, h:h + 1]
        q, k, v = q_ref[:, lanes], k_ref[:, lanes], v_ref[:, lanes]
        n0 = n0_ref[:, lanes]
        m_t = jnp.maximum(lf + m0, ig)
        w_inter = jnp.exp(lf + m0 - m_t)
        g_in = jnp.exp(ig - m_t)
        a = g_in * jnp.sum(q * k, axis=1, keepdims=True)
        den = w_inter * jnp.sum(n0 * q, axis=1, keepdims=True) + a
        q_b = q.astype(BF16)
        gv = jnp.concatenate([g_in * v, zrows], axis=0)
        k_b = jnp.concatenate([k, zrows], axis=0).astype(BF16)
        cq_rows = []
        for r in range(tb):
            c_prev = c0_ref[r, h]
            cq_rows.append(lax.dot_general(q_b, c_prev.astype(BF16), _NT, preferred_element_type=F32)[r:r + 1, :])
            outer = lax.dot_general(jnp.where(sub == r, gv, 0.0).astype(BF16), k_b, _TN,
                                    preferred_element_type=F32)
            c_ref[r, h] = w_inter[r:r + 1, :] * c_prev + outer
        cq = jnp.concatenate(cq_rows, axis=0)
        hh = (w_inter * cq + a * v) / jnp.maximum(jnp.abs(den), jnp.exp(-m_t))
        mem_ref[:, lanes] = _mlstm_head_out(hh, gain_ref[:, lanes], o_ref[:, lanes]).astype(mem_ref.dtype)
        n_ref[:, lanes] = w_inter * n0 + g_in * k
        m_ref[:, h:h + 1] = m_t


N_FINISH_INPUTS = 15


def _finish_kernel(*refs, alpha, ff_chunk, att_transposed, guest_rows, seq_tiles):
    if guest_rows:
        pt_ref, sel_ref, refs = refs[0], refs[1], refs[2:]
    if seq_tiles:
        att_hi_ref, refs = refs[2], refs[:2] + refs[3:]
    (x_ref, att_ref, mem_ref, g1_ref, sh2_ref, sc2_ref, g2_ref, wo_ref, ln1g_ref, ln1b_ref,
     wg_ref, wu_ref, wd_ref, ln2g_ref, ln2b_ref) = refs[:N_FINISH_INPUTS]
    if guest_rows:
        pe_ref, stats_ref, vnt_ref, cv_ref, y_ref, o_ref, vbuf, sem = refs[N_FINISH_INPUTS:]
    else:
        (y_ref,) = refs[N_FINISH_INPUTS:]

    if seq_tiles:
        in_upper_half = (pl.program_id(0) % seq_tiles) >= seq_tiles // 2
        att = jnp.where(in_upper_half, att_hi_ref[...], att_ref[...])
    else:
        att = att_ref[...].T if att_transposed else att_ref[...]
    aw = att.shape[1]
    mix = (jnp.dot(att.astype(BF16), wo_ref[0:aw, :], preferred_element_type=F32)
           + jnp.dot(mem_ref[...].astype(BF16), wo_ref[aw:, :], preferred_element_type=F32))
    x1 = _layernorm(alpha * x_ref[...] + (1.0 + g1_ref[...]) * mix, ln1g_ref[...], ln1b_ref[...])
    h2 = (x1 * (1.0 + sc2_ref[...]) + sh2_ref[...]).astype(BF16)
    n_ff = wg_ref.shape[1] // ff_chunk

    def ffn_chunk(c):
        cols = slice(c * ff_chunk, (c + 1) * ff_chunk)
        gate = jnp.dot(h2, wg_ref[:, cols], preferred_element_type=F32)
        up = jnp.dot(h2, wu_ref[:, cols], preferred_element_type=F32)
        act = (gate * jax.nn.sigmoid(gate) * up).astype(BF16)
        return jnp.dot(act, wd_ref[cols, :], preferred_element_type=F32)

    ffn = [jnp.zeros(x1.shape, F32)]
    if not guest_rows:
        for c in range(n_ff):
            ffn[0] = ffn[0] + ffn_chunk(c)
    else:
        i = pl.program_id(0)
        per_row = -(-n_ff // guest_rows)

        @pl.when(i == 0)
        def _init_out():
            o_ref[...] = jnp.zeros_like(o_ref)

        def chunk_copy(row, slot, h, c):
            page = pt_ref[row, _value_chunk_page(sel_ref, row, h, c)]
            return pltpu.make_async_copy(cv_ref.at[page, h], vbuf.at[slot, h * N_VALUE_CHUNKS + c], sem.at[slot])

        def start_row(row, slot):
            for h in range(H_ATT):
                for c in range(N_VALUE_CHUNKS):
                    chunk_copy(row, slot, h, c).start(priority=c % 2)

        def wait_row(row, slot):
            for h in range(H_ATT):
                for c in range(N_VALUE_CHUNKS):
                    chunk_copy(row, slot, h, c).wait()

        def work(r, row, slot):
            _moba_sample_mix_row(row, sel_ref, pe_ref.at[r], stats_ref.at[r], vnt_ref, vbuf.at[slot], o_ref)
            for c in range(r * per_row, min((r + 1) * per_row, n_ff)):
                ffn[0] = ffn[0] + ffn_chunk(c)

        _guest_rows_row_ahead(i, pl.num_programs(0), guest_rows, start_row, wait_row, work)
    y_ref[...] = _layernorm(alpha * x1 + (1.0 + g2_ref[...]) * ffn[0], ln2g_ref[...], ln2b_ref[...])


def _finish(x, att, mem, mods, weights, tm, rows_per_mod, alpha, att_transposed=False, guest=None):
    t, d = x.shape
    assert not att_transposed or tm == t
    steps = t // tm
    w_out, ln1_g, ln1_b, w_gate, w_up, w_down, ln2_g, ln2_b = weights
    if rows_per_mod is None:
        mod_spec = pl.BlockSpec((tm, d), lambda i, *_: (i, 0))
    else:
        per = rows_per_mod // tm
        mod_spec = pl.BlockSpec((None, 1, d), lambda i, *_: (i // per, 0, 0))
    tok = lambda width: pl.BlockSpec((tm, width), lambda i, *_: (i, 0))
    ff_chunk = FF_CHUNK
    assert w_gate.shape[1] % ff_chunk == 0
    seq_tiles = 0
    if isinstance(att, tuple):
        seq_tiles = rows_per_mod // tm
        half = seq_tiles // 2
        assert not att_transposed and seq_tiles % 2 == 0
        att_specs = [pl.BlockSpec((tm, att[0].shape[1]),
                                  lambda i, *_: ((i // seq_tiles) * half + jnp.minimum(i % seq_tiles, half - 1), 0)),
                     pl.BlockSpec((tm, att[1].shape[1]),
                                  lambda i, *_: ((i // seq_tiles) * half + jnp.maximum(i % seq_tiles - half, 0), 0))]
        att_args = att
    else:
        att_specs = [pl.BlockSpec(att.shape, lambda *_: (0, 0)) if att_transposed else tok(att.shape[1])]
        att_args = (att,)
    in_specs = [tok(d), *att_specs, tok(mem.shape[1]), mod_spec, mod_spec, mod_spec, mod_spec,
                _const_spec(w_out.shape), _const_spec(ln1_g.shape), _const_spec(ln1_b.shape),
                _const_spec(w_gate.shape), _const_spec(w_up.shape), _const_spec(w_down.shape),
                _const_spec(ln2_g.shape), _const_spec(ln2_b.shape)]
    args = (x, *att_args, mem, *mods, w_out, ln1_g, ln1_b, w_gate, w_up, w_down, ln2_g, ln2_b)
    assert len(in_specs) == N_FINISH_INPUTS + len(att_args) - 1
    body = functools.partial(_finish_kernel, alpha=alpha, ff_chunk=ff_chunk, att_transposed=att_transposed,
                             guest_rows=0, seq_tiles=seq_tiles)
    y_shape = jax.ShapeDtypeStruct((t, d), F32)
    if guest is None:
        return pl.pallas_call(body, grid=(steps,), in_specs=in_specs, out_specs=tok(d), out_shape=y_shape,
                              compiler_params=_params("arbitrary"))(*args)

    pe, stats, sel, vnt, cache_v, page_table = guest
    w, bs = vnt.shape
    n_pages = page_table.shape[1]
    guest_rows = bs // steps
    assert bs % steps == 0 and bs == LANE
    rows = lambda *dims: pl.BlockSpec((guest_rows,) + dims, lambda i, *_: (i,) + (0,) * len(dims))
    whole = pl.BlockSpec((w, bs), lambda *_: (0, 0))
    return pl.pallas_call(
        functools.partial(body, guest_rows=guest_rows),
        grid_spec=pltpu.PrefetchScalarGridSpec(
            num_scalar_prefetch=2,
            grid=(steps,),
            in_specs=in_specs + [rows(n_pages, H_ATT, LANE), rows(H_ATT, LANE), whole,
                                 pl.BlockSpec(memory_space=pl.ANY)],
            out_specs=[tok(d), whole],
            scratch_shapes=[pltpu.VMEM((2, H_ATT * N_VALUE_CHUNKS, DH_ATT, PAGE_SIZE), F32),
                            pltpu.SemaphoreType.DMA((2,))],
        ),
        out_shape=[y_shape, jax.ShapeDtypeStruct((w, bs), F32)],
        compiler_params=_params("arbitrary"),
    )(page_table, sel, *args, pe, stats, vnt, _cache_pages(cache_v))


def kernel(x_prompt, x_sample, cache_k, cache_v, state_C, state_n, state_m, page_table, c_prompt, c_sample,
           w_ada, b_ada, w_in, b_if, mlstm_norm_g, w_out, ln1_g, ln1_b, w_gate, w_up, w_down, ln2_g, ln2_b):
    depth = w_in.shape[0]
    assert depth == 1, "single-layer step"
    alpha = (2.0 * depth) ** 0.25
    bp, s, d = x_prompt.shape
    bs = x_sample.shape[0]
    assert x_sample.shape[1] == 1, "single-token decode step"

    w_main, w_att_t, w_gates = _proj_weights(w_in[0].T)
    b_gates = jnp.pad(b_if[0], (0, LANE - 2 * H_MLSTM)).reshape(1, LANE)
    gain = mlstm_norm_g[0].reshape(1, MLSTM_WIDTH)
    row = lambda a: a[0].reshape(1, -1)
    fin_w = (w_out[0].astype(BF16), row(ln1_g), row(ln1_b), w_gate[0].astype(BF16), w_up[0].astype(BF16),
             w_down[0].astype(BF16), row(ln2_g), row(ln2_b))

    c_all = jnp.concatenate([c_prompt, c_sample], axis=0)
    mod = _adaln(c_all, w_ada[0], b_ada[0])
    sh1, sc1, g1, sh2, sc2, g2 = (mod[:, i * d:(i + 1) * d] for i in range(6))
    pm = lambda a: a[:bp].reshape(bp, 1, d)
    sm = lambda a: a[bp:]

    xs = x_sample.reshape(bs, d)
    aq_s, ak_s, av_s, mq_s, mk_s, mv_s, mo_s, gates_s = _in_proj(
        xs, sm(sc1), sm(sh1), w_main, w_att_t, w_gates, b_gates, bs, None, (F32,) * N_PROJ_GROUPS,
        transposed=(0, 1, 2))

    xp = x_prompt.reshape(bp * s, d)
    tm = PROMPT_TOKEN_TILE
    aq, ak_t, av_t, mq, mk, mv, mo, gates, mem_s, c_s, n_s, m_s = _in_proj(
        xp, pm(sc1), pm(sh1), w_main, w_att_t, w_gates, b_gates, bp * s * SAMPLE_STATE_TILE // bs, s,
        (F32, F32, F32, BF16, BF16, BF16, F32), transposed=(1, 2),
        state_guest=(mq_s, mk_s, mv_s, mo_s, gates_s, gain, state_C[0], state_n[0].reshape(bs, -1), state_m[0]))
    seq = lambda a: a.reshape(bp, s, a.shape[-1])
    att, mem, c_p, n_p, m_p, pe, stats, ranked = _prompt_mixers(
        seq(aq), ak_t, av_t, seq(mq), seq(mk), seq(mv), seq(mo), seq(gates), gain,
        aq_s, ak_s, cache_k[0], page_table)
    y_p, att_s = _finish(
        xp, tuple(a.reshape(bp * s // 2, -1) for a in att), mem.reshape(bp * s, -1),
        (pm(g1), pm(sh2), pm(sc2), pm(g2)),
        fin_w, tm, s, alpha, guest=(pe, stats, _selected_blocks(ranked), av_s, cache_v[0], page_table))

    y_s = _finish(xs, att_s, mem_s, (sm(g1), sm(sh2), sm(sc2), sm(g2)), fin_w, bs, None, alpha,
                  att_transposed=True)

    rows_p = lambda a: jnp.transpose(a.reshape(bp, H_ATT, DH_ATT, s), (0, 3, 1, 2))[None]
    rows_s = lambda a: jnp.transpose(a.reshape(H_ATT, DH_ATT, bs), (2, 0, 1)).reshape(1, bs, 1, H_ATT, DH_ATT)
    return (y_p.reshape(bp, s, d), y_s.reshape(bs, 1, d),
            rows_p(ak_t), rows_p(av_t),
            c_p[None], n_p[None, :, :H_MLSTM, :], m_p[None, :, :H_MLSTM, 0],
            rows_s(ak_s), rows_s(av_s),
            c_s[None], n_s.reshape(1, bs, H_MLSTM, DK_MLSTM), m_s[None])
```
